```python
import jax, jax.numpy as jnp
from jax import lax
import numpy as np

D_MODEL = 1024
BATCH = 4
SEQ = 8192
DEPTH = 2

HEAD_DIM = 64
ROPE_THETA = 10000.0
GRID_W = 64
EPS = 1e-6
NEG_INF = -1e30

A_HEADS = 8
A_KV_HEADS = 2
A_BLOCK = 128

LRU_WIDTH = 512
LRU_BLOCKS = 8
LRU_BLOCK_W = LRU_WIDTH // LRU_BLOCKS
LRU_C = 8.0
CONV_W = 4
CONV_LEFT = 2

C_HEADS = 8
C_KV_HEADS = 2
C_HALF_WINDOW = 128
C_BLOCK = 128

D_PATTERNS = ((128, 1), (512, 4), (2048, 16))
D_GROUPS = 3
D_HEADS = 4
D_BLOCK = 64

N_BRANCH = 4
MLP_HIDDEN = 4 * D_MODEL

IN_SIZES = (
    A_HEADS * HEAD_DIM, A_KV_HEADS * HEAD_DIM, A_KV_HEADS * HEAD_DIM,
    LRU_WIDTH, LRU_WIDTH,
    C_HEADS * HEAD_DIM, C_KV_HEADS * HEAD_DIM, C_KV_HEADS * HEAD_DIM,
    D_GROUPS * D_HEADS * HEAD_DIM, D_GROUPS * D_HEADS * HEAD_DIM, D_GROUPS * D_HEADS * HEAD_DIM,
    N_BRANCH * D_MODEL,
)
N_IN = sum(IN_SIZES)

kernel_name = "hybrid_parallel_gated_encoder"


def rmsnorm(x, g):
    xf = x.astype(jnp.float32)
    y = xf * lax.rsqrt(jnp.mean(xf * xf, axis=-1, keepdims=True) + EPS) * g.astype(jnp.float32)
    return y.astype(x.dtype)


def rope_tables(pos, dim):
    inv = ROPE_THETA ** (-jnp.arange(0, dim, 2, dtype=jnp.float32) / dim)
    ang = pos.astype(jnp.float32)[:, None] * inv[None, :]
    return jnp.cos(ang), jnp.sin(ang)


def apply_rotary(x, cos, sin):
    half = x.shape[-1] // 2
    xf = x.astype(jnp.float32)
    x1, x2 = xf[..., :half], xf[..., half:]
    c, s = cos[:, None, :], sin[:, None, :]
    return jnp.concatenate([x1 * c - x2 * s, x2 * c + x1 * s], axis=-1).astype(x.dtype)


def axial_rotary(x, row_cs, col_cs):
    h = x.shape[-1] // 2
    return jnp.concatenate([apply_rotary(x[..., :h], *row_cs), apply_rotary(x[..., h:], *col_cs)], axis=-1)


def dense_block_attention(q, k, v, block):
    b, s, hk, g, hd = q.shape
    nb = s // block
    qb = jnp.moveaxis(q.reshape(b, nb, block, hk, g, hd), 1, 0)

    def one_block(qblk):
        sc = jnp.einsum('bqhgd,bkhd->bhgqk', qblk, k).astype(jnp.float32) * (hd ** -0.5)
        p = jax.nn.softmax(sc, axis=-1)
        return jnp.einsum('bhgqk,bkhd->bqhgd', p.astype(v.dtype), v)

    o = lax.map(one_block, qb)
    return jnp.moveaxis(o, 0, 1).reshape(b, s, hk * g * hd)


def banded_attention(q, k, v, half_window, block, sink=None):
    b, L, hk, g, hd = q.shape
    nb = -(-L // block)
    lp = nb * block
    nw = -(-half_window // block)
    q = jnp.pad(q, ((0, 0), (0, lp - L), (0, 0), (0, 0), (0, 0)))
    pad_k = ((0, 0), (nw * block, lp - L + nw * block), (0, 0), (0, 0))
    kb = jnp.pad(k, pad_k).reshape(b, nb + 2 * nw, block, hk, hd)
    vb = jnp.pad(v, pad_k).reshape(b, nb + 2 * nw, block, hk, hd)
    kw = jnp.concatenate([kb[:, j:j + nb] for j in range(2 * nw + 1)], axis=2)
    vw = jnp.concatenate([vb[:, j:j + nb] for j in range(2 * nw + 1)], axis=2)
    qb = q.reshape(b, nb, block, hk, g, hd)
    sc = jnp.einsum('bnqhgd,bnkhd->bnhgqk', qb, kw).astype(jnp.float32) * (hd ** -0.5)
    qpos = jnp.arange(nb)[:, None] * block + jnp.arange(block)[None, :]
    kpos = (jnp.arange(nb)[:, None] - nw) * block + jnp.arange((2 * nw + 1) * block)[None, :]
    rel = kpos[:, None, :] - qpos[:, :, None]
    valid = (jnp.abs(rel) <= half_window) & (kpos[:, None, :] >= 0) & (kpos[:, None, :] < L)
    sc = jnp.where(valid[None, :, None, None], sc, NEG_INF)
    m = jnp.max(sc, axis=-1)
    if sink is not None:
        sink_b = sink.astype(jnp.float32).reshape(hk, g)[None, None, :, :, None]
        m = jnp.maximum(m, sink_b)
    p = jnp.exp(sc - m[..., None])
    l = jnp.sum(p, axis=-1)
    if sink is not None:
        l = l + jnp.exp(sink_b - m)
    o = jnp.einsum('bnhgqk,bnkhd->bnqhgd', p.astype(v.dtype), vw).astype(jnp.float32)
    l_t = jnp.transpose(l, (0, 1, 4, 2, 3))
    lse = jnp.transpose(m, (0, 1, 4, 2, 3)) + jnp.log(l_t)
    o = (o / l_t[..., None]).reshape(b, lp, hk, g, hd)[:, :L].astype(v.dtype)
    return o, lse.reshape(b, lp, hk, g)[:, :L]


def mixer_a(q, k, v, qk_g, row_cs, col_cs):
    b, s = q.shape[:2]
    q = rmsnorm(q.reshape(b, s, A_HEADS, HEAD_DIM), qk_g[0])
    k = rmsnorm(k.reshape(b, s, A_KV_HEADS, HEAD_DIM), qk_g[1])
    v = v.reshape(b, s, A_KV_HEADS, HEAD_DIM)
    q = axial_rotary(q, row_cs, col_cs).reshape(b, s, A_KV_HEADS, A_HEADS // A_KV_HEADS, HEAD_DIM)
    k = axial_rotary(k, row_cs, col_cs)
    return dense_block_attention(q, k, v, A_BLOCK)


def rg_lru_scan(xc, gate_w, gate_b, lam, reverse):
    b, s, w = xc.shape
    xh = xc.reshape(b, s, LRU_BLOCKS, LRU_BLOCK_W)
    gl = jnp.einsum('bsnc,gncd->gbsnd', xh, gate_w.astype(jnp.float32)).reshape(2, b, s, w)
    gl = gl + gate_b.astype(jnp.float32)[:, None, None, :]
    r = jax.nn.sigmoid(gl[0])
    i = jax.nn.sigmoid(gl[1])
    log_a = -LRU_C * r * jax.nn.softplus(-lam.astype(jnp.float32))
    a = jnp.exp(log_a)
    u = jnp.sqrt(-jnp.expm1(2.0 * log_a)) * (i * xc)

    def combine(e1, e2):
        a1, b1 = e1
        a2, b2 = e2
        return a1 * a2, a2 * b1 + b2

    _, h = lax.associative_scan(combine, (a, u), axis=1, reverse=reverse)
    return h


def mixer_b(xb, yb, conv_w, conv_b, gate_w, gate_b, lam):
    s = xb.shape[1]
    xf = xb.astype(jnp.float32)
    xp = jnp.pad(xf, ((0, 0), (CONV_LEFT, CONV_W - 1 - CONV_LEFT), (0, 0)))
    xc = sum(xp[:, j:j + s] * conv_w[j].astype(jnp.float32) for j in range(CONV_W)) + conv_b.astype(jnp.float32)
    h = rg_lru_scan(xc, gate_w[0], gate_b[0], lam[0], False) + rg_lru_scan(xc, gate_w[1], gate_b[1], lam[1], True)
    return (h * jax.nn.gelu(yb.astype(jnp.float32))).astype(xb.dtype)


def mixer_c(q, k, v, sink, cs):
    b, s = q.shape[:2]
    q = apply_rotary(q.reshape(b, s, C_HEADS, HEAD_DIM), *cs).reshape(b, s, C_KV_HEADS, C_HEADS // C_KV_HEADS, HEAD_DIM)
    k = apply_rotary(k.reshape(b, s, C_KV_HEADS, HEAD_DIM), *cs)
    v = v.reshape(b, s, C_KV_HEADS, HEAD_DIM)
    o, _ = banded_attention(q, k, v, C_HALF_WINDOW, C_BLOCK, sink)
    return o.reshape(b, s, C_HEADS * HEAD_DIM)


def mixer_d(q, k, v, cs):
    b, s = q.shape[:2]
    n_h = D_GROUPS * D_HEADS
    q = apply_rotary(q.reshape(b, s, n_h, HEAD_DIM), *cs).reshape(b, s, D_GROUPS, D_HEADS, HEAD_DIM)
    k = apply_rotary(k.reshape(b, s, n_h, HEAD_DIM), *cs).reshape(b, s, D_GROUPS, D_HEADS, HEAD_DIM)
    v = v.reshape(b, s, D_GROUPS, D_HEADS, HEAD_DIM)
    outs, lses = [], []
    for gi, (window, dil) in enumerate(D_PATTERNS):
        L = s // dil

        def to_residue(t):
            t = t[:, :, gi].reshape(b, L, dil, D_HEADS, HEAD_DIM)
            return jnp.transpose(t, (0, 2, 1, 3, 4)).reshape(b * dil, L, D_HEADS, HEAD_DIM)

        o, lse = banded_attention(to_residue(q)[:, :, :, None, :], to_residue(k), to_residue(v),
                                  window // (2 * dil), D_BLOCK)
        o = jnp.transpose(o.reshape(b, dil, L, D_HEADS, HEAD_DIM), (0, 2, 1, 3, 4)).reshape(b, s, D_HEADS, HEAD_DIM)
        lse = jnp.transpose(lse.reshape(b, dil, L, D_HEADS), (0, 2, 1, 3)).reshape(b, s, D_HEADS)
        outs.append(o.astype(jnp.float32))
        lses.append(lse)
    wts = jax.nn.softmax(jnp.stack(lses, axis=0), axis=0)
    out = jnp.sum(wts[..., None] * jnp.stack(outs, axis=0), axis=0)
    return out.reshape(b, s, D_HEADS * HEAD_DIM).astype(q.dtype)


def setup_inputs(seed: int = 0) -> dict:
    key = jax.random.key(seed)
    ks = jax.random.split(key, 20)
    f32 = jnp.float32

    def nrm(k, shape, scale):
        return jax.random.normal(k, shape, f32) * scale

    a0 = jax.random.uniform(ks[9], (DEPTH, 2, LRU_WIDTH), f32, 0.9, 0.999)
    p = a0 ** (1.0 / LRU_C)
    lru_lambda = jnp.log(p) - jnp.log1p(-p)
    return {
        "x": nrm(ks[0], (BATCH, SEQ, D_MODEL), 1.0),
        "norm_mix_g": 1.0 + nrm(ks[1], (DEPTH, D_MODEL), 0.02),
        "w_in": nrm(ks[2], (DEPTH, D_MODEL, N_IN), D_MODEL ** -0.5),
        "gate_bias": nrm(ks[3], (DEPTH, N_BRANCH, D_MODEL), 0.02),
        "qk_norm_g": 1.0 + nrm(ks[4], (DEPTH, 2, HEAD_DIM), 0.02),
        "conv_w": nrm(ks[5], (DEPTH, CONV_W, LRU_WIDTH), CONV_W ** -0.5),
        "conv_b": nrm(ks[6], (DEPTH, LRU_WIDTH), 0.02),
        "lru_gate_w": nrm(ks[7], (DEPTH, 2, 2, LRU_BLOCKS, LRU_BLOCK_W, LRU_BLOCK_W), LRU_BLOCK_W ** -0.5),
        "lru_gate_b": nrm(ks[8], (DEPTH, 2, 2, LRU_WIDTH), 0.02),
        "lru_lambda": lru_lambda,
        "sink_logit": nrm(ks[10], (DEPTH, C_HEADS), 1.0),
        "w_proj_a": nrm(ks[11], (DEPTH, A_HEADS * HEAD_DIM, D_MODEL), (A_HEADS * HEAD_DIM) ** -0.5),
        "w_proj_b": nrm(ks[12], (DEPTH, LRU_WIDTH, D_MODEL), LRU_WIDTH ** -0.5),
        "w_proj_c": nrm(ks[13], (DEPTH, C_HEADS * HEAD_DIM, D_MODEL), (C_HEADS * HEAD_DIM) ** -0.5),
        "w_proj_d": nrm(ks[14], (DEPTH, D_HEADS * HEAD_DIM, D_MODEL), (D_HEADS * HEAD_DIM) ** -0.5),
        "w_out": nrm(ks[15], (DEPTH, D_MODEL, D_MODEL), D_MODEL ** -0.5),
        "norm_mlp_g": 1.0 + nrm(ks[16], (DEPTH, D_MODEL), 0.02),
        "w_mlp1": nrm(ks[17], (DEPTH, D_MODEL, MLP_HIDDEN), D_MODEL ** -0.5),
        "w_mlp2": nrm(ks[18], (DEPTH, MLP_HIDDEN, D_MODEL), MLP_HIDDEN ** -0.5),
        "norm_final_g": 1.0 + nrm(ks[19], (D_MODEL,), 0.02),
    }


def reference(x, norm_mix_g, w_in, gate_bias, qk_norm_g, conv_w, conv_b, lru_gate_w, lru_gate_b,
              lru_lambda, sink_logit, w_proj_a, w_proj_b, w_proj_c, w_proj_d, w_out, norm_mlp_g,
              w_mlp1, w_mlp2, norm_final_g):
    b, s, _ = x.shape
    rows = s // GRID_W
    pos = jnp.arange(s)
    row_pos = jnp.repeat(jnp.arange(rows), GRID_W)
    col_pos = jnp.tile(jnp.arange(GRID_W), rows)
    row_cs = rope_tables(row_pos, HEAD_DIM // 2)
    col_cs = rope_tables(col_pos, HEAD_DIM // 2)
    seq_cs = rope_tables(pos, HEAD_DIM)
    split_points = np.cumsum(IN_SIZES)[:-1]

    for l in range(DEPTH):
        hn = rmsnorm(x, norm_mix_g[l])
        proj = hn @ w_in[l]
        (aq, ak, av, bx, by, cq, ck, cv, dq, dk, dv, gl) = jnp.split(proj, split_points, axis=-1)
        ya = mixer_a(aq, ak, av, qk_norm_g[l], row_cs, col_cs)
        yb = mixer_b(bx, by, conv_w[l], conv_b[l], lru_gate_w[l], lru_gate_b[l], lru_lambda[l])
        yc = mixer_c(cq, ck, cv, sink_logit[l], seq_cs)
        yd = mixer_d(dq, dk, dv, seq_cs)
        gates = jax.nn.sigmoid((gl.reshape(b, s, N_BRANCH, D_MODEL) + gate_bias[l]).astype(jnp.float32)).astype(x.dtype)
        merged = (gates[:, :, 0] * (ya @ w_proj_a[l]) + gates[:, :, 1] * (yb @ w_proj_b[l])
                  + gates[:, :, 2] * (yc @ w_proj_c[l]) + gates[:, :, 3] * (yd @ w_proj_d[l]))
        x = x + merged @ w_out[l]
        hn = rmsnorm(x, norm_mlp_g[l])
        x = x + jnp.square(jax.nn.relu(hn @ w_mlp1[l])) @ w_mlp2[l]
    return rmsnorm(x, norm_final_g)
```

```python
import functools
import math

import jax
import jax.numpy as jnp
from jax import lax
from jax.experimental import pallas as pl
from jax.experimental.pallas import tpu as pltpu

F32 = jnp.float32
BF16 = jnp.bfloat16

HEAD_DIM = 64
ROPE_THETA = 10000.0
GRID_W = 64
EPS = 1e-6
MASK_VALUE = -1e30
LOG2E = math.log2(math.e)

A_HEADS, A_KV_HEADS = 8, 2
LRU_WIDTH, LRU_BLOCKS, LRU_C = 512, 8, 8.0
CONV_W, CONV_LEFT = 4, 2
C_HEADS, C_KV_HEADS, C_HALF_WINDOW = 8, 2, 128
D_PATTERNS = ((128, 1), (512, 4), (2048, 16))
D_HEADS = 4
N_BRANCH = 4

LANES = 128
SUBLANES = 8
VMEM_LIMIT_BYTES = 56 * 1024 * 1024

TM_INPROJ = 512
TQ_ATTN_A = 512
TQ_BAND = 256
TT_LRU = 512
TM_MERGE = 512
TM_MLP = 512
MLP_CHUNK = 1024

Q_SCALE = HEAD_DIM ** -0.5 * LOG2E


def _cparams(sem):
    return pltpu.CompilerParams(dimension_semantics=sem, vmem_limit_bytes=VMEM_LIMIT_BYTES)


def _resident(shape):
    nd = len(shape)
    return pl.BlockSpec(shape, lambda *_: (0,) * nd, pipeline_mode=pl.Buffered(1))


def _rope_tables(seq):
    pos = jnp.arange(seq, dtype=F32)
    lane = jnp.arange(LANES)
    d = lane % HEAD_DIM

    inv = ROPE_THETA ** (-jnp.arange(0, HEAD_DIM, 2, dtype=F32) / HEAD_DIM)
    ang = pos[:, None] * inv[None, :]
    f = d % (HEAD_DIM // 2)
    cos_s, sin_s = jnp.cos(ang)[:, f], jnp.sin(ang)[:, f]
    lo = (d < HEAD_DIM // 2)[None, :]
    seq_t = (cos_s, jnp.where(lo, -sin_s, 0.0), jnp.where(lo, 0.0, sin_s))

    quarter = HEAD_DIM // 4
    inv_ax = ROPE_THETA ** (-jnp.arange(0, HEAD_DIM // 2, 2, dtype=F32) / (HEAD_DIM // 2))
    row_pos = (jnp.arange(seq) // GRID_W).astype(F32)
    col_pos = (jnp.arange(seq) % GRID_W).astype(F32)
    e = d % (HEAD_DIM // 2)
    fa = e % quarter
    is_col = (d >= HEAD_DIM // 2)[None, :]
    ang_ax = jnp.where(is_col, col_pos[:, None], row_pos[:, None]) * inv_ax[fa][None, :]
    cos_a, sin_a = jnp.cos(ang_ax), jnp.sin(ang_ax)
    lo_a = (e < quarter)[None, :]
    ax_t = (cos_a, jnp.where(lo_a, -sin_a, 0.0), jnp.where(lo_a, 0.0, sin_a))
    return ax_t, seq_t


def _rotate(x, tables, half):
    cos, sin_a, sin_b = tables
    return x * cos + pltpu.roll(x, LANES - half, 1) * sin_a + pltpu.roll(x, half, 1) * sin_b


def _split_dot(x, m):
    hi = x.astype(BF16)
    lo = (x - hi.astype(F32)).astype(BF16)
    return (jnp.dot(hi, m, preferred_element_type=F32) + jnp.dot(lo, m, preferred_element_type=F32))


def _head_mean_matrix():
    r = lax.broadcasted_iota(jnp.int32, (LANES, LANES), 0) // HEAD_DIM
    c = lax.broadcasted_iota(jnp.int32, (LANES, LANES), 1) // HEAD_DIM
    return jnp.where(r == c, 1.0 / HEAD_DIM, 0.0).astype(BF16)


def _head_rmsnorm(x, gain, mean_mat):
    var = _split_dot(x * x, mean_mat)
    return x * lax.rsqrt(var + EPS) * gain


IN_WIDTHS = (512, 128, 128, 512, 512, 512, 128, 128, 768, 768, 768, 4096)
IN_OFFSETS = tuple(sum(IN_WIDTHS[:i]) for i in range(len(IN_WIDTHS)))
N_IN = sum(IN_WIDTHS)


def _inproj_kernel(x_ref, g_ref, w_ref, qkg_ref,
                   ca_ref, saa_ref, sab_ref, cs_ref, ssa_ref, ssb_ref,
                   qta_ref, ka_ref, vta_ref, bx_ref, by_ref,
                   qc_ref, kc_ref, vc_ref, qd_ref, kd_ref, vd_ref, gl_ref):
    x = x_ref[0]
    var = jnp.mean(x * x, axis=-1, keepdims=True)
    hn = (x * lax.rsqrt(var + EPS) * g_ref[...]).astype(BF16)

    def proj(seg, lo=0, width=None):
        off = IN_OFFSETS[seg] + lo
        width = IN_WIDTHS[seg] if width is None else width
        return jnp.dot(hn, w_ref[:, off:off + width], preferred_element_type=F32)

    ax_t = (ca_ref[...], saa_ref[...], sab_ref[...])
    seq_t = (cs_ref[...], ssa_ref[...], ssb_ref[...])
    mean_mat = _head_mean_matrix()
    gq = qkg_ref[0:1, :]
    gk = qkg_ref[1:2, :]

    aq = proj(0)
    for j in range(IN_WIDTHS[0] // LANES):
        piece = _head_rmsnorm(aq[:, j * LANES:(j + 1) * LANES], gq, mean_mat)
        piece = _rotate(piece, ax_t, HEAD_DIM // 4) * Q_SCALE
        qta_ref[0, j * LANES:(j + 1) * LANES, :] = piece.T.astype(BF16)
    ak = _rotate(_head_rmsnorm(proj(1), gk, mean_mat), ax_t, HEAD_DIM // 4).astype(BF16)
    for g in range(A_KV_HEADS):
        ka_ref[0, g] = ak[:, g * HEAD_DIM:(g + 1) * HEAD_DIM]
    avt = proj(2).T.astype(BF16)
    for g in range(A_KV_HEADS):
        vta_ref[0, g, 0] = avt[g * HEAD_DIM:(g + 1) * HEAD_DIM, :]

    bx_ref[0] = proj(3)
    by_ref[0] = proj(4)

    def rope_store(ref, seg, scale):
        acc = proj(seg)
        for j in range(IN_WIDTHS[seg] // LANES):
            piece = _rotate(acc[:, j * LANES:(j + 1) * LANES], seq_t, HEAD_DIM // 2)
            if scale is not None:
                piece = piece * scale
            ref[0, :, j * LANES:(j + 1) * LANES] = piece.astype(BF16)

    rope_store(qc_ref, 5, Q_SCALE)
    rope_store(kc_ref, 6, None)
    vc_ref[0] = proj(7).astype(BF16)
    rope_store(qd_ref, 8, Q_SCALE)
    rope_store(kd_ref, 9, None)
    vd_ref[0] = proj(10).astype(BF16)

    chunk = 1024
    for c in range(IN_WIDTHS[11] // chunk):
        gl_ref[0, :, c * chunk:(c + 1) * chunk] = proj(11, c * chunk, chunk).astype(BF16)


def _inproj(x, g, w_bf16, qk_gain128, ax_t, seq_t):
    b, s, d = x.shape
    tm = min(TM_INPROJ, s)
    nt = s // tm
    tok = lambda width: pl.BlockSpec((1, tm, width), lambda i, bb: (bb, i, 0))
    tab = pl.BlockSpec((tm, LANES), lambda i, bb: (i, 0))
    out_shape = (
        jax.ShapeDtypeStruct((b, A_HEADS * HEAD_DIM, s), BF16),
        jax.ShapeDtypeStruct((b, A_KV_HEADS, s, HEAD_DIM), BF16),
        jax.ShapeDtypeStruct((b, A_KV_HEADS, nt, HEAD_DIM, tm), BF16),
        jax.ShapeDtypeStruct((b, s, LRU_WIDTH), F32),
        jax.ShapeDtypeStruct((b, s, LRU_WIDTH), F32),
        jax.ShapeDtypeStruct((b, s, 512), BF16),
        jax.ShapeDtypeStruct((b, s, 128), BF16),
        jax.ShapeDtypeStruct((b, s, 128), BF16),
        jax.ShapeDtypeStruct((b, s, 768), BF16),
        jax.ShapeDtypeStruct((b, s, 768), BF16),
        jax.ShapeDtypeStruct((b, s, 768), BF16),
        jax.ShapeDtypeStruct((b, s, 4096), BF16),
    )
    out_specs = (
        pl.BlockSpec((1, A_HEADS * HEAD_DIM, tm), lambda i, bb: (bb, 0, i)),
        pl.BlockSpec((1, A_KV_HEADS, tm, HEAD_DIM), lambda i, bb: (bb, 0, i, 0)),
        pl.BlockSpec((1, A_KV_HEADS, 1, HEAD_DIM, tm), lambda i, bb: (bb, 0, i, 0, 0)),
        tok(LRU_WIDTH), tok(LRU_WIDTH), tok(512), tok(128), tok(128),
        tok(768), tok(768), tok(768), tok(4096),
    )
    return pl.pallas_call(
        _inproj_kernel,
        grid=(nt, b),
        in_specs=[tok(d), _resident((1, d)), _resident((d, N_IN)), _resident((2, LANES)),
                  tab, tab, tab, tab, tab, tab],
        out_specs=out_specs,
        out_shape=out_shape,
        compiler_params=_cparams(("parallel", "parallel")),
        name="inproj",
    )(x, g, w_bf16, qk_gain128, *ax_t, *seq_t)


def _attn_a_kernel(q_ref, k_ref, v_ref, o_ref, m_ref, l_ref, acc_ref, *, group):
    j = pl.program_id(3)

    @pl.when(j == 0)
    def _():
        m_ref[...] = jnp.full(m_ref.shape, MASK_VALUE, F32)
        l_ref[...] = jnp.zeros(l_ref.shape, F32)
        acc_ref[...] = jnp.zeros(acc_ref.shape, F32)

    k = k_ref[0, 0]
    vt = v_ref[0, 0, 0]
    for h in range(group):
        qt = q_ref[0, h * HEAD_DIM:(h + 1) * HEAD_DIM, :]
        s = jnp.dot(k, qt, preferred_element_type=F32)
        m_old = m_ref[h]
        m_new = jnp.maximum(m_old, jnp.max(s, axis=0, keepdims=True))
        alpha = jnp.exp2(m_old - m_new)
        p = jnp.exp2(s - m_new)
        l_ref[h] = alpha * l_ref[h] + jnp.sum(p, axis=0, keepdims=True)
        acc_ref[h] = acc_ref[h] * alpha + jnp.dot(vt, p.astype(BF16), preferred_element_type=F32)
        m_ref[h] = m_new

    @pl.when(j == pl.num_programs(3) - 1)
    def _():
        ot = jnp.concatenate([acc_ref[h] / l_ref[h] for h in range(group)], axis=0)
        o_ref[0] = ot.T.astype(o_ref.dtype)


def _attn_a(qt, k, vt):
    b, hd_all, s = qt.shape
    nkv, tk = vt.shape[1], vt.shape[4]
    group = hd_all // HEAD_DIM // nkv
    tq = min(TQ_ATTN_A, s)
    gw = group * HEAD_DIM
    return pl.pallas_call(
        functools.partial(_attn_a_kernel, group=group),
        grid=(b, nkv, s // tq, s // tk),
        in_specs=[pl.BlockSpec((1, gw, tq), lambda bb, g, i, j: (bb, g, i)),
                  pl.BlockSpec((1, 1, tk, HEAD_DIM), lambda bb, g, i, j: (bb, g, j, 0)),
                  pl.BlockSpec((1, 1, 1, HEAD_DIM, tk), lambda bb, g, i, j: (bb, g, j, 0, 0))],
        out_specs=pl.BlockSpec((1, tq, gw), lambda bb, g, i, j: (bb, i, g)),
        out_shape=jax.ShapeDtypeStruct((b, s, hd_all), BF16),
        scratch_shapes=[pltpu.VMEM((group, 1, tq), F32), pltpu.VMEM((group, 1, tq), F32),
                        pltpu.VMEM((group, HEAD_DIM, tq), F32)],
        compiler_params=_cparams(("parallel", "parallel", "parallel", "arbitrary")),
        name="attn_a",
    )(qt, k, vt)


def _band_kernel(*refs, nkv, group, half_window, seq_len, has_sink, want_lse):
    q_ref, kp_ref, km_ref, kn_ref, vp_ref, vm_ref, vn_ref = refs[:7]
    rest = list(refs[7:])
    sink_ref = rest.pop(0) if has_sink else None
    o_ref = rest.pop(0)
    lse_ref = rest.pop(0) if want_lse else None

    tq = q_ref.shape[1]
    halo = kp_ref.shape[1]
    nk = tq + 2 * halo
    q0 = pl.program_id(2) * tq
    q = q_ref[0]
    kcat = jnp.concatenate([kp_ref[0], km_ref[0], kn_ref[0]], axis=0)
    vcat = jnp.concatenate([vp_ref[0], vm_ref[0], vn_ref[0]], axis=0)

    row = lax.broadcasted_iota(jnp.int32, (tq, nk), 0)
    col = lax.broadcasted_iota(jnp.int32, (tq, nk), 1)
    kpos = q0 - halo + col
    rel = col - halo - row
    valid = (jnp.abs(rel) <= half_window) & (kpos >= 0) & (kpos < seq_len)

    for g in range(nkv):
        kg = kcat[:, g * HEAD_DIM:(g + 1) * HEAD_DIM]
        vg = vcat[:, g * HEAD_DIM:(g + 1) * HEAD_DIM]
        for hh in range(group):
            h = g * group + hh
            qh = q[:, h * HEAD_DIM:(h + 1) * HEAD_DIM]
            s = lax.dot_general(qh, kg, (((1,), (1,)), ((), ())), preferred_element_type=F32)
            s = jnp.where(valid, s, MASK_VALUE)
            m = jnp.max(s, axis=1, keepdims=True)
            if has_sink:
                sink = sink_ref[h] * LOG2E
                m = jnp.maximum(m, sink)
            p = jnp.exp2(s - m)
            l = jnp.sum(p, axis=1, keepdims=True)
            if has_sink:
                l = l + jnp.exp2(sink - m)
            o = jnp.dot(p.astype(BF16), vg, preferred_element_type=F32) / l
            o_ref[0, :, h * HEAD_DIM:(h + 1) * HEAD_DIM] = o.astype(o_ref.dtype)
            if want_lse:
                lse = m + jnp.log2(l)
                lse_ref[0, :, h * HEAD_DIM:(h + 1) * HEAD_DIM] = jnp.broadcast_to(lse, (tq, HEAD_DIM))


def _band_attention(q, k, v, *, n_seq, col_of, nkv, group, half_window, sink=None, want_lse=False):
    b, seq_len = q.shape[0], q.shape[1]
    wq = nkv * group * HEAD_DIM
    wk = nkv * HEAD_DIM
    halo = half_window
    tq = min(TQ_BAND, seq_len)
    per = tq // halo
    n_halo = seq_len // halo

    main = lambda width: pl.BlockSpec((1, tq, width), lambda bb, r, i: (bb, i, col_of(r)))
    prev = lambda width: pl.BlockSpec(
        (1, halo, width), lambda bb, r, i: (bb, jnp.maximum(i * per - 1, 0), col_of(r)))
    nxt = lambda width: pl.BlockSpec(
        (1, halo, width), lambda bb, r, i: (bb, jnp.minimum((i + 1) * per, n_halo - 1), col_of(r)))
    in_specs = [main(wq), prev(wk), main(wk), nxt(wk), prev(wk), main(wk), nxt(wk)]
    args = [q, k, k, k, v, v, v]
    if sink is not None:
        in_specs.append(pl.BlockSpec(memory_space=pltpu.SMEM))
        args.append(sink)
    out_block = pl.BlockSpec((1, tq, wq), lambda bb, r, i: (bb, i, r))
    out_shape = [jax.ShapeDtypeStruct((b, seq_len, n_seq * wq), BF16)]
    out_specs = [out_block]
    if want_lse:
        out_shape.append(jax.ShapeDtypeStruct((b, seq_len, n_seq * wq), F32))
        out_specs.append(out_block)
    res = pl.pallas_call(
        functools.partial(_band_kernel, nkv=nkv, group=group, half_window=half_window,
                          seq_len=seq_len, has_sink=sink is not None, want_lse=want_lse),
        grid=(b, n_seq, seq_len // tq),
        in_specs=in_specs,
        out_specs=out_specs,
        out_shape=out_shape,
        compiler_params=_cparams(("parallel", "parallel", "parallel")),
        name="band_attn",
    )(*args)
    return res if want_lse else res[0]


def _softplus(x):
    return jnp.maximum(x, 0.0) + jnp.log1p(jnp.exp(-jnp.abs(x)))


def _lru_kernel(xf_ref, xfp_ref, xfn_ref, xb_ref, xbp_ref, xbn_ref,
                cw_ref, cb_ref, gw_ref, gb_ref, lam_ref,
                hf_ref, hb_ref, a_scr, u_scr, carry_ref):
    i = pl.program_id(1)
    nt = pl.num_programs(1)
    tt = xf_ref.shape[1]
    nblk = tt // SUBLANES

    @pl.when(i == 0)
    def _():
        carry_ref[...] = jnp.zeros(carry_ref.shape, F32)

    row = lax.broadcasted_iota(jnp.int32, (SUBLANES, LRU_WIDTH), 0)

    def prepare(d, x_ref, xp_ref, xn_ref, tile):
        prev_rows = jnp.where(tile > 0, xp_ref[0], 0.0)
        next_rows = jnp.where(tile < nt - 1, xn_ref[0], 0.0)
        xp = jnp.concatenate([prev_rows, x_ref[0], next_rows], axis=0)
        xc = cb_ref[...]
        base = SUBLANES - CONV_LEFT
        for jj in range(CONV_W):
            xc = xc + xp[base + jj:base + jj + tt, :] * cw_ref[jj:jj + 1, :]
        xcb = xc.astype(BF16)
        r = jax.nn.sigmoid(jnp.dot(xcb, gw_ref[d, 0], preferred_element_type=F32) + gb_ref[d, 0:1, :])
        gi = jax.nn.sigmoid(jnp.dot(xcb, gw_ref[d, 1], preferred_element_type=F32) + gb_ref[d, 1:2, :])
        log_a = -LRU_C * r * _softplus(-lam_ref[d:d + 1, :])
        a = jnp.exp(log_a)
        a_scr[d] = a
        u_scr[d] = jnp.sqrt(-jnp.tanh(log_a) * (a * a + 1.0)) * (gi * xc)

    prepare(0, xf_ref, xfp_ref, xfn_ref, i)
    prepare(1, xb_ref, xbp_ref, xbn_ref, nt - 1 - i)

    def scan_block(d, blk, carry, out_ref):
        a8 = a_scr[d, pl.ds(blk * SUBLANES, SUBLANES), :]
        u8 = u_scr[d, pl.ds(blk * SUBLANES, SUBLANES), :]
        for step in (1, 2, 4):
            if d == 0:
                shift, ident = step, row < step
            else:
                shift, ident = SUBLANES - step, row >= SUBLANES - step
            u_sh = jnp.where(ident, 0.0, pltpu.roll(u8, shift, 0))
            a_sh = jnp.where(ident, 1.0, pltpu.roll(a8, shift, 0))
            u8 = u8 + a8 * u_sh
            a8 = a8 * a_sh
        h8 = u8 + a8 * carry
        out_ref[0, pl.ds(blk * SUBLANES, SUBLANES), :] = h8
        last = h8[SUBLANES - 1:SUBLANES, :] if d == 0 else h8[0:1, :]
        return jnp.broadcast_to(last, (SUBLANES, LRU_WIDTH))

    def body(kb, carries):
        cf, cb = carries
        cf = scan_block(0, kb, cf, hf_ref)
        cb = scan_block(1, nblk - 1 - kb, cb, hb_ref)
        return cf, cb

    cf, cb = lax.fori_loop(0, nblk, body, (carry_ref[0], carry_ref[1]))
    carry_ref[0] = cf
    carry_ref[1] = cb


def _lru(bx, conv_w, conv_b, gate_w_dense, gate_b, lam):
    b, s, w = bx.shape
    tt = min(TT_LRU, s)
    nt = s // tt
    per = tt // SUBLANES
    n8 = s // SUBLANES
    fwd = lambda i: i
    bwd = lambda i: nt - 1 - i

    def specs(tile_of):
        return [
            pl.BlockSpec((1, tt, w), lambda bb, i: (bb, tile_of(i), 0)),
            pl.BlockSpec((1, SUBLANES, w), lambda bb, i: (bb, jnp.maximum(tile_of(i) * per - 1, 0), 0)),
            pl.BlockSpec((1, SUBLANES, w), lambda bb, i: (bb, jnp.minimum((tile_of(i) + 1) * per, n8 - 1), 0)),
        ]

    return pl.pallas_call(
        _lru_kernel,
        grid=(b, nt),
        in_specs=specs(fwd) + specs(bwd) + [
            _resident(conv_w.shape), _resident(conv_b.shape), _resident(gate_w_dense.shape),
            _resident(gate_b.shape), _resident(lam.shape)],
        out_specs=[pl.BlockSpec((1, tt, w), lambda bb, i: (bb, i, 0)),
                   pl.BlockSpec((1, tt, w), lambda bb, i: (bb, nt - 1 - i, 0))],
        out_shape=[jax.ShapeDtypeStruct((b, s, w), F32), jax.ShapeDtypeStruct((b, s, w), F32)],
        scratch_shapes=[pltpu.VMEM((2, tt, w), F32), pltpu.VMEM((2, tt, w), F32),
                        pltpu.VMEM((2, SUBLANES, w), F32)],
        compiler_params=_cparams(("parallel", "arbitrary")),
        name="lru",
    )(bx, bx, bx, bx, bx, bx, conv_w, conv_b, gate_w_dense, gate_b, lam)


def _gelu_tanh(x):
    return 0.5 * x * (1.0 + jnp.tanh(math.sqrt(2.0 / math.pi) * (x + 0.044715 * (x * x * x))))


def _merge_kernel(x_ref, ya_ref, hf_ref, hb_ref, by_ref, yc_ref,
                  o0_ref, o1_ref, o2_ref, l0_ref, l1_ref, l2_ref, gl_ref, gbias_ref,
                  wa_ref, wb_ref, wc_ref, wd_ref, wo_ref, out_ref):
    d_model = x_ref.shape[2]

    def gate(kk):
        logits = gl_ref[0, :, kk * d_model:(kk + 1) * d_model].astype(F32) + gbias_ref[kk:kk + 1, :]
        return jax.nn.sigmoid(logits)

    merged = gate(0) * jnp.dot(ya_ref[0], wa_ref[...], preferred_element_type=F32)

    yb = (hf_ref[0] + hb_ref[0]) * _gelu_tanh(by_ref[0])
    merged = merged + gate(1) * jnp.dot(yb.astype(BF16), wb_ref[...], preferred_element_type=F32)

    merged = merged + gate(2) * jnp.dot(yc_ref[0], wc_ref[...], preferred_element_type=F32)

    l0, l1, l2 = l0_ref[0], l1_ref[0], l2_ref[0]
    mx = jnp.maximum(jnp.maximum(l0, l1), l2)
    e0, e1, e2 = jnp.exp2(l0 - mx), jnp.exp2(l1 - mx), jnp.exp2(l2 - mx)
    yd = (e0 * o0_ref[0].astype(F32) + e1 * o1_ref[0].astype(F32) + e2 * o2_ref[0].astype(F32)) / (e0 + e1 + e2)
    merged = merged + gate(3) * jnp.dot(yd.astype(BF16), wd_ref[...], preferred_element_type=F32)

    out_ref[0] = x_ref[0] + jnp.dot(merged.astype(BF16), wo_ref[...], preferred_element_type=F32)


def _merge(x, ya, hf, hb, by, yc, o_d, lse_d, gl, gbias, wa, wb, wc, wd, wo):
    b, s, d = x.shape
    tm = min(TM_MERGE, s)
    tok = lambda width: pl.BlockSpec((1, tm, width), lambda bb, i: (bb, i, 0))
    dw = D_HEADS * HEAD_DIM
    return pl.pallas_call(
        _merge_kernel,
        grid=(b, s // tm),
        in_specs=[tok(d), tok(512), tok(LRU_WIDTH), tok(LRU_WIDTH), tok(LRU_WIDTH), tok(512),
                  tok(dw), tok(dw), tok(dw), tok(dw), tok(dw), tok(dw), tok(N_BRANCH * d),
                  _resident(gbias.shape), _resident(wa.shape), _resident(wb.shape),
                  _resident(wc.shape), _resident(wd.shape), _resident(wo.shape)],
        out_specs=tok(d),
        out_shape=jax.ShapeDtypeStruct((b, s, d), F32),
        compiler_params=_cparams(("parallel", "parallel")),
        name="merge",
    )(x, ya, hf, hb, by, yc, *o_d, *lse_d, gl, gbias, wa, wb, wc, wd, wo)


def _mlp_kernel(x_ref, g_ref, w1_ref, w2_ref, gf_ref, out_ref, *, final_norm):
    x = x_ref[0]
    var = jnp.mean(x * x, axis=-1, keepdims=True)
    hn = (x * lax.rsqrt(var + EPS) * g_ref[...]).astype(BF16)
    hidden = w1_ref.shape[1]
    acc = x
    for c in range(hidden // MLP_CHUNK):
        h = jnp.dot(hn, w1_ref[:, c * MLP_CHUNK:(c + 1) * MLP_CHUNK], preferred_element_type=F32)
        h = jnp.square(jnp.maximum(h, 0.0)).astype(BF16)
        acc = acc + jnp.dot(h, w2_ref[c * MLP_CHUNK:(c + 1) * MLP_CHUNK, :], preferred_element_type=F32)
    if final_norm:
        var = jnp.mean(acc * acc, axis=-1, keepdims=True)
        acc = acc * lax.rsqrt(var + EPS) * gf_ref[...]
    out_ref[0] = acc


def _mlp(x, g, w1, w2, g_final, final_norm):
    b, s, d = x.shape
    tm = min(TM_MLP, s)
    tok = pl.BlockSpec((1, tm, d), lambda bb, i: (bb, i, 0))
    return pl.pallas_call(
        functools.partial(_mlp_kernel, final_norm=final_norm),
        grid=(b, s // tm),
        in_specs=[tok, _resident(g.shape), _resident(w1.shape), _resident(w2.shape), _resident(g_final.shape)],
        out_specs=tok,
        out_shape=jax.ShapeDtypeStruct((b, s, d), F32),
        compiler_params=_cparams(("parallel", "parallel")),
        name="mlp",
    )(x, g, w1, w2, g_final)


def _dense_block_diag(w):
    nb, bw, _ = w.shape
    eye = jnp.eye(nb, dtype=w.dtype)
    return jnp.einsum('ncd,nm->ncmd', w, eye).reshape(nb * bw, nb * bw)


def kernel(x, norm_mix_g, w_in, gate_bias, qk_norm_g, conv_w, conv_b, lru_gate_w, lru_gate_b, lru_lambda, sink_logit, w_proj_a, w_proj_b, w_proj_c, w_proj_d, w_out, norm_mlp_g, w_mlp1, w_mlp2, norm_final_g):
    b, s, d = x.shape
    depth = w_in.shape[0]
    ax_t, seq_t = _rope_tables(s)
    dw = D_HEADS * HEAD_DIM
    n_groups = len(D_PATTERNS)

    for l in range(depth):
        qk_gain = jnp.tile(qk_norm_g[l], (1, LANES // HEAD_DIM))
        (qta, ka, vta, bx, by, qc, kc, vc, qd, kd, vd, gl) = _inproj(
            x, norm_mix_g[l][None, :], w_in[l].astype(BF16), qk_gain, ax_t, seq_t)

        ya = _attn_a(qta, ka, vta)

        gate_w_dense = jnp.stack([jnp.stack([_dense_block_diag(lru_gate_w[l, dd, gg]) for gg in range(2)])
                                  for dd in range(2)]).astype(BF16)
        hf, hb = _lru(bx, conv_w[l], conv_b[l][None, :], gate_w_dense, lru_gate_b[l], lru_lambda[l])

        yc = _band_attention(qc, kc, vc, n_seq=1, col_of=lambda r: 0, nkv=C_KV_HEADS,
                             group=C_HEADS // C_KV_HEADS, half_window=C_HALF_WINDOW, sink=sink_logit[l])

        o_d, lse_d = [], []
        for gi, (window, dil) in enumerate(D_PATTERNS):
            ll = s // dil
            view = lambda t: t.reshape(b, ll, dil * n_groups * dw)
            o, lse = _band_attention(view(qd), view(kd), view(vd), n_seq=dil,
                                     col_of=lambda r, gi=gi: r * n_groups + gi, nkv=D_HEADS, group=1,
                                     half_window=window // (2 * dil), want_lse=True)
            o_d.append(o.reshape(b, s, dw))
            lse_d.append(lse.reshape(b, s, dw))

        x = _merge(x, ya, hf, hb, by, yc, o_d, lse_d, gl, gate_bias[l],
                   w_proj_a[l].astype(BF16), w_proj_b[l].astype(BF16), w_proj_c[l].astype(BF16),
                   w_proj_d[l].astype(BF16), w_out[l].astype(BF16))
        x = _mlp(x, norm_mlp_g[l][None, :], w_mlp1[l].astype(BF16), w_mlp2[l].astype(BF16),
                 norm_final_g[None, :], final_norm=(l == depth - 1))
    return x
```

```python
import functools
import math

import jax
import jax.numpy as jnp
from jax import lax
from jax.experimental import pallas as pl
from jax.experimental.pallas import tpu as pltpu

F32 = jnp.float32
BF16 = jnp.bfloat16

HEAD_DIM = 64
ROPE_THETA = 10000.0
GRID_W = 64
EPS = 1e-6
MASK_VALUE = -1e30
LOG2E = math.log2(math.e)

A_HEADS, A_KV_HEADS = 8, 2
LRU_WIDTH, LRU_BLOCKS, LRU_C = 512, 8, 8.0
CONV_W, CONV_LEFT = 4, 2
C_HEADS, C_KV_HEADS, C_HALF_WINDOW = 8, 2, 128
D_PATTERNS = ((128, 1), (512, 4), (2048, 16))
D_HEADS = 4
N_BRANCH = 4

LANES = 128
SUBLANES = 8
VMEM_LIMIT_BYTES = 56 * 1024 * 1024

TM_INPROJ = 512
TQ_ATTN_A = 256
SUB_ATTN_A = 128
LOOKAHEAD_ATTN_A = 10
CHUNKS_PER_BODY_ATTN_A = 4
TQ_BAND = 256
TT_LRU = 512
TM_MERGE = 512
TM_MLP = 512
MLP_CHUNK = 1024

Q_SCALE = HEAD_DIM ** -0.5 * LOG2E


def _cparams(sem):
    return pltpu.CompilerParams(dimension_semantics=sem, vmem_limit_bytes=VMEM_LIMIT_BYTES)


def _resident(shape):
    nd = len(shape)
    return pl.BlockSpec(shape, lambda *_: (0,) * nd, pipeline_mode=pl.Buffered(1))


def _rope_tables(seq):
    pos = jnp.arange(seq, dtype=F32)
    lane = jnp.arange(LANES)
    d = lane % HEAD_DIM

    inv = ROPE_THETA ** (-jnp.arange(0, HEAD_DIM, 2, dtype=F32) / HEAD_DIM)
    ang = pos[:, None] * inv[None, :]
    f = d % (HEAD_DIM // 2)
    cos_s, sin_s = jnp.cos(ang)[:, f], jnp.sin(ang)[:, f]
    lo = (d < HEAD_DIM // 2)[None, :]
    seq_t = (cos_s, jnp.where(lo, -sin_s, 0.0), jnp.where(lo, 0.0, sin_s))

    quarter = HEAD_DIM // 4
    inv_ax = ROPE_THETA ** (-jnp.arange(0, HEAD_DIM // 2, 2, dtype=F32) / (HEAD_DIM // 2))
    row_pos = (jnp.arange(seq) // GRID_W).astype(F32)
    col_pos = (jnp.arange(seq) % GRID_W).astype(F32)
    e = d % (HEAD_DIM // 2)
    fa = e % quarter
    is_col = (d >= HEAD_DIM // 2)[None, :]
    ang_ax = jnp.where(is_col, col_pos[:, None], row_pos[:, None]) * inv_ax[fa][None, :]
    cos_a, sin_a = jnp.cos(ang_ax), jnp.sin(ang_ax)
    lo_a = (e < quarter)[None, :]
    ax_t = (cos_a, jnp.where(lo_a, -sin_a, 0.0), jnp.where(lo_a, 0.0, sin_a))
    return ax_t, seq_t


def _rotate(x, tables, half):
    cos, sin_a, sin_b = tables
    return x * cos + pltpu.roll(x, LANES - half, 1) * sin_a + pltpu.roll(x, half, 1) * sin_b


def _split_dot(x, m):
    hi = x.astype(BF16)
    lo = (x - hi.astype(F32)).astype(BF16)
    return (jnp.dot(hi, m, preferred_element_type=F32) + jnp.dot(lo, m, preferred_element_type=F32))


def _head_mean_matrix():
    r = lax.broadcasted_iota(jnp.int32, (LANES, LANES), 0) // HEAD_DIM
    c = lax.broadcasted_iota(jnp.int32, (LANES, LANES), 1) // HEAD_DIM
    return jnp.where(r == c, 1.0 / HEAD_DIM, 0.0).astype(BF16)


def _head_rmsnorm(x, gain, mean_mat):
    var = _split_dot(x * x, mean_mat)
    return x * lax.rsqrt(var + EPS) * gain


IN_WIDTHS = (512, 128, 128, 512, 512, 512, 128, 128, 768, 768, 768, 4096)
IN_OFFSETS = tuple(sum(IN_WIDTHS[:i]) for i in range(len(IN_WIDTHS)))
N_IN = sum(IN_WIDTHS)


def _inproj_kernel(x_ref, g_ref, w_ref, qkg_ref,
                   ca_ref, saa_ref, sab_ref, cs_ref, ssa_ref, ssb_ref,
                   qta_ref, ka_ref, vta_ref, bx_ref, by_ref,
                   qc_ref, kc_ref, vc_ref, qd_ref, kd_ref, vd_ref, gl_ref):
    x = x_ref[0]
    var = jnp.mean(x * x, axis=-1, keepdims=True)
    hn = (x * lax.rsqrt(var + EPS) * g_ref[...]).astype(BF16)

    def proj(seg, lo=0, width=None):
        off = IN_OFFSETS[seg] + lo
        width = IN_WIDTHS[seg] if width is None else width
        return jnp.dot(hn, w_ref[:, off:off + width], preferred_element_type=F32)

    ax_t = (ca_ref[...], saa_ref[...], sab_ref[...])
    seq_t = (cs_ref[...], ssa_ref[...], ssb_ref[...])
    mean_mat = _head_mean_matrix()
    gq = qkg_ref[0:1, :]
    gk = qkg_ref[1:2, :]

    aq = proj(0)
    for j in range(IN_WIDTHS[0] // LANES):
        piece = _head_rmsnorm(aq[:, j * LANES:(j + 1) * LANES], gq, mean_mat)
        piece = _rotate(piece, ax_t, HEAD_DIM // 4) * Q_SCALE
        qta_ref[0, j * LANES:(j + 1) * LANES, :] = piece.T.astype(BF16)
    ak = _rotate(_head_rmsnorm(proj(1), gk, mean_mat), ax_t, HEAD_DIM // 4).astype(BF16)
    for g in range(A_KV_HEADS):
        ka_ref[0, g] = ak[:, g * HEAD_DIM:(g + 1) * HEAD_DIM]
    avt = proj(2).T.astype(BF16)
    for g in range(A_KV_HEADS):
        vta_ref[0, g, 0] = avt[g * HEAD_DIM:(g + 1) * HEAD_DIM, :]

    bx_ref[0] = proj(3)
    by_ref[0] = proj(4)

    def rope_store(ref, seg, scale):
        acc = proj(seg)
        for j in range(IN_WIDTHS[seg] // LANES):
            piece = _rotate(acc[:, j * LANES:(j + 1) * LANES], seq_t, HEAD_DIM // 2)
            if scale is not None:
                piece = piece * scale
            ref[0, :, j * LANES:(j + 1) * LANES] = piece.astype(BF16)

    rope_store(qc_ref, 5, Q_SCALE)
    rope_store(kc_ref, 6, None)
    vc_ref[0] = proj(7).astype(BF16)
    rope_store(qd_ref, 8, Q_SCALE)
    rope_store(kd_ref, 9, None)
    vd_ref[0] = proj(10).astype(BF16)

    chunk = 1024
    for c in range(IN_WIDTHS[11] // chunk):
        gl_ref[0, :, c * chunk:(c + 1) * chunk] = proj(11, c * chunk, chunk).astype(BF16)


def _inproj(x, g, w_bf16, qk_gain128, ax_t, seq_t):
    b, s, d = x.shape
    tm = min(TM_INPROJ, s)
    nt = s // tm
    tok = lambda width: pl.BlockSpec((1, tm, width), lambda i, bb: (bb, i, 0))
    tab = pl.BlockSpec((tm, LANES), lambda i, bb: (i, 0))
    out_shape = (
        jax.ShapeDtypeStruct((b, A_HEADS * HEAD_DIM, s), BF16),
        jax.ShapeDtypeStruct((b, A_KV_HEADS, s, HEAD_DIM), BF16),
        jax.ShapeDtypeStruct((b, A_KV_HEADS, nt, HEAD_DIM, tm), BF16),
        jax.ShapeDtypeStruct((b, s, LRU_WIDTH), F32),
        jax.ShapeDtypeStruct((b, s, LRU_WIDTH), F32),
        jax.ShapeDtypeStruct((b, s, 512), BF16),
        jax.ShapeDtypeStruct((b, s, 128), BF16),
        jax.ShapeDtypeStruct((b, s, 128), BF16),
        jax.ShapeDtypeStruct((b, s, 768), BF16),
        jax.ShapeDtypeStruct((b, s, 768), BF16),
        jax.ShapeDtypeStruct((b, s, 768), BF16),
        jax.ShapeDtypeStruct((b, s, 4096), BF16),
    )
    out_specs = (
        pl.BlockSpec((1, A_HEADS * HEAD_DIM, tm), lambda i, bb: (bb, 0, i)),
        pl.BlockSpec((1, A_KV_HEADS, tm, HEAD_DIM), lambda i, bb: (bb, 0, i, 0)),
        pl.BlockSpec((1, A_KV_HEADS, 1, HEAD_DIM, tm), lambda i, bb: (bb, 0, i, 0, 0)),
        tok(LRU_WIDTH), tok(LRU_WIDTH), tok(512), tok(128), tok(128),
        tok(768), tok(768), tok(768), tok(4096),
    )
    return pl.pallas_call(
        _inproj_kernel,
        grid=(nt, b),
        in_specs=[tok(d), _resident((1, d)), _resident((d, N_IN)), _resident((2, LANES)),
                  tab, tab, tab, tab, tab, tab],
        out_specs=out_specs,
        out_shape=out_shape,
        compiler_params=_cparams(("parallel", "parallel")),
        name="inproj",
    )(x, g, w_bf16, qk_gain128, *ax_t, *seq_t)


def _sublane_allmax(x):
    for shift in (4, 2, 1):
        x = jnp.maximum(x, pltpu.roll(x, shift, 0))
    return x


def _attn_a_kernel(q_ref, k_ref, v_ref, o_ref, m_ref, l_ref, acc_ref, *, group, sub):
    n_chunks, chunk = v_ref.shape[2], v_ref.shape[4]
    tq = q_ref.shape[2]
    m_ref[...] = jnp.full(m_ref.shape, MASK_VALUE, F32)
    l_ref[...] = jnp.zeros(l_ref.shape, F32)
    acc_ref[...] = jnp.zeros(acc_ref.shape, F32)

    per_body = min(CHUNKS_PER_BODY_ATTN_A, n_chunks)
    units = [(cc, u, h) for cc in range(per_body) for u in range(chunk // sub) for h in range(group)]

    def body(it, carry):
        c0 = it * per_body

        def scores(idx):
            cc, u, h = units[idx]
            row0 = pl.multiple_of((c0 + cc) * chunk + u * sub, sub)
            k = k_ref[0, 0, pl.ds(row0, sub), :]
            qt = q_ref[0, h * HEAD_DIM:(h + 1) * HEAD_DIM, :]
            return jnp.dot(k, qt, preferred_element_type=F32)

        pending = [scores(i) for i in range(LOOKAHEAD_ATTN_A)]
        for idx, (cc, u, h) in enumerate(units):
            if idx + LOOKAHEAD_ATTN_A < len(units):
                pending.append(scores(idx + LOOKAHEAD_ATTN_A))
            s = pending.pop(0)
            vt = v_ref[0, 0, c0 + cc, :, u * sub:(u + 1) * sub]
            s3 = s.reshape(sub // SUBLANES, SUBLANES, tq)
            m_old = m_ref[h]
            m_new = jnp.maximum(m_old, _sublane_allmax(jnp.max(s3, axis=0)))
            alpha = jnp.exp2(m_old - m_new)
            p3 = jnp.exp2(s3 - m_new[None])
            l_ref[h] = alpha * l_ref[h] + jnp.sum(p3, axis=0)
            pv = jnp.dot(vt, p3.reshape(sub, tq).astype(BF16), preferred_element_type=F32)
            acc_ref[h] = (acc_ref[h] * alpha[None] + pv.reshape(HEAD_DIM // SUBLANES, SUBLANES, tq))
            m_ref[h] = m_new
        return carry

    lax.fori_loop(0, n_chunks // per_body, body, 0)
    outs = []
    for h in range(group):
        l = jnp.sum(l_ref[h], axis=0, keepdims=True)
        outs.append(acc_ref[h].reshape(HEAD_DIM, tq) / l)
    o_ref[0] = jnp.concatenate(outs, axis=0).T.astype(o_ref.dtype)


def _attn_a(qt, k, vt):
    b, hd_all, s = qt.shape
    nkv, nt, chunk = vt.shape[1], vt.shape[2], vt.shape[4]
    group = hd_all // HEAD_DIM // nkv
    tq = min(TQ_ATTN_A, s)
    gw = group * HEAD_DIM
    return pl.pallas_call(
        functools.partial(_attn_a_kernel, group=group, sub=min(SUB_ATTN_A, chunk)),
        grid=(b, nkv, s // tq),
        in_specs=[pl.BlockSpec((1, gw, tq), lambda bb, g, i: (bb, g, i)),
                  pl.BlockSpec((1, 1, s, HEAD_DIM), lambda bb, g, i: (bb, g, 0, 0)),
                  pl.BlockSpec((1, 1, nt, HEAD_DIM, chunk), lambda bb, g, i: (bb, g, 0, 0, 0))],
        out_specs=pl.BlockSpec((1, tq, gw), lambda bb, g, i: (bb, i, g)),
        out_shape=jax.ShapeDtypeStruct((b, s, hd_all), BF16),
        scratch_shapes=[pltpu.VMEM((group, SUBLANES, tq), F32), pltpu.VMEM((group, SUBLANES, tq), F32),
                        pltpu.VMEM((group, HEAD_DIM // SUBLANES, SUBLANES, tq), F32)],
        compiler_params=_cparams(("parallel", "parallel", "parallel")),
        name="attn_a",
    )(qt, k, vt)


def _band_kernel(*refs, nkv, group, half_window, seq_len, has_sink, want_lse):
    q_ref, kp_ref, km_ref, kn_ref, vp_ref, vm_ref, vn_ref = refs[:7]
    rest = list(refs[7:])
    sink_ref = rest.pop(0) if has_sink else None
    o_ref = rest.pop(0)
    lse_ref = rest.pop(0) if want_lse else None

    tq = q_ref.shape[1]
    halo = kp_ref.shape[1]
    nk = tq + 2 * halo
    q0 = pl.program_id(2) * tq
    q = q_ref[0]
    kcat = jnp.concatenate([kp_ref[0], km_ref[0], kn_ref[0]], axis=0)
    vcat = jnp.concatenate([vp_ref[0], vm_ref[0], vn_ref[0]], axis=0)

    row = lax.broadcasted_iota(jnp.int32, (tq, nk), 0)
    col = lax.broadcasted_iota(jnp.int32, (tq, nk), 1)
    kpos = q0 - halo + col
    rel = col - halo - row
    valid = (jnp.abs(rel) <= half_window) & (kpos >= 0) & (kpos < seq_len)

    for g in range(nkv):
        kg = kcat[:, g * HEAD_DIM:(g + 1) * HEAD_DIM]
        vg = vcat[:, g * HEAD_DIM:(g + 1) * HEAD_DIM]
        for hh in range(group):
            h = g * group + hh
            qh = q[:, h * HEAD_DIM:(h + 1) * HEAD_DIM]
            s = lax.dot_general(qh, kg, (((1,), (1,)), ((), ())), preferred_element_type=F32)
            s = jnp.where(valid, s, MASK_VALUE)
            m = jnp.max(s, axis=1, keepdims=True)
            if has_sink:
                sink = sink_ref[h] * LOG2E
                m = jnp.maximum(m, sink)
            p = jnp.exp2(s - m)
            l = jnp.sum(p, axis=1, keepdims=True)
            if has_sink:
                l = l + jnp.exp2(sink - m)
            o = jnp.dot(p.astype(BF16), vg, preferred_element_type=F32) / l
            o_ref[0, :, h * HEAD_DIM:(h + 1) * HEAD_DIM] = o.astype(o_ref.dtype)
            if want_lse:
                lse = m + jnp.log2(l)
                lse_ref[0, :, h * HEAD_DIM:(h + 1) * HEAD_DIM] = jnp.broadcast_to(lse, (tq, HEAD_DIM))


def _band_attention(q, k, v, *, n_seq, col_of, nkv, group, half_window, sink=None, want_lse=False):
    b, seq_len = q.shape[0], q.shape[1]
    wq = nkv * group * HEAD_DIM
    wk = nkv * HEAD_DIM
    halo = half_window
    tq = min(TQ_BAND, seq_len)
    per = tq // halo
    n_halo = seq_len // halo

    main = lambda width: pl.BlockSpec((1, tq, width), lambda bb, r, i: (bb, i, col_of(r)))
    prev = lambda width: pl.BlockSpec(
        (1, halo, width), lambda bb, r, i: (bb, jnp.maximum(i * per - 1, 0), col_of(r)))
    nxt = lambda width: pl.BlockSpec(
        (1, halo, width), lambda bb, r, i: (bb, jnp.minimum((i + 1) * per, n_halo - 1), col_of(r)))
    in_specs = [main(wq), prev(wk), main(wk), nxt(wk), prev(wk), main(wk), nxt(wk)]
    args = [q, k, k, k, v, v, v]
    if sink is not None:
        in_specs.append(pl.BlockSpec(memory_space=pltpu.SMEM))
        args.append(sink)
    out_block = pl.BlockSpec((1, tq, wq), lambda bb, r, i: (bb, i, r))
    out_shape = [jax.ShapeDtypeStruct((b, seq_len, n_seq * wq), BF16)]
    out_specs = [out_block]
    if want_lse:
        out_shape.append(jax.ShapeDtypeStruct((b, seq_len, n_seq * wq), F32))
        out_specs.append(out_block)
    res = pl.pallas_call(
        functools.partial(_band_kernel, nkv=nkv, group=group, half_window=half_window,
                          seq_len=seq_len, has_sink=sink is not None, want_lse=want_lse),
        grid=(b, n_seq, seq_len // tq),
        in_specs=in_specs,
        out_specs=out_specs,
        out_shape=out_shape,
        compiler_params=_cparams(("parallel", "parallel", "parallel")),
        name="band_attn",
    )(*args)
    return res if want_lse else res[0]


def _softplus(x):
    return jnp.maximum(x, 0.0) + jnp.log1p(jnp.exp(-jnp.abs(x)))


def _lru_kernel(xf_ref, xfp_ref, xfn_ref, xb_ref, xbp_ref, xbn_ref,
                cw_ref, cb_ref, gw_ref, gb_ref, lam_ref,
                hf_ref, hb_ref, a_scr, u_scr, carry_ref):
    i = pl.program_id(1)
    nt = pl.num_programs(1)
    tt = xf_ref.shape[1]
    nblk = tt // SUBLANES

    @pl.when(i == 0)
    def _():
        carry_ref[...] = jnp.zeros(carry_ref.shape, F32)

    row = lax.broadcasted_iota(jnp.int32, (SUBLANES, LRU_WIDTH), 0)

    def prepare(d, x_ref, xp_ref, xn_ref, tile):
        prev_rows = jnp.where(tile > 0, xp_ref[0], 0.0)
        next_rows = jnp.where(tile < nt - 1, xn_ref[0], 0.0)
        xp = jnp.concatenate([prev_rows, x_ref[0], next_rows], axis=0)
        xc = cb_ref[...]
        base = SUBLANES - CONV_LEFT
        for jj in range(CONV_W):
            xc = xc + xp[base + jj:base + jj + tt, :] * cw_ref[jj:jj + 1, :]
        xcb = xc.astype(BF16)
        r = jax.nn.sigmoid(jnp.dot(xcb, gw_ref[d, 0], preferred_element_type=F32) + gb_ref[d, 0:1, :])
        gi = jax.nn.sigmoid(jnp.dot(xcb, gw_ref[d, 1], preferred_element_type=F32) + gb_ref[d, 1:2, :])
        log_a = -LRU_C * r * _softplus(-lam_ref[d:d + 1, :])
        a = jnp.exp(log_a)
        a_scr[d] = a
        u_scr[d] = jnp.sqrt(-jnp.tanh(log_a) * (a * a + 1.0)) * (gi * xc)

    prepare(0, xf_ref, xfp_ref, xfn_ref, i)
    prepare(1, xb_ref, xbp_ref, xbn_ref, nt - 1 - i)

    def scan_block(d, blk, carry, out_ref):
        a8 = a_scr[d, pl.ds(blk * SUBLANES, SUBLANES), :]
        u8 = u_scr[d, pl.ds(blk * SUBLANES, SUBLANES), :]
        for step in (1, 2, 4):
            if d == 0:
                shift, ident = step, row < step
            else:
                shift, ident = SUBLANES - step, row >= SUBLANES - step
            u_sh = jnp.where(ident, 0.0, pltpu.roll(u8, shift, 0))
            a_sh = jnp.where(ident, 1.0, pltpu.roll(a8, shift, 0))
            u8 = u8 + a8 * u_sh
            a8 = a8 * a_sh
        h8 = u8 + a8 * carry
        out_ref[0, pl.ds(blk * SUBLANES, SUBLANES), :] = h8
        last = h8[SUBLANES - 1:SUBLANES, :] if d == 0 else h8[0:1, :]
        return jnp.broadcast_to(last, (SUBLANES, LRU_WIDTH))

    def body(kb, carries):
        cf, cb = carries
        cf = scan_block(0, kb, cf, hf_ref)
        cb = scan_block(1, nblk - 1 - kb, cb, hb_ref)
        return cf, cb

    cf, cb = lax.fori_loop(0, nblk, body, (carry_ref[0], carry_ref[1]))
    carry_ref[0] = cf
    carry_ref[1] = cb


def _lru(bx, conv_w, conv_b, gate_w_dense, gate_b, lam):
    b, s, w = bx.shape
    tt = min(TT_LRU, s)
    nt = s // tt
    per = tt // SUBLANES
    n8 = s // SUBLANES
    fwd = lambda i: i
    bwd = lambda i: nt - 1 - i

    def specs(tile_of):
        return [
            pl.BlockSpec((1, tt, w), lambda bb, i: (bb, tile_of(i), 0)),
            pl.BlockSpec((1, SUBLANES, w), lambda bb, i: (bb, jnp.maximum(tile_of(i) * per - 1, 0), 0)),
            pl.BlockSpec((1, SUBLANES, w), lambda bb, i: (bb, jnp.minimum((tile_of(i) + 1) * per, n8 - 1), 0)),
        ]

    return pl.pallas_call(
        _lru_kernel,
        grid=(b, nt),
        in_specs=specs(fwd) + specs(bwd) + [
            _resident(conv_w.shape), _resident(conv_b.shape), _resident(gate_w_dense.shape),
            _resident(gate_b.shape), _resident(lam.shape)],
        out_specs=[pl.BlockSpec((1, tt, w), lambda bb, i: (bb, i, 0)),
                   pl.BlockSpec((1, tt, w), lambda bb, i: (bb, nt - 1 - i, 0))],
        out_shape=[jax.ShapeDtypeStruct((b, s, w), F32), jax.ShapeDtypeStruct((b, s, w), F32)],
        scratch_shapes=[pltpu.VMEM((2, tt, w), F32), pltpu.VMEM((2, tt, w), F32),
                        pltpu.VMEM((2, SUBLANES, w), F32)],
        compiler_params=_cparams(("parallel", "arbitrary")),
        name="lru",
    )(bx, bx, bx, bx, bx, bx, conv_w, conv_b, gate_w_dense, gate_b, lam)


def _gelu_tanh(x):
    return 0.5 * x * (1.0 + jnp.tanh(math.sqrt(2.0 / math.pi) * (x + 0.044715 * (x * x * x))))


def _merge_kernel(x_ref, ya_ref, hf_ref, hb_ref, by_ref, yc_ref,
                  o0_ref, o1_ref, o2_ref, l0_ref, l1_ref, l2_ref, gl_ref, gbias_ref,
                  wa_ref, wb_ref, wc_ref, wd_ref, wo_ref, out_ref):
    d_model = x_ref.shape[2]

    def gate(kk):
        logits = gl_ref[0, :, kk * d_model:(kk + 1) * d_model].astype(F32) + gbias_ref[kk:kk + 1, :]
        return jax.nn.sigmoid(logits)

    merged = gate(0) * jnp.dot(ya_ref[0], wa_ref[...], preferred_element_type=F32)

    yb = (hf_ref[0] + hb_ref[0]) * _gelu_tanh(by_ref[0])
    merged = merged + gate(1) * jnp.dot(yb.astype(BF16), wb_ref[...], preferred_element_type=F32)

    merged = merged + gate(2) * jnp.dot(yc_ref[0], wc_ref[...], preferred_element_type=F32)

    l0, l1, l2 = l0_ref[0], l1_ref[0], l2_ref[0]
    mx = jnp.maximum(jnp.maximum(l0, l1), l2)
    e0, e1, e2 = jnp.exp2(l0 - mx), jnp.exp2(l1 - mx), jnp.exp2(l2 - mx)
    yd = (e0 * o0_ref[0].astype(F32) + e1 * o1_ref[0].astype(F32) + e2 * o2_ref[0].astype(F32)) / (e0 + e1 + e2)
    merged = merged + gate(3) * jnp.dot(yd.astype(BF16), wd_ref[...], preferred_element_type=F32)

    out_ref[0] = x_ref[0] + jnp.dot(merged.astype(BF16), wo_ref[...], preferred_element_type=F32)


def _merge(x, ya, hf, hb, by, yc, o_d, lse_d, gl, gbias, wa, wb, wc, wd, wo):
    b, s, d = x.shape
    tm = min(TM_MERGE, s)
    tok = lambda width: pl.BlockSpec((1, tm, width), lambda bb, i: (bb, i, 0))
    dw = D_HEADS * HEAD_DIM
    return pl.pallas_call(
        _merge_kernel,
        grid=(b, s // tm),
        in_specs=[tok(d), tok(512), tok(LRU_WIDTH), tok(LRU_WIDTH), tok(LRU_WIDTH), tok(512),
                  tok(dw), tok(dw), tok(dw), tok(dw), tok(dw), tok(dw), tok(N_BRANCH * d),
                  _resident(gbias.shape), _resident(wa.shape), _resident(wb.shape),
                  _resident(wc.shape), _resident(wd.shape), _resident(wo.shape)],
        out_specs=tok(d),
        out_shape=jax.ShapeDtypeStruct((b, s, d), F32),
        compiler_params=_cparams(("parallel", "parallel")),
        name="merge",
    )(x, ya, hf, hb, by, yc, *o_d, *lse_d, gl, gbias, wa, wb, wc, wd, wo)


def _mlp_kernel(x_ref, g_ref, w1_ref, w2_ref, gf_ref, out_ref, *, final_norm):
    x = x_ref[0]
    var = jnp.mean(x * x, axis=-1, keepdims=True)
    hn = (x * lax.rsqrt(var + EPS) * g_ref[...]).astype(BF16)
    hidden = w1_ref.shape[1]
    acc = x
    for c in range(hidden // MLP_CHUNK):
        h = jnp.dot(hn, w1_ref[:, c * MLP_CHUNK:(c + 1) * MLP_CHUNK], preferred_element_type=F32)
        h = jnp.square(jnp.maximum(h, 0.0)).astype(BF16)
        acc = acc + jnp.dot(h, w2_ref[c * MLP_CHUNK:(c + 1) * MLP_CHUNK, :], preferred_element_type=F32)
    if final_norm:
        var = jnp.mean(acc * acc, axis=-1, keepdims=True)
        acc = acc * lax.rsqrt(var + EPS) * gf_ref[...]
    out_ref[0] = acc


def _mlp(x, g, w1, w2, g_final, final_norm):
    b, s, d = x.shape
    tm = min(TM_MLP, s)
    tok = pl.BlockSpec((1, tm, d), lambda bb, i: (bb, i, 0))
    return pl.pallas_call(
        functools.partial(_mlp_kernel, final_norm=final_norm),
        grid=(b, s // tm),
        in_specs=[tok, _resident(g.shape), _resident(w1.shape), _resident(w2.shape), _resident(g_final.shape)],
        out_specs=tok,
        out_shape=jax.ShapeDtypeStruct((b, s, d), F32),
        compiler_params=_cparams(("parallel", "parallel")),
        name="mlp",
    )(x, g, w1, w2, g_final)


def _dense_block_diag(w):
    nb, bw, _ = w.shape
    eye = jnp.eye(nb, dtype=w.dtype)
    return jnp.einsum('ncd,nm->ncmd', w, eye).reshape(nb * bw, nb * bw)


def kernel(x, norm_mix_g, w_in, gate_bias, qk_norm_g, conv_w, conv_b, lru_gate_w, lru_gate_b, lru_lambda, sink_logit, w_proj_a, w_proj_b, w_proj_c, w_proj_d, w_out, norm_mlp_g, w_mlp1, w_mlp2, norm_final_g):
    b, s, d = x.shape
    depth = w_in.shape[0]
    ax_t, seq_t = _rope_tables(s)
    dw = D_HEADS * HEAD_DIM
    n_groups = len(D_PATTERNS)

    for l in range(depth):
        qk_gain = jnp.tile(qk_norm_g[l], (1, LANES // HEAD_DIM))
        (qta, ka, vta, bx, by, qc, kc, vc, qd, kd, vd, gl) = _inproj(
            x, norm_mix_g[l][None, :], w_in[l].astype(BF16), qk_gain, ax_t, seq_t)

        ya = _attn_a(qta, ka, vta)

        gate_w_dense = jnp.stack([jnp.stack([_dense_block_diag(lru_gate_w[l, dd, gg]) for gg in range(2)])
                                  for dd in range(2)]).astype(BF16)
        hf, hb = _lru(bx, conv_w[l], conv_b[l][None, :], gate_w_dense, lru_gate_b[l], lru_lambda[l])

        yc = _band_attention(qc, kc, vc, n_seq=1, col_of=lambda r: 0, nkv=C_KV_HEADS,
                             group=C_HEADS // C_KV_HEADS, half_window=C_HALF_WINDOW, sink=sink_logit[l])

        o_d, lse_d = [], []
        for gi, (window, dil) in enumerate(D_PATTERNS):
            ll = s // dil
            view = lambda t: t.reshape(b, ll, dil * n_groups * dw)
            o, lse = _band_attention(view(qd), view(kd), view(vd), n_seq=dil,
                                     col_of=lambda r, gi=gi: r * n_groups + gi, nkv=D_HEADS, group=1,
                                     half_window=window // (2 * dil), want_lse=True)
            o_d.append(o.reshape(b, s, dw))
            lse_d.append(lse.reshape(b, s, dw))

        x = _merge(x, ya, hf, hb, by, yc, o_d, lse_d, gl, gate_bias[l],
                   w_proj_a[l].astype(BF16), w_proj_b[l].astype(BF16), w_proj_c[l].astype(BF16),
                   w_proj_d[l].astype(BF16), w_out[l].astype(BF16))
        x = _mlp(x, norm_mlp_g[l][None, :], w_mlp1[l].astype(BF16), w_mlp2[l].astype(BF16),
                 norm_final_g[None, :], final_norm=(l == depth - 1))
    return x
```

```python
import functools
import math

import jax
import jax.numpy as jnp
from jax import lax
from jax.experimental import pallas as pl
from jax.experimental.pallas import tpu as pltpu

F32 = jnp.float32
BF16 = jnp.bfloat16

HEAD_DIM = 64
ROPE_THETA = 10000.0
GRID_W = 64
EPS = 1e-6
MASK_VALUE = -1e30
LOG2E = math.log2(math.e)

A_HEADS, A_KV_HEADS = 8, 2
LRU_WIDTH, LRU_BLOCKS, LRU_C = 512, 8, 8.0
CONV_W, CONV_LEFT = 4, 2
C_HEADS, C_KV_HEADS, C_HALF_WINDOW = 8, 2, 128
D_PATTERNS = ((128, 1), (512, 4), (2048, 16))
D_HEADS = 4
N_BRANCH = 4

LANES = 128
SUBLANES = 8
VMEM_LIMIT_BYTES = 56 * 1024 * 1024

TM_INPROJ = 512
TQ_ATTN_A = 256
SUB_ATTN_A = 128
LOOKAHEAD_ATTN_A = 10
LOOKAHEAD_BAND = 2
CHUNKS_PER_BODY_ATTN_A = 4
TQ_BAND = 256
TT_LRU = 512
TM_MERGE = 512
TM_MLP = 512
MLP_CHUNK = 1024

Q_SCALE = HEAD_DIM ** -0.5 * LOG2E


def _cparams(sem):
    return pltpu.CompilerParams(dimension_semantics=sem, vmem_limit_bytes=VMEM_LIMIT_BYTES)


def _resident(shape):
    nd = len(shape)
    return pl.BlockSpec(shape, lambda *_: (0,) * nd, pipeline_mode=pl.Buffered(1))


def _rope_tables(seq):
    pos = jnp.arange(seq, dtype=F32)
    lane = jnp.arange(LANES)
    d = lane % HEAD_DIM

    inv = ROPE_THETA ** (-jnp.arange(0, HEAD_DIM, 2, dtype=F32) / HEAD_DIM)
    ang = pos[:, None] * inv[None, :]
    f = d % (HEAD_DIM // 2)
    cos_s, sin_s = jnp.cos(ang)[:, f], jnp.sin(ang)[:, f]
    lo = (d < HEAD_DIM // 2)[None, :]
    seq_t = (cos_s, jnp.where(lo, -sin_s, 0.0), jnp.where(lo, 0.0, sin_s))

    quarter = HEAD_DIM // 4
    inv_ax = ROPE_THETA ** (-jnp.arange(0, HEAD_DIM // 2, 2, dtype=F32) / (HEAD_DIM // 2))
    row_pos = (jnp.arange(seq) // GRID_W).astype(F32)
    col_pos = (jnp.arange(seq) % GRID_W).astype(F32)
    e = d % (HEAD_DIM // 2)
    fa = e % quarter
    is_col = (d >= HEAD_DIM // 2)[None, :]
    ang_ax = jnp.where(is_col, col_pos[:, None], row_pos[:, None]) * inv_ax[fa][None, :]
    cos_a, sin_a = jnp.cos(ang_ax), jnp.sin(ang_ax)
    lo_a = (e < quarter)[None, :]
    ax_t = (cos_a, jnp.where(lo_a, -sin_a, 0.0), jnp.where(lo_a, 0.0, sin_a))
    return ax_t, seq_t


def _rotate(x, tables, half):
    cos, sin_a, sin_b = tables
    return x * cos + pltpu.roll(x, LANES - half, 1) * sin_a + pltpu.roll(x, half, 1) * sin_b


def _split_dot(x, m):
    hi = x.astype(BF16)
    lo = (x - hi.astype(F32)).astype(BF16)
    return (jnp.dot(hi, m, preferred_element_type=F32) + jnp.dot(lo, m, preferred_element_type=F32))


def _head_mean_matrix():
    r = lax.broadcasted_iota(jnp.int32, (LANES, LANES), 0) // HEAD_DIM
    c = lax.broadcasted_iota(jnp.int32, (LANES, LANES), 1) // HEAD_DIM
    return jnp.where(r == c, 1.0 / HEAD_DIM, 0.0).astype(BF16)


def _head_rmsnorm(x, gain, mean_mat):
    var = _split_dot(x * x, mean_mat)
    return x * lax.rsqrt(var + EPS) * gain


IN_WIDTHS = (512, 128, 128, 512, 512, 512, 128, 128, 768, 768, 768, 4096)
IN_OFFSETS = tuple(sum(IN_WIDTHS[:i]) for i in range(len(IN_WIDTHS)))
N_IN = sum(IN_WIDTHS)


def _inproj_kernel(x_ref, g_ref, w_ref, qkg_ref,
                   ca_ref, saa_ref, sab_ref, cs_ref, ssa_ref, ssb_ref,
                   qta_ref, ka_ref, vta_ref, bx_ref, by_ref,
                   qc_ref, kc_ref, vc_ref, *rest):
    d_refs, gl_ref, perm_scr = rest[:9], rest[9], rest[10]
    x = x_ref[0]
    var = jnp.mean(x * x, axis=-1, keepdims=True)
    hn = (x * lax.rsqrt(var + EPS) * g_ref[...]).astype(BF16)

    def proj(seg, lo=0, width=None):
        off = IN_OFFSETS[seg] + lo
        width = IN_WIDTHS[seg] if width is None else width
        return jnp.dot(hn, w_ref[:, off:off + width], preferred_element_type=F32)

    ax_t = (ca_ref[...], saa_ref[...], sab_ref[...])
    seq_t = (cs_ref[...], ssa_ref[...], ssb_ref[...])
    mean_mat = _head_mean_matrix()
    gq = qkg_ref[0:1, :]
    gk = qkg_ref[1:2, :]

    aq = proj(0)
    for j in range(IN_WIDTHS[0] // LANES):
        piece = _head_rmsnorm(aq[:, j * LANES:(j + 1) * LANES], gq, mean_mat)
        piece = _rotate(piece, ax_t, HEAD_DIM // 4) * Q_SCALE
        qta_ref[0, j * LANES:(j + 1) * LANES, :] = piece.T.astype(BF16)
    ak = _rotate(_head_rmsnorm(proj(1), gk, mean_mat), ax_t, HEAD_DIM // 4).astype(BF16)
    for g in range(A_KV_HEADS):
        ka_ref[0, g] = ak[:, g * HEAD_DIM:(g + 1) * HEAD_DIM]
    avt = proj(2).T.astype(BF16)
    for g in range(A_KV_HEADS):
        vta_ref[0, g, 0] = avt[g * HEAD_DIM:(g + 1) * HEAD_DIM, :]

    bx_ref[0] = proj(3)
    by_ref[0] = proj(4)

    def rope_store(ref, seg, scale):
        acc = proj(seg)
        for j in range(IN_WIDTHS[seg] // LANES):
            piece = _rotate(acc[:, j * LANES:(j + 1) * LANES], seq_t, HEAD_DIM // 2)
            if scale is not None:
                piece = piece * scale
            ref[0, :, j * LANES:(j + 1) * LANES] = piece.astype(BF16)

    rope_store(qc_ref, 5, Q_SCALE)
    rope_store(kc_ref, 6, None)
    vc_ref[0] = proj(7).astype(BF16)

    tm = x.shape[0]
    dw = D_HEADS * HEAD_DIM
    for gi, (_, dil) in enumerate(D_PATTERNS):
        for ti, (seg, scale, rope) in enumerate(((8, Q_SCALE, True), (9, None, True), (10, None, False))):
            ref = d_refs[ti * len(D_PATTERNS) + gi]
            acc = proj(seg, gi * dw, dw)
            for j in range(dw // LANES):
                piece = acc[:, j * LANES:(j + 1) * LANES]
                if rope:
                    piece = _rotate(piece, seq_t, HEAD_DIM // 2)
                if scale is not None:
                    piece = piece * scale
                if dil == 1:
                    ref[0, 0, :, j * LANES:(j + 1) * LANES] = piece.astype(BF16)
                else:
                    perm_scr[j] = piece
            if dil > 1:
                for r in range(dil):
                    rows = [perm_scr[j, pl.ds(r, tm // dil, stride=dil), :] for j in range(dw // LANES)]
                    ref[0, r] = jnp.concatenate(rows, axis=1).astype(BF16)

    chunk = 1024
    for c in range(IN_WIDTHS[11] // chunk):
        gl_ref[0, :, c * chunk:(c + 1) * chunk] = proj(11, c * chunk, chunk).astype(BF16)


def _inproj(x, g, w_bf16, qk_gain128, ax_t, seq_t):
    b, s, d = x.shape
    tm = min(TM_INPROJ, s)
    nt = s // tm
    tok = lambda width: pl.BlockSpec((1, tm, width), lambda i, bb: (bb, i, 0))
    tab = pl.BlockSpec((tm, LANES), lambda i, bb: (i, 0))
    dw = D_HEADS * HEAD_DIM
    d_shapes = tuple(jax.ShapeDtypeStruct((b, dil, s // dil, dw), BF16) for _, dil in D_PATTERNS)
    d_specs = tuple(pl.BlockSpec((1, dil, tm // dil, dw), lambda i, bb: (bb, 0, i, 0)) for _, dil in D_PATTERNS)
    out_shape = (
        jax.ShapeDtypeStruct((b, A_HEADS * HEAD_DIM, s), BF16),
        jax.ShapeDtypeStruct((b, A_KV_HEADS, s, HEAD_DIM), BF16),
        jax.ShapeDtypeStruct((b, A_KV_HEADS, nt, HEAD_DIM, tm), BF16),
        jax.ShapeDtypeStruct((b, s, LRU_WIDTH), F32),
        jax.ShapeDtypeStruct((b, s, LRU_WIDTH), F32),
        jax.ShapeDtypeStruct((b, s, 512), BF16),
        jax.ShapeDtypeStruct((b, s, 128), BF16),
        jax.ShapeDtypeStruct((b, s, 128), BF16),
    ) + d_shapes * 3 + (jax.ShapeDtypeStruct((b, s, 4096), BF16),)
    out_specs = (
        pl.BlockSpec((1, A_HEADS * HEAD_DIM, tm), lambda i, bb: (bb, 0, i)),
        pl.BlockSpec((1, A_KV_HEADS, tm, HEAD_DIM), lambda i, bb: (bb, 0, i, 0)),
        pl.BlockSpec((1, A_KV_HEADS, 1, HEAD_DIM, tm), lambda i, bb: (bb, 0, i, 0, 0)),
        tok(LRU_WIDTH), tok(LRU_WIDTH), tok(512), tok(128), tok(128),
    ) + d_specs * 3 + (tok(4096),)
    return pl.pallas_call(
        _inproj_kernel,
        grid=(nt, b),
        in_specs=[tok(d), _resident((1, d)), _resident((d, N_IN)), _resident((2, LANES)),
                  tab, tab, tab, tab, tab, tab],
        out_specs=out_specs,
        out_shape=out_shape,
        scratch_shapes=[pltpu.VMEM((dw // LANES, tm, LANES), F32)],
        compiler_params=_cparams(("parallel", "parallel")),
        name="inproj",
    )(x, g, w_bf16, qk_gain128, *ax_t, *seq_t)


def _sublane_allmax(x):
    for shift in (4, 2, 1):
        x = jnp.maximum(x, pltpu.roll(x, shift, 0))
    return x


def _attn_a_kernel(q_ref, k_ref, v_ref, o_ref, m_ref, l_ref, acc_ref, *, group, sub):
    n_chunks, chunk = v_ref.shape[2], v_ref.shape[4]
    tq = q_ref.shape[2]
    m_ref[...] = jnp.full(m_ref.shape, MASK_VALUE, F32)
    l_ref[...] = jnp.zeros(l_ref.shape, F32)
    acc_ref[...] = jnp.zeros(acc_ref.shape, F32)

    per_body = min(CHUNKS_PER_BODY_ATTN_A, n_chunks)
    units = [(cc, u, h) for cc in range(per_body) for u in range(chunk // sub) for h in range(group)]

    def body(it, carry):
        c0 = it * per_body

        def scores(idx):
            cc, u, h = units[idx]
            row0 = pl.multiple_of((c0 + cc) * chunk + u * sub, sub)
            k = k_ref[0, 0, pl.ds(row0, sub), :]
            qt = q_ref[0, h * HEAD_DIM:(h + 1) * HEAD_DIM, :]
            return jnp.dot(k, qt, preferred_element_type=F32)

        pending = [scores(i) for i in range(LOOKAHEAD_ATTN_A)]
        for idx, (cc, u, h) in enumerate(units):
            if idx + LOOKAHEAD_ATTN_A < len(units):
                pending.append(scores(idx + LOOKAHEAD_ATTN_A))
            s = pending.pop(0)
            vt = v_ref[0, 0, c0 + cc, :, u * sub:(u + 1) * sub]
            s3 = s.reshape(sub // SUBLANES, SUBLANES, tq)
            m_old = m_ref[h]
            m_new = jnp.maximum(m_old, _sublane_allmax(jnp.max(s3, axis=0)))
            alpha = jnp.exp2(m_old - m_new)
            p3 = jnp.exp2(s3 - m_new[None])
            l_ref[h] = alpha * l_ref[h] + jnp.sum(p3, axis=0)
            pv = jnp.dot(vt, p3.reshape(sub, tq).astype(BF16), preferred_element_type=F32)
            acc_ref[h] = (acc_ref[h] * alpha[None] + pv.reshape(HEAD_DIM // SUBLANES, SUBLANES, tq))
            m_ref[h] = m_new
        return carry

    lax.fori_loop(0, n_chunks // per_body, body, 0)
    outs = []
    for h in range(group):
        l = jnp.sum(l_ref[h], axis=0, keepdims=True)
        outs.append(acc_ref[h].reshape(HEAD_DIM, tq) / l)
    o_ref[0] = jnp.concatenate(outs, axis=0).T.astype(o_ref.dtype)


def _attn_a(qt, k, vt):
    b, hd_all, s = qt.shape
    nkv, nt, chunk = vt.shape[1], vt.shape[2], vt.shape[4]
    group = hd_all // HEAD_DIM // nkv
    tq = min(TQ_ATTN_A, s)
    gw = group * HEAD_DIM
    return pl.pallas_call(
        functools.partial(_attn_a_kernel, group=group, sub=min(SUB_ATTN_A, chunk)),
        grid=(b, nkv, s // tq),
        in_specs=[pl.BlockSpec((1, gw, tq), lambda bb, g, i: (bb, g, i)),
                  pl.BlockSpec((1, 1, s, HEAD_DIM), lambda bb, g, i: (bb, g, 0, 0)),
                  pl.BlockSpec((1, 1, nt, HEAD_DIM, chunk), lambda bb, g, i: (bb, g, 0, 0, 0))],
        out_specs=pl.BlockSpec((1, tq, gw), lambda bb, g, i: (bb, i, g)),
        out_shape=jax.ShapeDtypeStruct((b, s, hd_all), BF16),
        scratch_shapes=[pltpu.VMEM((group, SUBLANES, tq), F32), pltpu.VMEM((group, SUBLANES, tq), F32),
                        pltpu.VMEM((group, HEAD_DIM // SUBLANES, SUBLANES, tq), F32)],
        compiler_params=_cparams(("parallel", "parallel", "parallel")),
        name="attn_a",
    )(qt, k, vt)


def _band_kernel(*refs, nkv, group, half_window, seq_len, has_sink, want_lse):
    q_ref, kp_ref, km_ref, kn_ref, vp_ref, vm_ref, vn_ref = refs[:7]
    rest = list(refs[7:])
    sink_ref = rest.pop(0) if has_sink else None
    o_ref = rest.pop(0)
    lse_ref = rest.pop(0) if want_lse else None

    tq = q_ref.shape[2]
    halo = kp_ref.shape[2]
    nk = tq + 2 * halo
    q0 = pl.program_id(2) * tq
    q = q_ref[0, 0]
    kcat = jnp.concatenate([kp_ref[0, 0], km_ref[0, 0], kn_ref[0, 0]], axis=0)
    vcat = jnp.concatenate([vp_ref[0, 0], vm_ref[0, 0], vn_ref[0, 0]], axis=0)

    row = lax.broadcasted_iota(jnp.int32, (tq, nk), 0)
    col = lax.broadcasted_iota(jnp.int32, (tq, nk), 1)
    kpos = q0 - halo + col
    rel = col - halo - row
    valid = (jnp.abs(rel) <= half_window) & (kpos >= 0) & (kpos < seq_len)

    heads = [(g, g * group + hh) for g in range(nkv) for hh in range(group)]

    def scores(idx):
        g, h = heads[idx]
        kg = kcat[:, g * HEAD_DIM:(g + 1) * HEAD_DIM]
        qh = q[:, h * HEAD_DIM:(h + 1) * HEAD_DIM]
        return lax.dot_general(qh, kg, (((1,), (1,)), ((), ())), preferred_element_type=F32)

    pending = [scores(i) for i in range(min(LOOKAHEAD_BAND, len(heads)))]
    for idx, (g, h) in enumerate(heads):
        if idx + LOOKAHEAD_BAND < len(heads):
            pending.append(scores(idx + LOOKAHEAD_BAND))
        s = jnp.where(valid, pending.pop(0), MASK_VALUE)
        vg = vcat[:, g * HEAD_DIM:(g + 1) * HEAD_DIM]
        m = jnp.max(s, axis=1, keepdims=True)
        if has_sink:
            sink = sink_ref[h] * LOG2E
            m = jnp.maximum(m, sink)
        p = jnp.exp2(s - m)
        l = jnp.sum(p, axis=1, keepdims=True)
        if has_sink:
            l = l + jnp.exp2(sink - m)
        o = jnp.dot(p.astype(BF16), vg, preferred_element_type=F32) / l
        o_ref[0, 0, :, h * HEAD_DIM:(h + 1) * HEAD_DIM] = o.astype(o_ref.dtype)
        if want_lse:
            lse = m + jnp.log2(l)
            lse_ref[0, 0, :, h * HEAD_DIM:(h + 1) * HEAD_DIM] = jnp.broadcast_to(lse, (tq, HEAD_DIM))


def _band_attention(q, k, v, *, nkv, group, half_window, sink=None, want_lse=False):
    b, n_seq, seq_len, wq = q.shape
    wk = nkv * HEAD_DIM
    halo = half_window
    tq = min(TQ_BAND, seq_len)
    per = tq // halo
    n_halo = seq_len // halo

    main = lambda width: pl.BlockSpec((1, 1, tq, width), lambda bb, r, i: (bb, r, i, 0))
    prev = lambda width: pl.BlockSpec(
        (1, 1, halo, width), lambda bb, r, i: (bb, r, jnp.maximum(i * per - 1, 0), 0))
    nxt = lambda width: pl.BlockSpec(
        (1, 1, halo, width), lambda bb, r, i: (bb, r, jnp.minimum((i + 1) * per, n_halo - 1), 0))
    in_specs = [main(wq), prev(wk), main(wk), nxt(wk), prev(wk), main(wk), nxt(wk)]
    args = [q, k, k, k, v, v, v]
    if sink is not None:
        in_specs.append(pl.BlockSpec(memory_space=pltpu.SMEM))
        args.append(sink)
    out_block = main(wq)
    out_shape = [jax.ShapeDtypeStruct(q.shape, BF16)]
    out_specs = [out_block]
    if want_lse:
        out_shape.append(jax.ShapeDtypeStruct(q.shape, F32))
        out_specs.append(out_block)
    res = pl.pallas_call(
        functools.partial(_band_kernel, nkv=nkv, group=group, half_window=half_window,
                          seq_len=seq_len, has_sink=sink is not None, want_lse=want_lse),
        grid=(b, n_seq, seq_len // tq),
        in_specs=in_specs,
        out_specs=out_specs,
        out_shape=out_shape,
        compiler_params=_cparams(("parallel", "parallel", "parallel")),
        name="band_attn",
    )(*args)
    return res if want_lse else res[0]


def _softplus(x):
    return jnp.maximum(x, 0.0) + jnp.log1p(jnp.exp(-jnp.abs(x)))


def _lru_kernel(xf_ref, xfp_ref, xfn_ref, xb_ref, xbp_ref, xbn_ref,
                cw_ref, cb_ref, gw_ref, gb_ref, lam_ref,
                hf_ref, hb_ref, a_scr, u_scr, carry_ref):
    i = pl.program_id(1)
    nt = pl.num_programs(1)
    tt = xf_ref.shape[1]
    nblk = tt // SUBLANES

    @pl.when(i == 0)
    def _():
        carry_ref[...] = jnp.zeros(carry_ref.shape, F32)

    row = lax.broadcasted_iota(jnp.int32, (SUBLANES, LRU_WIDTH), 0)

    def prepare(d, x_ref, xp_ref, xn_ref, tile):
        prev_rows = jnp.where(tile > 0, xp_ref[0], 0.0)
        next_rows = jnp.where(tile < nt - 1, xn_ref[0], 0.0)
        xp = jnp.concatenate([prev_rows, x_ref[0], next_rows], axis=0)
        xc = cb_ref[...]
        base = SUBLANES - CONV_LEFT
        for jj in range(CONV_W):
            xc = xc + xp[base + jj:base + jj + tt, :] * cw_ref[jj:jj + 1, :]
        xcb = xc.astype(BF16)
        r = jax.nn.sigmoid(jnp.dot(xcb, gw_ref[d, 0], preferred_element_type=F32) + gb_ref[d, 0:1, :])
        gi = jax.nn.sigmoid(jnp.dot(xcb, gw_ref[d, 1], preferred_element_type=F32) + gb_ref[d, 1:2, :])
        log_a = -LRU_C * r * _softplus(-lam_ref[d:d + 1, :])
        a = jnp.exp(log_a)
        a_scr[d] = a
        u_scr[d] = jnp.sqrt(-jnp.tanh(log_a) * (a * a + 1.0)) * (gi * xc)

    prepare(0, xf_ref, xfp_ref, xfn_ref, i)
    prepare(1, xb_ref, xbp_ref, xbn_ref, nt - 1 - i)

    def scan_block(d, blk, carry, out_ref):
        a8 = a_scr[d, pl.ds(blk * SUBLANES, SUBLANES), :]
        u8 = u_scr[d, pl.ds(blk * SUBLANES, SUBLANES), :]
        for step in (1, 2, 4):
            if d == 0:
                shift, ident = step, row < step
            else:
                shift, ident = SUBLANES - step, row >= SUBLANES - step
            u_sh = jnp.where(ident, 0.0, pltpu.roll(u8, shift, 0))
            a_sh = jnp.where(ident, 1.0, pltpu.roll(a8, shift, 0))
            u8 = u8 + a8 * u_sh
            a8 = a8 * a_sh
        h8 = u8 + a8 * carry
        out_ref[0, pl.ds(blk * SUBLANES, SUBLANES), :] = h8
        last = h8[SUBLANES - 1:SUBLANES, :] if d == 0 else h8[0:1, :]
        return jnp.broadcast_to(last, (SUBLANES, LRU_WIDTH))

    def body(kb, carries):
        cf, cb = carries
        cf = scan_block(0, kb, cf, hf_ref)
        cb = scan_block(1, nblk - 1 - kb, cb, hb_ref)
        return cf, cb

    cf, cb = lax.fori_loop(0, nblk, body, (carry_ref[0], carry_ref[1]))
    carry_ref[0] = cf
    carry_ref[1] = cb


def _lru(bx, conv_w, conv_b, gate_w_dense, gate_b, lam):
    b, s, w = bx.shape
    tt = min(TT_LRU, s)
    nt = s // tt
    per = tt // SUBLANES
    n8 = s // SUBLANES
    fwd = lambda i: i
    bwd = lambda i: nt - 1 - i

    def specs(tile_of):
        return [
            pl.BlockSpec((1, tt, w), lambda bb, i: (bb, tile_of(i), 0)),
            pl.BlockSpec((1, SUBLANES, w), lambda bb, i: (bb, jnp.maximum(tile_of(i) * per - 1, 0), 0)),
            pl.BlockSpec((1, SUBLANES, w), lambda bb, i: (bb, jnp.minimum((tile_of(i) + 1) * per, n8 - 1), 0)),
        ]

    return pl.pallas_call(
        _lru_kernel,
        grid=(b, nt),
        in_specs=specs(fwd) + specs(bwd) + [
            _resident(conv_w.shape), _resident(conv_b.shape), _resident(gate_w_dense.shape),
            _resident(gate_b.shape), _resident(lam.shape)],
        out_specs=[pl.BlockSpec((1, tt, w), lambda bb, i: (bb, i, 0)),
                   pl.BlockSpec((1, tt, w), lambda bb, i: (bb, nt - 1 - i, 0))],
        out_shape=[jax.ShapeDtypeStruct((b, s, w), F32), jax.ShapeDtypeStruct((b, s, w), F32)],
        scratch_shapes=[pltpu.VMEM((2, tt, w), F32), pltpu.VMEM((2, tt, w), F32),
                        pltpu.VMEM((2, SUBLANES, w), F32)],
        compiler_params=_cparams(("parallel", "arbitrary")),
        name="lru",
    )(bx, bx, bx, bx, bx, bx, conv_w, conv_b, gate_w_dense, gate_b, lam)


def _gelu_tanh(x):
    return 0.5 * x * (1.0 + jnp.tanh(math.sqrt(2.0 / math.pi) * (x + 0.044715 * (x * x * x))))


def _merge_kernel(x_ref, ya_ref, hf_ref, hb_ref, by_ref, yc_ref,
                  o0_ref, o1_ref, o2_ref, l0_ref, l1_ref, l2_ref, gl_ref, gbias_ref,
                  wa_ref, wb_ref, wc_ref, wd_ref, wo_ref, out_ref, perm_scr):
    d_model = x_ref.shape[2]
    tm = x_ref.shape[1]

    def natural(ref, slot):
        dil = ref.shape[1]
        if dil == 1:
            return ref[0, 0].astype(F32)
        for r in range(dil):
            rows = ref[0, r].astype(F32)
            for j in range(rows.shape[1] // LANES):
                perm_scr[slot, j, pl.ds(r, tm // dil, stride=dil), :] = rows[:, j * LANES:(j + 1) * LANES]
        return jnp.concatenate([perm_scr[slot, j] for j in range(perm_scr.shape[1])], axis=1)

    def gate(kk):
        logits = gl_ref[0, :, kk * d_model:(kk + 1) * d_model].astype(F32) + gbias_ref[kk:kk + 1, :]
        return jax.nn.sigmoid(logits)

    merged = gate(0) * jnp.dot(ya_ref[0], wa_ref[...], preferred_element_type=F32)

    yb = (hf_ref[0] + hb_ref[0]) * _gelu_tanh(by_ref[0])
    merged = merged + gate(1) * jnp.dot(yb.astype(BF16), wb_ref[...], preferred_element_type=F32)

    merged = merged + gate(2) * jnp.dot(yc_ref[0], wc_ref[...], preferred_element_type=F32)

    l0, l1, l2 = natural(l0_ref, 0), natural(l1_ref, 0), natural(l2_ref, 1)
    o0, o1, o2 = natural(o0_ref, 2), natural(o1_ref, 2), natural(o2_ref, 3)
    mx = jnp.maximum(jnp.maximum(l0, l1), l2)
    e0, e1, e2 = jnp.exp2(l0 - mx), jnp.exp2(l1 - mx), jnp.exp2(l2 - mx)
    yd = (e0 * o0 + e1 * o1 + e2 * o2) / (e0 + e1 + e2)
    merged = merged + gate(3) * jnp.dot(yd.astype(BF16), wd_ref[...], preferred_element_type=F32)

    out_ref[0] = x_ref[0] + jnp.dot(merged.astype(BF16), wo_ref[...], preferred_element_type=F32)


def _merge(x, ya, hf, hb, by, yc, o_d, lse_d, gl, gbias, wa, wb, wc, wd, wo):
    b, s, d = x.shape
    tm = min(TM_MERGE, s)
    tok = lambda width: pl.BlockSpec((1, tm, width), lambda bb, i: (bb, i, 0))
    dw = D_HEADS * HEAD_DIM
    res = [pl.BlockSpec((1, dil, tm // dil, dw), lambda bb, i: (bb, 0, i, 0)) for _, dil in D_PATTERNS]
    return pl.pallas_call(
        _merge_kernel,
        grid=(b, s // tm),
        in_specs=[tok(d), tok(512), tok(LRU_WIDTH), tok(LRU_WIDTH), tok(LRU_WIDTH), tok(512),
                  *res, *res, tok(N_BRANCH * d),
                  _resident(gbias.shape), _resident(wa.shape), _resident(wb.shape),
                  _resident(wc.shape), _resident(wd.shape), _resident(wo.shape)],
        out_specs=tok(d),
        out_shape=jax.ShapeDtypeStruct((b, s, d), F32),
        scratch_shapes=[pltpu.VMEM((4, dw // LANES, tm, LANES), F32)],
        compiler_params=_cparams(("parallel", "parallel")),
        name="merge",
    )(x, ya, hf, hb, by, yc, *o_d, *lse_d, gl, gbias, wa, wb, wc, wd, wo)


def _mlp_kernel(x_ref, g_ref, w1_ref, w2_ref, gf_ref, out_ref, *, final_norm):
    x = x_ref[0]
    var = jnp.mean(x * x, axis=-1, keepdims=True)
    hn = (x * lax.rsqrt(var + EPS) * g_ref[...]).astype(BF16)
    hidden = w1_ref.shape[1]
    acc = x
    for c in range(hidden // MLP_CHUNK):
        h = jnp.dot(hn, w1_ref[:, c * MLP_CHUNK:(c + 1) * MLP_CHUNK], preferred_element_type=F32)
        h = jnp.square(jnp.maximum(h, 0.0)).astype(BF16)
        acc = acc + jnp.dot(h, w2_ref[c * MLP_CHUNK:(c + 1) * MLP_CHUNK, :], preferred_element_type=F32)
    if final_norm:
        var = jnp.mean(acc * acc, axis=-1, keepdims=True)
        acc = acc * lax.rsqrt(var + EPS) * gf_ref[...]
    out_ref[0] = acc


def _mlp(x, g, w1, w2, g_final, final_norm):
    b, s, d = x.shape
    tm = min(TM_MLP, s)
    tok = pl.BlockSpec((1, tm, d), lambda bb, i: (bb, i, 0))
    return pl.pallas_call(
        functools.partial(_mlp_kernel, final_norm=final_norm),
        grid=(b, s // tm),
        in_specs=[tok, _resident(g.shape), _resident(w1.shape), _resident(w2.shape), _resident(g_final.shape)],
        out_specs=tok,
        out_shape=jax.ShapeDtypeStruct((b, s, d), F32),
        compiler_params=_cparams(("parallel", "parallel")),
        name="mlp",
    )(x, g, w1, w2, g_final)


def _dense_block_diag(w):
    nb, bw, _ = w.shape
    eye = jnp.eye(nb, dtype=w.dtype)
    return jnp.einsum('ncd,nm->ncmd', w, eye).reshape(nb * bw, nb * bw)


def kernel(x, norm_mix_g, w_in, gate_bias, qk_norm_g, conv_w, conv_b, lru_gate_w, lru_gate_b, lru_lambda, sink_logit, w_proj_a, w_proj_b, w_proj_c, w_proj_d, w_out, norm_mlp_g, w_mlp1, w_mlp2, norm_final_g):
    b, s, d = x.shape
    depth = w_in.shape[0]
    ax_t, seq_t = _rope_tables(s)
    dw = D_HEADS * HEAD_DIM
    n_groups = len(D_PATTERNS)

    for l in range(depth):
        qk_gain = jnp.tile(qk_norm_g[l], (1, LANES // HEAD_DIM))
        outs = _inproj(x, norm_mix_g[l][None, :], w_in[l].astype(BF16), qk_gain, ax_t, seq_t)
        qta, ka, vta, bx, by, qc, kc, vc = outs[:8]
        qd, kd, vd = outs[8:8 + n_groups], outs[8 + n_groups:8 + 2 * n_groups], outs[8 + 2 * n_groups:8 + 3 * n_groups]
        gl = outs[8 + 3 * n_groups]

        ya = _attn_a(qta, ka, vta)

        gate_w_dense = jnp.stack([jnp.stack([_dense_block_diag(lru_gate_w[l, dd, gg]) for gg in range(2)])
                                  for dd in range(2)]).astype(BF16)
        hf, hb = _lru(bx, conv_w[l], conv_b[l][None, :], gate_w_dense, lru_gate_b[l], lru_lambda[l])

        yc = _band_attention(qc[:, None], kc[:, None], vc[:, None], nkv=C_KV_HEADS,
                             group=C_HEADS // C_KV_HEADS, half_window=C_HALF_WINDOW, sink=sink_logit[l])[:, 0]

        o_d, lse_d = [], []
        for gi, (window, dil) in enumerate(D_PATTERNS):
            o, lse = _band_attention(qd[gi], kd[gi], vd[gi], nkv=D_HEADS, group=1,
                                     half_window=window // (2 * dil), want_lse=True)
            o_d.append(o)
            lse_d.append(lse)

        x = _merge(x, ya, hf, hb, by, yc, o_d, lse_d, gl, gate_bias[l],
                   w_proj_a[l].astype(BF16), w_proj_b[l].astype(BF16), w_proj_c[l].astype(BF16),
                   w_proj_d[l].astype(BF16), w_out[l].astype(BF16))
        x = _mlp(x, norm_mlp_g[l][None, :], w_mlp1[l].astype(BF16), w_mlp2[l].astype(BF16),
                 norm_final_g[None, :], final_norm=(l == depth - 1))
    return x
```

```python
import functools
import math

import jax
import jax.numpy as jnp
from jax import lax
from jax.experimental import pallas as pl
from jax.experimental.pallas import tpu as pltpu

F32 = jnp.float32
BF16 = jnp.bfloat16

HEAD_DIM = 64
ROPE_THETA = 10000.0
GRID_W = 64
EPS = 1e-6
MASK_VALUE = -1e30
LOG2E = math.log2(math.e)

A_HEADS, A_KV_HEADS = 8, 2
LRU_WIDTH, LRU_BLOCKS, LRU_C = 512, 8, 8.0
CONV_W, CONV_LEFT = 4, 2
C_HEADS, C_KV_HEADS, C_HALF_WINDOW = 8, 2, 128
D_PATTERNS = ((128, 1), (512, 4), (2048, 16))
D_HEADS = 4
N_BRANCH = 4

LANES = 128
SUBLANES = 8
VMEM_LIMIT_BYTES = 56 * 1024 * 1024

TM_INPROJ = 512
TQ_ATTN_A = 256
SUB_ATTN_A = 128
LOOKAHEAD_ATTN_A = 10
LOOKAHEAD_BAND = 6
SUB_BAND = 128
CHUNKS_PER_BODY_ATTN_A = 4
TQ_BAND = 256
TT_LRU = 512
TM_MERGE = 512
TM_MLP = 512
MLP_CHUNK = 1024

Q_SCALE = HEAD_DIM ** -0.5 * LOG2E


def _cparams(sem):
    return pltpu.CompilerParams(dimension_semantics=sem, vmem_limit_bytes=VMEM_LIMIT_BYTES)


def _resident(shape):
    nd = len(shape)
    return pl.BlockSpec(shape, lambda *_: (0,) * nd, pipeline_mode=pl.Buffered(1))


def _rope_tables(seq):
    pos = jnp.arange(seq, dtype=F32)
    lane = jnp.arange(LANES)
    d = lane % HEAD_DIM

    inv = ROPE_THETA ** (-jnp.arange(0, HEAD_DIM, 2, dtype=F32) / HEAD_DIM)
    ang = pos[:, None] * inv[None, :]
    f = d % (HEAD_DIM // 2)
    cos_s, sin_s = jnp.cos(ang)[:, f], jnp.sin(ang)[:, f]
    lo = (d < HEAD_DIM // 2)[None, :]
    seq_t = (cos_s, jnp.where(lo, -sin_s, 0.0), jnp.where(lo, 0.0, sin_s))

    quarter = HEAD_DIM // 4
    inv_ax = ROPE_THETA ** (-jnp.arange(0, HEAD_DIM // 2, 2, dtype=F32) / (HEAD_DIM // 2))
    row_pos = (jnp.arange(seq) // GRID_W).astype(F32)
    col_pos = (jnp.arange(seq) % GRID_W).astype(F32)
    e = d % (HEAD_DIM // 2)
    fa = e % quarter
    is_col = (d >= HEAD_DIM // 2)[None, :]
    ang_ax = jnp.where(is_col, col_pos[:, None], row_pos[:, None]) * inv_ax[fa][None, :]
    cos_a, sin_a = jnp.cos(ang_ax), jnp.sin(ang_ax)
    lo_a = (e < quarter)[None, :]
    ax_t = (cos_a, jnp.where(lo_a, -sin_a, 0.0), jnp.where(lo_a, 0.0, sin_a))
    return ax_t, seq_t


def _rotate(x, tables, half):
    cos, sin_a, sin_b = tables
    return x * cos + pltpu.roll(x, LANES - half, 1) * sin_a + pltpu.roll(x, half, 1) * sin_b


def _split_dot(x, m):
    hi = x.astype(BF16)
    lo = (x - hi.astype(F32)).astype(BF16)
    return (jnp.dot(hi, m, preferred_element_type=F32) + jnp.dot(lo, m, preferred_element_type=F32))


def _head_mean_matrix():
    r = lax.broadcasted_iota(jnp.int32, (LANES, LANES), 0) // HEAD_DIM
    c = lax.broadcasted_iota(jnp.int32, (LANES, LANES), 1) // HEAD_DIM
    return jnp.where(r == c, 1.0 / HEAD_DIM, 0.0).astype(BF16)


def _head_rmsnorm(x, gain, mean_mat):
    var = _split_dot(x * x, mean_mat)
    return x * lax.rsqrt(var + EPS) * gain


IN_WIDTHS = (512, 128, 128, 512, 512, 512, 128, 128, 768, 768, 768)
IN_OFFSETS = tuple(sum(IN_WIDTHS[:i]) for i in range(len(IN_WIDTHS)))
N_IN = sum(IN_WIDTHS)


def _inproj_kernel(x_ref, g_ref, w_ref, qkg_ref,
                   ca_ref, saa_ref, sab_ref, cs_ref, ssa_ref, ssb_ref,
                   qta_ref, ka_ref, vta_ref, bx_ref, by_ref,
                   qc_ref, kc_ref, vc_ref, *rest):
    d_refs, perm_scr = rest[:9], rest[9]
    x = x_ref[0]
    var = jnp.mean(x * x, axis=-1, keepdims=True)
    hn = (x * lax.rsqrt(var + EPS) * g_ref[...]).astype(BF16)

    def proj(seg, lo=0, width=None):
        off = IN_OFFSETS[seg] + lo
        width = IN_WIDTHS[seg] if width is None else width
        return jnp.dot(hn, w_ref[:, off:off + width], preferred_element_type=F32)

    ax_t = (ca_ref[...], saa_ref[...], sab_ref[...])
    seq_t = (cs_ref[...], ssa_ref[...], ssb_ref[...])
    mean_mat = _head_mean_matrix()
    gq = qkg_ref[0:1, :]
    gk = qkg_ref[1:2, :]

    aq = proj(0)
    for j in range(IN_WIDTHS[0] // LANES):
        piece = _head_rmsnorm(aq[:, j * LANES:(j + 1) * LANES], gq, mean_mat)
        piece = _rotate(piece, ax_t, HEAD_DIM // 4) * Q_SCALE
        qta_ref[0, j * LANES:(j + 1) * LANES, :] = piece.T.astype(BF16)
    ak = _rotate(_head_rmsnorm(proj(1), gk, mean_mat), ax_t, HEAD_DIM // 4).astype(BF16)
    for g in range(A_KV_HEADS):
        ka_ref[0, g] = ak[:, g * HEAD_DIM:(g + 1) * HEAD_DIM]
    avt = proj(2).T.astype(BF16)
    for g in range(A_KV_HEADS):
        vta_ref[0, g, 0] = avt[g * HEAD_DIM:(g + 1) * HEAD_DIM, :]

    bx_ref[0] = proj(3)
    by_ref[0] = proj(4).astype(BF16)

    def rope_store(ref, seg, scale):
        acc = proj(seg)
        for j in range(IN_WIDTHS[seg] // LANES):
            piece = _rotate(acc[:, j * LANES:(j + 1) * LANES], seq_t, HEAD_DIM // 2)
            if scale is not None:
                piece = piece * scale
            ref[0, :, j * LANES:(j + 1) * LANES] = piece.astype(BF16)

    rope_store(qc_ref, 5, Q_SCALE)
    rope_store(kc_ref, 6, None)
    vc_ref[0] = proj(7).astype(BF16)

    tm = x.shape[0]
    dw = D_HEADS * HEAD_DIM
    for gi, (_, dil) in enumerate(D_PATTERNS):
        for ti, (seg, scale, rope) in enumerate(((8, Q_SCALE, True), (9, None, True), (10, None, False))):
            ref = d_refs[ti * len(D_PATTERNS) + gi]
            acc = proj(seg, gi * dw, dw)
            for j in range(dw // LANES):
                piece = acc[:, j * LANES:(j + 1) * LANES]
                if rope:
                    piece = _rotate(piece, seq_t, HEAD_DIM // 2)
                if scale is not None:
                    piece = piece * scale
                if dil == 1:
                    ref[0, 0, :, j * LANES:(j + 1) * LANES] = piece.astype(BF16)
                else:
                    perm_scr[j] = piece
            if dil > 1:
                for r in range(dil):
                    rows = [perm_scr[j, pl.ds(r, tm // dil, stride=dil), :] for j in range(dw // LANES)]
                    ref[0, r] = jnp.concatenate(rows, axis=1).astype(BF16)


def _inproj(x, g, w_bf16, qk_gain128, ax_t, seq_t):
    b, s, d = x.shape
    tm = min(TM_INPROJ, s)
    nt = s // tm
    tok = lambda width: pl.BlockSpec((1, tm, width), lambda i, bb: (bb, i, 0))
    tab = pl.BlockSpec((tm, LANES), lambda i, bb: (i, 0))
    dw = D_HEADS * HEAD_DIM
    d_shapes = tuple(jax.ShapeDtypeStruct((b, dil, s // dil, dw), BF16) for _, dil in D_PATTERNS)
    d_specs = tuple(pl.BlockSpec((1, dil, tm // dil, dw), lambda i, bb: (bb, 0, i, 0)) for _, dil in D_PATTERNS)
    out_shape = (
        jax.ShapeDtypeStruct((b, A_HEADS * HEAD_DIM, s), BF16),
        jax.ShapeDtypeStruct((b, A_KV_HEADS, s, HEAD_DIM), BF16),
        jax.ShapeDtypeStruct((b, A_KV_HEADS, nt, HEAD_DIM, tm), BF16),
        jax.ShapeDtypeStruct((b, s, LRU_WIDTH), F32),
        jax.ShapeDtypeStruct((b, s, LRU_WIDTH), BF16),
        jax.ShapeDtypeStruct((b, s, 512), BF16),
        jax.ShapeDtypeStruct((b, s, 128), BF16),
        jax.ShapeDtypeStruct((b, s, 128), BF16),
    ) + d_shapes * 3
    out_specs = (
        pl.BlockSpec((1, A_HEADS * HEAD_DIM, tm), lambda i, bb: (bb, 0, i)),
        pl.BlockSpec((1, A_KV_HEADS, tm, HEAD_DIM), lambda i, bb: (bb, 0, i, 0)),
        pl.BlockSpec((1, A_KV_HEADS, 1, HEAD_DIM, tm), lambda i, bb: (bb, 0, i, 0, 0)),
        tok(LRU_WIDTH), tok(LRU_WIDTH), tok(512), tok(128), tok(128),
    ) + d_specs * 3
    return pl.pallas_call(
        _inproj_kernel,
        grid=(nt, b),
        in_specs=[tok(d), _resident((1, d)), _resident((d, N_IN)), _resident((2, LANES)),
                  tab, tab, tab, tab, tab, tab],
        out_specs=out_specs,
        out_shape=out_shape,
        scratch_shapes=[pltpu.VMEM((dw // LANES, tm, LANES), F32)],
        compiler_params=_cparams(("parallel", "parallel")),
        name="inproj",
    )(x, g, w_bf16, qk_gain128, *ax_t, *seq_t)


def _sublane_allmax(x):
    for shift in (4, 2, 1):
        x = jnp.maximum(x, pltpu.roll(x, shift, 0))
    return x


def _attn_a_kernel(q_ref, k_ref, v_ref, o_ref, m_ref, l_ref, acc_ref, *, group, sub):
    n_chunks, chunk = v_ref.shape[2], v_ref.shape[4]
    tq = q_ref.shape[2]
    m_ref[...] = jnp.full(m_ref.shape, MASK_VALUE, F32)
    l_ref[...] = jnp.zeros(l_ref.shape, F32)
    acc_ref[...] = jnp.zeros(acc_ref.shape, F32)

    per_body = min(CHUNKS_PER_BODY_ATTN_A, n_chunks)
    units = [(cc, u, h) for cc in range(per_body) for u in range(chunk // sub) for h in range(group)]

    def body(it, carry):
        c0 = it * per_body

        def scores(idx):
            cc, u, h = units[idx]
            row0 = pl.multiple_of((c0 + cc) * chunk + u * sub, sub)
            k = k_ref[0, 0, pl.ds(row0, sub), :]
            qt = q_ref[0, h * HEAD_DIM:(h + 1) * HEAD_DIM, :]
            return jnp.dot(k, qt, preferred_element_type=F32)

        pending = [scores(i) for i in range(LOOKAHEAD_ATTN_A)]
        for idx, (cc, u, h) in enumerate(units):
            if idx + LOOKAHEAD_ATTN_A < len(units):
                pending.append(scores(idx + LOOKAHEAD_ATTN_A))
            s = pending.pop(0)
            vt = v_ref[0, 0, c0 + cc, :, u * sub:(u + 1) * sub]
            s3 = s.reshape(sub // SUBLANES, SUBLANES, tq)
            m_old = m_ref[h]
            m_new = jnp.maximum(m_old, _sublane_allmax(jnp.max(s3, axis=0)))
            alpha = jnp.exp2(m_old - m_new)
            p3 = jnp.exp2(s3 - m_new[None])
            l_ref[h] = alpha * l_ref[h] + jnp.sum(p3, axis=0)
            pv = jnp.dot(vt, p3.reshape(sub, tq).astype(BF16), preferred_element_type=F32)
            acc_ref[h] = (acc_ref[h] * alpha[None] + pv.reshape(HEAD_DIM // SUBLANES, SUBLANES, tq))
            m_ref[h] = m_new
        return carry

    lax.fori_loop(0, n_chunks // per_body, body, 0)
    outs = []
    for h in range(group):
        l = jnp.sum(l_ref[h], axis=0, keepdims=True)
        outs.append(acc_ref[h].reshape(HEAD_DIM, tq) / l)
    o_ref[0] = jnp.concatenate(outs, axis=0).T.astype(o_ref.dtype)


def _attn_a(qt, k, vt):
    b, hd_all, s = qt.shape
    nkv, nt, chunk = vt.shape[1], vt.shape[2], vt.shape[4]
    group = hd_all // HEAD_DIM // nkv
    tq = min(TQ_ATTN_A, s)
    gw = group * HEAD_DIM
    return pl.pallas_call(
        functools.partial(_attn_a_kernel, group=group, sub=min(SUB_ATTN_A, chunk)),
        grid=(b, nkv, s // tq),
        in_specs=[pl.BlockSpec((1, gw, tq), lambda bb, g, i: (bb, g, i)),
                  pl.BlockSpec((1, 1, s, HEAD_DIM), lambda bb, g, i: (bb, g, 0, 0)),
                  pl.BlockSpec((1, 1, nt, HEAD_DIM, chunk), lambda bb, g, i: (bb, g, 0, 0, 0))],
        out_specs=pl.BlockSpec((1, tq, gw), lambda bb, g, i: (bb, i, g)),
        out_shape=jax.ShapeDtypeStruct((b, s, hd_all), BF16),
        scratch_shapes=[pltpu.VMEM((group, SUBLANES, tq), F32), pltpu.VMEM((group, SUBLANES, tq), F32),
                        pltpu.VMEM((group, HEAD_DIM // SUBLANES, SUBLANES, tq), F32)],
        compiler_params=_cparams(("parallel", "parallel", "parallel")),
        name="attn_a",
    )(qt, k, vt)


def _band_kernel(*refs, nkv, group, half_window, seq_len, has_sink, want_lse):
    q_ref, kp_ref, km_ref, kn_ref, vp_ref, vm_ref, vn_ref = refs[:7]
    rest = list(refs[7:])
    sink_ref = rest.pop(0) if has_sink else None
    o_ref = rest.pop(0)
    lse_ref = rest.pop(0) if want_lse else None

    tq = q_ref.shape[2]
    halo = kp_ref.shape[2]
    nk = tq + 2 * halo
    sub = SUB_BAND if nk % SUB_BAND == 0 else halo
    n_heads = nkv * group
    q0 = pl.program_id(2) * tq

    qt = q_ref[0, 0].T
    kcat = jnp.concatenate([kp_ref[0, 0], km_ref[0, 0], kn_ref[0, 0]], axis=0)
    vt = jnp.concatenate([vp_ref[0, 0], vm_ref[0, 0], vn_ref[0, 0]], axis=0).T

    kw = min(LANES, nkv * HEAD_DIM)
    zeros = jnp.zeros((HEAD_DIM, tq), qt.dtype)

    def padded_query(h):
        g = h // group
        slot = (g * HEAD_DIM % kw) // HEAD_DIM
        parts = [zeros] * (kw // HEAD_DIM)
        parts[slot] = qt[h * HEAD_DIM:(h + 1) * HEAD_DIM, :]
        return parts[0] if len(parts) == 1 else jnp.concatenate(parts, axis=0)

    q_pad = [padded_query(h) for h in range(n_heads)]
    units = [(kb, h) for kb in range(nk // sub) for h in range(n_heads)]

    def scores(idx):
        kb, h = units[idx]
        lane0 = (h // group) * HEAD_DIM // kw * kw
        return jnp.dot(kcat[kb * sub:(kb + 1) * sub, lane0:lane0 + kw], q_pad[h], preferred_element_type=F32)

    key_row = lax.broadcasted_iota(jnp.int32, (sub, tq), 0)
    query = lax.broadcasted_iota(jnp.int32, (sub, tq), 1)

    def valid_mask(kb):
        kpos = q0 - halo + kb * sub + key_row
        rel = kpos - (q0 + query)
        return (jnp.abs(rel) <= half_window) & (kpos >= 0) & (kpos < seq_len)

    if has_sink:
        m = [jnp.full((SUBLANES, tq), sink_ref[h] * LOG2E, F32) for h in range(n_heads)]
        first = (lax.broadcasted_iota(jnp.int32, (SUBLANES, tq), 0) == 0).astype(F32)
        l = [first] * n_heads
    else:
        m = [jnp.full((SUBLANES, tq), 0.1 * MASK_VALUE, F32)] * n_heads
        l = [jnp.zeros((SUBLANES, tq), F32)] * n_heads
    acc = [jnp.zeros((HEAD_DIM // SUBLANES, SUBLANES, tq), F32)] * n_heads

    look = min(LOOKAHEAD_BAND, len(units))
    pending = [scores(i) for i in range(look)]
    masks = {}
    for idx, (kb, h) in enumerate(units):
        if idx + look < len(units):
            pending.append(scores(idx + look))
        if kb not in masks:
            masks = {kb: valid_mask(kb)}
        g = h // group
        s3 = jnp.where(masks[kb], pending.pop(0), MASK_VALUE).reshape(sub // SUBLANES, SUBLANES, tq)
        m_new = jnp.maximum(m[h], _sublane_allmax(jnp.max(s3, axis=0)))
        alpha = jnp.exp2(m[h] - m_new)
        p3 = jnp.exp2(s3 - m_new[None])
        l[h] = alpha * l[h] + jnp.sum(p3, axis=0)
        pv = jnp.dot(vt[g * HEAD_DIM:(g + 1) * HEAD_DIM, kb * sub:(kb + 1) * sub],
                     p3.reshape(sub, tq).astype(BF16), preferred_element_type=F32)
        acc[h] = acc[h] * alpha[None] + pv.reshape(HEAD_DIM // SUBLANES, SUBLANES, tq)
        m[h] = m_new

    outs, lses = [], []
    for h in range(n_heads):
        l_row = jnp.sum(l[h], axis=0, keepdims=True)
        outs.append(acc[h].reshape(HEAD_DIM, tq) / l_row)
        if want_lse:
            lses.append(jnp.broadcast_to(m[h][0:1] + jnp.log2(l_row), (HEAD_DIM, tq)))
    o_ref[0, 0] = jnp.concatenate(outs, axis=0).T.astype(o_ref.dtype)
    if want_lse:
        lse_ref[0, 0] = jnp.concatenate(lses, axis=0).T


def _band_attention(q, k, v, *, nkv, group, half_window, sink=None, want_lse=False):
    b, n_seq, seq_len, wq = q.shape
    wk = nkv * HEAD_DIM
    halo = half_window
    tq = min(TQ_BAND, seq_len)
    per = tq // halo
    n_halo = seq_len // halo

    main = lambda width: pl.BlockSpec((1, 1, tq, width), lambda bb, r, i: (bb, r, i, 0))
    prev = lambda width: pl.BlockSpec(
        (1, 1, halo, width), lambda bb, r, i: (bb, r, jnp.maximum(i * per - 1, 0), 0))
    nxt = lambda width: pl.BlockSpec(
        (1, 1, halo, width), lambda bb, r, i: (bb, r, jnp.minimum((i + 1) * per, n_halo - 1), 0))
    in_specs = [main(wq), prev(wk), main(wk), nxt(wk), prev(wk), main(wk), nxt(wk)]
    args = [q, k, k, k, v, v, v]
    if sink is not None:
        in_specs.append(pl.BlockSpec(memory_space=pltpu.SMEM))
        args.append(sink)
    out_block = main(wq)
    out_shape = [jax.ShapeDtypeStruct(q.shape, BF16)]
    out_specs = [out_block]
    if want_lse:
        out_shape.append(jax.ShapeDtypeStruct(q.shape, F32))
        out_specs.append(out_block)
    res = pl.pallas_call(
        functools.partial(_band_kernel, nkv=nkv, group=group, half_window=half_window,
                          seq_len=seq_len, has_sink=sink is not None, want_lse=want_lse),
        grid=(b, n_seq, seq_len // tq),
        in_specs=in_specs,
        out_specs=out_specs,
        out_shape=out_shape,
        compiler_params=_cparams(("parallel", "parallel", "parallel")),
        name="band_attn",
    )(*args)
    return res if want_lse else res[0]


def _softplus(x):
    return jnp.maximum(x, 0.0) + jnp.log1p(jnp.exp(-jnp.abs(x)))


def _lru_kernel(xf_ref, xfp_ref, xfn_ref, xb_ref, xbp_ref, xbn_ref,
                cw_ref, cb_ref, gw_ref, gb_ref, lam_ref,
                hf_ref, hb_ref, a_scr, u_scr, carry_ref):
    i = pl.program_id(1)
    nt = pl.num_programs(1)
    tt = xf_ref.shape[1]
    nblk = tt // SUBLANES

    @pl.when(i == 0)
    def _():
        carry_ref[...] = jnp.zeros(carry_ref.shape, F32)

    row = lax.broadcasted_iota(jnp.int32, (SUBLANES, LRU_WIDTH), 0)

    def prepare(d, x_ref, xp_ref, xn_ref, tile):
        prev_rows = jnp.where(tile > 0, xp_ref[0], 0.0)
        next_rows = jnp.where(tile < nt - 1, xn_ref[0], 0.0)
        xp = jnp.concatenate([prev_rows, x_ref[0], next_rows], axis=0)
        xc = cb_ref[...]
        base = SUBLANES - CONV_LEFT
        for jj in range(CONV_W):
            xc = xc + xp[base + jj:base + jj + tt, :] * cw_ref[jj:jj + 1, :]
        xcb = xc.astype(BF16)
        r = jax.nn.sigmoid(jnp.dot(xcb, gw_ref[d, 0], preferred_element_type=F32) + gb_ref[d, 0:1, :])
        gi = jax.nn.sigmoid(jnp.dot(xcb, gw_ref[d, 1], preferred_element_type=F32) + gb_ref[d, 1:2, :])
        log_a = -LRU_C * r * _softplus(-lam_ref[d:d + 1, :])
        a = jnp.exp(log_a)
        a_scr[d] = a
        u_scr[d] = jnp.sqrt(-jnp.tanh(log_a) * (a * a + 1.0)) * (gi * xc)

    prepare(0, xf_ref, xfp_ref, xfn_ref, i)
    prepare(1, xb_ref, xbp_ref, xbn_ref, nt - 1 - i)

    def scan_block(d, blk, carry, out_ref):
        a8 = a_scr[d, pl.ds(blk * SUBLANES, SUBLANES), :]
        u8 = u_scr[d, pl.ds(blk * SUBLANES, SUBLANES), :]
        for step in (1, 2, 4):
            if d == 0:
                shift, ident = step, row < step
            else:
                shift, ident = SUBLANES - step, row >= SUBLANES - step
            u_sh = jnp.where(ident, 0.0, pltpu.roll(u8, shift, 0))
            a_sh = jnp.where(ident, 1.0, pltpu.roll(a8, shift, 0))
            u8 = u8 + a8 * u_sh
            a8 = a8 * a_sh
        h8 = u8 + a8 * carry
        out_ref[0, pl.ds(blk * SUBLANES, SUBLANES), :] = h8
        last = h8[SUBLANES - 1:SUBLANES, :] if d == 0 else h8[0:1, :]
        return jnp.broadcast_to(last, (SUBLANES, LRU_WIDTH))

    def body(kb, carries):
        cf, cb = carries
        cf = scan_block(0, kb, cf, hf_ref)
        cb = scan_block(1, nblk - 1 - kb, cb, hb_ref)
        return cf, cb

    cf, cb = lax.fori_loop(0, nblk, body, (carry_ref[0], carry_ref[1]))
    carry_ref[0] = cf
    carry_ref[1] = cb


def _lru(bx, conv_w, conv_b, gate_w_dense, gate_b, lam):
    b, s, w = bx.shape
    tt = min(TT_LRU, s)
    nt = s // tt
    per = tt // SUBLANES
    n8 = s // SUBLANES
    fwd = lambda i: i
    bwd = lambda i: nt - 1 - i

    def specs(tile_of):
        return [
            pl.BlockSpec((1, tt, w), lambda bb, i: (bb, tile_of(i), 0)),
            pl.BlockSpec((1, SUBLANES, w), lambda bb, i: (bb, jnp.maximum(tile_of(i) * per - 1, 0), 0)),
            pl.BlockSpec((1, SUBLANES, w), lambda bb, i: (bb, jnp.minimum((tile_of(i) + 1) * per, n8 - 1), 0)),
        ]

    return pl.pallas_call(
        _lru_kernel,
        grid=(b, nt),
        in_specs=specs(fwd) + specs(bwd) + [
            _resident(conv_w.shape), _resident(conv_b.shape), _resident(gate_w_dense.shape),
            _resident(gate_b.shape), _resident(lam.shape)],
        out_specs=[pl.BlockSpec((1, tt, w), lambda bb, i: (bb, i, 0)),
                   pl.BlockSpec((1, tt, w), lambda bb, i: (bb, nt - 1 - i, 0))],
        out_shape=[jax.ShapeDtypeStruct((b, s, w), F32), jax.ShapeDtypeStruct((b, s, w), F32)],
        scratch_shapes=[pltpu.VMEM((2, tt, w), F32), pltpu.VMEM((2, tt, w), F32),
                        pltpu.VMEM((2, SUBLANES, w), F32)],
        compiler_params=_cparams(("parallel", "arbitrary")),
        name="lru",
    )(bx, bx, bx, bx, bx, bx, conv_w, conv_b, gate_w_dense, gate_b, lam)


def _gelu_tanh(x):
    return 0.5 * x * (1.0 + jnp.tanh(math.sqrt(2.0 / math.pi) * (x + 0.044715 * (x * x * x))))


def _merge_kernel(x_ref, ya_ref, hf_ref, hb_ref, by_ref, yc_ref,
                  o0_ref, o1_ref, o2_ref, l0_ref, l1_ref, l2_ref, g_ref, wg_ref, gbias_ref,
                  wa_ref, wb_ref, wc_ref, wd_ref, wo_ref, out_ref, perm_scr):
    d_model = x_ref.shape[2]
    tm = x_ref.shape[1]
    x = x_ref[0]
    var = jnp.mean(x * x, axis=-1, keepdims=True)
    hn = (x * lax.rsqrt(var + EPS) * g_ref[...]).astype(BF16)

    def natural(ref, slot):
        dil = ref.shape[1]
        if dil == 1:
            return ref[0, 0].astype(F32)
        for r in range(dil):
            rows = ref[0, r].astype(F32)
            for j in range(rows.shape[1] // LANES):
                perm_scr[slot, j, pl.ds(r, tm // dil, stride=dil), :] = rows[:, j * LANES:(j + 1) * LANES]
        return jnp.concatenate([perm_scr[slot, j] for j in range(perm_scr.shape[1])], axis=1)

    def gate(kk):
        logits = jnp.dot(hn, wg_ref[:, kk * d_model:(kk + 1) * d_model], preferred_element_type=F32)
        return jax.nn.sigmoid(logits + gbias_ref[kk:kk + 1, :])

    merged = gate(0) * jnp.dot(ya_ref[0], wa_ref[...], preferred_element_type=F32)

    yb = (hf_ref[0] + hb_ref[0]) * _gelu_tanh(by_ref[0].astype(F32))
    merged = merged + gate(1) * jnp.dot(yb.astype(BF16), wb_ref[...], preferred_element_type=F32)

    merged = merged + gate(2) * jnp.dot(yc_ref[0], wc_ref[...], preferred_element_type=F32)

    l0, l1, l2 = natural(l0_ref, 0), natural(l1_ref, 0), natural(l2_ref, 1)
    o0, o1, o2 = natural(o0_ref, 2), natural(o1_ref, 2), natural(o2_ref, 3)
    mx = jnp.maximum(jnp.maximum(l0, l1), l2)
    e0, e1, e2 = jnp.exp2(l0 - mx), jnp.exp2(l1 - mx), jnp.exp2(l2 - mx)
    yd = (e0 * o0 + e1 * o1 + e2 * o2) / (e0 + e1 + e2)
    merged = merged + gate(3) * jnp.dot(yd.astype(BF16), wd_ref[...], preferred_element_type=F32)

    out_ref[0] = x + jnp.dot(merged.astype(BF16), wo_ref[...], preferred_element_type=F32)


def _merge(x, ya, hf, hb, by, yc, o_d, lse_d, g, wg, gbias, wa, wb, wc, wd, wo):
    b, s, d = x.shape
    tm = min(TM_MERGE, s)
    tok = lambda width: pl.BlockSpec((1, tm, width), lambda bb, i: (bb, i, 0))
    dw = D_HEADS * HEAD_DIM
    res = [pl.BlockSpec((1, dil, tm // dil, dw), lambda bb, i: (bb, 0, i, 0)) for _, dil in D_PATTERNS]
    return pl.pallas_call(
        _merge_kernel,
        grid=(b, s // tm),
        in_specs=[tok(d), tok(512), tok(LRU_WIDTH), tok(LRU_WIDTH), tok(LRU_WIDTH), tok(512),
                  *res, *res, _resident(g.shape), _resident(wg.shape),
                  _resident(gbias.shape), _resident(wa.shape), _resident(wb.shape),
                  _resident(wc.shape), _resident(wd.shape), _resident(wo.shape)],
        out_specs=tok(d),
        out_shape=jax.ShapeDtypeStruct((b, s, d), F32),
        scratch_shapes=[pltpu.VMEM((4, dw // LANES, tm, LANES), F32)],
        compiler_params=_cparams(("parallel", "parallel")),
        name="merge",
    )(x, ya, hf, hb, by, yc, *o_d, *lse_d, g, wg, gbias, wa, wb, wc, wd, wo)


def _mlp_kernel(x_ref, g_ref, w1_ref, w2_ref, gf_ref, out_ref, *, final_norm):
    x = x_ref[0]
    var = jnp.mean(x * x, axis=-1, keepdims=True)
    hn = (x * lax.rsqrt(var + EPS) * g_ref[...]).astype(BF16)
    hidden = w1_ref.shape[1]
    acc = x
    for c in range(hidden // MLP_CHUNK):
        h = jnp.dot(hn, w1_ref[:, c * MLP_CHUNK:(c + 1) * MLP_CHUNK], preferred_element_type=F32)
        h = jnp.square(jnp.maximum(h, 0.0)).astype(BF16)
        acc = acc + jnp.dot(h, w2_ref[c * MLP_CHUNK:(c + 1) * MLP_CHUNK, :], preferred_element_type=F32)
    if final_norm:
        var = jnp.mean(acc * acc, axis=-1, keepdims=True)
        acc = acc * lax.rsqrt(var + EPS) * gf_ref[...]
    out_ref[0] = acc


def _mlp(x, g, w1, w2, g_final, final_norm):
    b, s, d = x.shape
    tm = min(TM_MLP, s)
    tok = pl.BlockSpec((1, tm, d), lambda bb, i: (bb, i, 0))
    return pl.pallas_call(
        functools.partial(_mlp_kernel, final_norm=final_norm),
        grid=(b, s // tm),
        in_specs=[tok, _resident(g.shape), _resident(w1.shape), _resident(w2.shape), _resident(g_final.shape)],
        out_specs=tok,
        out_shape=jax.ShapeDtypeStruct((b, s, d), F32),
        compiler_params=_cparams(("parallel", "parallel")),
        name="mlp",
    )(x, g, w1, w2, g_final)


def _dense_block_diag(w):
    nb, bw, _ = w.shape
    eye = jnp.eye(nb, dtype=w.dtype)
    return jnp.einsum('ncd,nm->ncmd', w, eye).reshape(nb * bw, nb * bw)


def kernel(x, norm_mix_g, w_in, gate_bias, qk_norm_g, conv_w, conv_b, lru_gate_w, lru_gate_b, lru_lambda, sink_logit, w_proj_a, w_proj_b, w_proj_c, w_proj_d, w_out, norm_mlp_g, w_mlp1, w_mlp2, norm_final_g):
    b, s, d = x.shape
    depth = w_in.shape[0]
    ax_t, seq_t = _rope_tables(s)
    dw = D_HEADS * HEAD_DIM
    n_groups = len(D_PATTERNS)

    for l in range(depth):
        qk_gain = jnp.tile(qk_norm_g[l], (1, LANES // HEAD_DIM))
        w_in_l = w_in[l].astype(BF16)
        outs = _inproj(x, norm_mix_g[l][None, :], w_in_l[:, :N_IN], qk_gain, ax_t, seq_t)
        qta, ka, vta, bx, by, qc, kc, vc = outs[:8]
        qd, kd, vd = outs[8:8 + n_groups], outs[8 + n_groups:8 + 2 * n_groups], outs[8 + 2 * n_groups:8 + 3 * n_groups]

        ya = _attn_a(qta, ka, vta)

        gate_w_dense = jnp.stack([jnp.stack([_dense_block_diag(lru_gate_w[l, dd, gg]) for gg in range(2)])
                                  for dd in range(2)]).astype(BF16)
        hf, hb = _lru(bx, conv_w[l], conv_b[l][None, :], gate_w_dense, lru_gate_b[l], lru_lambda[l])

        yc = _band_attention(qc[:, None], kc[:, None], vc[:, None], nkv=C_KV_HEADS,
                             group=C_HEADS // C_KV_HEADS, half_window=C_HALF_WINDOW, sink=sink_logit[l])[:, 0]

        o_d, lse_d = [], []
        for gi, (window, dil) in enumerate(D_PATTERNS):
            o, lse = _band_attention(qd[gi], kd[gi], vd[gi], nkv=D_HEADS, group=1,
                                     half_window=window // (2 * dil), want_lse=True)
            o_d.append(o)
            lse_d.append(lse)

        x = _merge(x, ya, hf, hb, by, yc, o_d, lse_d, norm_mix_g[l][None, :], w_in_l[:, N_IN:], gate_bias[l],
                   w_proj_a[l].astype(BF16), w_proj_b[l].astype(BF16), w_proj_c[l].astype(BF16),
                   w_proj_d[l].astype(BF16), w_out[l].astype(BF16))
        x = _mlp(x, norm_mlp_g[l][None, :], w_mlp1[l].astype(BF16), w_mlp2[l].astype(BF16),
                 norm_final_g[None, :], final_norm=(l == depth - 1))
    return x
```

```python
import functools
import math

import jax
import jax.numpy as jnp
from jax import lax
from jax.experimental import pallas as pl
from jax.experimental.pallas import tpu as pltpu

F32 = jnp.float32
BF16 = jnp.bfloat16

HEAD_DIM = 64
ROPE_THETA = 10000.0
GRID_W = 64
EPS = 1e-6
MASK_VALUE = -1e30
LOG2E = math.log2(math.e)

A_HEADS, A_KV_HEADS = 8, 2
LRU_WIDTH, LRU_BLOCKS, LRU_C = 512, 8, 8.0
CONV_W, CONV_LEFT = 4, 2
C_HEADS, C_KV_HEADS, C_HALF_WINDOW = 8, 2, 128
D_PATTERNS = ((128, 1), (512, 4), (2048, 16))
D_HEADS = 4
N_BRANCH = 4

LANES = 128
SUBLANES = 8
VMEM_LIMIT_BYTES = 56 * 1024 * 1024

TM_INPROJ = 512
TQ_ATTN_A = 256
SUB_ATTN_A = 128
LOOKAHEAD_ATTN_A = 14
LOOKAHEAD_BAND = 6
SUB_BAND = 128
CHUNKS_PER_BODY_ATTN_A = 16
VT_ROWS_A = HEAD_DIM + 16
TQ_BAND = 256
TT_LRU = 512
TM_MERGE = 512
TM_MLP = 512
MLP_CHUNK = 1024

Q_SCALE = HEAD_DIM ** -0.5 * LOG2E


def _cparams(sem):
    return pltpu.CompilerParams(dimension_semantics=sem, vmem_limit_bytes=VMEM_LIMIT_BYTES)


def _resident(shape):
    nd = len(shape)
    return pl.BlockSpec(shape, lambda *_: (0,) * nd, pipeline_mode=pl.Buffered(1))


def _rope_tables(seq):
    pos = jnp.arange(seq, dtype=F32)
    lane = jnp.arange(LANES)
    d = lane % HEAD_DIM

    inv = ROPE_THETA ** (-jnp.arange(0, HEAD_DIM, 2, dtype=F32) / HEAD_DIM)
    ang = pos[:, None] * inv[None, :]
    f = d % (HEAD_DIM // 2)
    cos_s, sin_s = jnp.cos(ang)[:, f], jnp.sin(ang)[:, f]
    lo = (d < HEAD_DIM // 2)[None, :]
    seq_t = (cos_s, jnp.where(lo, -sin_s, 0.0), jnp.where(lo, 0.0, sin_s))

    quarter = HEAD_DIM // 4
    inv_ax = ROPE_THETA ** (-jnp.arange(0, HEAD_DIM // 2, 2, dtype=F32) / (HEAD_DIM // 2))
    row_pos = (jnp.arange(seq) // GRID_W).astype(F32)
    col_pos = (jnp.arange(seq) % GRID_W).astype(F32)
    e = d % (HEAD_DIM // 2)
    fa = e % quarter
    is_col = (d >= HEAD_DIM // 2)[None, :]
    ang_ax = jnp.where(is_col, col_pos[:, None], row_pos[:, None]) * inv_ax[fa][None, :]
    cos_a, sin_a = jnp.cos(ang_ax), jnp.sin(ang_ax)
    lo_a = (e < quarter)[None, :]
    ax_t = (cos_a, jnp.where(lo_a, -sin_a, 0.0), jnp.where(lo_a, 0.0, sin_a))
    return ax_t, seq_t


def _rotate(x, tables, half):
    cos, sin_a, sin_b = tables
    return x * cos + pltpu.roll(x, LANES - half, 1) * sin_a + pltpu.roll(x, half, 1) * sin_b


def _split_dot(x, m):
    hi = x.astype(BF16)
    lo = (x - hi.astype(F32)).astype(BF16)
    return (jnp.dot(hi, m, preferred_element_type=F32) + jnp.dot(lo, m, preferred_element_type=F32))


def _head_mean_matrix():
    r = lax.broadcasted_iota(jnp.int32, (LANES, LANES), 0) // HEAD_DIM
    c = lax.broadcasted_iota(jnp.int32, (LANES, LANES), 1) // HEAD_DIM
    return jnp.where(r == c, 1.0 / HEAD_DIM, 0.0).astype(BF16)


def _head_rmsnorm(x, gain, mean_mat):
    var = _split_dot(x * x, mean_mat)
    return x * lax.rsqrt(var + EPS) * gain


IN_WIDTHS = (512, 128, 128, 512, 512, 512, 128, 128, 768, 768, 768)
IN_OFFSETS = tuple(sum(IN_WIDTHS[:i]) for i in range(len(IN_WIDTHS)))
N_IN = sum(IN_WIDTHS)


def _inproj_kernel(x_ref, g_ref, w_ref, qkg_ref,
                   ca_ref, saa_ref, sab_ref, cs_ref, ssa_ref, ssb_ref,
                   qta_ref, ka_ref, vta_ref, bx_ref, by_ref,
                   qc_ref, kc_ref, vc_ref, *rest):
    d_refs, perm_scr = rest[:9], rest[9]
    x = x_ref[0]
    var = jnp.mean(x * x, axis=-1, keepdims=True)
    hn = (x * lax.rsqrt(var + EPS) * g_ref[...]).astype(BF16)

    def proj(seg, lo=0, width=None):
        off = IN_OFFSETS[seg] + lo
        width = IN_WIDTHS[seg] if width is None else width
        return jnp.dot(hn, w_ref[:, off:off + width], preferred_element_type=F32)

    ax_t = (ca_ref[...], saa_ref[...], sab_ref[...])
    seq_t = (cs_ref[...], ssa_ref[...], ssb_ref[...])
    mean_mat = _head_mean_matrix()
    gq = qkg_ref[0:1, :]
    gk = qkg_ref[1:2, :]

    aq = proj(0)
    for j in range(IN_WIDTHS[0] // LANES):
        piece = _head_rmsnorm(aq[:, j * LANES:(j + 1) * LANES], gq, mean_mat)
        piece = _rotate(piece, ax_t, HEAD_DIM // 4) * Q_SCALE
        qta_ref[0, j * LANES:(j + 1) * LANES, :] = piece.T.astype(BF16)
    ak = _rotate(_head_rmsnorm(proj(1), gk, mean_mat), ax_t, HEAD_DIM // 4).astype(BF16)
    for g in range(A_KV_HEADS):
        ka_ref[0, g] = ak[:, g * HEAD_DIM:(g + 1) * HEAD_DIM]
    avt = proj(2).T.astype(BF16)
    pad_row = lax.broadcasted_iota(jnp.int32, (VT_ROWS_A - HEAD_DIM, avt.shape[1]), 0)
    ones_rows = (pad_row == 0).astype(BF16)
    for g in range(A_KV_HEADS):
        vta_ref[0, g, 0] = jnp.concatenate([avt[g * HEAD_DIM:(g + 1) * HEAD_DIM, :], ones_rows], axis=0)

    bx_ref[0] = proj(3)
    by_ref[0] = proj(4).astype(BF16)

    def rope_store(ref, seg, scale):
        acc = proj(seg)
        for j in range(IN_WIDTHS[seg] // LANES):
            piece = _rotate(acc[:, j * LANES:(j + 1) * LANES], seq_t, HEAD_DIM // 2)
            if scale is not None:
                piece = piece * scale
            ref[0, :, j * LANES:(j + 1) * LANES] = piece.astype(BF16)

    rope_store(qc_ref, 5, Q_SCALE)
    rope_store(kc_ref, 6, None)
    vc_ref[0] = proj(7).astype(BF16)

    tm = x.shape[0]
    dw = D_HEADS * HEAD_DIM
    for gi, (_, dil) in enumerate(D_PATTERNS):
        for ti, (seg, scale, rope) in enumerate(((8, Q_SCALE, True), (9, None, True), (10, None, False))):
            ref = d_refs[ti * len(D_PATTERNS) + gi]
            acc = proj(seg, gi * dw, dw)
            for j in range(dw // LANES):
                piece = acc[:, j * LANES:(j + 1) * LANES]
                if rope:
                    piece = _rotate(piece, seq_t, HEAD_DIM // 2)
                if scale is not None:
                    piece = piece * scale
                if dil == 1:
                    ref[0, 0, :, j * LANES:(j + 1) * LANES] = piece.astype(BF16)
                else:
                    perm_scr[j] = piece
            if dil > 1:
                for r in range(dil):
                    rows = [perm_scr[j, pl.ds(r, tm // dil, stride=dil), :] for j in range(dw // LANES)]
                    ref[0, r] = jnp.concatenate(rows, axis=1).astype(BF16)


def _inproj(x, g, w_bf16, qk_gain128, ax_t, seq_t):
    b, s, d = x.shape
    tm = min(TM_INPROJ, s)
    nt = s // tm
    tok = lambda width: pl.BlockSpec((1, tm, width), lambda i, bb: (bb, i, 0))
    tab = pl.BlockSpec((tm, LANES), lambda i, bb: (i, 0))
    dw = D_HEADS * HEAD_DIM
    d_shapes = tuple(jax.ShapeDtypeStruct((b, dil, s // dil, dw), BF16) for _, dil in D_PATTERNS)
    d_specs = tuple(pl.BlockSpec((1, dil, tm // dil, dw), lambda i, bb: (bb, 0, i, 0)) for _, dil in D_PATTERNS)
    out_shape = (
        jax.ShapeDtypeStruct((b, A_HEADS * HEAD_DIM, s), BF16),
        jax.ShapeDtypeStruct((b, A_KV_HEADS, s, HEAD_DIM), BF16),
        jax.ShapeDtypeStruct((b, A_KV_HEADS, nt, VT_ROWS_A, tm), BF16),
        jax.ShapeDtypeStruct((b, s, LRU_WIDTH), F32),
        jax.ShapeDtypeStruct((b, s, LRU_WIDTH), BF16),
        jax.ShapeDtypeStruct((b, s, 512), BF16),
        jax.ShapeDtypeStruct((b, s, 128), BF16),
        jax.ShapeDtypeStruct((b, s, 128), BF16),
    ) + d_shapes * 3
    out_specs = (
        pl.BlockSpec((1, A_HEADS * HEAD_DIM, tm), lambda i, bb: (bb, 0, i)),
        pl.BlockSpec((1, A_KV_HEADS, tm, HEAD_DIM), lambda i, bb: (bb, 0, i, 0)),
        pl.BlockSpec((1, A_KV_HEADS, 1, VT_ROWS_A, tm), lambda i, bb: (bb, 0, i, 0, 0)),
        tok(LRU_WIDTH), tok(LRU_WIDTH), tok(512), tok(128), tok(128),
    ) + d_specs * 3
    return pl.pallas_call(
        _inproj_kernel,
        grid=(nt, b),
        in_specs=[tok(d), _resident((1, d)), _resident((d, N_IN)), _resident((2, LANES)),
                  tab, tab, tab, tab, tab, tab],
        out_specs=out_specs,
        out_shape=out_shape,
        scratch_shapes=[pltpu.VMEM((dw // LANES, tm, LANES), F32)],
        compiler_params=_cparams(("parallel", "parallel")),
        name="inproj",
    )(x, g, w_bf16, qk_gain128, *ax_t, *seq_t)


def _sublane_allmax(x):
    for shift in (4, 2, 1):
        x = jnp.maximum(x, pltpu.roll(x, shift, 0))
    return x


def _attn_a_kernel(q_ref, k_ref, v_ref, o_ref, m_ref, acc_ref, *, group, sub):
    n_chunks, chunk = v_ref.shape[2], v_ref.shape[4]
    tq = q_ref.shape[2]
    m_ref[...] = jnp.full(m_ref.shape, MASK_VALUE, F32)
    acc_ref[...] = jnp.zeros(acc_ref.shape, F32)

    per_body = min(CHUNKS_PER_BODY_ATTN_A, n_chunks)
    n_sub = chunk // sub
    pair = 2 if n_sub % 2 == 0 else 1
    units = [(cc, u, h) for cc in range(per_body) for up in range(n_sub // pair) for h in range(group)
             for u in range(up * pair, (up + 1) * pair)]

    def body(it, carry):
        c0 = it * per_body

        def scores(idx):
            cc, u, h = units[idx]
            row0 = pl.multiple_of((c0 + cc) * chunk + u * sub, sub)
            k = k_ref[0, 0, pl.ds(row0, sub), :]
            qt = q_ref[0, h * HEAD_DIM:(h + 1) * HEAD_DIM, :]
            return jnp.dot(k, qt, preferred_element_type=F32)

        pending = [scores(i) for i in range(LOOKAHEAD_ATTN_A)]
        probs = []
        for idx, (cc, u, h) in enumerate(units):
            if idx + LOOKAHEAD_ATTN_A < len(units):
                pending.append(scores(idx + LOOKAHEAD_ATTN_A))
            s3 = pending.pop(0).reshape(sub // SUBLANES, SUBLANES, tq)
            if not probs:
                m_start = m_ref[h]
                m_old = m_start
            m_new = jnp.maximum(m_old, _sublane_allmax(jnp.max(s3, axis=0)))
            if probs:
                beta = jnp.exp2(m_old - m_new).astype(BF16)
                beta = jnp.concatenate([beta, beta], axis=0)[None]
                probs = [(p.reshape(sub // 16, 16, tq) * beta).reshape(sub, tq) for p in probs]
            probs.append(jnp.exp2(s3 - m_new[None]).reshape(sub, tq).astype(BF16))
            m_old = m_new
            if len(probs) == pair:
                u0 = u - (pair - 1)
                vt = v_ref[0, 0, c0 + cc, :, u0 * sub:(u + 1) * sub]
                pv = jnp.dot(vt, jnp.concatenate(probs, axis=0), preferred_element_type=F32)
                alpha = jnp.exp2(m_start - m_new)
                acc_ref[h] = acc_ref[h] * alpha[None] + pv.reshape(VT_ROWS_A // SUBLANES, SUBLANES, tq)
                m_ref[h] = m_new
                probs = []
        return carry

    lax.fori_loop(0, n_chunks // per_body, body, 0)
    outs = []
    for h in range(group):
        acc = acc_ref[h].reshape(VT_ROWS_A, tq)
        outs.append(acc[:HEAD_DIM] / acc[HEAD_DIM:HEAD_DIM + 1])
    o_ref[0] = jnp.concatenate(outs, axis=0).T.astype(o_ref.dtype)


def _attn_a(qt, k, vt):
    b, hd_all, s = qt.shape
    nkv, nt, chunk = vt.shape[1], vt.shape[2], vt.shape[4]
    group = hd_all // HEAD_DIM // nkv
    tq = min(TQ_ATTN_A, s)
    gw = group * HEAD_DIM
    return pl.pallas_call(
        functools.partial(_attn_a_kernel, group=group, sub=min(SUB_ATTN_A, chunk)),
        grid=(b, nkv, s // tq),
        in_specs=[pl.BlockSpec((1, gw, tq), lambda bb, g, i: (bb, g, i)),
                  pl.BlockSpec((1, 1, s, HEAD_DIM), lambda bb, g, i: (bb, g, 0, 0)),
                  pl.BlockSpec((1, 1, nt, VT_ROWS_A, chunk), lambda bb, g, i: (bb, g, 0, 0, 0))],
        out_specs=pl.BlockSpec((1, tq, gw), lambda bb, g, i: (bb, i, g)),
        out_shape=jax.ShapeDtypeStruct((b, s, hd_all), BF16),
        scratch_shapes=[pltpu.VMEM((group, SUBLANES, tq), F32),
                        pltpu.VMEM((group, VT_ROWS_A // SUBLANES, SUBLANES, tq), F32)],
        compiler_params=_cparams(("parallel", "parallel", "parallel")),
        name="attn_a",
    )(qt, k, vt)


def _band_kernel(*refs, nkv, group, half_window, seq_len, has_sink, want_lse):
    q_ref, kp_ref, km_ref, kn_ref, vp_ref, vm_ref, vn_ref = refs[:7]
    rest = list(refs[7:])
    sink_ref = rest.pop(0) if has_sink else None
    o_ref = rest.pop(0)
    lse_ref = rest.pop(0) if want_lse else None

    tq = q_ref.shape[2]
    halo = kp_ref.shape[2]
    nk = tq + 2 * halo
    sub = SUB_BAND if nk % SUB_BAND == 0 else halo
    n_heads = nkv * group
    q0 = pl.program_id(2) * tq

    qt = q_ref[0, 0].T
    kcat = jnp.concatenate([kp_ref[0, 0], km_ref[0, 0], kn_ref[0, 0]], axis=0)
    vt = jnp.concatenate([vp_ref[0, 0], vm_ref[0, 0], vn_ref[0, 0]], axis=0).T

    kw = min(LANES, nkv * HEAD_DIM)
    zeros = jnp.zeros((HEAD_DIM, tq), qt.dtype)

    def padded_query(h):
        g = h // group
        slot = (g * HEAD_DIM % kw) // HEAD_DIM
        parts = [zeros] * (kw // HEAD_DIM)
        parts[slot] = qt[h * HEAD_DIM:(h + 1) * HEAD_DIM, :]
        return parts[0] if len(parts) == 1 else jnp.concatenate(parts, axis=0)

    q_pad = [padded_query(h) for h in range(n_heads)]
    units = [(kb, h) for kb in range(nk // sub) for h in range(n_heads)]

    def scores(idx):
        kb, h = units[idx]
        lane0 = (h // group) * HEAD_DIM // kw * kw
        return jnp.dot(kcat[kb * sub:(kb + 1) * sub, lane0:lane0 + kw], q_pad[h], preferred_element_type=F32)

    key_row = lax.broadcasted_iota(jnp.int32, (sub, tq), 0)
    query = lax.broadcasted_iota(jnp.int32, (sub, tq), 1)

    def valid_mask(kb):
        kpos = q0 - halo + kb * sub + key_row
        rel = kpos - (q0 + query)
        return (jnp.abs(rel) <= half_window) & (kpos >= 0) & (kpos < seq_len)

    if has_sink:
        m = [jnp.full((SUBLANES, tq), sink_ref[h] * LOG2E, F32) for h in range(n_heads)]
        first = (lax.broadcasted_iota(jnp.int32, (SUBLANES, tq), 0) == 0).astype(F32)
        l = [first] * n_heads
    else:
        m = [jnp.full((SUBLANES, tq), 0.1 * MASK_VALUE, F32)] * n_heads
        l = [jnp.zeros((SUBLANES, tq), F32)] * n_heads
    acc = [jnp.zeros((HEAD_DIM // SUBLANES, SUBLANES, tq), F32)] * n_heads

    look = min(LOOKAHEAD_BAND, len(units))
    pending = [scores(i) for i in range(look)]
    masks = {}
    for idx, (kb, h) in enumerate(units):
        if idx + look < len(units):
            pending.append(scores(idx + look))
        if kb not in masks:
            masks = {kb: valid_mask(kb)}
        g = h // group
        s3 = jnp.where(masks[kb], pending.pop(0), MASK_VALUE).reshape(sub // SUBLANES, SUBLANES, tq)
        m_new = jnp.maximum(m[h], _sublane_allmax(jnp.max(s3, axis=0)))
        alpha = jnp.exp2(m[h] - m_new)
        p3 = jnp.exp2(s3 - m_new[None])
        l[h] = alpha * l[h] + jnp.sum(p3, axis=0)
        pv = jnp.dot(vt[g * HEAD_DIM:(g + 1) * HEAD_DIM, kb * sub:(kb + 1) * sub],
                     p3.reshape(sub, tq).astype(BF16), preferred_element_type=F32)
        acc[h] = acc[h] * alpha[None] + pv.reshape(HEAD_DIM // SUBLANES, SUBLANES, tq)
        m[h] = m_new

    outs, lses = [], []
    for h in range(n_heads):
        l_row = jnp.sum(l[h], axis=0, keepdims=True)
        outs.append(acc[h].reshape(HEAD_DIM, tq) / l_row)
        if want_lse:
            lses.append(jnp.broadcast_to(m[h][0:1] + jnp.log2(l_row), (HEAD_DIM, tq)))
    o_ref[0, 0] = jnp.concatenate(outs, axis=0).T.astype(o_ref.dtype)
    if want_lse:
        lse_ref[0, 0] = jnp.concatenate(lses, axis=0).T


def _band_attention(q, k, v, *, nkv, group, half_window, sink=None, want_lse=False):
    b, n_seq, seq_len, wq = q.shape
    wk = nkv * HEAD_DIM
    halo = half_window
    tq = min(TQ_BAND, seq_len)
    per = tq // halo
    n_halo = seq_len // halo

    main = lambda width: pl.BlockSpec((1, 1, tq, width), lambda bb, r, i: (bb, r, i, 0))
    prev = lambda width: pl.BlockSpec(
        (1, 1, halo, width), lambda bb, r, i: (bb, r, jnp.maximum(i * per - 1, 0), 0))
    nxt = lambda width: pl.BlockSpec(
        (1, 1, halo, width), lambda bb, r, i: (bb, r, jnp.minimum((i + 1) * per, n_halo - 1), 0))
    in_specs = [main(wq), prev(wk), main(wk), nxt(wk), prev(wk), main(wk), nxt(wk)]
    args = [q, k, k, k, v, v, v]
    if sink is not None:
        in_specs.append(pl.BlockSpec(memory_space=pltpu.SMEM))
        args.append(sink)
    out_block = main(wq)
    out_shape = [jax.ShapeDtypeStruct(q.shape, BF16)]
    out_specs = [out_block]
    if want_lse:
        out_shape.append(jax.ShapeDtypeStruct(q.shape, F32))
        out_specs.append(out_block)
    res = pl.pallas_call(
        functools.partial(_band_kernel, nkv=nkv, group=group, half_window=half_window,
                          seq_len=seq_len, has_sink=sink is not None, want_lse=want_lse),
        grid=(b, n_seq, seq_len // tq),
        in_specs=in_specs,
        out_specs=out_specs,
        out_shape=out_shape,
        compiler_params=_cparams(("parallel", "parallel", "parallel")),
        name="band_attn",
    )(*args)
    return res if want_lse else res[0]


def _softplus(x):
    return jnp.maximum(x, 0.0) + jnp.log1p(jnp.exp(-jnp.abs(x)))


def _lru_kernel(xf_ref, xfp_ref, xfn_ref, xb_ref, xbp_ref, xbn_ref,
                cw_ref, cb_ref, gw_ref, gb_ref, lam_ref,
                hf_ref, hb_ref, a_scr, u_scr, carry_ref):
    i = pl.program_id(1)
    nt = pl.num_programs(1)
    tt = xf_ref.shape[1]
    nblk = tt // SUBLANES

    @pl.when(i == 0)
    def _():
        carry_ref[...] = jnp.zeros(carry_ref.shape, F32)

    row = lax.broadcasted_iota(jnp.int32, (SUBLANES, LRU_WIDTH), 0)

    def prepare(d, x_ref, xp_ref, xn_ref, tile):
        prev_rows = jnp.where(tile > 0, xp_ref[0], 0.0)
        next_rows = jnp.where(tile < nt - 1, xn_ref[0], 0.0)
        xp = jnp.concatenate([prev_rows, x_ref[0], next_rows], axis=0)
        xc = cb_ref[...]
        base = SUBLANES - CONV_LEFT
        for jj in range(CONV_W):
            xc = xc + xp[base + jj:base + jj + tt, :] * cw_ref[jj:jj + 1, :]
        xcb = xc.astype(BF16)
        r = jax.nn.sigmoid(jnp.dot(xcb, gw_ref[d, 0], preferred_element_type=F32) + gb_ref[d, 0:1, :])
        gi = jax.nn.sigmoid(jnp.dot(xcb, gw_ref[d, 1], preferred_element_type=F32) + gb_ref[d, 1:2, :])
        log_a = -LRU_C * r * _softplus(-lam_ref[d:d + 1, :])
        a = jnp.exp(log_a)
        a_scr[d] = a
        u_scr[d] = jnp.sqrt(-jnp.tanh(log_a) * (a * a + 1.0)) * (gi * xc)

    prepare(0, xf_ref, xfp_ref, xfn_ref, i)
    prepare(1, xb_ref, xbp_ref, xbn_ref, nt - 1 - i)

    def scan_block(d, blk, carry, out_ref):
        a8 = a_scr[d, pl.ds(blk * SUBLANES, SUBLANES), :]
        u8 = u_scr[d, pl.ds(blk * SUBLANES, SUBLANES), :]
        for step in (1, 2, 4):
            if d == 0:
                shift, ident = step, row < step
            else:
                shift, ident = SUBLANES - step, row >= SUBLANES - step
            u_sh = jnp.where(ident, 0.0, pltpu.roll(u8, shift, 0))
            a_sh = jnp.where(ident, 1.0, pltpu.roll(a8, shift, 0))
            u8 = u8 + a8 * u_sh
            a8 = a8 * a_sh
        h8 = u8 + a8 * carry
        out_ref[0, pl.ds(blk * SUBLANES, SUBLANES), :] = h8
        last = h8[SUBLANES - 1:SUBLANES, :] if d == 0 else h8[0:1, :]
        return jnp.broadcast_to(last, (SUBLANES, LRU_WIDTH))

    def body(kb, carries):
        cf, cb = carries
        cf = scan_block(0, kb, cf, hf_ref)
        cb = scan_block(1, nblk - 1 - kb, cb, hb_ref)
        return cf, cb

    cf, cb = lax.fori_loop(0, nblk, body, (carry_ref[0], carry_ref[1]))
    carry_ref[0] = cf
    carry_ref[1] = cb


def _lru(bx, conv_w, conv_b, gate_w_dense, gate_b, lam):
    b, s, w = bx.shape
    tt = min(TT_LRU, s)
    nt = s // tt
    per = tt // SUBLANES
    n8 = s // SUBLANES
    fwd = lambda i: i
    bwd = lambda i: nt - 1 - i

    def specs(tile_of):
        return [
            pl.BlockSpec((1, tt, w), lambda bb, i: (bb, tile_of(i), 0)),
            pl.BlockSpec((1, SUBLANES, w), lambda bb, i: (bb, jnp.maximum(tile_of(i) * per - 1, 0), 0)),
            pl.BlockSpec((1, SUBLANES, w), lambda bb, i: (bb, jnp.minimum((tile_of(i) + 1) * per, n8 - 1), 0)),
        ]

    return pl.pallas_call(
        _lru_kernel,
        grid=(b, nt),
        in_specs=specs(fwd) + specs(bwd) + [
            _resident(conv_w.shape), _resident(conv_b.shape), _resident(gate_w_dense.shape),
            _resident(gate_b.shape), _resident(lam.shape)],
        out_specs=[pl.BlockSpec((1, tt, w), lambda bb, i: (bb, i, 0)),
                   pl.BlockSpec((1, tt, w), lambda bb, i: (bb, nt - 1 - i, 0))],
        out_shape=[jax.ShapeDtypeStruct((b, s, w), F32), jax.ShapeDtypeStruct((b, s, w), F32)],
        scratch_shapes=[pltpu.VMEM((2, tt, w), F32), pltpu.VMEM((2, tt, w), F32),
                        pltpu.VMEM((2, SUBLANES, w), F32)],
        compiler_params=_cparams(("parallel", "arbitrary")),
        name="lru",
    )(bx, bx, bx, bx, bx, bx, conv_w, conv_b, gate_w_dense, gate_b, lam)


def _gelu_tanh(x):
    return 0.5 * x * (1.0 + jnp.tanh(math.sqrt(2.0 / math.pi) * (x + 0.044715 * (x * x * x))))


def _merge_kernel(x_ref, ya_ref, hf_ref, hb_ref, by_ref, yc_ref,
                  o0_ref, o1_ref, o2_ref, l0_ref, l1_ref, l2_ref, g_ref, wg_ref, gbias_ref,
                  wa_ref, wb_ref, wc_ref, wd_ref, wo_ref, out_ref, perm_scr):
    d_model = x_ref.shape[2]
    tm = x_ref.shape[1]
    x = x_ref[0]
    var = jnp.mean(x * x, axis=-1, keepdims=True)
    hn = (x * lax.rsqrt(var + EPS) * g_ref[...]).astype(BF16)

    def natural(ref, slot):
        dil = ref.shape[1]
        if dil == 1:
            return ref[0, 0].astype(F32)
        for r in range(dil):
            rows = ref[0, r].astype(F32)
            for j in range(rows.shape[1] // LANES):
                perm_scr[slot, j, pl.ds(r, tm // dil, stride=dil), :] = rows[:, j * LANES:(j + 1) * LANES]
        return jnp.concatenate([perm_scr[slot, j] for j in range(perm_scr.shape[1])], axis=1)

    def gate(kk):
        logits = jnp.dot(hn, wg_ref[:, kk * d_model:(kk + 1) * d_model], preferred_element_type=F32)
        return jax.nn.sigmoid(logits + gbias_ref[kk:kk + 1, :])

    merged = gate(0) * jnp.dot(ya_ref[0], wa_ref[...], preferred_element_type=F32)

    yb = (hf_ref[0] + hb_ref[0]) * _gelu_tanh(by_ref[0].astype(F32))
    merged = merged + gate(1) * jnp.dot(yb.astype(BF16), wb_ref[...], preferred_element_type=F32)

    merged = merged + gate(2) * jnp.dot(yc_ref[0], wc_ref[...], preferred_element_type=F32)

    l0, l1, l2 = natural(l0_ref, 0), natural(l1_ref, 0), natural(l2_ref, 1)
    o0, o1, o2 = natural(o0_ref, 2), natural(o1_ref, 2), natural(o2_ref, 3)
    mx = jnp.maximum(jnp.maximum(l0, l1), l2)
    e0, e1, e2 = jnp.exp2(l0 - mx), jnp.exp2(l1 - mx), jnp.exp2(l2 - mx)
    yd = (e0 * o0 + e1 * o1 + e2 * o2) / (e0 + e1 + e2)
    merged = merged + gate(3) * jnp.dot(yd.astype(BF16), wd_ref[...], preferred_element_type=F32)

    out_ref[0] = x + jnp.dot(merged.astype(BF16), wo_ref[...], preferred_element_type=F32)


def _merge(x, ya, hf, hb, by, yc, o_d, lse_d, g, wg, gbias, wa, wb, wc, wd, wo):
    b, s, d = x.shape
    tm = min(TM_MERGE, s)
    tok = lambda width: pl.BlockSpec((1, tm, width), lambda bb, i: (bb, i, 0))
    dw = D_HEADS * HEAD_DIM
    res = [pl.BlockSpec((1, dil, tm // dil, dw), lambda bb, i: (bb, 0, i, 0)) for _, dil in D_PATTERNS]
    return pl.pallas_call(
        _merge_kernel,
        grid=(b, s // tm),
        in_specs=[tok(d), tok(512), tok(LRU_WIDTH), tok(LRU_WIDTH), tok(LRU_WIDTH), tok(512),
                  *res, *res, _resident(g.shape), _resident(wg.shape),
                  _resident(gbias.shape), _resident(wa.shape), _resident(wb.shape),
                  _resident(wc.shape), _resident(wd.shape), _resident(wo.shape)],
        out_specs=tok(d),
        out_shape=jax.ShapeDtypeStruct((b, s, d), F32),
        scratch_shapes=[pltpu.VMEM((4, dw // LANES, tm, LANES), F32)],
        compiler_params=_cparams(("parallel", "parallel")),
        name="merge",
    )(x, ya, hf, hb, by, yc, *o_d, *lse_d, g, wg, gbias, wa, wb, wc, wd, wo)


def _mlp_kernel(x_ref, g_ref, w1_ref, w2_ref, gf_ref, out_ref, *, final_norm):
    x = x_ref[0]
    var = jnp.mean(x * x, axis=-1, keepdims=True)
    hn = (x * lax.rsqrt(var + EPS) * g_ref[...]).astype(BF16)
    hidden = w1_ref.shape[1]
    acc = x
    for c in range(hidden // MLP_CHUNK):
        h = jnp.dot(hn, w1_ref[:, c * MLP_CHUNK:(c + 1) * MLP_CHUNK], preferred_element_type=F32)
        h = jnp.square(jnp.maximum(h, 0.0)).astype(BF16)
        acc = acc + jnp.dot(h, w2_ref[c * MLP_CHUNK:(c + 1) * MLP_CHUNK, :], preferred_element_type=F32)
    if final_norm:
        var = jnp.mean(acc * acc, axis=-1, keepdims=True)
        acc = acc * lax.rsqrt(var + EPS) * gf_ref[...]
    out_ref[0] = acc


def _mlp(x, g, w1, w2, g_final, final_norm):
    b, s, d = x.shape
    tm = min(TM_MLP, s)
    tok = pl.BlockSpec((1, tm, d), lambda bb, i: (bb, i, 0))
    return pl.pallas_call(
        functools.partial(_mlp_kernel, final_norm=final_norm),
        grid=(b, s // tm),
        in_specs=[tok, _resident(g.shape), _resident(w1.shape), _resident(w2.shape), _resident(g_final.shape)],
        out_specs=tok,
        out_shape=jax.ShapeDtypeStruct((b, s, d), F32),
        compiler_params=_cparams(("parallel", "parallel")),
        name="mlp",
    )(x, g, w1, w2, g_final)


def _dense_block_diag(w):
    nb, bw, _ = w.shape
    eye = jnp.eye(nb, dtype=w.dtype)
    return jnp.einsum('ncd,nm->ncmd', w, eye).reshape(nb * bw, nb * bw)


def kernel(x, norm_mix_g, w_in, gate_bias, qk_norm_g, conv_w, conv_b, lru_gate_w, lru_gate_b, lru_lambda, sink_logit, w_proj_a, w_proj_b, w_proj_c, w_proj_d, w_out, norm_mlp_g, w_mlp1, w_mlp2, norm_final_g):
    b, s, d = x.shape
    depth = w_in.shape[0]
    ax_t, seq_t = _rope_tables(s)
    dw = D_HEADS * HEAD_DIM
    n_groups = len(D_PATTERNS)

    for l in range(depth):
        qk_gain = jnp.tile(qk_norm_g[l], (1, LANES // HEAD_DIM))
        w_in_l = w_in[l].astype(BF16)
        outs = _inproj(x, norm_mix_g[l][None, :], w_in_l[:, :N_IN], qk_gain, ax_t, seq_t)
        qta, ka, vta, bx, by, qc, kc, vc = outs[:8]
        qd, kd, vd = outs[8:8 + n_groups], outs[8 + n_groups:8 + 2 * n_groups], outs[8 + 2 * n_groups:8 + 3 * n_groups]

        ya = _attn_a(qta, ka, vta)

        gate_w_dense = jnp.stack([jnp.stack([_dense_block_diag(lru_gate_w[l, dd, gg]) for gg in range(2)])
                                  for dd in range(2)]).astype(BF16)
        hf, hb = _lru(bx, conv_w[l], conv_b[l][None, :], gate_w_dense, lru_gate_b[l], lru_lambda[l])

        yc = _band_attention(qc[:, None], kc[:, None], vc[:, None], nkv=C_KV_HEADS,
                             group=C_HEADS // C_KV_HEADS, half_window=C_HALF_WINDOW, sink=sink_logit[l])[:, 0]

        o_d, lse_d = [], []
        for gi, (window, dil) in enumerate(D_PATTERNS):
            o, lse = _band_attention(qd[gi], kd[gi], vd[gi], nkv=D_HEADS, group=1,
                                     half_window=window // (2 * dil), want_lse=True)
            o_d.append(o)
            lse_d.append(lse)

        x = _merge(x, ya, hf, hb, by, yc, o_d, lse_d, norm_mix_g[l][None, :], w_in_l[:, N_IN:], gate_bias[l],
                   w_proj_a[l].astype(BF16), w_proj_b[l].astype(BF16), w_proj_c[l].astype(BF16),
                   w_proj_d[l].astype(BF16), w_out[l].astype(BF16))
        x = _mlp(x, norm_mlp_g[l][None, :], w_mlp1[l].astype(BF16), w_mlp2[l].astype(BF16),
                 norm_final_g[None, :], final_norm=(l == depth - 1))
    return x
```

```python
import functools
import math

import jax
import jax.numpy as jnp
from jax import lax
from jax.experimental import pallas as pl
from jax.experimental.pallas import tpu as pltpu

F32 = jnp.float32
BF16 = jnp.bfloat16

HEAD_DIM = 64
ROPE_THETA = 10000.0
GRID_W = 64
EPS = 1e-6
MASK_VALUE = -1e30
LOG2E = math.log2(math.e)

A_HEADS, A_KV_HEADS = 8, 2
LRU_WIDTH, LRU_BLOCKS, LRU_C = 512, 8, 8.0
CONV_W, CONV_LEFT = 4, 2
C_HEADS, C_KV_HEADS, C_HALF_WINDOW = 8, 2, 128
D_PATTERNS = ((128, 1), (512, 4), (2048, 16))
D_HEADS = 4
N_BRANCH = 4

LANES = 128
SUBLANES = 8
VMEM_LIMIT_BYTES = 56 * 1024 * 1024

TM_INPROJ = 512
TQ_ATTN_A = 256
SUB_ATTN_A = 128
LOOKAHEAD_ATTN_A = 14
LOOKAHEAD_BAND = 6
SUB_BAND = 128
CHUNKS_PER_BODY_ATTN_A = 16
VT_ROWS_A = HEAD_DIM + 16
TQ_BAND = 256
BAND_TILES_PER_STEP = 4
TT_LRU = 512
TM_MERGE = 512
TM_MLP = 512
MLP_CHUNK = 1024

Q_SCALE = HEAD_DIM ** -0.5 * LOG2E


def _cparams(sem):
    return pltpu.CompilerParams(dimension_semantics=sem, vmem_limit_bytes=VMEM_LIMIT_BYTES)


def _resident(shape):
    nd = len(shape)
    return pl.BlockSpec(shape, lambda *_: (0,) * nd, pipeline_mode=pl.Buffered(1))


def _rope_tables(seq):
    pos = jnp.arange(seq, dtype=F32)
    lane = jnp.arange(LANES)
    d = lane % HEAD_DIM

    inv = ROPE_THETA ** (-jnp.arange(0, HEAD_DIM, 2, dtype=F32) / HEAD_DIM)
    ang = pos[:, None] * inv[None, :]
    f = d % (HEAD_DIM // 2)
    cos_s, sin_s = jnp.cos(ang)[:, f], jnp.sin(ang)[:, f]
    lo = (d < HEAD_DIM // 2)[None, :]
    seq_t = (cos_s, jnp.where(lo, -sin_s, 0.0), jnp.where(lo, 0.0, sin_s))

    quarter = HEAD_DIM // 4
    inv_ax = ROPE_THETA ** (-jnp.arange(0, HEAD_DIM // 2, 2, dtype=F32) / (HEAD_DIM // 2))
    row_pos = (jnp.arange(seq) // GRID_W).astype(F32)
    col_pos = (jnp.arange(seq) % GRID_W).astype(F32)
    e = d % (HEAD_DIM // 2)
    fa = e % quarter
    is_col = (d >= HEAD_DIM // 2)[None, :]
    ang_ax = jnp.where(is_col, col_pos[:, None], row_pos[:, None]) * inv_ax[fa][None, :]
    cos_a, sin_a = jnp.cos(ang_ax), jnp.sin(ang_ax)
    lo_a = (e < quarter)[None, :]
    ax_t = (cos_a, jnp.where(lo_a, -sin_a, 0.0), jnp.where(lo_a, 0.0, sin_a))
    return ax_t, seq_t


def _rotate(x, tables, half):
    cos, sin_a, sin_b = tables
    return x * cos + pltpu.roll(x, LANES - half, 1) * sin_a + pltpu.roll(x, half, 1) * sin_b


def _split_dot(x, m):
    hi = x.astype(BF16)
    lo = (x - hi.astype(F32)).astype(BF16)
    return (jnp.dot(hi, m, preferred_element_type=F32) + jnp.dot(lo, m, preferred_element_type=F32))


def _head_mean_matrix():
    r = lax.broadcasted_iota(jnp.int32, (LANES, LANES), 0) // HEAD_DIM
    c = lax.broadcasted_iota(jnp.int32, (LANES, LANES), 1) // HEAD_DIM
    return jnp.where(r == c, 1.0 / HEAD_DIM, 0.0).astype(BF16)


def _head_rmsnorm(x, gain, mean_mat):
    var = _split_dot(x * x, mean_mat)
    return x * lax.rsqrt(var + EPS) * gain


IN_WIDTHS = (512, 128, 128, 512, 512, 512, 128, 128, 768, 768, 768)
IN_OFFSETS = tuple(sum(IN_WIDTHS[:i]) for i in range(len(IN_WIDTHS)))
N_IN = sum(IN_WIDTHS)


def _inproj_kernel(x_ref, g_ref, w_ref, qkg_ref,
                   ca_ref, saa_ref, sab_ref, cs_ref, ssa_ref, ssb_ref,
                   qta_ref, ka_ref, vta_ref, bx_ref, by_ref,
                   qc_ref, kc_ref, vc_ref, *rest):
    d_refs, perm_scr = rest[:9], rest[9]
    x = x_ref[0]
    var = jnp.mean(x * x, axis=-1, keepdims=True)
    hn = (x * lax.rsqrt(var + EPS) * g_ref[...]).astype(BF16)

    def proj(seg, lo=0, width=None):
        off = IN_OFFSETS[seg] + lo
        width = IN_WIDTHS[seg] if width is None else width
        return jnp.dot(hn, w_ref[:, off:off + width], preferred_element_type=F32)

    ax_t = (ca_ref[...], saa_ref[...], sab_ref[...])
    seq_t = (cs_ref[...], ssa_ref[...], ssb_ref[...])
    mean_mat = _head_mean_matrix()
    gq = qkg_ref[0:1, :]
    gk = qkg_ref[1:2, :]

    aq = proj(0)
    for j in range(IN_WIDTHS[0] // LANES):
        piece = _head_rmsnorm(aq[:, j * LANES:(j + 1) * LANES], gq, mean_mat)
        piece = _rotate(piece, ax_t, HEAD_DIM // 4) * Q_SCALE
        qta_ref[0, j * LANES:(j + 1) * LANES, :] = piece.T.astype(BF16)
    ak = _rotate(_head_rmsnorm(proj(1), gk, mean_mat), ax_t, HEAD_DIM // 4).astype(BF16)
    for g in range(A_KV_HEADS):
        ka_ref[0, g] = ak[:, g * HEAD_DIM:(g + 1) * HEAD_DIM]
    avt = proj(2).T.astype(BF16)
    pad_row = lax.broadcasted_iota(jnp.int32, (VT_ROWS_A - HEAD_DIM, avt.shape[1]), 0)
    ones_rows = (pad_row == 0).astype(BF16)
    for g in range(A_KV_HEADS):
        vta_ref[0, g, 0] = jnp.concatenate([avt[g * HEAD_DIM:(g + 1) * HEAD_DIM, :], ones_rows], axis=0)

    bx_ref[0] = proj(3)
    by_ref[0] = proj(4).astype(BF16)

    def rope_store(ref, seg, scale):
        acc = proj(seg)
        for j in range(IN_WIDTHS[seg] // LANES):
            piece = _rotate(acc[:, j * LANES:(j + 1) * LANES], seq_t, HEAD_DIM // 2)
            if scale is not None:
                piece = piece * scale
            ref[0, :, j * LANES:(j + 1) * LANES] = piece.astype(BF16)

    rope_store(qc_ref, 5, Q_SCALE)
    rope_store(kc_ref, 6, None)
    vc_ref[0] = proj(7).astype(BF16)

    tm = x.shape[0]
    dw = D_HEADS * HEAD_DIM
    for gi, (_, dil) in enumerate(D_PATTERNS):
        for ti, (seg, scale, rope) in enumerate(((8, Q_SCALE, True), (9, None, True), (10, None, False))):
            ref = d_refs[ti * len(D_PATTERNS) + gi]
            acc = proj(seg, gi * dw, dw)
            for j in range(dw // LANES):
                piece = acc[:, j * LANES:(j + 1) * LANES]
                if rope:
                    piece = _rotate(piece, seq_t, HEAD_DIM // 2)
                if scale is not None:
                    piece = piece * scale
                if dil == 1:
                    ref[0, 0, :, j * LANES:(j + 1) * LANES] = piece.astype(BF16)
                else:
                    perm_scr[j] = piece
            if dil > 1:
                for r in range(dil):
                    rows = [perm_scr[j, pl.ds(r, tm // dil, stride=dil), :] for j in range(dw // LANES)]
                    ref[0, r] = jnp.concatenate(rows, axis=1).astype(BF16)


def _inproj(x, g, w_bf16, qk_gain128, ax_t, seq_t):
    b, s, d = x.shape
    tm = min(TM_INPROJ, s)
    nt = s // tm
    tok = lambda width: pl.BlockSpec((1, tm, width), lambda i, bb: (bb, i, 0))
    tab = pl.BlockSpec((tm, LANES), lambda i, bb: (i, 0))
    dw = D_HEADS * HEAD_DIM
    d_shapes = tuple(jax.ShapeDtypeStruct((b, dil, s // dil, dw), BF16) for _, dil in D_PATTERNS)
    d_specs = tuple(pl.BlockSpec((1, dil, tm // dil, dw), lambda i, bb: (bb, 0, i, 0)) for _, dil in D_PATTERNS)
    out_shape = (
        jax.ShapeDtypeStruct((b, A_HEADS * HEAD_DIM, s), BF16),
        jax.ShapeDtypeStruct((b, A_KV_HEADS, s, HEAD_DIM), BF16),
        jax.ShapeDtypeStruct((b, A_KV_HEADS, nt, VT_ROWS_A, tm), BF16),
        jax.ShapeDtypeStruct((b, s, LRU_WIDTH), F32),
        jax.ShapeDtypeStruct((b, s, LRU_WIDTH), BF16),
        jax.ShapeDtypeStruct((b, s, 512), BF16),
        jax.ShapeDtypeStruct((b, s, 128), BF16),
        jax.ShapeDtypeStruct((b, s, 128), BF16),
    ) + d_shapes * 3
    out_specs = (
        pl.BlockSpec((1, A_HEADS * HEAD_DIM, tm), lambda i, bb: (bb, 0, i)),
        pl.BlockSpec((1, A_KV_HEADS, tm, HEAD_DIM), lambda i, bb: (bb, 0, i, 0)),
        pl.BlockSpec((1, A_KV_HEADS, 1, VT_ROWS_A, tm), lambda i, bb: (bb, 0, i, 0, 0)),
        tok(LRU_WIDTH), tok(LRU_WIDTH), tok(512), tok(128), tok(128),
    ) + d_specs * 3
    return pl.pallas_call(
        _inproj_kernel,
        grid=(nt, b),
        in_specs=[tok(d), _resident((1, d)), _resident((d, N_IN)), _resident((2, LANES)),
                  tab, tab, tab, tab, tab, tab],
        out_specs=out_specs,
        out_shape=out_shape,
        scratch_shapes=[pltpu.VMEM((dw // LANES, tm, LANES), F32)],
        compiler_params=_cparams(("parallel", "parallel")),
        name="inproj",
    )(x, g, w_bf16, qk_gain128, *ax_t, *seq_t)


def _sublane_allmax(x):
    for shift in (4, 2, 1):
        x = jnp.maximum(x, pltpu.roll(x, shift, 0))
    return x


def _attn_a_kernel(q_ref, k_ref, v_ref, o_ref, m_ref, acc_ref, *, group, sub):
    n_chunks, chunk = v_ref.shape[2], v_ref.shape[4]
    tq = q_ref.shape[2]
    m_ref[...] = jnp.full(m_ref.shape, MASK_VALUE, F32)
    acc_ref[...] = jnp.zeros(acc_ref.shape, F32)

    per_body = min(CHUNKS_PER_BODY_ATTN_A, n_chunks)
    n_sub = chunk // sub
    pair = 2 if n_sub % 2 == 0 else 1
    units = [(cc, u, h) for cc in range(per_body) for up in range(n_sub // pair) for h in range(group)
             for u in range(up * pair, (up + 1) * pair)]

    def body(it, carry):
        c0 = it * per_body

        def scores(idx):
            cc, u, h = units[idx]
            row0 = pl.multiple_of((c0 + cc) * chunk + u * sub, sub)
            k = k_ref[0, 0, pl.ds(row0, sub), :]
            qt = q_ref[0, h * HEAD_DIM:(h + 1) * HEAD_DIM, :]
            return jnp.dot(k, qt, preferred_element_type=F32)

        pending = [scores(i) for i in range(LOOKAHEAD_ATTN_A)]
        probs = []
        for idx, (cc, u, h) in enumerate(units):
            if idx + LOOKAHEAD_ATTN_A < len(units):
                pending.append(scores(idx + LOOKAHEAD_ATTN_A))
            s3 = pending.pop(0).reshape(sub // SUBLANES, SUBLANES, tq)
            if not probs:
                m_start = m_ref[h]
                m_old = m_start
            m_new = jnp.maximum(m_old, _sublane_allmax(jnp.max(s3, axis=0)))
            if probs:
                beta = jnp.exp2(m_old - m_new).astype(BF16)
                beta = jnp.concatenate([beta, beta], axis=0)[None]
                probs = [(p.reshape(sub // 16, 16, tq) * beta).reshape(sub, tq) for p in probs]
            probs.append(jnp.exp2(s3 - m_new[None]).reshape(sub, tq).astype(BF16))
            m_old = m_new
            if len(probs) == pair:
                u0 = u - (pair - 1)
                vt = v_ref[0, 0, c0 + cc, :, u0 * sub:(u + 1) * sub]
                pv = jnp.dot(vt, jnp.concatenate(probs, axis=0), preferred_element_type=F32)
                alpha = jnp.exp2(m_start - m_new)
                acc_ref[h] = acc_ref[h] * alpha[None] + pv.reshape(VT_ROWS_A // SUBLANES, SUBLANES, tq)
                m_ref[h] = m_new
                probs = []
        return carry

    lax.fori_loop(0, n_chunks // per_body, body, 0)
    outs = []
    for h in range(group):
        acc = acc_ref[h].reshape(VT_ROWS_A, tq)
        outs.append(acc[:HEAD_DIM] / acc[HEAD_DIM:HEAD_DIM + 1])
    o_ref[0] = jnp.concatenate(outs, axis=0).T.astype(o_ref.dtype)


def _attn_a(qt, k, vt):
    b, hd_all, s = qt.shape
    nkv, nt, chunk = vt.shape[1], vt.shape[2], vt.shape[4]
    group = hd_all // HEAD_DIM // nkv
    tq = min(TQ_ATTN_A, s)
    gw = group * HEAD_DIM
    return pl.pallas_call(
        functools.partial(_attn_a_kernel, group=group, sub=min(SUB_ATTN_A, chunk)),
        grid=(b, nkv, s // tq),
        in_specs=[pl.BlockSpec((1, gw, tq), lambda bb, g, i: (bb, g, i)),
                  pl.BlockSpec((1, 1, s, HEAD_DIM), lambda bb, g, i: (bb, g, 0, 0)),
                  pl.BlockSpec((1, 1, nt, VT_ROWS_A, chunk), lambda bb, g, i: (bb, g, 0, 0, 0))],
        out_specs=pl.BlockSpec((1, tq, gw), lambda bb, g, i: (bb, i, g)),
        out_shape=jax.ShapeDtypeStruct((b, s, hd_all), BF16),
        scratch_shapes=[pltpu.VMEM((group, SUBLANES, tq), F32),
                        pltpu.VMEM((group, VT_ROWS_A // SUBLANES, SUBLANES, tq), F32)],
        compiler_params=_cparams(("parallel", "parallel", "parallel")),
        name="attn_a",
    )(qt, k, vt)


def _band_kernel(*refs, nkv, group, half_window, seq_len, has_sink, want_lse):
    q_ref, kp_ref, km_ref, kn_ref, vp_ref, vm_ref, vn_ref = refs[:7]
    rest = list(refs[7:])
    sink_ref = rest.pop(0) if has_sink else None
    o_ref = rest.pop(0)
    lse_ref = rest.pop(0) if want_lse else None

    tq_blk = q_ref.shape[2]
    tq = min(TQ_BAND, tq_blk)
    halo = kp_ref.shape[2]
    nk = tq + 2 * halo
    sub = SUB_BAND if nk % SUB_BAND == 0 else halo
    n_heads = nkv * group
    blk0 = pl.program_id(2) * tq_blk

    kcat = jnp.concatenate([kp_ref[0, 0], km_ref[0, 0], kn_ref[0, 0]], axis=0)
    vt = jnp.concatenate([vp_ref[0, 0], vm_ref[0, 0], vn_ref[0, 0]], axis=0).T

    kw = min(LANES, nkv * HEAD_DIM)
    zeros = jnp.zeros((HEAD_DIM, tq), kcat.dtype)
    q_pads = {}

    def padded_queries(j):
        if j not in q_pads:
            qt = q_ref[0, 0, j * tq:(j + 1) * tq, :].T
            pads = []
            for h in range(n_heads):
                slot = ((h // group) * HEAD_DIM % kw) // HEAD_DIM
                parts = [zeros] * (kw // HEAD_DIM)
                parts[slot] = qt[h * HEAD_DIM:(h + 1) * HEAD_DIM, :]
                pads.append(parts[0] if len(parts) == 1 else jnp.concatenate(parts, axis=0))
            q_pads.clear()
            q_pads[j] = pads
        return q_pads[j]

    units = [(j, kb, h) for j in range(tq_blk // tq) for kb in range(nk // sub) for h in range(n_heads)]

    cw = LANES if tq % LANES == 0 else tq
    n_ch = tq // cw

    def query_chunks(kb):
        k_lo = kb * sub - halo
        lo = max(0, (k_lo - half_window) // cw)
        hi = min(n_ch, -(-(k_lo + sub + half_window) // cw))
        return list(range(lo, hi))

    def scores(idx):
        j, kb, h = units[idx]
        row0 = j * tq + kb * sub
        lane0 = (h // group) * HEAD_DIM // kw * kw
        cs = query_chunks(kb)
        return jnp.dot(kcat[row0:row0 + sub, lane0:lane0 + kw],
                       padded_queries(j)[h][:, cs[0] * cw:(cs[-1] + 1) * cw], preferred_element_type=F32)

    key_row = lax.broadcasted_iota(jnp.int32, (sub, cw), 0)
    query = lax.broadcasted_iota(jnp.int32, (sub, cw), 1)

    def valid_mask(j, kb, c):
        kpos = blk0 + j * tq - halo + kb * sub + key_row
        rel = kpos - (blk0 + j * tq + c * cw + query)
        return (jnp.abs(rel) <= half_window) & (kpos >= 0) & (kpos < seq_len)

    def start_state():
        if has_sink:
            m0 = [jnp.full((SUBLANES, cw), sink_ref[h] * LOG2E, F32) for h in range(n_heads)]
            l0 = (lax.broadcasted_iota(jnp.int32, (SUBLANES, cw), 0) == 0).astype(F32)
        else:
            m0 = [jnp.full((SUBLANES, cw), 0.1 * MASK_VALUE, F32)] * n_heads
            l0 = jnp.zeros((SUBLANES, cw), F32)
        a0 = jnp.zeros((HEAD_DIM // SUBLANES, SUBLANES, cw), F32)
        return ([[m0[h]] * n_ch for h in range(n_heads)], [[l0] * n_ch for _ in range(n_heads)],
                [[a0] * n_ch for _ in range(n_heads)])

    def join(parts):
        return parts[0] if len(parts) == 1 else jnp.concatenate(parts, axis=1)

    def finish(j, m, l, acc):
        outs, lses = [], []
        for h in range(n_heads):
            l_row = join([jnp.sum(x, axis=0, keepdims=True) for x in l[h]])
            outs.append(join([x.reshape(HEAD_DIM, cw) for x in acc[h]]) / l_row)
            if want_lse:
                lse = join([x[0:1] for x in m[h]]) + jnp.log2(l_row)
                lses.append(jnp.broadcast_to(lse, (HEAD_DIM, tq)))
        o_ref[0, 0, j * tq:(j + 1) * tq, :] = jnp.concatenate(outs, axis=0).T.astype(o_ref.dtype)
        if want_lse:
            lse_ref[0, 0, j * tq:(j + 1) * tq, :] = jnp.concatenate(lses, axis=0).T

    look = min(LOOKAHEAD_BAND, len(units))
    pending = [scores(i) for i in range(look)]
    mask_key, masks = None, None
    for idx, (j, kb, h) in enumerate(units):
        if idx + look < len(units):
            pending.append(scores(idx + look))
        if kb == 0 and h == 0:
            m, l, acc = start_state()
        cs = query_chunks(kb)
        if mask_key != (j, kb):
            mask_key, masks = (j, kb), [valid_mask(j, kb, c) for c in cs]
        g = h // group
        s = pending.pop(0)
        probs, alphas = [], []
        for i, c in enumerate(cs):
            s_c = s if len(cs) == 1 else s[:, i * cw:(i + 1) * cw]
            s3 = jnp.where(masks[i], s_c, MASK_VALUE).reshape(sub // SUBLANES, SUBLANES, cw)
            m_new = jnp.maximum(m[h][c], _sublane_allmax(jnp.max(s3, axis=0)))
            alpha = jnp.exp2(m[h][c] - m_new)
            p3 = jnp.exp2(s3 - m_new[None])
            l[h][c] = alpha * l[h][c] + jnp.sum(p3, axis=0)
            m[h][c] = m_new
            probs.append(p3.reshape(sub, cw).astype(BF16))
            alphas.append(alpha)
        col0 = j * tq + kb * sub
        pv = jnp.dot(vt[g * HEAD_DIM:(g + 1) * HEAD_DIM, col0:col0 + sub], join(probs),
                     preferred_element_type=F32)
        for i, c in enumerate(cs):
            pv_c = pv if len(cs) == 1 else pv[:, i * cw:(i + 1) * cw]
            acc[h][c] = acc[h][c] * alphas[i][None] + pv_c.reshape(HEAD_DIM // SUBLANES, SUBLANES, cw)
        if kb == nk // sub - 1 and h == n_heads - 1:
            finish(j, m, l, acc)


def _band_attention(q, k, v, *, nkv, group, half_window, sink=None, want_lse=False):
    b, n_seq, seq_len, wq = q.shape
    wk = nkv * HEAD_DIM
    halo = half_window
    tq = min(TQ_BAND * BAND_TILES_PER_STEP, seq_len)
    per = tq // halo
    n_halo = seq_len // halo

    main = lambda width: pl.BlockSpec((1, 1, tq, width), lambda bb, r, i: (bb, r, i, 0))
    prev = lambda width: pl.BlockSpec(
        (1, 1, halo, width), lambda bb, r, i: (bb, r, jnp.maximum(i * per - 1, 0), 0))
    nxt = lambda width: pl.BlockSpec(
        (1, 1, halo, width), lambda bb, r, i: (bb, r, jnp.minimum((i + 1) * per, n_halo - 1), 0))
    in_specs = [main(wq), prev(wk), main(wk), nxt(wk), prev(wk), main(wk), nxt(wk)]
    args = [q, k, k, k, v, v, v]
    if sink is not None:
        in_specs.append(pl.BlockSpec(memory_space=pltpu.SMEM))
        args.append(sink)
    out_block = main(wq)
    out_shape = [jax.ShapeDtypeStruct(q.shape, BF16)]
    out_specs = [out_block]
    if want_lse:
        out_shape.append(jax.ShapeDtypeStruct(q.shape, F32))
        out_specs.append(out_block)
    res = pl.pallas_call(
        functools.partial(_band_kernel, nkv=nkv, group=group, half_window=half_window,
                          seq_len=seq_len, has_sink=sink is not None, want_lse=want_lse),
        grid=(b, n_seq, seq_len // tq),
        in_specs=in_specs,
        out_specs=out_specs,
        out_shape=out_shape,
        compiler_params=_cparams(("parallel", "parallel", "parallel")),
        name="band_attn",
    )(*args)
    return res if want_lse else res[0]


def _softplus(x):
    return jnp.maximum(x, 0.0) + jnp.log1p(jnp.exp(-jnp.abs(x)))


def _lru_kernel(xf_ref, xfp_ref, xfn_ref, xb_ref, xbp_ref, xbn_ref,
                cw_ref, cb_ref, gw_ref, gb_ref, lam_ref,
                hf_ref, hb_ref, a_scr, u_scr, carry_ref):
    i = pl.program_id(1)
    nt = pl.num_programs(1)
    tt = xf_ref.shape[1]
    nblk = tt // SUBLANES

    @pl.when(i == 0)
    def _():
        carry_ref[...] = jnp.zeros(carry_ref.shape, F32)

    row = lax.broadcasted_iota(jnp.int32, (SUBLANES, LRU_WIDTH), 0)

    def prepare(d, x_ref, xp_ref, xn_ref, tile):
        prev_rows = jnp.where(tile > 0, xp_ref[0], 0.0)
        next_rows = jnp.where(tile < nt - 1, xn_ref[0], 0.0)
        xp = jnp.concatenate([prev_rows, x_ref[0], next_rows], axis=0)
        xc = cb_ref[...]
        base = SUBLANES - CONV_LEFT
        for jj in range(CONV_W):
            xc = xc + xp[base + jj:base + jj + tt, :] * cw_ref[jj:jj + 1, :]
        xcb = xc.astype(BF16)
        r = jax.nn.sigmoid(jnp.dot(xcb, gw_ref[d, 0], preferred_element_type=F32) + gb_ref[d, 0:1, :])
        gi = jax.nn.sigmoid(jnp.dot(xcb, gw_ref[d, 1], preferred_element_type=F32) + gb_ref[d, 1:2, :])
        log_a = -LRU_C * r * _softplus(-lam_ref[d:d + 1, :])
        a = jnp.exp(log_a)
        a_scr[d] = a
        u_scr[d] = jnp.sqrt(-jnp.tanh(log_a) * (a * a + 1.0)) * (gi * xc)

    prepare(0, xf_ref, xfp_ref, xfn_ref, i)
    prepare(1, xb_ref, xbp_ref, xbn_ref, nt - 1 - i)

    def scan_block(d, blk, carry, out_ref):
        a8 = a_scr[d, pl.ds(blk * SUBLANES, SUBLANES), :]
        u8 = u_scr[d, pl.ds(blk * SUBLANES, SUBLANES), :]
        for step in (1, 2, 4):
            if d == 0:
                shift, ident = step, row < step
            else:
                shift, ident = SUBLANES - step, row >= SUBLANES - step
            u_sh = jnp.where(ident, 0.0, pltpu.roll(u8, shift, 0))
            a_sh = jnp.where(ident, 1.0, pltpu.roll(a8, shift, 0))
            u8 = u8 + a8 * u_sh
            a8 = a8 * a_sh
        h8 = u8 + a8 * carry
        out_ref[0, pl.ds(blk * SUBLANES, SUBLANES), :] = h8
        last = h8[SUBLANES - 1:SUBLANES, :] if d == 0 else h8[0:1, :]
        return jnp.broadcast_to(last, (SUBLANES, LRU_WIDTH))

    def body(kb, carries):
        cf, cb = carries
        cf = scan_block(0, kb, cf, hf_ref)
        cb = scan_block(1, nblk - 1 - kb, cb, hb_ref)
        return cf, cb

    cf, cb = lax.fori_loop(0, nblk, body, (carry_ref[0], carry_ref[1]))
    carry_ref[0] = cf
    carry_ref[1] = cb


def _lru(bx, conv_w, conv_b, gate_w_dense, gate_b, lam):
    b, s, w = bx.shape
    tt = min(TT_LRU, s)
    nt = s // tt
    per = tt // SUBLANES
    n8 = s // SUBLANES
    fwd = lambda i: i
    bwd = lambda i: nt - 1 - i

    def specs(tile_of):
        return [
            pl.BlockSpec((1, tt, w), lambda bb, i: (bb, tile_of(i), 0)),
            pl.BlockSpec((1, SUBLANES, w), lambda bb, i: (bb, jnp.maximum(tile_of(i) * per - 1, 0), 0)),
            pl.BlockSpec((1, SUBLANES, w), lambda bb, i: (bb, jnp.minimum((tile_of(i) + 1) * per, n8 - 1), 0)),
        ]

    return pl.pallas_call(
        _lru_kernel,
        grid=(b, nt),
        in_specs=specs(fwd) + specs(bwd) + [
            _resident(conv_w.shape), _resident(conv_b.shape), _resident(gate_w_dense.shape),
            _resident(gate_b.shape), _resident(lam.shape)],
        out_specs=[pl.BlockSpec((1, tt, w), lambda bb, i: (bb, i, 0)),
                   pl.BlockSpec((1, tt, w), lambda bb, i: (bb, nt - 1 - i, 0))],
        out_shape=[jax.ShapeDtypeStruct((b, s, w), F32), jax.ShapeDtypeStruct((b, s, w), F32)],
        scratch_shapes=[pltpu.VMEM((2, tt, w), F32), pltpu.VMEM((2, tt, w), F32),
                        pltpu.VMEM((2, SUBLANES, w), F32)],
        compiler_params=_cparams(("parallel", "arbitrary")),
        name="lru",
    )(bx, bx, bx, bx, bx, bx, conv_w, conv_b, gate_w_dense, gate_b, lam)


def _gelu_tanh(x):
    return 0.5 * x * (1.0 + jnp.tanh(math.sqrt(2.0 / math.pi) * (x + 0.044715 * (x * x * x))))


def _merge_kernel(x_ref, ya_ref, hf_ref, hb_ref, by_ref, yc_ref,
                  o0_ref, o1_ref, o2_ref, l0_ref, l1_ref, l2_ref, g_ref, wg_ref, gbias_ref,
                  wa_ref, wb_ref, wc_ref, wd_ref, wo_ref, out_ref, perm_scr):
    d_model = x_ref.shape[2]
    tm = x_ref.shape[1]
    x = x_ref[0]
    var = jnp.mean(x * x, axis=-1, keepdims=True)
    hn = (x * lax.rsqrt(var + EPS) * g_ref[...]).astype(BF16)

    def natural(ref, slot):
        dil = ref.shape[1]
        if dil == 1:
            return ref[0, 0].astype(F32)
        for r in range(dil):
            rows = ref[0, r].astype(F32)
            for j in range(rows.shape[1] // LANES):
                perm_scr[slot, j, pl.ds(r, tm // dil, stride=dil), :] = rows[:, j * LANES:(j + 1) * LANES]
        return jnp.concatenate([perm_scr[slot, j] for j in range(perm_scr.shape[1])], axis=1)

    def gate(kk):
        logits = jnp.dot(hn, wg_ref[:, kk * d_model:(kk + 1) * d_model], preferred_element_type=F32)
        return jax.nn.sigmoid(logits + gbias_ref[kk:kk + 1, :])

    merged = gate(0) * jnp.dot(ya_ref[0], wa_ref[...], preferred_element_type=F32)

    yb = (hf_ref[0] + hb_ref[0]) * _gelu_tanh(by_ref[0].astype(F32))
    merged = merged + gate(1) * jnp.dot(yb.astype(BF16), wb_ref[...], preferred_element_type=F32)

    merged = merged + gate(2) * jnp.dot(yc_ref[0], wc_ref[...], preferred_element_type=F32)

    l0, l1, l2 = natural(l0_ref, 0), natural(l1_ref, 0), natural(l2_ref, 1)
    o0, o1, o2 = natural(o0_ref, 2), natural(o1_ref, 2), natural(o2_ref, 3)
    mx = jnp.maximum(jnp.maximum(l0, l1), l2)
    e0, e1, e2 = jnp.exp2(l0 - mx), jnp.exp2(l1 - mx), jnp.exp2(l2 - mx)
    yd = (e0 * o0 + e1 * o1 + e2 * o2) / (e0 + e1 + e2)
    merged = merged + gate(3) * jnp.dot(yd.astype(BF16), wd_ref[...], preferred_element_type=F32)

    out_ref[0] = x + jnp.dot(merged.astype(BF16), wo_ref[...], preferred_element_type=F32)


def _merge(x, ya, hf, hb, by, yc, o_d, lse_d, g, wg, gbias, wa, wb, wc, wd, wo):
    b, s, d = x.shape
    tm = min(TM_MERGE, s)
    tok = lambda width: pl.BlockSpec((1, tm, width), lambda bb, i: (bb, i, 0))
    dw = D_HEADS * HEAD_DIM
    res = [pl.BlockSpec((1, dil, tm // dil, dw), lambda bb, i: (bb, 0, i, 0)) for _, dil in D_PATTERNS]
    return pl.pallas_call(
        _merge_kernel,
        grid=(b, s // tm),
        in_specs=[tok(d), tok(512), tok(LRU_WIDTH), tok(LRU_WIDTH), tok(LRU_WIDTH), tok(512),
                  *res, *res, _resident(g.shape), _resident(wg.shape),
                  _resident(gbias.shape), _resident(wa.shape), _resident(wb.shape),
                  _resident(wc.shape), _resident(wd.shape), _resident(wo.shape)],
        out_specs=tok(d),
        out_shape=jax.ShapeDtypeStruct((b, s, d), F32),
        scratch_shapes=[pltpu.VMEM((4, dw // LANES, tm, LANES), F32)],
        compiler_params=_cparams(("parallel", "parallel")),
        name="merge",
    )(x, ya, hf, hb, by, yc, *o_d, *lse_d, g, wg, gbias, wa, wb, wc, wd, wo)


def _mlp_kernel(x_ref, g_ref, w1_ref, w2_ref, gf_ref, out_ref, *, final_norm):
    x = x_ref[0]
    var = jnp.mean(x * x, axis=-1, keepdims=True)
    hn = (x * lax.rsqrt(var + EPS) * g_ref[...]).astype(BF16)
    hidden = w1_ref.shape[1]
    acc = x
    for c in range(hidden // MLP_CHUNK):
        h = jnp.dot(hn, w1_ref[:, c * MLP_CHUNK:(c + 1) * MLP_CHUNK], preferred_element_type=F32)
        h = jnp.square(jnp.maximum(h, 0.0)).astype(BF16)
        acc = acc + jnp.dot(h, w2_ref[c * MLP_CHUNK:(c + 1) * MLP_CHUNK, :], preferred_element_type=F32)
    if final_norm:
        var = jnp.mean(acc * acc, axis=-1, keepdims=True)
        acc = acc * lax.rsqrt(var + EPS) * gf_ref[...]
    out_ref[0] = acc


def _mlp(x, g, w1, w2, g_final, final_norm):
    b, s, d = x.shape
    tm = min(TM_MLP, s)
    tok = pl.BlockSpec((1, tm, d), lambda bb, i: (bb, i, 0))
    return pl.pallas_call(
        functools.partial(_mlp_kernel, final_norm=final_norm),
        grid=(b, s // tm),
        in_specs=[tok, _resident(g.shape), _resident(w1.shape), _resident(w2.shape), _resident(g_final.shape)],
        out_specs=tok,
        out_shape=jax.ShapeDtypeStruct((b, s, d), F32),
        compiler_params=_cparams(("parallel", "parallel")),
        name="mlp",
    )(x, g, w1, w2, g_final)


def _dense_block_diag(w):
    nb, bw, _ = w.shape
    eye = jnp.eye(nb, dtype=w.dtype)
    return jnp.einsum('ncd,nm->ncmd', w, eye).reshape(nb * bw, nb * bw)


def kernel(x, norm_mix_g, w_in, gate_bias, qk_norm_g, conv_w, conv_b, lru_gate_w, lru_gate_b, lru_lambda, sink_logit, w_proj_a, w_proj_b, w_proj_c, w_proj_d, w_out, norm_mlp_g, w_mlp1, w_mlp2, norm_final_g):
    b, s, d = x.shape
    depth = w_in.shape[0]
    ax_t, seq_t = _rope_tables(s)
    dw = D_HEADS * HEAD_DIM
    n_groups = len(D_PATTERNS)

    for l in range(depth):
        qk_gain = jnp.tile(qk_norm_g[l], (1, LANES // HEAD_DIM))
        w_in_l = w_in[l].astype(BF16)
        outs = _inproj(x, norm_mix_g[l][None, :], w_in_l[:, :N_IN], qk_gain, ax_t, seq_t)
        qta, ka, vta, bx, by, qc, kc, vc = outs[:8]
        qd, kd, vd = outs[8:8 + n_groups], outs[8 + n_groups:8 + 2 * n_groups], outs[8 + 2 * n_groups:8 + 3 * n_groups]

        ya = _attn_a(qta, ka, vta)

        gate_w_dense = jnp.stack([jnp.stack([_dense_block_diag(lru_gate_w[l, dd, gg]) for gg in range(2)])
                                  for dd in range(2)]).astype(BF16)
        hf, hb = _lru(bx, conv_w[l], conv_b[l][None, :], gate_w_dense, lru_gate_b[l], lru_lambda[l])

        yc = _band_attention(qc[:, None], kc[:, None], vc[:, None], nkv=C_KV_HEADS,
                             group=C_HEADS // C_KV_HEADS, half_window=C_HALF_WINDOW, sink=sink_logit[l])[:, 0]

        o_d, lse_d = [], []
        for gi, (window, dil) in enumerate(D_PATTERNS):
            o, lse = _band_attention(qd[gi], kd[gi], vd[gi], nkv=D_HEADS, group=1,
                                     half_window=window // (2 * dil), want_lse=True)
            o_d.append(o)
            lse_d.append(lse)

        x = _merge(x, ya, hf, hb, by, yc, o_d, lse_d, norm_mix_g[l][None, :], w_in_l[:, N_IN:], gate_bias[l],
                   w_proj_a[l].astype(BF16), w_proj_b[l].astype(BF16), w_proj_c[l].astype(BF16),
                   w_proj_d[l].astype(BF16), w_out[l].astype(BF16))
        x = _mlp(x, norm_mlp_g[l][None, :], w_mlp1[l].astype(BF16), w_mlp2[l].astype(BF16),
                 norm_final_g[None, :], final_norm=(l == depth - 1))
    return x
```

```python
import functools
import math

import jax
import jax.numpy as jnp
from jax import lax
from jax.experimental import pallas as pl
from jax.experimental.pallas import tpu as pltpu

F32 = jnp.float32
BF16 = jnp.bfloat16

HEAD_DIM = 64
ROPE_THETA = 10000.0
GRID_W = 64
EPS = 1e-6
MASK_VALUE = -1e30
LOG2E = math.log2(math.e)

A_HEADS, A_KV_HEADS = 8, 2
LRU_WIDTH, LRU_BLOCKS, LRU_C = 512, 8, 8.0
CONV_W, CONV_LEFT = 4, 2
C_HEADS, C_KV_HEADS, C_HALF_WINDOW = 8, 2, 128
D_PATTERNS = ((128, 1), (512, 4), (2048, 16))
D_HEADS = 4
N_BRANCH = 4

LANES = 128
SUBLANES = 8
VMEM_LIMIT_BYTES = 56 * 1024 * 1024

TM_INPROJ = 512
TQ_ATTN_A = 256
SUB_ATTN_A = 128
LOOKAHEAD_ATTN_A = 14
LOOKAHEAD_BAND = 6
SUB_BAND = 128
CHUNKS_PER_BODY_ATTN_A = 16
VT_ROWS_A = HEAD_DIM + 16
TQ_BAND = 256
BAND_TILES_PER_STEP = 4
TT_LRU = 512
TM_MERGE = 512
TM_MLP = 512
MLP_CHUNK = 1024

Q_SCALE = HEAD_DIM ** -0.5 * LOG2E


def _cparams(sem):
    return pltpu.CompilerParams(dimension_semantics=sem, vmem_limit_bytes=VMEM_LIMIT_BYTES)


def _resident(shape):
    nd = len(shape)
    return pl.BlockSpec(shape, lambda *_: (0,) * nd, pipeline_mode=pl.Buffered(1))


def _rope_tables(seq):
    pos = jnp.arange(seq, dtype=F32)
    lane = jnp.arange(LANES)
    d = lane % HEAD_DIM

    inv = ROPE_THETA ** (-jnp.arange(0, HEAD_DIM, 2, dtype=F32) / HEAD_DIM)
    ang = pos[:, None] * inv[None, :]
    f = d % (HEAD_DIM // 2)
    cos_s, sin_s = jnp.cos(ang)[:, f], jnp.sin(ang)[:, f]
    lo = (d < HEAD_DIM // 2)[None, :]
    seq_t = (cos_s, jnp.where(lo, -sin_s, 0.0), jnp.where(lo, 0.0, sin_s))

    quarter = HEAD_DIM // 4
    inv_ax = ROPE_THETA ** (-jnp.arange(0, HEAD_DIM // 2, 2, dtype=F32) / (HEAD_DIM // 2))
    row_pos = (jnp.arange(seq) // GRID_W).astype(F32)
    col_pos = (jnp.arange(seq) % GRID_W).astype(F32)
    e = d % (HEAD_DIM // 2)
    fa = e % quarter
    is_col = (d >= HEAD_DIM // 2)[None, :]
    ang_ax = jnp.where(is_col, col_pos[:, None], row_pos[:, None]) * inv_ax[fa][None, :]
    cos_a, sin_a = jnp.cos(ang_ax), jnp.sin(ang_ax)
    lo_a = (e < quarter)[None, :]
    ax_t = (cos_a, jnp.where(lo_a, -sin_a, 0.0), jnp.where(lo_a, 0.0, sin_a))
    return ax_t, seq_t


def _rotate(x, tables, half):
    cos, sin_a, sin_b = tables
    return x * cos + pltpu.roll(x, LANES - half, 1) * sin_a + pltpu.roll(x, half, 1) * sin_b


def _split_dot(x, m):
    hi = x.astype(BF16)
    lo = (x - hi.astype(F32)).astype(BF16)
    return jnp.dot(jnp.concatenate([hi, lo], axis=1), jnp.concatenate([m, m], axis=0),
                   preferred_element_type=F32)


def _head_mean_matrix():
    r = lax.broadcasted_iota(jnp.int32, (LANES, LANES), 0) // HEAD_DIM
    c = lax.broadcasted_iota(jnp.int32, (LANES, LANES), 1) // HEAD_DIM
    return jnp.where(r == c, 1.0 / HEAD_DIM, 0.0).astype(BF16)


def _head_rmsnorm(x, gain, mean_mat):
    var = _split_dot(x * x, mean_mat)
    return x * lax.rsqrt(var + EPS) * gain


IN_WIDTHS = (512, 128, 128, 512, 512, 512, 128, 128, 768, 768, 768)
IN_OFFSETS = tuple(sum(IN_WIDTHS[:i]) for i in range(len(IN_WIDTHS)))
N_IN = sum(IN_WIDTHS)


def _inproj_kernel(x_ref, g_ref, w_ref, qkg_ref,
                   ca_ref, saa_ref, sab_ref, cs_ref, ssa_ref, ssb_ref,
                   qta_ref, ka_ref, vta_ref, bx_ref, by_ref,
                   qc_ref, kc_ref, vc_ref, *rest):
    d_refs, perm_scr = rest[:9], rest[9]
    x = x_ref[0]
    var = jnp.mean(x * x, axis=-1, keepdims=True)
    hn = (x * lax.rsqrt(var + EPS) * g_ref[...]).astype(BF16)

    def proj(seg, lo=0, width=None):
        off = IN_OFFSETS[seg] + lo
        width = IN_WIDTHS[seg] if width is None else width
        return lambda: jnp.dot(hn, w_ref[:, off:off + width], preferred_element_type=F32)

    ax_t = (ca_ref[...], saa_ref[...], sab_ref[...])
    seq_t = (cs_ref[...], ssa_ref[...], ssb_ref[...])
    mean_mat = _head_mean_matrix()
    gq = qkg_ref[0:1, :]
    gk = qkg_ref[1:2, :]
    tm = x.shape[0]
    dw = D_HEADS * HEAD_DIM

    def finish_aq(acc):
        for j in range(IN_WIDTHS[0] // LANES):
            piece = _head_rmsnorm(acc[:, j * LANES:(j + 1) * LANES], gq, mean_mat)
            piece = _rotate(piece, ax_t, HEAD_DIM // 4) * Q_SCALE
            qta_ref[0, j * LANES:(j + 1) * LANES, :] = piece.T.astype(BF16)

    def finish_akv(acc):
        ak = _rotate(_head_rmsnorm(acc[:, :LANES], gk, mean_mat), ax_t, HEAD_DIM // 4).astype(BF16)
        avt = acc[:, LANES:].T.astype(BF16)
        pad_row = lax.broadcasted_iota(jnp.int32, (VT_ROWS_A - HEAD_DIM, tm), 0)
        ones_rows = (pad_row == 0).astype(BF16)
        for g in range(A_KV_HEADS):
            ka_ref[0, g] = ak[:, g * HEAD_DIM:(g + 1) * HEAD_DIM]
            vta_ref[0, g, 0] = jnp.concatenate([avt[g * HEAD_DIM:(g + 1) * HEAD_DIM, :], ones_rows], axis=0)

    def finish_bx(acc):
        bx_ref[0] = acc

    def finish_by(acc):
        by_ref[0] = acc.astype(BF16)

    def rope_piece(piece, scale):
        piece = _rotate(piece, seq_t, HEAD_DIM // 2)
        return piece if scale is None else piece * scale

    def finish_cq(acc):
        for j in range(IN_WIDTHS[5] // LANES):
            qc_ref[0, :, j * LANES:(j + 1) * LANES] = rope_piece(acc[:, j * LANES:(j + 1) * LANES], Q_SCALE).astype(BF16)

    def finish_ckv(acc):
        kc_ref[0] = rope_piece(acc[:, :LANES], None).astype(BF16)
        vc_ref[0] = acc[:, LANES:].astype(BF16)

    def finish_d(ref, dil, scale, rope):
        def finish(acc):
            for j in range(dw // LANES):
                piece = acc[:, j * LANES:(j + 1) * LANES]
                if rope:
                    piece = rope_piece(piece, scale)
                if dil == 1:
                    ref[0, 0, :, j * LANES:(j + 1) * LANES] = piece.astype(BF16)
                else:
                    perm_scr[j] = piece
            if dil > 1:
                for r in range(dil):
                    rows = [perm_scr[j, pl.ds(r, tm // dil, stride=dil), :] for j in range(dw // LANES)]
                    ref[0, r] = jnp.concatenate(rows, axis=1).astype(BF16)
        return finish

    segments = [(proj(0), finish_aq), (proj(1, 0, 2 * LANES), finish_akv), (proj(3), finish_bx),
                (proj(4), finish_by), (proj(5), finish_cq), (proj(6, 0, 2 * LANES), finish_ckv)]
    for gi, (_, dil) in enumerate(D_PATTERNS):
        for ti, (seg, scale, rope) in enumerate(((8, Q_SCALE, True), (9, None, True), (10, None, False))):
            segments.append((proj(seg, gi * dw, dw), finish_d(d_refs[ti * len(D_PATTERNS) + gi], dil, scale, rope)))

    pending = [segments[0][0]()]
    for idx, (_, finish) in enumerate(segments):
        if idx + 1 < len(segments):
            pending.append(segments[idx + 1][0]())
        finish(pending.pop(0))


def _inproj(x, g, w_bf16, qk_gain128, ax_t, seq_t):
    b, s, d = x.shape
    tm = min(TM_INPROJ, s)
    nt = s // tm
    tok = lambda width: pl.BlockSpec((1, tm, width), lambda i, bb: (bb, i, 0))
    tab = pl.BlockSpec((tm, LANES), lambda i, bb: (i, 0))
    dw = D_HEADS * HEAD_DIM
    d_shapes = tuple(jax.ShapeDtypeStruct((b, dil, s // dil, dw), BF16) for _, dil in D_PATTERNS)
    d_specs = tuple(pl.BlockSpec((1, dil, tm // dil, dw), lambda i, bb: (bb, 0, i, 0)) for _, dil in D_PATTERNS)
    out_shape = (
        jax.ShapeDtypeStruct((b, A_HEADS * HEAD_DIM, s), BF16),
        jax.ShapeDtypeStruct((b, A_KV_HEADS, s, HEAD_DIM), BF16),
        jax.ShapeDtypeStruct((b, A_KV_HEADS, nt, VT_ROWS_A, tm), BF16),
        jax.ShapeDtypeStruct((b, s, LRU_WIDTH), F32),
        jax.ShapeDtypeStruct((b, s, LRU_WIDTH), BF16),
        jax.ShapeDtypeStruct((b, s, 512), BF16),
        jax.ShapeDtypeStruct((b, s, 128), BF16),
        jax.ShapeDtypeStruct((b, s, 128), BF16),
    ) + d_shapes * 3
    out_specs = (
        pl.BlockSpec((1, A_HEADS * HEAD_DIM, tm), lambda i, bb: (bb, 0, i)),
        pl.BlockSpec((1, A_KV_HEADS, tm, HEAD_DIM), lambda i, bb: (bb, 0, i, 0)),
        pl.BlockSpec((1, A_KV_HEADS, 1, VT_ROWS_A, tm), lambda i, bb: (bb, 0, i, 0, 0)),
        tok(LRU_WIDTH), tok(LRU_WIDTH), tok(512), tok(128), tok(128),
    ) + d_specs * 3
    return pl.pallas_call(
        _inproj_kernel,
        grid=(nt, b),
        in_specs=[tok(d), _resident((1, d)), _resident((d, N_IN)), _resident((2, LANES)),
                  tab, tab, tab, tab, tab, tab],
        out_specs=out_specs,
        out_shape=out_shape,
        scratch_shapes=[pltpu.VMEM((dw // LANES, tm, LANES), F32)],
        compiler_params=_cparams(("parallel", "parallel")),
        name="inproj",
    )(x, g, w_bf16, qk_gain128, *ax_t, *seq_t)


def _sublane_allmax(x):
    for shift in (4, 2, 1):
        x = jnp.maximum(x, pltpu.roll(x, shift, 0))
    return x


def _attn_a_kernel(q_ref, k_ref, v_ref, o_ref, m_ref, acc_ref, *, group, sub):
    n_chunks, chunk = v_ref.shape[2], v_ref.shape[4]
    tq = q_ref.shape[2]
    m_ref[...] = jnp.full(m_ref.shape, MASK_VALUE, F32)
    acc_ref[...] = jnp.zeros(acc_ref.shape, F32)

    per_body = min(CHUNKS_PER_BODY_ATTN_A, n_chunks)
    n_sub = chunk // sub
    pair = 2 if n_sub % 2 == 0 else 1
    units = [(cc, u, h) for cc in range(per_body) for up in range(n_sub // pair) for h in range(group)
             for u in range(up * pair, (up + 1) * pair)]

    def body(it, carry):
        c0 = it * per_body

        def scores(idx):
            cc, u, h = units[idx]
            row0 = pl.multiple_of((c0 + cc) * chunk + u * sub, sub)
            k = k_ref[0, 0, pl.ds(row0, sub), :]
            qt = q_ref[0, h * HEAD_DIM:(h + 1) * HEAD_DIM, :]
            return jnp.dot(k, qt, preferred_element_type=F32)

        pending = [scores(i) for i in range(LOOKAHEAD_ATTN_A)]
        probs = []
        for idx, (cc, u, h) in enumerate(units):
            if idx + LOOKAHEAD_ATTN_A < len(units):
                pending.append(scores(idx + LOOKAHEAD_ATTN_A))
            s3 = pending.pop(0).reshape(sub // SUBLANES, SUBLANES, tq)
            if not probs:
                m_start = m_ref[h]
                m_old = m_start
            m_new = jnp.maximum(m_old, _sublane_allmax(jnp.max(s3, axis=0)))
            if probs:
                beta = jnp.exp2(m_old - m_new).astype(BF16)
                beta = jnp.concatenate([beta, beta], axis=0)[None]
                probs = [(p.reshape(sub // 16, 16, tq) * beta).reshape(sub, tq) for p in probs]
            probs.append(jnp.exp2(s3 - m_new[None]).reshape(sub, tq).astype(BF16))
            m_old = m_new
            if len(probs) == pair:
                u0 = u - (pair - 1)
                vt = v_ref[0, 0, c0 + cc, :, u0 * sub:(u + 1) * sub]
                pv = jnp.dot(vt, jnp.concatenate(probs, axis=0), preferred_element_type=F32)
                alpha = jnp.exp2(m_start - m_new)
                acc_ref[h] = acc_ref[h] * alpha[None] + pv.reshape(VT_ROWS_A // SUBLANES, SUBLANES, tq)
                m_ref[h] = m_new
                probs = []
        return carry

    lax.fori_loop(0, n_chunks // per_body, body, 0)
    outs = []
    for h in range(group):
        acc = acc_ref[h].reshape(VT_ROWS_A, tq)
        outs.append(acc[:HEAD_DIM] / acc[HEAD_DIM:HEAD_DIM + 1])
    o_ref[0] = jnp.concatenate(outs, axis=0).T.astype(o_ref.dtype)


def _attn_a(qt, k, vt):
    b, hd_all, s = qt.shape
    nkv, nt, chunk = vt.shape[1], vt.shape[2], vt.shape[4]
    group = hd_all // HEAD_DIM // nkv
    tq = min(TQ_ATTN_A, s)
    gw = group * HEAD_DIM
    return pl.pallas_call(
        functools.partial(_attn_a_kernel, group=group, sub=min(SUB_ATTN_A, chunk)),
        grid=(b, nkv, s // tq),
        in_specs=[pl.BlockSpec((1, gw, tq), lambda bb, g, i: (bb, g, i)),
                  pl.BlockSpec((1, 1, s, HEAD_DIM), lambda bb, g, i: (bb, g, 0, 0)),
                  pl.BlockSpec((1, 1, nt, VT_ROWS_A, chunk), lambda bb, g, i: (bb, g, 0, 0, 0))],
        out_specs=pl.BlockSpec((1, tq, gw), lambda bb, g, i: (bb, i, g)),
        out_shape=jax.ShapeDtypeStruct((b, s, hd_all), BF16),
        scratch_shapes=[pltpu.VMEM((group, SUBLANES, tq), F32),
                        pltpu.VMEM((group, VT_ROWS_A // SUBLANES, SUBLANES, tq), F32)],
        compiler_params=_cparams(("parallel", "parallel", "parallel")),
        name="attn_a",
    )(qt, k, vt)


def _band_kernel(*refs, nkv, group, half_window, seq_len, has_sink, want_lse):
    q_ref, kp_ref, km_ref, kn_ref, vp_ref, vm_ref, vn_ref = refs[:7]
    rest = list(refs[7:])
    sink_ref = rest.pop(0) if has_sink else None
    o_ref = rest.pop(0)
    lse_ref = rest.pop(0) if want_lse else None

    tq_blk = q_ref.shape[2]
    tq = min(TQ_BAND, tq_blk)
    halo = kp_ref.shape[2]
    nk = tq + 2 * halo
    sub = SUB_BAND if nk % SUB_BAND == 0 else halo
    n_heads = nkv * group
    blk0 = pl.program_id(2) * tq_blk

    kcat = jnp.concatenate([kp_ref[0, 0], km_ref[0, 0], kn_ref[0, 0]], axis=0)
    vt = jnp.concatenate([vp_ref[0, 0], vm_ref[0, 0], vn_ref[0, 0]], axis=0).T

    kw = min(LANES, nkv * HEAD_DIM)
    zeros = jnp.zeros((HEAD_DIM, tq), kcat.dtype)
    q_pads = {}

    def padded_queries(j):
        if j not in q_pads:
            qt = q_ref[0, 0, j * tq:(j + 1) * tq, :].T
            pads = []
            for h in range(n_heads):
                slot = ((h // group) * HEAD_DIM % kw) // HEAD_DIM
                parts = [zeros] * (kw // HEAD_DIM)
                parts[slot] = qt[h * HEAD_DIM:(h + 1) * HEAD_DIM, :]
                pads.append(parts[0] if len(parts) == 1 else jnp.concatenate(parts, axis=0))
            q_pads.clear()
            q_pads[j] = pads
        return q_pads[j]

    units = [(j, kb, h) for j in range(tq_blk // tq) for kb in range(nk // sub) for h in range(n_heads)]

    cw = LANES if tq % LANES == 0 else tq
    n_ch = tq // cw

    def query_chunks(kb):
        k_lo = kb * sub - halo
        lo = max(0, (k_lo - half_window) // cw)
        hi = min(n_ch, -(-(k_lo + sub + half_window) // cw))
        return list(range(lo, hi))

    def scores(idx):
        j, kb, h = units[idx]
        row0 = j * tq + kb * sub
        lane0 = (h // group) * HEAD_DIM // kw * kw
        cs = query_chunks(kb)
        return jnp.dot(kcat[row0:row0 + sub, lane0:lane0 + kw],
                       padded_queries(j)[h][:, cs[0] * cw:(cs[-1] + 1) * cw], preferred_element_type=F32)

    key_row = lax.broadcasted_iota(jnp.int32, (sub, cw), 0)
    query = lax.broadcasted_iota(jnp.int32, (sub, cw), 1)

    def valid_mask(j, kb, c):
        kpos = blk0 + j * tq - halo + kb * sub + key_row
        rel = kpos - (blk0 + j * tq + c * cw + query)
        return (jnp.abs(rel) <= half_window) & (kpos >= 0) & (kpos < seq_len)

    def start_state():
        if has_sink:
            m0 = [jnp.full((SUBLANES, cw), sink_ref[h] * LOG2E, F32) for h in range(n_heads)]
            l0 = (lax.broadcasted_iota(jnp.int32, (SUBLANES, cw), 0) == 0).astype(F32)
        else:
            m0 = [jnp.full((SUBLANES, cw), 0.1 * MASK_VALUE, F32)] * n_heads
            l0 = jnp.zeros((SUBLANES, cw), F32)
        a0 = jnp.zeros((HEAD_DIM // SUBLANES, SUBLANES, cw), F32)
        return ([[m0[h]] * n_ch for h in range(n_heads)], [[l0] * n_ch for _ in range(n_heads)],
                [[a0] * n_ch for _ in range(n_heads)])

    def join(parts):
        return parts[0] if len(parts) == 1 else jnp.concatenate(parts, axis=1)

    def finish(j, m, l, acc):
        outs, lses = [], []
        for h in range(n_heads):
            l_row = join([jnp.sum(x, axis=0, keepdims=True) for x in l[h]])
            outs.append(join([x.reshape(HEAD_DIM, cw) for x in acc[h]]) / l_row)
            if want_lse:
                lse = join([x[0:1] for x in m[h]]) + jnp.log2(l_row)
                lses.append(jnp.broadcast_to(lse, (HEAD_DIM, tq)))
        o_ref[0, 0, j * tq:(j + 1) * tq, :] = jnp.concatenate(outs, axis=0).T.astype(o_ref.dtype)
        if want_lse:
            lse_ref[0, 0, j * tq:(j + 1) * tq, :] = jnp.concatenate(lses, axis=0).T

    look = min(LOOKAHEAD_BAND, len(units))
    pending = [scores(i) for i in range(look)]
    mask_key, masks = None, None
    for idx, (j, kb, h) in enumerate(units):
        if idx + look < len(units):
            pending.append(scores(idx + look))
        if kb == 0 and h == 0:
            m, l, acc = start_state()
        cs = query_chunks(kb)
        if mask_key != (j, kb):
            mask_key, masks = (j, kb), [valid_mask(j, kb, c) for c in cs]
        g = h // group
        s = pending.pop(0)
        probs, alphas = [], []
        for i, c in enumerate(cs):
            s_c = s if len(cs) == 1 else s[:, i * cw:(i + 1) * cw]
            s3 = jnp.where(masks[i], s_c, MASK_VALUE).reshape(sub // SUBLANES, SUBLANES, cw)
            m_new = jnp.maximum(m[h][c], _sublane_allmax(jnp.max(s3, axis=0)))
            alpha = jnp.exp2(m[h][c] - m_new)
            p3 = jnp.exp2(s3 - m_new[None])
            l[h][c] = alpha * l[h][c] + jnp.sum(p3, axis=0)
            m[h][c] = m_new
            probs.append(p3.reshape(sub, cw).astype(BF16))
            alphas.append(alpha)
        col0 = j * tq + kb * sub
        pv = jnp.dot(vt[g * HEAD_DIM:(g + 1) * HEAD_DIM, col0:col0 + sub], join(probs),
                     preferred_element_type=F32)
        for i, c in enumerate(cs):
            pv_c = pv if len(cs) == 1 else pv[:, i * cw:(i + 1) * cw]
            acc[h][c] = acc[h][c] * alphas[i][None] + pv_c.reshape(HEAD_DIM // SUBLANES, SUBLANES, cw)
        if kb == nk // sub - 1 and h == n_heads - 1:
            finish(j, m, l, acc)


def _band_attention(q, k, v, *, nkv, group, half_window, sink=None, want_lse=False):
    b, n_seq, seq_len, wq = q.shape
    wk = nkv * HEAD_DIM
    halo = half_window
    tq = min(TQ_BAND * BAND_TILES_PER_STEP, seq_len)
    per = tq // halo
    n_halo = seq_len // halo

    main = lambda width: pl.BlockSpec((1, 1, tq, width), lambda bb, r, i: (bb, r, i, 0))
    prev = lambda width: pl.BlockSpec(
        (1, 1, halo, width), lambda bb, r, i: (bb, r, jnp.maximum(i * per - 1, 0), 0))
    nxt = lambda width: pl.BlockSpec(
        (1, 1, halo, width), lambda bb, r, i: (bb, r, jnp.minimum((i + 1) * per, n_halo - 1), 0))
    in_specs = [main(wq), prev(wk), main(wk), nxt(wk), prev(wk), main(wk), nxt(wk)]
    args = [q, k, k, k, v, v, v]
    if sink is not None:
        in_specs.append(pl.BlockSpec(memory_space=pltpu.SMEM))
        args.append(sink)
    out_block = main(wq)
    out_shape = [jax.ShapeDtypeStruct(q.shape, BF16)]
    out_specs = [out_block]
    if want_lse:
        out_shape.append(jax.ShapeDtypeStruct(q.shape, F32))
        out_specs.append(out_block)
    res = pl.pallas_call(
        functools.partial(_band_kernel, nkv=nkv, group=group, half_window=half_window,
                          seq_len=seq_len, has_sink=sink is not None, want_lse=want_lse),
        grid=(b, n_seq, seq_len // tq),
        in_specs=in_specs,
        out_specs=out_specs,
        out_shape=out_shape,
        compiler_params=_cparams(("parallel", "parallel", "parallel")),
        name="band_attn",
    )(*args)
    return res if want_lse else res[0]


def _sigmoid(x):
    return 0.5 * jnp.tanh(0.5 * x) + 0.5


def _softplus(x):
    return jnp.maximum(x, 0.0) + jnp.log1p(jnp.exp(-jnp.abs(x)))


def _lru_kernel(xf_ref, xfp_ref, xfn_ref, xb_ref, xbp_ref, xbn_ref,
                cw_ref, cb_ref, gw_ref, gb_ref, lam_ref,
                hf_ref, hb_ref, a_scr, u_scr, carry_ref, xpad_scr):
    i = pl.program_id(1)
    nt = pl.num_programs(1)
    tt = xf_ref.shape[1]
    nblk = tt // SUBLANES

    @pl.when(i == 0)
    def _():
        carry_ref[...] = jnp.zeros(carry_ref.shape, F32)

    row = lax.broadcasted_iota(jnp.int32, (SUBLANES, LRU_WIDTH), 0)

    def prepare(d, x_ref, xp_ref, xn_ref, tile):
        xpad_scr[d, 0:SUBLANES] = jnp.where(tile > 0, xp_ref[0], 0.0)
        xpad_scr[d, SUBLANES:SUBLANES + tt] = x_ref[0]
        xpad_scr[d, SUBLANES + tt:] = jnp.where(tile < nt - 1, xn_ref[0], 0.0)
        xc = cb_ref[...]
        base = SUBLANES - CONV_LEFT
        for jj in range(CONV_W):
            xc = xc + xpad_scr[d, base + jj:base + jj + tt, :] * cw_ref[jj:jj + 1, :]
        xcb = xc.astype(BF16)
        r = _sigmoid(jnp.dot(xcb, gw_ref[d, 0], preferred_element_type=F32) + gb_ref[d, 0:1, :])
        gi = _sigmoid(jnp.dot(xcb, gw_ref[d, 1], preferred_element_type=F32) + gb_ref[d, 1:2, :])
        log_a = -LRU_C * r * _softplus(-lam_ref[d:d + 1, :])
        a = jnp.exp(log_a)
        a_scr[d] = a
        u_scr[d] = jnp.sqrt(-jnp.tanh(log_a) * (a * a + 1.0)) * (gi * xc)

    prepare(0, xf_ref, xfp_ref, xfn_ref, i)
    prepare(1, xb_ref, xbp_ref, xbn_ref, nt - 1 - i)

    def scan_block(d, blk, carry, out_ref):
        a8 = a_scr[d, pl.ds(blk * SUBLANES, SUBLANES), :]
        u8 = u_scr[d, pl.ds(blk * SUBLANES, SUBLANES), :]
        for step in (1, 2, 4):
            if d == 0:
                shift, ident = step, row < step
            else:
                shift, ident = SUBLANES - step, row >= SUBLANES - step
            u_sh = jnp.where(ident, 0.0, pltpu.roll(u8, shift, 0))
            a_sh = jnp.where(ident, 1.0, pltpu.roll(a8, shift, 0))
            u8 = u8 + a8 * u_sh
            a8 = a8 * a_sh
        h8 = u8 + a8 * carry
        out_ref[0, pl.ds(blk * SUBLANES, SUBLANES), :] = h8
        last = h8[SUBLANES - 1:SUBLANES, :] if d == 0 else h8[0:1, :]
        return jnp.broadcast_to(last, (SUBLANES, LRU_WIDTH))

    def body(kb, carries):
        cf, cb = carries
        cf = scan_block(0, kb, cf, hf_ref)
        cb = scan_block(1, nblk - 1 - kb, cb, hb_ref)
        return cf, cb

    cf, cb = lax.fori_loop(0, nblk, body, (carry_ref[0], carry_ref[1]))
    carry_ref[0] = cf
    carry_ref[1] = cb


def _lru(bx, conv_w, conv_b, gate_w_dense, gate_b, lam):
    b, s, w = bx.shape
    tt = min(TT_LRU, s)
    nt = s // tt
    per = tt // SUBLANES
    n8 = s // SUBLANES
    fwd = lambda i: i
    bwd = lambda i: nt - 1 - i

    def specs(tile_of):
        return [
            pl.BlockSpec((1, tt, w), lambda bb, i: (bb, tile_of(i), 0)),
            pl.BlockSpec((1, SUBLANES, w), lambda bb, i: (bb, jnp.maximum(tile_of(i) * per - 1, 0), 0)),
            pl.BlockSpec((1, SUBLANES, w), lambda bb, i: (bb, jnp.minimum((tile_of(i) + 1) * per, n8 - 1), 0)),
        ]

    return pl.pallas_call(
        _lru_kernel,
        grid=(b, nt),
        in_specs=specs(fwd) + specs(bwd) + [
            _resident(conv_w.shape), _resident(conv_b.shape), _resident(gate_w_dense.shape),
            _resident(gate_b.shape), _resident(lam.shape)],
        out_specs=[pl.BlockSpec((1, tt, w), lambda bb, i: (bb, i, 0)),
                   pl.BlockSpec((1, tt, w), lambda bb, i: (bb, nt - 1 - i, 0))],
        out_shape=[jax.ShapeDtypeStruct((b, s, w), F32), jax.ShapeDtypeStruct((b, s, w), F32)],
        scratch_shapes=[pltpu.VMEM((2, tt, w), F32), pltpu.VMEM((2, tt, w), F32),
                        pltpu.VMEM((2, SUBLANES, w), F32), pltpu.VMEM((2, tt + 2 * SUBLANES, w), F32)],
        compiler_params=_cparams(("parallel", "arbitrary")),
        name="lru",
    )(bx, bx, bx, bx, bx, bx, conv_w, conv_b, gate_w_dense, gate_b, lam)


def _gelu_tanh(x):
    return 0.5 * x * (1.0 + jnp.tanh(math.sqrt(2.0 / math.pi) * (x + 0.044715 * (x * x * x))))


def _merge_kernel(x_ref, ya_ref, hf_ref, hb_ref, by_ref, yc_ref,
                  o0_ref, o1_ref, o2_ref, l0_ref, l1_ref, l2_ref, g_ref, wg_ref, gbias_ref,
                  wa_ref, wb_ref, wc_ref, wd_ref, wo_ref, out_ref, perm_scr):
    d_model = x_ref.shape[2]
    tm = x_ref.shape[1]
    x = x_ref[0]
    var = jnp.mean(x * x, axis=-1, keepdims=True)
    hn = (x * lax.rsqrt(var + EPS) * g_ref[...]).astype(BF16)

    def natural(ref, slot):
        dil = ref.shape[1]
        if dil == 1:
            return ref[0, 0].astype(F32)
        for r in range(dil):
            rows = ref[0, r].astype(F32)
            for j in range(rows.shape[1] // LANES):
                perm_scr[slot, j, pl.ds(r, tm // dil, stride=dil), :] = rows[:, j * LANES:(j + 1) * LANES]
        return jnp.concatenate([perm_scr[slot, j] for j in range(perm_scr.shape[1])], axis=1)

    yb = ((hf_ref[0] + hb_ref[0]) * _gelu_tanh(by_ref[0].astype(F32))).astype(BF16)

    l0, l1, l2 = natural(l0_ref, 0), natural(l1_ref, 0), natural(l2_ref, 1)
    o0, o1, o2 = natural(o0_ref, 2), natural(o1_ref, 2), natural(o2_ref, 3)
    mx = jnp.maximum(jnp.maximum(l0, l1), l2)
    e0, e1, e2 = jnp.exp2(l0 - mx), jnp.exp2(l1 - mx), jnp.exp2(l2 - mx)
    yd = ((e0 * o0 + e1 * o1 + e2 * o2) / (e0 + e1 + e2)).astype(BF16)

    def gate_logits(kk):
        return lambda: jnp.dot(hn, wg_ref[:, kk * d_model:(kk + 1) * d_model], preferred_element_type=F32)

    def branch(y, w_ref):
        return lambda: jnp.dot(y, w_ref[...], preferred_element_type=F32)

    dots = []
    for kk, (y, w_ref) in enumerate(((ya_ref[0], wa_ref), (yb, wb_ref), (yc_ref[0], wc_ref), (yd, wd_ref))):
        dots += [gate_logits(kk), branch(y, w_ref)]
    pending = [dots[0](), dots[1]()]
    merged = None
    for kk in range(N_BRANCH):
        pending += [d() for d in dots[2 * kk + 2:2 * kk + 4]]
        logits, proj = pending.pop(0), pending.pop(0)
        term = _sigmoid(logits + gbias_ref[kk:kk + 1, :]) * proj
        merged = term if merged is None else merged + term

    out_ref[0] = x + jnp.dot(merged.astype(BF16), wo_ref[...], preferred_element_type=F32)


def _merge(x, ya, hf, hb, by, yc, o_d, lse_d, g, wg, gbias, wa, wb, wc, wd, wo):
    b, s, d = x.shape
    tm = min(TM_MERGE, s)
    tok = lambda width: pl.BlockSpec((1, tm, width), lambda bb, i: (bb, i, 0))
    dw = D_HEADS * HEAD_DIM
    res = [pl.BlockSpec((1, dil, tm // dil, dw), lambda bb, i: (bb, 0, i, 0)) for _, dil in D_PATTERNS]
    return pl.pallas_call(
        _merge_kernel,
        grid=(b, s // tm),
        in_specs=[tok(d), tok(512), tok(LRU_WIDTH), tok(LRU_WIDTH), tok(LRU_WIDTH), tok(512),
                  *res, *res, _resident(g.shape), _resident(wg.shape),
                  _resident(gbias.shape), _resident(wa.shape), _resident(wb.shape),
                  _resident(wc.shape), _resident(wd.shape), _resident(wo.shape)],
        out_specs=tok(d),
        out_shape=jax.ShapeDtypeStruct((b, s, d), F32),
        scratch_shapes=[pltpu.VMEM((4, dw // LANES, tm, LANES), F32)],
        compiler_params=_cparams(("parallel", "parallel")),
        name="merge",
    )(x, ya, hf, hb, by, yc, *o_d, *lse_d, g, wg, gbias, wa, wb, wc, wd, wo)


def _mlp_kernel(x_ref, g_ref, w1_ref, w2_ref, gf_ref, out_ref, *, final_norm):
    x = x_ref[0]
    var = jnp.mean(x * x, axis=-1, keepdims=True)
    hn = (x * lax.rsqrt(var + EPS) * g_ref[...]).astype(BF16)
    hidden = w1_ref.shape[1]
    acc = x
    for c in range(hidden // MLP_CHUNK):
        h = jnp.dot(hn, w1_ref[:, c * MLP_CHUNK:(c + 1) * MLP_CHUNK], preferred_element_type=F32)
        h = jnp.square(jnp.maximum(h, 0.0)).astype(BF16)
        acc = acc + jnp.dot(h, w2_ref[c * MLP_CHUNK:(c + 1) * MLP_CHUNK, :], preferred_element_type=F32)
    if final_norm:
        var = jnp.mean(acc * acc, axis=-1, keepdims=True)
        acc = acc * lax.rsqrt(var + EPS) * gf_ref[...]
    out_ref[0] = acc


def _mlp(x, g, w1, w2, g_final, final_norm):
    b, s, d = x.shape
    tm = min(TM_MLP, s)
    tok = pl.BlockSpec((1, tm, d), lambda bb, i: (bb, i, 0))
    return pl.pallas_call(
        functools.partial(_mlp_kernel, final_norm=final_norm),
        grid=(b, s // tm),
        in_specs=[tok, _resident(g.shape), _resident(w1.shape), _resident(w2.shape), _resident(g_final.shape)],
        out_specs=tok,
        out_shape=jax.ShapeDtypeStruct((b, s, d), F32),
        compiler_params=_cparams(("parallel", "parallel")),
        name="mlp",
    )(x, g, w1, w2, g_final)


def _dense_block_diag(w):
    nb, bw, _ = w.shape
    eye = jnp.eye(nb, dtype=w.dtype)
    return jnp.einsum('ncd,nm->ncmd', w, eye).reshape(nb * bw, nb * bw)


def kernel(x, norm_mix_g, w_in, gate_bias, qk_norm_g, conv_w, conv_b, lru_gate_w, lru_gate_b, lru_lambda, sink_logit, w_proj_a, w_proj_b, w_proj_c, w_proj_d, w_out, norm_mlp_g, w_mlp1, w_mlp2, norm_final_g):
    b, s, d = x.shape
    depth = w_in.shape[0]
    ax_t, seq_t = _rope_tables(s)
    dw = D_HEADS * HEAD_DIM
    n_groups = len(D_PATTERNS)

    for l in range(depth):
        qk_gain = jnp.tile(qk_norm_g[l], (1, LANES // HEAD_DIM))
        w_in_l = w_in[l].astype(BF16)
        outs = _inproj(x, norm_mix_g[l][None, :], w_in_l[:, :N_IN], qk_gain, ax_t, seq_t)
        qta, ka, vta, bx, by, qc, kc, vc = outs[:8]
        qd, kd, vd = outs[8:8 + n_groups], outs[8 + n_groups:8 + 2 * n_groups], outs[8 + 2 * n_groups:8 + 3 * n_groups]

        ya = _attn_a(qta, ka, vta)

        gate_w_dense = jnp.stack([jnp.stack([_dense_block_diag(lru_gate_w[l, dd, gg]) for gg in range(2)])
                                  for dd in range(2)]).astype(BF16)
        hf, hb = _lru(bx, conv_w[l], conv_b[l][None, :], gate_w_dense, lru_gate_b[l], lru_lambda[l])

        yc = _band_attention(qc[:, None], kc[:, None], vc[:, None], nkv=C_KV_HEADS,
                             group=C_HEADS // C_KV_HEADS, half_window=C_HALF_WINDOW, sink=sink_logit[l])[:, 0]

        o_d, lse_d = [], []
        for gi, (window, dil) in enumerate(D_PATTERNS):
            o, lse = _band_attention(qd[gi], kd[gi], vd[gi], nkv=D_HEADS, group=1,
                                     half_window=window // (2 * dil), want_lse=True)
            o_d.append(o)
            lse_d.append(lse)

        x = _merge(x, ya, hf, hb, by, yc, o_d, lse_d, norm_mix_g[l][None, :], w_in_l[:, N_IN:], gate_bias[l],
                   w_proj_a[l].astype(BF16), w_proj_b[l].astype(BF16), w_proj_c[l].astype(BF16),
                   w_proj_d[l].astype(BF16), w_out[l].astype(BF16))
        x = _mlp(x, norm_mlp_g[l][None, :], w_mlp1[l].astype(BF16), w_mlp2[l].astype(BF16),
                 norm_final_g[None, :], final_norm=(l == depth - 1))
    return x
```

```python
import functools
import math

import jax
import jax.numpy as jnp
from jax import lax
from jax.experimental import pallas as pl
from jax.experimental.pallas import tpu as pltpu

F32 = jnp.float32
BF16 = jnp.bfloat16

HEAD_DIM = 64
ROPE_THETA = 10000.0
GRID_W = 64
EPS = 1e-6
MASK_VALUE = -1e30
LOG2E = math.log2(math.e)

A_HEADS, A_KV_HEADS = 8, 2
LRU_WIDTH, LRU_BLOCKS, LRU_C = 512, 8, 8.0
CONV_W, CONV_LEFT = 4, 2
C_HEADS, C_KV_HEADS, C_HALF_WINDOW = 8, 2, 128
D_PATTERNS = ((128, 1), (512, 4), (2048, 16))
D_HEADS = 4
N_BRANCH = 4

LANES = 128
SUBLANES = 8
VMEM_LIMIT_BYTES = 56 * 1024 * 1024

TM_INPROJ = 512
TQ_ATTN_A = 256
SUB_ATTN_A = 128
LOOKAHEAD_ATTN_A = 14
LOOKAHEAD_BAND = 6
LOOKAHEAD_INPROJ = 6
SUB_BAND = 128
CHUNKS_PER_BODY_ATTN_A = 16
VT_ROWS_A = HEAD_DIM + 16
TQ_BAND = 256
BAND_TILES_PER_STEP = 4
TT_LRU = 1024
TM_MERGE = 512
TM_MLP = 512
MLP_CHUNK = 1024

Q_SCALE = HEAD_DIM ** -0.5 * LOG2E


def _cparams(sem):
    return pltpu.CompilerParams(dimension_semantics=sem, vmem_limit_bytes=VMEM_LIMIT_BYTES)


def _resident(shape):
    nd = len(shape)
    return pl.BlockSpec(shape, lambda *_: (0,) * nd, pipeline_mode=pl.Buffered(1))


def _rope_tables(seq):
    pos = jnp.arange(seq, dtype=F32)
    lane = jnp.arange(LANES)
    d = lane % HEAD_DIM

    inv = ROPE_THETA ** (-jnp.arange(0, HEAD_DIM, 2, dtype=F32) / HEAD_DIM)
    ang = pos[:, None] * inv[None, :]
    f = d % (HEAD_DIM // 2)
    cos_s, sin_s = jnp.cos(ang)[:, f], jnp.sin(ang)[:, f]
    lo = (d < HEAD_DIM // 2)[None, :]
    seq_t = (cos_s, jnp.where(lo, -sin_s, 0.0), jnp.where(lo, 0.0, sin_s))

    quarter = HEAD_DIM // 4
    inv_ax = ROPE_THETA ** (-jnp.arange(0, HEAD_DIM // 2, 2, dtype=F32) / (HEAD_DIM // 2))
    row_pos = (jnp.arange(seq) // GRID_W).astype(F32)
    col_pos = (jnp.arange(seq) % GRID_W).astype(F32)
    e = d % (HEAD_DIM // 2)
    fa = e % quarter
    is_col = (d >= HEAD_DIM // 2)[None, :]
    ang_ax = jnp.where(is_col, col_pos[:, None], row_pos[:, None]) * inv_ax[fa][None, :]
    cos_a, sin_a = jnp.cos(ang_ax), jnp.sin(ang_ax)
    lo_a = (e < quarter)[None, :]
    ax_t = (cos_a, jnp.where(lo_a, -sin_a, 0.0), jnp.where(lo_a, 0.0, sin_a))
    return ax_t, seq_t


def _rotate(x, tables, half):
    cos, sin_a, sin_b = tables
    return x * cos + pltpu.roll(x, LANES - half, 1) * sin_a + pltpu.roll(x, half, 1) * sin_b


def _split_dot(x, m):
    hi = x.astype(BF16)
    lo = (x - hi.astype(F32)).astype(BF16)
    return jnp.dot(jnp.concatenate([hi, lo], axis=1), jnp.concatenate([m, m], axis=0),
                   preferred_element_type=F32)


def _head_mean_matrix():
    r = lax.broadcasted_iota(jnp.int32, (LANES, LANES), 0) // HEAD_DIM
    c = lax.broadcasted_iota(jnp.int32, (LANES, LANES), 1) // HEAD_DIM
    return jnp.where(r == c, 1.0 / HEAD_DIM, 0.0).astype(BF16)


def _head_rmsnorm(x, gain, mean_mat):
    var = _split_dot(x * x, mean_mat)
    return x * lax.rsqrt(var + EPS) * gain


IN_WIDTHS = (512, 128, 128, 512, 512, 512, 128, 128, 768, 768, 768)
IN_OFFSETS = tuple(sum(IN_WIDTHS[:i]) for i in range(len(IN_WIDTHS)))
N_IN = sum(IN_WIDTHS)


def _inproj_kernel(x_ref, xprev_ref, xnext_ref, g_ref, w_ref, qkg_ref,
                   ca_ref, saa_ref, sab_ref, cs_ref, ssa_ref, ssb_ref,
                   cw_ref, cb_ref, gw_ref, gb_ref, lam_ref,
                   qta_ref, ka_ref, vta_ref, af_ref, uf_ref, ab_ref, ub_ref, by_ref,
                   qc_ref, kc_ref, vc_ref, *rest):
    d_refs, perm_scr, xpad_scr = rest[:9], rest[9], rest[10]
    tile = pl.program_id(0)
    n_tiles = pl.num_programs(0)

    def normed(rows):
        var = jnp.mean(rows * rows, axis=-1, keepdims=True)
        return (rows * lax.rsqrt(var + EPS) * g_ref[...]).astype(BF16)

    x = x_ref[0]
    hn = normed(x)

    def proj(seg, lo=0, width=None):
        off = IN_OFFSETS[seg] + lo
        width = IN_WIDTHS[seg] if width is None else width
        return lambda: jnp.dot(hn, w_ref[:, off:off + width], preferred_element_type=F32)

    ax_t = (ca_ref[...], saa_ref[...], sab_ref[...])
    seq_t = (cs_ref[...], ssa_ref[...], ssb_ref[...])
    mean_mat = _head_mean_matrix()
    gq = qkg_ref[0:1, :]
    gk = qkg_ref[1:2, :]
    tm = x.shape[0]
    dw = D_HEADS * HEAD_DIM

    def finish_aq(acc):
        for j in range(IN_WIDTHS[0] // LANES):
            piece = _head_rmsnorm(acc[:, j * LANES:(j + 1) * LANES], gq, mean_mat)
            piece = _rotate(piece, ax_t, HEAD_DIM // 4) * Q_SCALE
            qta_ref[0, j * LANES:(j + 1) * LANES, :] = piece.T.astype(BF16)

    def finish_akv(acc):
        ak = _rotate(_head_rmsnorm(acc[:, :LANES], gk, mean_mat), ax_t, HEAD_DIM // 4).astype(BF16)
        avt = acc[:, LANES:].T.astype(BF16)
        pad_row = lax.broadcasted_iota(jnp.int32, (VT_ROWS_A - HEAD_DIM, tm), 0)
        ones_rows = (pad_row == 0).astype(BF16)
        for g in range(A_KV_HEADS):
            ka_ref[0, g] = ak[:, g * HEAD_DIM:(g + 1) * HEAD_DIM]
            vta_ref[0, g, 0] = jnp.concatenate([avt[g * HEAD_DIM:(g + 1) * HEAD_DIM, :], ones_rows], axis=0)

    lru = {}

    def finish_bx(acc):
        halo = jnp.dot(normed(jnp.concatenate([xprev_ref[0], xnext_ref[0]], axis=0)),
                       w_ref[:, IN_OFFSETS[3]:IN_OFFSETS[3] + LRU_WIDTH], preferred_element_type=F32)
        xpad_scr[0:SUBLANES] = jnp.where(tile > 0, halo[:SUBLANES], 0.0)
        xpad_scr[SUBLANES:SUBLANES + tm] = acc
        xpad_scr[SUBLANES + tm:] = jnp.where(tile < n_tiles - 1, halo[SUBLANES:], 0.0)
        xc = cb_ref[...]
        base = SUBLANES - CONV_LEFT
        for jj in range(CONV_W):
            xc = xc + xpad_scr[base + jj:base + jj + tm, :] * cw_ref[jj:jj + 1, :]
        lru["xc"] = xc
        lru["xcb"] = xc.astype(BF16)

    def gate_dots():
        return [jnp.dot(lru["xcb"], gw_ref[d, g], preferred_element_type=F32) for d in range(2) for g in range(2)]

    def finish_gates(dots):
        xc = lru["xc"]
        for d, (a_ref, u_ref) in enumerate(((af_ref, uf_ref), (ab_ref, ub_ref))):
            c2 = (-0.5 * LRU_C * LOG2E) * _softplus(-lam_ref[d:d + 1, :])
            a = jnp.exp2(c2 * jnp.tanh(0.5 * dots[2 * d] + 0.5 * gb_ref[d, 0:1, :]) + c2)
            gi = _sigmoid(dots[2 * d + 1] + gb_ref[d, 1:2, :])
            a_ref[0] = a
            z = (1.0 - a) * (1.0 + a)
            root = jnp.where(z > 0.0, z * lax.rsqrt(z), 0.0)
            u_ref[0] = root * (gi * xc)

    def finish_by(acc):
        by_ref[0] = acc.astype(BF16)

    def rope_piece(piece, scale):
        piece = _rotate(piece, seq_t, HEAD_DIM // 2)
        return piece if scale is None else piece * scale

    def finish_cq(acc):
        for j in range(IN_WIDTHS[5] // LANES):
            qc_ref[0, :, j * LANES:(j + 1) * LANES] = rope_piece(acc[:, j * LANES:(j + 1) * LANES], Q_SCALE).astype(BF16)

    def finish_ckv(acc):
        kc_ref[0] = rope_piece(acc[:, :LANES], None).astype(BF16)
        vc_ref[0] = acc[:, LANES:].astype(BF16)

    def finish_d(ref, dil, scale, rope):
        def finish(acc):
            for j in range(dw // LANES):
                piece = acc[:, j * LANES:(j + 1) * LANES]
                if rope:
                    piece = rope_piece(piece, scale)
                if dil == 1:
                    ref[0, 0, :, j * LANES:(j + 1) * LANES] = piece.astype(BF16)
                else:
                    perm_scr[j] = piece
            if dil > 1:
                for r in range(dil):
                    rows = [perm_scr[j, pl.ds(r, tm // dil, stride=dil), :] for j in range(dw // LANES)]
                    ref[0, r] = jnp.concatenate(rows, axis=1).astype(BF16)
        return finish

    segments = [(proj(0), finish_aq), (proj(1, 0, 2 * LANES), finish_akv), (proj(3), finish_bx),
                (proj(4), finish_by), (gate_dots, finish_gates), (proj(5), finish_cq),
                (proj(6, 0, 2 * LANES), finish_ckv)]
    for gi, (_, dil) in enumerate(D_PATTERNS):
        for ti, (seg, scale, rope) in enumerate(((8, Q_SCALE, True), (9, None, True), (10, None, False))):
            segments.append((proj(seg, gi * dw, dw), finish_d(d_refs[ti * len(D_PATTERNS) + gi], dil, scale, rope)))

    issued, pending = 0, []
    for idx, (_, finish) in enumerate(segments):
        horizon = min(len(segments), idx + 1 + LOOKAHEAD_INPROJ)
        while issued < horizon and (segments[issued][0] is not gate_dots or "xcb" in lru):
            pending.append(segments[issued][0]())
            issued += 1
        finish(pending.pop(0))


def _inproj(x, g, w_bf16, qk_gain128, ax_t, seq_t, conv_w, conv_b, gate_w, gate_b, lam):
    b, s, d = x.shape
    tm = min(TM_INPROJ, s)
    nt = s // tm
    per, n8 = tm // SUBLANES, s // SUBLANES
    halo_prev = pl.BlockSpec((1, SUBLANES, d), lambda i, bb: (bb, jnp.maximum(i * per - 1, 0), 0))
    halo_next = pl.BlockSpec((1, SUBLANES, d), lambda i, bb: (bb, jnp.minimum((i + 1) * per, n8 - 1), 0))
    tok = lambda width: pl.BlockSpec((1, tm, width), lambda i, bb: (bb, i, 0))
    tab = pl.BlockSpec((tm, LANES), lambda i, bb: (i, 0))
    dw = D_HEADS * HEAD_DIM
    d_shapes = tuple(jax.ShapeDtypeStruct((b, dil, s // dil, dw), BF16) for _, dil in D_PATTERNS)
    d_specs = tuple(pl.BlockSpec((1, dil, tm // dil, dw), lambda i, bb: (bb, 0, i, 0)) for _, dil in D_PATTERNS)
    out_shape = (
        jax.ShapeDtypeStruct((b, A_HEADS * HEAD_DIM, s), BF16),
        jax.ShapeDtypeStruct((b, A_KV_HEADS, s, HEAD_DIM), BF16),
        jax.ShapeDtypeStruct((b, A_KV_HEADS, nt, VT_ROWS_A, tm), BF16),
        jax.ShapeDtypeStruct((b, s, LRU_WIDTH), F32),
        jax.ShapeDtypeStruct((b, s, LRU_WIDTH), F32),
        jax.ShapeDtypeStruct((b, s, LRU_WIDTH), F32),
        jax.ShapeDtypeStruct((b, s, LRU_WIDTH), F32),
        jax.ShapeDtypeStruct((b, s, LRU_WIDTH), BF16),
        jax.ShapeDtypeStruct((b, s, 512), BF16),
        jax.ShapeDtypeStruct((b, s, 128), BF16),
        jax.ShapeDtypeStruct((b, s, 128), BF16),
    ) + d_shapes * 3
    out_specs = (
        pl.BlockSpec((1, A_HEADS * HEAD_DIM, tm), lambda i, bb: (bb, 0, i)),
        pl.BlockSpec((1, A_KV_HEADS, tm, HEAD_DIM), lambda i, bb: (bb, 0, i, 0)),
        pl.BlockSpec((1, A_KV_HEADS, 1, VT_ROWS_A, tm), lambda i, bb: (bb, 0, i, 0, 0)),
        tok(LRU_WIDTH), tok(LRU_WIDTH), tok(LRU_WIDTH), tok(LRU_WIDTH), tok(LRU_WIDTH), tok(512), tok(128), tok(128),
    ) + d_specs * 3
    return pl.pallas_call(
        _inproj_kernel,
        grid=(nt, b),
        in_specs=[tok(d), halo_prev, halo_next, _resident((1, d)), _resident((d, N_IN)), _resident((2, LANES)),
                  tab, tab, tab, tab, tab, tab,
                  _resident(conv_w.shape), _resident(conv_b.shape), _resident(gate_w.shape),
                  _resident(gate_b.shape), _resident(lam.shape)],
        out_specs=out_specs,
        out_shape=out_shape,
        scratch_shapes=[pltpu.VMEM((dw // LANES, tm, LANES), F32),
                        pltpu.VMEM((tm + 2 * SUBLANES, LRU_WIDTH), F32)],
        compiler_params=_cparams(("parallel", "parallel")),
        name="inproj",
    )(x, x, x, g, w_bf16, qk_gain128, *ax_t, *seq_t, conv_w, conv_b, gate_w, gate_b, lam)


def _sublane_allmax(x):
    for shift in (4, 2, 1):
        x = jnp.maximum(x, pltpu.roll(x, shift, 0))
    return x


def _attn_a_kernel(q_ref, k_ref, v_ref, o_ref, m_ref, acc_ref, *, group, sub):
    n_chunks, chunk = v_ref.shape[2], v_ref.shape[4]
    tq = q_ref.shape[2]
    m_ref[...] = jnp.full(m_ref.shape, MASK_VALUE, F32)
    acc_ref[...] = jnp.zeros(acc_ref.shape, F32)

    per_body = min(CHUNKS_PER_BODY_ATTN_A, n_chunks)
    n_sub = chunk // sub
    pair = 2 if n_sub % 2 == 0 else 1
    units = [(cc, u, h) for cc in range(per_body) for up in range(n_sub // pair) for h in range(group)
             for u in range(up * pair, (up + 1) * pair)]

    def body(it, carry):
        c0 = it * per_body

        def scores(idx):
            cc, u, h = units[idx]
            row0 = pl.multiple_of((c0 + cc) * chunk + u * sub, sub)
            k = k_ref[0, 0, pl.ds(row0, sub), :]
            qt = q_ref[0, h * HEAD_DIM:(h + 1) * HEAD_DIM, :]
            return jnp.dot(k, qt, preferred_element_type=F32)

        pending = [scores(i) for i in range(LOOKAHEAD_ATTN_A)]
        probs = []
        for idx, (cc, u, h) in enumerate(units):
            if idx + LOOKAHEAD_ATTN_A < len(units):
                pending.append(scores(idx + LOOKAHEAD_ATTN_A))
            s3 = pending.pop(0).reshape(sub // SUBLANES, SUBLANES, tq)
            if not probs:
                m_start = m_ref[h]
                m_old = m_start
            m_new = jnp.maximum(m_old, _sublane_allmax(jnp.max(s3, axis=0)))
            if probs:
                beta = jnp.exp2(m_old - m_new).astype(BF16)
                beta = jnp.concatenate([beta, beta], axis=0)[None]
                probs = [(p.reshape(sub // 16, 16, tq) * beta).reshape(sub, tq) for p in probs]
            probs.append(jnp.exp2(s3 - m_new[None]).reshape(sub, tq).astype(BF16))
            m_old = m_new
            if len(probs) == pair:
                u0 = u - (pair - 1)
                vt = v_ref[0, 0, c0 + cc, :, u0 * sub:(u + 1) * sub]
                pv = jnp.dot(vt, jnp.concatenate(probs, axis=0), preferred_element_type=F32)
                alpha = jnp.exp2(m_start - m_new)
                acc_ref[h] = acc_ref[h] * alpha[None] + pv.reshape(VT_ROWS_A // SUBLANES, SUBLANES, tq)
                m_ref[h] = m_new
                probs = []
        return carry

    lax.fori_loop(0, n_chunks // per_body, body, 0)
    outs = []
    for h in range(group):
        acc = acc_ref[h].reshape(VT_ROWS_A, tq)
        outs.append(acc[:HEAD_DIM] / acc[HEAD_DIM:HEAD_DIM + 1])
    o_ref[0] = jnp.concatenate(outs, axis=0).T.astype(o_ref.dtype)


def _attn_a(qt, k, vt):
    b, hd_all, s = qt.shape
    nkv, nt, chunk = vt.shape[1], vt.shape[2], vt.shape[4]
    group = hd_all // HEAD_DIM // nkv
    tq = min(TQ_ATTN_A, s)
    gw = group * HEAD_DIM
    return pl.pallas_call(
        functools.partial(_attn_a_kernel, group=group, sub=min(SUB_ATTN_A, chunk)),
        grid=(b, nkv, s // tq),
        in_specs=[pl.BlockSpec((1, gw, tq), lambda bb, g, i: (bb, g, i)),
                  pl.BlockSpec((1, 1, s, HEAD_DIM), lambda bb, g, i: (bb, g, 0, 0)),
                  pl.BlockSpec((1, 1, nt, VT_ROWS_A, chunk), lambda bb, g, i: (bb, g, 0, 0, 0))],
        out_specs=pl.BlockSpec((1, tq, gw), lambda bb, g, i: (bb, i, g)),
        out_shape=jax.ShapeDtypeStruct((b, s, hd_all), BF16),
        scratch_shapes=[pltpu.VMEM((group, SUBLANES, tq), F32),
                        pltpu.VMEM((group, VT_ROWS_A // SUBLANES, SUBLANES, tq), F32)],
        compiler_params=_cparams(("parallel", "parallel", "parallel")),
        name="attn_a",
    )(qt, k, vt)


def _band_kernel(*refs, nkv, group, half_window, seq_len, has_sink, want_lse):
    q_ref, kp_ref, km_ref, kn_ref, vp_ref, vm_ref, vn_ref = refs[:7]
    rest = list(refs[7:])
    sink_ref = rest.pop(0) if has_sink else None
    o_ref = rest.pop(0)
    lse_ref = rest.pop(0) if want_lse else None

    tq_blk = q_ref.shape[2]
    tq = min(TQ_BAND, tq_blk)
    halo = kp_ref.shape[2]
    nk = tq + 2 * halo
    sub = SUB_BAND if nk % SUB_BAND == 0 else halo
    n_heads = nkv * group
    blk0 = pl.program_id(2) * tq_blk

    kcat = jnp.concatenate([kp_ref[0, 0], km_ref[0, 0], kn_ref[0, 0]], axis=0)
    vt = jnp.concatenate([vp_ref[0, 0], vm_ref[0, 0], vn_ref[0, 0]], axis=0).T

    kw = min(LANES, nkv * HEAD_DIM)
    zeros = jnp.zeros((HEAD_DIM, tq), kcat.dtype)
    q_pads = {}

    def padded_queries(j):
        if j not in q_pads:
            qt = q_ref[0, 0, j * tq:(j + 1) * tq, :].T
            pads = []
            for h in range(n_heads):
                slot = ((h // group) * HEAD_DIM % kw) // HEAD_DIM
                parts = [zeros] * (kw // HEAD_DIM)
                parts[slot] = qt[h * HEAD_DIM:(h + 1) * HEAD_DIM, :]
                pads.append(parts[0] if len(parts) == 1 else jnp.concatenate(parts, axis=0))
            q_pads.clear()
            q_pads[j] = pads
        return q_pads[j]

    units = [(j, kb, h) for j in range(tq_blk // tq) for kb in range(nk // sub) for h in range(n_heads)]

    cw = LANES if tq % LANES == 0 else tq
    n_ch = tq // cw

    def query_chunks(kb):
        k_lo = kb * sub - halo
        lo = max(0, (k_lo - half_window) // cw)
        hi = min(n_ch, -(-(k_lo + sub + half_window) // cw))
        return list(range(lo, hi))

    def scores(idx):
        j, kb, h = units[idx]
        row0 = j * tq + kb * sub
        lane0 = (h // group) * HEAD_DIM // kw * kw
        cs = query_chunks(kb)
        return jnp.dot(kcat[row0:row0 + sub, lane0:lane0 + kw],
                       padded_queries(j)[h][:, cs[0] * cw:(cs[-1] + 1) * cw], preferred_element_type=F32)

    key_row = lax.broadcasted_iota(jnp.int32, (sub, cw), 0)
    query = lax.broadcasted_iota(jnp.int32, (sub, cw), 1)

    def valid_mask(j, kb, c):
        kpos = blk0 + j * tq - halo + kb * sub + key_row
        rel = kpos - (blk0 + j * tq + c * cw + query)
        return (jnp.abs(rel) <= half_window) & (kpos >= 0) & (kpos < seq_len)

    def start_state():
        if has_sink:
            m0 = [jnp.full((SUBLANES, cw), sink_ref[h] * LOG2E, F32) for h in range(n_heads)]
            l0 = (lax.broadcasted_iota(jnp.int32, (SUBLANES, cw), 0) == 0).astype(F32)
        else:
            m0 = [jnp.full((SUBLANES, cw), 0.1 * MASK_VALUE, F32)] * n_heads
            l0 = jnp.zeros((SUBLANES, cw), F32)
        a0 = jnp.zeros((HEAD_DIM // SUBLANES, SUBLANES, cw), F32)
        return ([[m0[h]] * n_ch for h in range(n_heads)], [[l0] * n_ch for _ in range(n_heads)],
                [[a0] * n_ch for _ in range(n_heads)])

    def join(parts):
        return parts[0] if len(parts) == 1 else jnp.concatenate(parts, axis=1)

    def finish(j, m, l, acc):
        outs, lses = [], []
        for h in range(n_heads):
            l_row = join([jnp.sum(x, axis=0, keepdims=True) for x in l[h]])
            outs.append(join([x.reshape(HEAD_DIM, cw) for x in acc[h]]) / l_row)
            if want_lse:
                lse = join([x[0:1] for x in m[h]]) + jnp.log2(l_row)
                lses.append(jnp.broadcast_to(lse, (HEAD_DIM, tq)))
        o_ref[0, 0, j * tq:(j + 1) * tq, :] = jnp.concatenate(outs, axis=0).T.astype(o_ref.dtype)
        if want_lse:
            lse_ref[0, 0, j * tq:(j + 1) * tq, :] = jnp.concatenate(lses, axis=0).T

    look = min(LOOKAHEAD_BAND, len(units))
    pending = [scores(i) for i in range(look)]
    mask_key, masks = None, None
    for idx, (j, kb, h) in enumerate(units):
        if idx + look < len(units):
            pending.append(scores(idx + look))
        if kb == 0 and h == 0:
            m, l, acc = start_state()
        cs = query_chunks(kb)
        if mask_key != (j, kb):
            mask_key, masks = (j, kb), [valid_mask(j, kb, c) for c in cs]
        g = h // group
        s = pending.pop(0)
        probs, alphas = [], []
        for i, c in enumerate(cs):
            s_c = s if len(cs) == 1 else s[:, i * cw:(i + 1) * cw]
            s3 = jnp.where(masks[i], s_c, MASK_VALUE).reshape(sub // SUBLANES, SUBLANES, cw)
            m_new = jnp.maximum(m[h][c], _sublane_allmax(jnp.max(s3, axis=0)))
            alpha = jnp.exp2(m[h][c] - m_new)
            p3 = jnp.exp2(s3 - m_new[None])
            l[h][c] = alpha * l[h][c] + jnp.sum(p3, axis=0)
            m[h][c] = m_new
            probs.append(p3.reshape(sub, cw).astype(BF16))
            alphas.append(alpha)
        col0 = j * tq + kb * sub
        pv = jnp.dot(vt[g * HEAD_DIM:(g + 1) * HEAD_DIM, col0:col0 + sub], join(probs),
                     preferred_element_type=F32)
        for i, c in enumerate(cs):
            pv_c = pv if len(cs) == 1 else pv[:, i * cw:(i + 1) * cw]
            acc[h][c] = acc[h][c] * alphas[i][None] + pv_c.reshape(HEAD_DIM // SUBLANES, SUBLANES, cw)
        if kb == nk // sub - 1 and h == n_heads - 1:
            finish(j, m, l, acc)


def _band_attention(q, k, v, *, nkv, group, half_window, sink=None, want_lse=False):
    b, n_seq, seq_len, wq = q.shape
    wk = nkv * HEAD_DIM
    halo = half_window
    tq = min(TQ_BAND * BAND_TILES_PER_STEP, seq_len)
    per = tq // halo
    n_halo = seq_len // halo

    main = lambda width: pl.BlockSpec((1, 1, tq, width), lambda bb, r, i: (bb, r, i, 0))
    prev = lambda width: pl.BlockSpec(
        (1, 1, halo, width), lambda bb, r, i: (bb, r, jnp.maximum(i * per - 1, 0), 0))
    nxt = lambda width: pl.BlockSpec(
        (1, 1, halo, width), lambda bb, r, i: (bb, r, jnp.minimum((i + 1) * per, n_halo - 1), 0))
    in_specs = [main(wq), prev(wk), main(wk), nxt(wk), prev(wk), main(wk), nxt(wk)]
    args = [q, k, k, k, v, v, v]
    if sink is not None:
        in_specs.append(pl.BlockSpec(memory_space=pltpu.SMEM))
        args.append(sink)
    out_block = main(wq)
    out_shape = [jax.ShapeDtypeStruct(q.shape, BF16)]
    out_specs = [out_block]
    if want_lse:
        out_shape.append(jax.ShapeDtypeStruct(q.shape, F32))
        out_specs.append(out_block)
    res = pl.pallas_call(
        functools.partial(_band_kernel, nkv=nkv, group=group, half_window=half_window,
                          seq_len=seq_len, has_sink=sink is not None, want_lse=want_lse),
        grid=(b, n_seq, seq_len // tq),
        in_specs=in_specs,
        out_specs=out_specs,
        out_shape=out_shape,
        compiler_params=_cparams(("parallel", "parallel", "parallel")),
        name="band_attn",
    )(*args)
    return res if want_lse else res[0]


def _sigmoid(x):
    return 0.5 * jnp.tanh(0.5 * x) + 0.5


def _softplus(x):
    return jnp.maximum(x, 0.0) + jnp.log1p(jnp.exp(-jnp.abs(x)))


def _lru_kernel(af_ref, uf_ref, ab_ref, ub_ref, hf_ref, hb_ref, carry_ref):
    i = pl.program_id(1)
    tt = af_ref.shape[1]
    nblk = tt // SUBLANES

    @pl.when(i == 0)
    def _():
        carry_ref[...] = jnp.zeros(carry_ref.shape, F32)

    row = lax.broadcasted_iota(jnp.int32, (SUBLANES, LRU_WIDTH), 0)

    def scan_block(d, blk, carry, a_ref, u_ref, out_ref):
        a8 = a_ref[0, pl.ds(blk * SUBLANES, SUBLANES), :]
        u8 = u_ref[0, pl.ds(blk * SUBLANES, SUBLANES), :]
        entry = row == (0 if d == 0 else SUBLANES - 1)
        u8 = u8 + jnp.where(entry, a8 * carry, 0.0)
        for step in (1, 2, 4):
            if d == 0:
                shift, ident = step, row < step
            else:
                shift, ident = SUBLANES - step, row >= SUBLANES - step
            u8 = u8 + a8 * jnp.where(ident, 0.0, pltpu.roll(u8, shift, 0))
            if step < 4:
                a8 = a8 * pltpu.roll(a8, shift, 0)
        out_ref[0, pl.ds(blk * SUBLANES, SUBLANES), :] = u8
        last = u8[SUBLANES - 1:SUBLANES, :] if d == 0 else u8[0:1, :]
        return jnp.broadcast_to(last, (SUBLANES, LRU_WIDTH))

    def body(kb, carries):
        cf, cb = carries
        cf = scan_block(0, kb, cf, af_ref, uf_ref, hf_ref)
        cb = scan_block(1, nblk - 1 - kb, cb, ab_ref, ub_ref, hb_ref)
        return cf, cb

    cf, cb = lax.fori_loop(0, nblk, body, (carry_ref[0], carry_ref[1]))
    carry_ref[0] = cf
    carry_ref[1] = cb


def _lru(af, uf, ab, ub):
    b, s, w = af.shape
    tt = min(TT_LRU, s)
    nt = s // tt
    fwd = pl.BlockSpec((1, tt, w), lambda bb, i: (bb, i, 0))
    bwd = pl.BlockSpec((1, tt, w), lambda bb, i: (bb, nt - 1 - i, 0))
    return pl.pallas_call(
        _lru_kernel,
        grid=(b, nt),
        in_specs=[fwd, fwd, bwd, bwd],
        out_specs=[fwd, bwd],
        out_shape=[jax.ShapeDtypeStruct((b, s, w), F32), jax.ShapeDtypeStruct((b, s, w), F32)],
        scratch_shapes=[pltpu.VMEM((2, SUBLANES, w), F32)],
        compiler_params=_cparams(("parallel", "arbitrary")),
        name="lru",
    )(af, uf, ab, ub)


def _gelu_tanh(x):
    return 0.5 * x * (1.0 + jnp.tanh(math.sqrt(2.0 / math.pi) * (x + 0.044715 * (x * x * x))))


def _merge_kernel(x_ref, ya_ref, hf_ref, hb_ref, by_ref, yc_ref,
                  o0_ref, o1_ref, o2_ref, l0_ref, l1_ref, l2_ref, g_ref, wg_ref, gbias_ref,
                  wa_ref, wb_ref, wc_ref, wd_ref, wo_ref, out_ref, perm_scr):
    d_model = x_ref.shape[2]
    tm = x_ref.shape[1]
    x = x_ref[0]
    var = jnp.mean(x * x, axis=-1, keepdims=True)
    hn = (x * lax.rsqrt(var + EPS) * g_ref[...]).astype(BF16)

    def natural(ref, slot):
        dil = ref.shape[1]
        if dil == 1:
            return ref[0, 0].astype(F32)
        for r in range(dil):
            rows = ref[0, r].astype(F32)
            for j in range(rows.shape[1] // LANES):
                perm_scr[slot, j, pl.ds(r, tm // dil, stride=dil), :] = rows[:, j * LANES:(j + 1) * LANES]
        return jnp.concatenate([perm_scr[slot, j] for j in range(perm_scr.shape[1])], axis=1)

    yb = ((hf_ref[0] + hb_ref[0]) * _gelu_tanh(by_ref[0].astype(F32))).astype(BF16)

    l0, l1, l2 = natural(l0_ref, 0), natural(l1_ref, 0), natural(l2_ref, 1)
    o0, o1, o2 = natural(o0_ref, 2), natural(o1_ref, 2), natural(o2_ref, 3)
    mx = jnp.maximum(jnp.maximum(l0, l1), l2)
    e0, e1, e2 = jnp.exp2(l0 - mx), jnp.exp2(l1 - mx), jnp.exp2(l2 - mx)
    yd = ((e0 * o0 + e1 * o1 + e2 * o2) / (e0 + e1 + e2)).astype(BF16)

    def gate_logits(kk):
        return lambda: jnp.dot(hn, wg_ref[:, kk * d_model:(kk + 1) * d_model], preferred_element_type=F32)

    def branch(y, w_ref):
        return lambda: jnp.dot(y, w_ref[...], preferred_element_type=F32)

    dots = []
    for kk, (y, w_ref) in enumerate(((ya_ref[0], wa_ref), (yb, wb_ref), (yc_ref[0], wc_ref), (yd, wd_ref))):
        dots += [gate_logits(kk), branch(y, w_ref)]
    pending = [dots[0](), dots[1]()]
    merged = None
    for kk in range(N_BRANCH):
        pending += [d() for d in dots[2 * kk + 2:2 * kk + 4]]
        logits, proj = pending.pop(0), pending.pop(0)
        term = _sigmoid(logits + gbias_ref[kk:kk + 1, :]) * proj
        merged = term if merged is None else merged + term

    out_ref[0] = x + jnp.dot(merged.astype(BF16), wo_ref[...], preferred_element_type=F32)


def _merge(x, ya, hf, hb, by, yc, o_d, lse_d, g, wg, gbias, wa, wb, wc, wd, wo):
    b, s, d = x.shape
    tm = min(TM_MERGE, s)
    tok = lambda width: pl.BlockSpec((1, tm, width), lambda bb, i: (bb, i, 0))
    dw = D_HEADS * HEAD_DIM
    res = [pl.BlockSpec((1, dil, tm // dil, dw), lambda bb, i: (bb, 0, i, 0)) for _, dil in D_PATTERNS]
    return pl.pallas_call(
        _merge_kernel,
        grid=(b, s // tm),
        in_specs=[tok(d), tok(512), tok(LRU_WIDTH), tok(LRU_WIDTH), tok(LRU_WIDTH), tok(512),
                  *res, *res, _resident(g.shape), _resident(wg.shape),
                  _resident(gbias.shape), _resident(wa.shape), _resident(wb.shape),
                  _resident(wc.shape), _resident(wd.shape), _resident(wo.shape)],
        out_specs=tok(d),
        out_shape=jax.ShapeDtypeStruct((b, s, d), F32),
        scratch_shapes=[pltpu.VMEM((4, dw // LANES, tm, LANES), F32)],
        compiler_params=_cparams(("parallel", "parallel")),
        name="merge",
    )(x, ya, hf, hb, by, yc, *o_d, *lse_d, g, wg, gbias, wa, wb, wc, wd, wo)


def _mlp_kernel(x_ref, g_ref, w1_ref, w2_ref, gf_ref, out_ref, *, final_norm):
    x = x_ref[0]
    var = jnp.mean(x * x, axis=-1, keepdims=True)
    hn = (x * lax.rsqrt(var + EPS) * g_ref[...]).astype(BF16)
    hidden = w1_ref.shape[1]
    acc = x
    for c in range(hidden // MLP_CHUNK):
        h = jnp.dot(hn, w1_ref[:, c * MLP_CHUNK:(c + 1) * MLP_CHUNK], preferred_element_type=F32)
        h = jnp.square(jnp.maximum(h, 0.0)).astype(BF16)
        acc = acc + jnp.dot(h, w2_ref[c * MLP_CHUNK:(c + 1) * MLP_CHUNK, :], preferred_element_type=F32)
    if final_norm:
        var = jnp.mean(acc * acc, axis=-1, keepdims=True)
        acc = acc * lax.rsqrt(var + EPS) * gf_ref[...]
    out_ref[0] = acc


def _mlp(x, g, w1, w2, g_final, final_norm):
    b, s, d = x.shape
    tm = min(TM_MLP, s)
    tok = pl.BlockSpec((1, tm, d), lambda bb, i: (bb, i, 0))
    return pl.pallas_call(
        functools.partial(_mlp_kernel, final_norm=final_norm),
        grid=(b, s // tm),
        in_specs=[tok, _resident(g.shape), _resident(w1.shape), _resident(w2.shape), _resident(g_final.shape)],
        out_specs=tok,
        out_shape=jax.ShapeDtypeStruct((b, s, d), F32),
        compiler_params=_cparams(("parallel", "parallel")),
        name="mlp",
    )(x, g, w1, w2, g_final)


def _dense_block_diag(w):
    nb, bw, _ = w.shape
    eye = jnp.eye(nb, dtype=w.dtype)
    return jnp.einsum('ncd,nm->ncmd', w, eye).reshape(nb * bw, nb * bw)


def kernel(x, norm_mix_g, w_in, gate_bias, qk_norm_g, conv_w, conv_b, lru_gate_w, lru_gate_b, lru_lambda, sink_logit, w_proj_a, w_proj_b, w_proj_c, w_proj_d, w_out, norm_mlp_g, w_mlp1, w_mlp2, norm_final_g):
    b, s, d = x.shape
    depth = w_in.shape[0]
    ax_t, seq_t = _rope_tables(s)
    dw = D_HEADS * HEAD_DIM
    n_groups = len(D_PATTERNS)

    for l in range(depth):
        qk_gain = jnp.tile(qk_norm_g[l], (1, LANES // HEAD_DIM))
        w_in_l = w_in[l].astype(BF16)
        gate_w_dense = jnp.stack([jnp.stack([_dense_block_diag(lru_gate_w[l, dd, gg]) for gg in range(2)])
                                  for dd in range(2)]).astype(BF16)
        outs = _inproj(x, norm_mix_g[l][None, :], w_in_l[:, :N_IN], qk_gain, ax_t, seq_t,
                       conv_w[l], conv_b[l][None, :], gate_w_dense, lru_gate_b[l], lru_lambda[l])
        qta, ka, vta, af, uf, ab, ub, by, qc, kc, vc = outs[:11]
        qd, kd, vd = outs[11:11 + n_groups], outs[11 + n_groups:11 + 2 * n_groups], outs[11 + 2 * n_groups:11 + 3 * n_groups]

        ya = _attn_a(qta, ka, vta)

        hf, hb = _lru(af, uf, ab, ub)

        yc = _band_attention(qc[:, None], kc[:, None], vc[:, None], nkv=C_KV_HEADS,
                             group=C_HEADS // C_KV_HEADS, half_window=C_HALF_WINDOW, sink=sink_logit[l])[:, 0]

        o_d, lse_d = [], []
        for gi, (window, dil) in enumerate(D_PATTERNS):
            o, lse = _band_attention(qd[gi], kd[gi], vd[gi], nkv=D_HEADS, group=1,
                                     half_window=window // (2 * dil), want_lse=True)
            o_d.append(o)
            lse_d.append(lse)

        x = _merge(x, ya, hf, hb, by, yc, o_d, lse_d, norm_mix_g[l][None, :], w_in_l[:, N_IN:], gate_bias[l],
                   w_proj_a[l].astype(BF16), w_proj_b[l].astype(BF16), w_proj_c[l].astype(BF16),
                   w_proj_d[l].astype(BF16), w_out[l].astype(BF16))
        x = _mlp(x, norm_mlp_g[l][None, :], w_mlp1[l].astype(BF16), w_mlp2[l].astype(BF16),
                 norm_final_g[None, :], final_norm=(l == depth - 1))
    return x
```

```python
import functools
import math

import jax
import jax.numpy as jnp
from jax import lax
from jax.experimental import pallas as pl
from jax.experimental.pallas import tpu as pltpu

F32 = jnp.float32
BF16 = jnp.bfloat16

HEAD_DIM = 64
ROPE_THETA = 10000.0
GRID_W = 64
EPS = 1e-6
MASK_VALUE = -1e30
LOG2E = math.log2(math.e)

A_HEADS, A_KV_HEADS = 8, 2
LRU_WIDTH, LRU_BLOCKS, LRU_C = 512, 8, 8.0
CONV_W, CONV_LEFT = 4, 2
C_HEADS, C_KV_HEADS, C_HALF_WINDOW = 8, 2, 128
D_PATTERNS = ((128, 1), (512, 4), (2048, 16))
D_HEADS = 4
N_BRANCH = 4

LANES = 128
SUBLANES = 8
VMEM_LIMIT_BYTES = 56 * 1024 * 1024

TM_INPROJ = 512
TQ_ATTN_A = 256
SUB_ATTN_A = 128
TILES_PER_STEP_ATTN_A = 2
LOOKAHEAD_ATTN_A = 14
LOOKAHEAD_BAND = 6
LOOKAHEAD_INPROJ = 6
SUB_BAND = 128
CHUNKS_PER_BODY_ATTN_A = 16
VT_ROWS_A = HEAD_DIM + 16
TQ_BAND = 256
BAND_TILES_PER_STEP = 4
TT_LRU = 1024
TM_MERGE = 512
TM_MLP = 512
MLP_CHUNK = 1024

Q_SCALE = HEAD_DIM ** -0.5 * LOG2E


def _cparams(sem):
    return pltpu.CompilerParams(dimension_semantics=sem, vmem_limit_bytes=VMEM_LIMIT_BYTES)


def _resident(shape):
    nd = len(shape)
    return pl.BlockSpec(shape, lambda *_: (0,) * nd, pipeline_mode=pl.Buffered(1))


def _rope_tables(seq):
    pos = jnp.arange(seq, dtype=F32)
    lane = jnp.arange(LANES)
    d = lane % HEAD_DIM

    inv = ROPE_THETA ** (-jnp.arange(0, HEAD_DIM, 2, dtype=F32) / HEAD_DIM)
    ang = pos[:, None] * inv[None, :]
    f = d % (HEAD_DIM // 2)
    cos_s, sin_s = jnp.cos(ang)[:, f], jnp.sin(ang)[:, f]
    lo = (d < HEAD_DIM // 2)[None, :]
    seq_t = (cos_s, jnp.where(lo, -sin_s, 0.0), jnp.where(lo, 0.0, sin_s))

    quarter = HEAD_DIM // 4
    inv_ax = ROPE_THETA ** (-jnp.arange(0, HEAD_DIM // 2, 2, dtype=F32) / (HEAD_DIM // 2))
    row_pos = (jnp.arange(seq) // GRID_W).astype(F32)
    col_pos = (jnp.arange(seq) % GRID_W).astype(F32)
    e = d % (HEAD_DIM // 2)
    fa = e % quarter
    is_col = (d >= HEAD_DIM // 2)[None, :]
    ang_ax = jnp.where(is_col, col_pos[:, None], row_pos[:, None]) * inv_ax[fa][None, :]
    cos_a, sin_a = jnp.cos(ang_ax), jnp.sin(ang_ax)
    lo_a = (e < quarter)[None, :]
    ax_t = (cos_a, jnp.where(lo_a, -sin_a, 0.0), jnp.where(lo_a, 0.0, sin_a))
    return ax_t, seq_t


def _rotate(x, tables, half):
    cos, sin_a, sin_b = tables
    return x * cos + pltpu.roll(x, LANES - half, 1) * sin_a + pltpu.roll(x, half, 1) * sin_b


def _split_dot(x, m):
    hi = x.astype(BF16)
    lo = (x - hi.astype(F32)).astype(BF16)
    return jnp.dot(jnp.concatenate([hi, lo], axis=1), jnp.concatenate([m, m], axis=0),
                   preferred_element_type=F32)


def _head_mean_matrix():
    r = lax.broadcasted_iota(jnp.int32, (LANES, LANES), 0) // HEAD_DIM
    c = lax.broadcasted_iota(jnp.int32, (LANES, LANES), 1) // HEAD_DIM
    return jnp.where(r == c, 1.0 / HEAD_DIM, 0.0).astype(BF16)


def _head_rmsnorm(x, gain, mean_mat):
    var = _split_dot(x * x, mean_mat)
    return x * lax.rsqrt(var + EPS) * gain


IN_WIDTHS = (512, 128, 128, 512, 512, 512, 128, 128, 768, 768, 768)
IN_OFFSETS = tuple(sum(IN_WIDTHS[:i]) for i in range(len(IN_WIDTHS)))
N_IN = sum(IN_WIDTHS)


def _inproj_kernel(x_ref, xprev_ref, xnext_ref, g_ref, w_ref, qkg_ref,
                   ca_ref, saa_ref, sab_ref, cs_ref, ssa_ref, ssb_ref,
                   cw_ref, cb_ref, gw_ref, gb_ref, lam_ref,
                   qta_ref, ka_ref, vta_ref, af_ref, uf_ref, ab_ref, ub_ref, by_ref,
                   qc_ref, kc_ref, vc_ref, *rest):
    d_refs, perm_scr, xpad_scr = rest[:9], rest[9], rest[10]
    tile = pl.program_id(0)
    n_tiles = pl.num_programs(0)

    def normed(rows):
        var = jnp.mean(rows * rows, axis=-1, keepdims=True)
        return (rows * lax.rsqrt(var + EPS) * g_ref[...]).astype(BF16)

    x = x_ref[0]
    hn = normed(x)

    def proj(seg, lo=0, width=None):
        off = IN_OFFSETS[seg] + lo
        width = IN_WIDTHS[seg] if width is None else width
        return lambda: jnp.dot(hn, w_ref[:, off:off + width], preferred_element_type=F32)

    ax_t = (ca_ref[...], saa_ref[...], sab_ref[...])
    seq_t = (cs_ref[...], ssa_ref[...], ssb_ref[...])
    mean_mat = _head_mean_matrix()
    gq = qkg_ref[0:1, :]
    gk = qkg_ref[1:2, :]
    tm = x.shape[0]
    dw = D_HEADS * HEAD_DIM

    def finish_aq(acc):
        for j in range(IN_WIDTHS[0] // LANES):
            piece = _head_rmsnorm(acc[:, j * LANES:(j + 1) * LANES], gq, mean_mat)
            piece = _rotate(piece, ax_t, HEAD_DIM // 4) * Q_SCALE
            qta_ref[0, j * LANES:(j + 1) * LANES, :] = piece.T.astype(BF16)

    def finish_akv(acc):
        ak = _rotate(_head_rmsnorm(acc[:, :LANES], gk, mean_mat), ax_t, HEAD_DIM // 4).astype(BF16)
        avt = acc[:, LANES:].T.astype(BF16)
        pad_row = lax.broadcasted_iota(jnp.int32, (VT_ROWS_A - HEAD_DIM, tm), 0)
        ones_rows = (pad_row == 0).astype(BF16)
        for g in range(A_KV_HEADS):
            ka_ref[0, g] = ak[:, g * HEAD_DIM:(g + 1) * HEAD_DIM]
            vta_ref[0, g, 0] = jnp.concatenate([avt[g * HEAD_DIM:(g + 1) * HEAD_DIM, :], ones_rows], axis=0)

    lru = {}

    def finish_bx(acc):
        halo = jnp.dot(normed(jnp.concatenate([xprev_ref[0], xnext_ref[0]], axis=0)),
                       w_ref[:, IN_OFFSETS[3]:IN_OFFSETS[3] + LRU_WIDTH], preferred_element_type=F32)
        xpad_scr[0:SUBLANES] = jnp.where(tile > 0, halo[:SUBLANES], 0.0)
        xpad_scr[SUBLANES:SUBLANES + tm] = acc
        xpad_scr[SUBLANES + tm:] = jnp.where(tile < n_tiles - 1, halo[SUBLANES:], 0.0)
        xc = cb_ref[...]
        base = SUBLANES - CONV_LEFT
        for jj in range(CONV_W):
            xc = xc + xpad_scr[base + jj:base + jj + tm, :] * cw_ref[jj:jj + 1, :]
        lru["xc"] = xc
        lru["xcb"] = xc.astype(BF16)

    def gate_dots():
        return [jnp.dot(lru["xcb"], gw_ref[d, g], preferred_element_type=F32) for d in range(2) for g in range(2)]

    def finish_gates(dots):
        xc = lru["xc"]
        for d, (a_ref, u_ref) in enumerate(((af_ref, uf_ref), (ab_ref, ub_ref))):
            c2 = (-0.5 * LRU_C * LOG2E) * _softplus(-lam_ref[d:d + 1, :])
            a = jnp.exp2(c2 * jnp.tanh(0.5 * dots[2 * d] + 0.5 * gb_ref[d, 0:1, :]) + c2)
            gi = _sigmoid(dots[2 * d + 1] + gb_ref[d, 1:2, :])
            a_ref[0] = a
            z = (1.0 - a) * (1.0 + a)
            root = jnp.where(z > 0.0, z * lax.rsqrt(z), 0.0)
            u_ref[0] = root * (gi * xc)

    def finish_by(acc):
        by_ref[0] = acc.astype(BF16)

    def rope_piece(piece, scale):
        piece = _rotate(piece, seq_t, HEAD_DIM // 2)
        return piece if scale is None else piece * scale

    def finish_cq(acc):
        for j in range(IN_WIDTHS[5] // LANES):
            qc_ref[0, :, j * LANES:(j + 1) * LANES] = rope_piece(acc[:, j * LANES:(j + 1) * LANES], Q_SCALE).astype(BF16)

    def finish_ckv(acc):
        kc_ref[0] = rope_piece(acc[:, :LANES], None).astype(BF16)
        vc_ref[0] = acc[:, LANES:].astype(BF16)

    def finish_d(ref, dil, scale, rope):
        def finish(acc):
            for j in range(dw // LANES):
                piece = acc[:, j * LANES:(j + 1) * LANES]
                if rope:
                    piece = rope_piece(piece, scale)
                if dil == 1:
                    ref[0, 0, :, j * LANES:(j + 1) * LANES] = piece.astype(BF16)
                else:
                    perm_scr[j] = piece
            if dil > 1:
                for r in range(dil):
                    rows = [perm_scr[j, pl.ds(r, tm // dil, stride=dil), :] for j in range(dw // LANES)]
                    ref[0, r] = jnp.concatenate(rows, axis=1).astype(BF16)
        return finish

    segments = [(proj(0), finish_aq), (proj(1, 0, 2 * LANES), finish_akv), (proj(3), finish_bx),
                (proj(4), finish_by), (gate_dots, finish_gates), (proj(5), finish_cq),
                (proj(6, 0, 2 * LANES), finish_ckv)]
    for gi, (_, dil) in enumerate(D_PATTERNS):
        for ti, (seg, scale, rope) in enumerate(((8, Q_SCALE, True), (9, None, True), (10, None, False))):
            segments.append((proj(seg, gi * dw, dw), finish_d(d_refs[ti * len(D_PATTERNS) + gi], dil, scale, rope)))

    issued, pending = 0, []
    for idx, (_, finish) in enumerate(segments):
        horizon = min(len(segments), idx + 1 + LOOKAHEAD_INPROJ)
        while issued < horizon and (segments[issued][0] is not gate_dots or "xcb" in lru):
            pending.append(segments[issued][0]())
            issued += 1
        finish(pending.pop(0))


def _inproj(x, g, w_bf16, qk_gain128, ax_t, seq_t, conv_w, conv_b, gate_w, gate_b, lam):
    b, s, d = x.shape
    tm = min(TM_INPROJ, s)
    nt = s // tm
    per, n8 = tm // SUBLANES, s // SUBLANES
    halo_prev = pl.BlockSpec((1, SUBLANES, d), lambda i, bb: (bb, jnp.maximum(i * per - 1, 0), 0))
    halo_next = pl.BlockSpec((1, SUBLANES, d), lambda i, bb: (bb, jnp.minimum((i + 1) * per, n8 - 1), 0))
    tok = lambda width: pl.BlockSpec((1, tm, width), lambda i, bb: (bb, i, 0))
    tab = pl.BlockSpec((tm, LANES), lambda i, bb: (i, 0))
    dw = D_HEADS * HEAD_DIM
    d_shapes = tuple(jax.ShapeDtypeStruct((b, dil, s // dil, dw), BF16) for _, dil in D_PATTERNS)
    d_specs = tuple(pl.BlockSpec((1, dil, tm // dil, dw), lambda i, bb: (bb, 0, i, 0)) for _, dil in D_PATTERNS)
    out_shape = (
        jax.ShapeDtypeStruct((b, A_HEADS * HEAD_DIM, s), BF16),
        jax.ShapeDtypeStruct((b, A_KV_HEADS, s, HEAD_DIM), BF16),
        jax.ShapeDtypeStruct((b, A_KV_HEADS, nt, VT_ROWS_A, tm), BF16),
        jax.ShapeDtypeStruct((b, s, LRU_WIDTH), F32),
        jax.ShapeDtypeStruct((b, s, LRU_WIDTH), F32),
        jax.ShapeDtypeStruct((b, s, LRU_WIDTH), F32),
        jax.ShapeDtypeStruct((b, s, LRU_WIDTH), F32),
        jax.ShapeDtypeStruct((b, s, LRU_WIDTH), BF16),
        jax.ShapeDtypeStruct((b, s, 512), BF16),
        jax.ShapeDtypeStruct((b, s, 128), BF16),
        jax.ShapeDtypeStruct((b, s, 128), BF16),
    ) + d_shapes * 3
    out_specs = (
        pl.BlockSpec((1, A_HEADS * HEAD_DIM, tm), lambda i, bb: (bb, 0, i)),
        pl.BlockSpec((1, A_KV_HEADS, tm, HEAD_DIM), lambda i, bb: (bb, 0, i, 0)),
        pl.BlockSpec((1, A_KV_HEADS, 1, VT_ROWS_A, tm), lambda i, bb: (bb, 0, i, 0, 0)),
        tok(LRU_WIDTH), tok(LRU_WIDTH), tok(LRU_WIDTH), tok(LRU_WIDTH), tok(LRU_WIDTH), tok(512), tok(128), tok(128),
    ) + d_specs * 3
    return pl.pallas_call(
        _inproj_kernel,
        grid=(nt, b),
        in_specs=[tok(d), halo_prev, halo_next, _resident((1, d)), _resident((d, N_IN)), _resident((2, LANES)),
                  tab, tab, tab, tab, tab, tab,
                  _resident(conv_w.shape), _resident(conv_b.shape), _resident(gate_w.shape),
                  _resident(gate_b.shape), _resident(lam.shape)],
        out_specs=out_specs,
        out_shape=out_shape,
        scratch_shapes=[pltpu.VMEM((dw // LANES, tm, LANES), F32),
                        pltpu.VMEM((tm + 2 * SUBLANES, LRU_WIDTH), F32)],
        compiler_params=_cparams(("parallel", "parallel")),
        name="inproj",
    )(x, x, x, g, w_bf16, qk_gain128, *ax_t, *seq_t, conv_w, conv_b, gate_w, gate_b, lam)


def _sublane_allmax(x):
    for shift in (4, 2, 1):
        x = jnp.maximum(x, pltpu.roll(x, shift, 0))
    return x


def _attn_a_kernel(q_ref, k_ref, v_ref, o_ref, m_ref, acc_ref, *, group, sub):
    n_chunks, chunk = v_ref.shape[2], v_ref.shape[4]
    tq = min(TQ_ATTN_A, q_ref.shape[2])
    n_tiles = q_ref.shape[2] // tq
    m_ref[...] = jnp.full(m_ref.shape, MASK_VALUE, F32)
    acc_ref[...] = jnp.zeros(acc_ref.shape, F32)

    per_body = min(CHUNKS_PER_BODY_ATTN_A, n_chunks)
    n_sub = chunk // sub
    pair = 2 if n_sub % 2 == 0 else 1
    units = [(cc, u, h) for cc in range(per_body) for up in range(n_sub // pair)
             for h in range(n_tiles * group) for u in range(up * pair, (up + 1) * pair)]

    def body(it, carry):
        c0 = it * per_body

        def scores(idx):
            cc, u, h = units[idx]
            row0 = pl.multiple_of((c0 + cc) * chunk + u * sub, sub)
            k = k_ref[0, 0, pl.ds(row0, sub), :]
            j, head = divmod(h, group)
            qt = q_ref[0, head * HEAD_DIM:(head + 1) * HEAD_DIM, j * tq:(j + 1) * tq]
            return jnp.dot(k, qt, preferred_element_type=F32)

        pending = [scores(i) for i in range(LOOKAHEAD_ATTN_A)]
        probs = []
        for idx, (cc, u, h) in enumerate(units):
            if idx + LOOKAHEAD_ATTN_A < len(units):
                pending.append(scores(idx + LOOKAHEAD_ATTN_A))
            s3 = pending.pop(0).reshape(sub // SUBLANES, SUBLANES, tq)
            if not probs:
                m_start = m_ref[h]
                m_old = m_start
            m_new = jnp.maximum(m_old, _sublane_allmax(jnp.max(s3, axis=0)))
            if probs:
                beta = jnp.exp2(m_old - m_new).astype(BF16)
                beta = jnp.concatenate([beta, beta], axis=0)[None]
                probs = [(p.reshape(sub // 16, 16, tq) * beta).reshape(sub, tq) for p in probs]
            probs.append(jnp.exp2(s3 - m_new[None]).reshape(sub, tq).astype(BF16))
            m_old = m_new
            if len(probs) == pair:
                u0 = u - (pair - 1)
                vt = v_ref[0, 0, c0 + cc, :, u0 * sub:(u + 1) * sub]
                pv = jnp.dot(vt, jnp.concatenate(probs, axis=0), preferred_element_type=F32)
                alpha = jnp.exp2(m_start - m_new)
                acc_ref[h] = acc_ref[h] * alpha[None] + pv.reshape(VT_ROWS_A // SUBLANES, SUBLANES, tq)
                m_ref[h] = m_new
                probs = []
        return carry

    lax.fori_loop(0, n_chunks // per_body, body, 0)
    for j in range(n_tiles):
        outs = []
        for h in range(j * group, (j + 1) * group):
            acc = acc_ref[h].reshape(VT_ROWS_A, tq)
            outs.append(acc[:HEAD_DIM] / acc[HEAD_DIM:HEAD_DIM + 1])
        o_ref[0, j * tq:(j + 1) * tq, :] = jnp.concatenate(outs, axis=0).T.astype(o_ref.dtype)


def _attn_a(qt, k, vt):
    b, hd_all, s = qt.shape
    nkv, nt, chunk = vt.shape[1], vt.shape[2], vt.shape[4]
    group = hd_all // HEAD_DIM // nkv
    tq = min(TQ_ATTN_A * TILES_PER_STEP_ATTN_A, s)
    n_state = group * (tq // min(TQ_ATTN_A, s))
    gw = group * HEAD_DIM
    return pl.pallas_call(
        functools.partial(_attn_a_kernel, group=group, sub=min(SUB_ATTN_A, chunk)),
        grid=(b, nkv, s // tq),
        in_specs=[pl.BlockSpec((1, gw, tq), lambda bb, g, i: (bb, g, i)),
                  pl.BlockSpec((1, 1, s, HEAD_DIM), lambda bb, g, i: (bb, g, 0, 0)),
                  pl.BlockSpec((1, 1, nt, VT_ROWS_A, chunk), lambda bb, g, i: (bb, g, 0, 0, 0))],
        out_specs=pl.BlockSpec((1, tq, gw), lambda bb, g, i: (bb, i, g)),
        out_shape=jax.ShapeDtypeStruct((b, s, hd_all), BF16),
        scratch_shapes=[pltpu.VMEM((n_state, SUBLANES, min(TQ_ATTN_A, s)), F32),
                        pltpu.VMEM((n_state, VT_ROWS_A // SUBLANES, SUBLANES, min(TQ_ATTN_A, s)), F32)],
        compiler_params=_cparams(("parallel", "parallel", "parallel")),
        name="attn_a",
    )(qt, k, vt)


def _band_kernel(*refs, nkv, group, half_window, seq_len, has_sink, want_lse):
    q_ref, kp_ref, km_ref, kn_ref, vp_ref, vm_ref, vn_ref = refs[:7]
    rest = list(refs[7:])
    sink_ref = rest.pop(0) if has_sink else None
    o_ref = rest.pop(0)
    lse_ref = rest.pop(0) if want_lse else None

    tq_blk = q_ref.shape[2]
    tq = min(TQ_BAND, tq_blk)
    halo = kp_ref.shape[2]
    nk = tq + 2 * halo
    sub = SUB_BAND if nk % SUB_BAND == 0 else halo
    n_heads = nkv * group
    blk0 = pl.program_id(2) * tq_blk

    kcat = jnp.concatenate([kp_ref[0, 0], km_ref[0, 0], kn_ref[0, 0]], axis=0)
    vt = jnp.concatenate([vp_ref[0, 0], vm_ref[0, 0], vn_ref[0, 0]], axis=0).T

    kw = min(LANES, nkv * HEAD_DIM)
    zeros = jnp.zeros((HEAD_DIM, tq), kcat.dtype)
    q_pads = {}

    def padded_queries(j):
        if j not in q_pads:
            qt = q_ref[0, 0, j * tq:(j + 1) * tq, :].T
            pads = []
            for h in range(n_heads):
                slot = ((h // group) * HEAD_DIM % kw) // HEAD_DIM
                parts = [zeros] * (kw // HEAD_DIM)
                parts[slot] = qt[h * HEAD_DIM:(h + 1) * HEAD_DIM, :]
                pads.append(parts[0] if len(parts) == 1 else jnp.concatenate(parts, axis=0))
            q_pads.clear()
            q_pads[j] = pads
        return q_pads[j]

    units = [(j, kb, h) for j in range(tq_blk // tq) for kb in range(nk // sub) for h in range(n_heads)]

    cw = LANES if tq % LANES == 0 else tq
    n_ch = tq // cw

    def query_chunks(kb):
        k_lo = kb * sub - halo
        lo = max(0, (k_lo - half_window) // cw)
        hi = min(n_ch, -(-(k_lo + sub + half_window) // cw))
        return list(range(lo, hi))

    def scores(idx):
        j, kb, h = units[idx]
        row0 = j * tq + kb * sub
        lane0 = (h // group) * HEAD_DIM // kw * kw
        cs = query_chunks(kb)
        return jnp.dot(kcat[row0:row0 + sub, lane0:lane0 + kw],
                       padded_queries(j)[h][:, cs[0] * cw:(cs[-1] + 1) * cw], preferred_element_type=F32)

    key_row = lax.broadcasted_iota(jnp.int32, (sub, cw), 0)
    query = lax.broadcasted_iota(jnp.int32, (sub, cw), 1)

    def valid_mask(j, kb, c):
        kpos = blk0 + j * tq - halo + kb * sub + key_row
        rel = kpos - (blk0 + j * tq + c * cw + query)
        return (jnp.abs(rel) <= half_window) & (kpos >= 0) & (kpos < seq_len)

    def start_state():
        if has_sink:
            m0 = [jnp.full((SUBLANES, cw), sink_ref[h] * LOG2E, F32) for h in range(n_heads)]
            l0 = (lax.broadcasted_iota(jnp.int32, (SUBLANES, cw), 0) == 0).astype(F32)
        else:
            m0 = [jnp.full((SUBLANES, cw), 0.1 * MASK_VALUE, F32)] * n_heads
            l0 = jnp.zeros((SUBLANES, cw), F32)
        a0 = jnp.zeros((HEAD_DIM // SUBLANES, SUBLANES, cw), F32)
        return ([[m0[h]] * n_ch for h in range(n_heads)], [[l0] * n_ch for _ in range(n_heads)],
                [[a0] * n_ch for _ in range(n_heads)])

    def join(parts):
        return parts[0] if len(parts) == 1 else jnp.concatenate(parts, axis=1)

    def finish(j, m, l, acc):
        outs, lses = [], []
        for h in range(n_heads):
            l_row = join([jnp.sum(x, axis=0, keepdims=True) for x in l[h]])
            outs.append(join([x.reshape(HEAD_DIM, cw) for x in acc[h]]) / l_row)
            if want_lse:
                lse = join([x[0:1] for x in m[h]]) + jnp.log2(l_row)
                lses.append(jnp.broadcast_to(lse, (HEAD_DIM, tq)))
        o_ref[0, 0, j * tq:(j + 1) * tq, :] = jnp.concatenate(outs, axis=0).T.astype(o_ref.dtype)
        if want_lse:
            lse_ref[0, 0, j * tq:(j + 1) * tq, :] = jnp.concatenate(lses, axis=0).T

    look = min(LOOKAHEAD_BAND, len(units))
    pending = [scores(i) for i in range(look)]
    mask_key, masks = None, None
    for idx, (j, kb, h) in enumerate(units):
        if idx + look < len(units):
            pending.append(scores(idx + look))
        if kb == 0 and h == 0:
            m, l, acc = start_state()
        cs = query_chunks(kb)
        if mask_key != (j, kb):
            mask_key, masks = (j, kb), [valid_mask(j, kb, c) for c in cs]
        g = h // group
        s = pending.pop(0)
        probs, alphas = [], []
        for i, c in enumerate(cs):
            s_c = s if len(cs) == 1 else s[:, i * cw:(i + 1) * cw]
            s3 = jnp.where(masks[i], s_c, MASK_VALUE).reshape(sub // SUBLANES, SUBLANES, cw)
            m_new = jnp.maximum(m[h][c], _sublane_allmax(jnp.max(s3, axis=0)))
            alpha = jnp.exp2(m[h][c] - m_new)
            p3 = jnp.exp2(s3 - m_new[None])
            l[h][c] = alpha * l[h][c] + jnp.sum(p3, axis=0)
            m[h][c] = m_new
            probs.append(p3.reshape(sub, cw).astype(BF16))
            alphas.append(alpha)
        col0 = j * tq + kb * sub
        pv = jnp.dot(vt[g * HEAD_DIM:(g + 1) * HEAD_DIM, col0:col0 + sub], join(probs),
                     preferred_element_type=F32)
        for i, c in enumerate(cs):
            pv_c = pv if len(cs) == 1 else pv[:, i * cw:(i + 1) * cw]
            acc[h][c] = acc[h][c] * alphas[i][None] + pv_c.reshape(HEAD_DIM // SUBLANES, SUBLANES, cw)
        if kb == nk // sub - 1 and h == n_heads - 1:
            finish(j, m, l, acc)


def _band_attention(q, k, v, *, nkv, group, half_window, sink=None, want_lse=False):
    b, n_seq, seq_len, wq = q.shape
    wk = nkv * HEAD_DIM
    halo = half_window
    tq = min(TQ_BAND * BAND_TILES_PER_STEP, seq_len)
    per = tq // halo
    n_halo = seq_len // halo

    main = lambda width: pl.BlockSpec((1, 1, tq, width), lambda bb, r, i: (bb, r, i, 0))
    prev = lambda width: pl.BlockSpec(
        (1, 1, halo, width), lambda bb, r, i: (bb, r, jnp.maximum(i * per - 1, 0), 0))
    nxt = lambda width: pl.BlockSpec(
        (1, 1, halo, width), lambda bb, r, i: (bb, r, jnp.minimum((i + 1) * per, n_halo - 1), 0))
    in_specs = [main(wq), prev(wk), main(wk), nxt(wk), prev(wk), main(wk), nxt(wk)]
    args = [q, k, k, k, v, v, v]
    if sink is not None:
        in_specs.append(pl.BlockSpec(memory_space=pltpu.SMEM))
        args.append(sink)
    out_block = main(wq)
    out_shape = [jax.ShapeDtypeStruct(q.shape, BF16)]
    out_specs = [out_block]
    if want_lse:
        out_shape.append(jax.ShapeDtypeStruct(q.shape, F32))
        out_specs.append(out_block)
    res = pl.pallas_call(
        functools.partial(_band_kernel, nkv=nkv, group=group, half_window=half_window,
                          seq_len=seq_len, has_sink=sink is not None, want_lse=want_lse),
        grid=(b, n_seq, seq_len // tq),
        in_specs=in_specs,
        out_specs=out_specs,
        out_shape=out_shape,
        compiler_params=_cparams(("parallel", "parallel", "parallel")),
        name="band_attn",
    )(*args)
    return res if want_lse else res[0]


def _sigmoid(x):
    return 0.5 * jnp.tanh(0.5 * x) + 0.5


def _softplus(x):
    return jnp.maximum(x, 0.0) + jnp.log1p(jnp.exp(-jnp.abs(x)))


def _lru_kernel(af_ref, uf_ref, ab_ref, ub_ref, hf_ref, hb_ref, carry_ref):
    i = pl.program_id(1)
    tt = af_ref.shape[1]
    nblk = tt // SUBLANES

    @pl.when(i == 0)
    def _():
        carry_ref[...] = jnp.zeros(carry_ref.shape, F32)

    row = lax.broadcasted_iota(jnp.int32, (SUBLANES, LRU_WIDTH), 0)

    def scan_block(d, blk, carry, a_ref, u_ref, out_ref):
        a8 = a_ref[0, pl.ds(blk * SUBLANES, SUBLANES), :]
        u8 = u_ref[0, pl.ds(blk * SUBLANES, SUBLANES), :]
        entry = row == (0 if d == 0 else SUBLANES - 1)
        u8 = u8 + jnp.where(entry, a8 * carry, 0.0)
        for step in (1, 2, 4):
            if d == 0:
                shift, ident = step, row < step
            else:
                shift, ident = SUBLANES - step, row >= SUBLANES - step
            u8 = u8 + a8 * jnp.where(ident, 0.0, pltpu.roll(u8, shift, 0))
            if step < 4:
                a8 = a8 * pltpu.roll(a8, shift, 0)
        out_ref[0, pl.ds(blk * SUBLANES, SUBLANES), :] = u8
        last = u8[SUBLANES - 1:SUBLANES, :] if d == 0 else u8[0:1, :]
        return jnp.broadcast_to(last, (SUBLANES, LRU_WIDTH))

    def body(kb, carries):
        cf, cb = carries
        cf = scan_block(0, kb, cf, af_ref, uf_ref, hf_ref)
        cb = scan_block(1, nblk - 1 - kb, cb, ab_ref, ub_ref, hb_ref)
        return cf, cb

    cf, cb = lax.fori_loop(0, nblk, body, (carry_ref[0], carry_ref[1]))
    carry_ref[0] = cf
    carry_ref[1] = cb


def _lru(af, uf, ab, ub):
    b, s, w = af.shape
    tt = min(TT_LRU, s)
    nt = s // tt
    fwd = pl.BlockSpec((1, tt, w), lambda bb, i: (bb, i, 0))
    bwd = pl.BlockSpec((1, tt, w), lambda bb, i: (bb, nt - 1 - i, 0))
    return pl.pallas_call(
        _lru_kernel,
        grid=(b, nt),
        in_specs=[fwd, fwd, bwd, bwd],
        out_specs=[fwd, bwd],
        out_shape=[jax.ShapeDtypeStruct((b, s, w), F32), jax.ShapeDtypeStruct((b, s, w), F32)],
        scratch_shapes=[pltpu.VMEM((2, SUBLANES, w), F32)],
        compiler_params=_cparams(("parallel", "arbitrary")),
        name="lru",
    )(af, uf, ab, ub)


def _gelu_tanh(x):
    return 0.5 * x * (1.0 + jnp.tanh(math.sqrt(2.0 / math.pi) * (x + 0.044715 * (x * x * x))))


def _merge_kernel(x_ref, ya_ref, hf_ref, hb_ref, by_ref, yc_ref,
                  o0_ref, o1_ref, o2_ref, l0_ref, l1_ref, l2_ref, g_ref, wg_ref, gbias_ref,
                  wa_ref, wb_ref, wc_ref, wd_ref, wo_ref, out_ref, perm_scr):
    d_model = x_ref.shape[2]
    tm = x_ref.shape[1]
    x = x_ref[0]
    var = jnp.mean(x * x, axis=-1, keepdims=True)
    hn = (x * lax.rsqrt(var + EPS) * g_ref[...]).astype(BF16)

    def natural(ref, slot):
        dil = ref.shape[1]
        if dil == 1:
            return ref[0, 0].astype(F32)
        for r in range(dil):
            rows = ref[0, r].astype(F32)
            for j in range(rows.shape[1] // LANES):
                perm_scr[slot, j, pl.ds(r, tm // dil, stride=dil), :] = rows[:, j * LANES:(j + 1) * LANES]
        return jnp.concatenate([perm_scr[slot, j] for j in range(perm_scr.shape[1])], axis=1)

    yb = ((hf_ref[0] + hb_ref[0]) * _gelu_tanh(by_ref[0].astype(F32))).astype(BF16)

    l0, l1, l2 = natural(l0_ref, 0), natural(l1_ref, 0), natural(l2_ref, 1)
    o0, o1, o2 = natural(o0_ref, 2), natural(o1_ref, 2), natural(o2_ref, 3)
    mx = jnp.maximum(jnp.maximum(l0, l1), l2)
    e0, e1, e2 = jnp.exp2(l0 - mx), jnp.exp2(l1 - mx), jnp.exp2(l2 - mx)
    yd = ((e0 * o0 + e1 * o1 + e2 * o2) / (e0 + e1 + e2)).astype(BF16)

    def gate_logits(kk):
        return lambda: jnp.dot(hn, wg_ref[:, kk * d_model:(kk + 1) * d_model], preferred_element_type=F32)

    def branch(y, w_ref):
        return lambda: jnp.dot(y, w_ref[...], preferred_element_type=F32)

    dots = []
    for kk, (y, w_ref) in enumerate(((ya_ref[0], wa_ref), (yb, wb_ref), (yc_ref[0], wc_ref), (yd, wd_ref))):
        dots += [gate_logits(kk), branch(y, w_ref)]
    pending = [dots[0](), dots[1]()]
    merged = None
    for kk in range(N_BRANCH):
        pending += [d() for d in dots[2 * kk + 2:2 * kk + 4]]
        logits, proj = pending.pop(0), pending.pop(0)
        term = _sigmoid(logits + gbias_ref[kk:kk + 1, :]) * proj
        merged = term if merged is None else merged + term

    out_ref[0] = x + jnp.dot(merged.astype(BF16), wo_ref[...], preferred_element_type=F32)


def _merge(x, ya, hf, hb, by, yc, o_d, lse_d, g, wg, gbias, wa, wb, wc, wd, wo):
    b, s, d = x.shape
    tm = min(TM_MERGE, s)
    tok = lambda width: pl.BlockSpec((1, tm, width), lambda bb, i: (bb, i, 0))
    dw = D_HEADS * HEAD_DIM
    res = [pl.BlockSpec((1, dil, tm // dil, dw), lambda bb, i: (bb, 0, i, 0)) for _, dil in D_PATTERNS]
    return pl.pallas_call(
        _merge_kernel,
        grid=(b, s // tm),
        in_specs=[tok(d), tok(512), tok(LRU_WIDTH), tok(LRU_WIDTH), tok(LRU_WIDTH), tok(512),
                  *res, *res, _resident(g.shape), _resident(wg.shape),
                  _resident(gbias.shape), _resident(wa.shape), _resident(wb.shape),
                  _resident(wc.shape), _resident(wd.shape), _resident(wo.shape)],
        out_specs=tok(d),
        out_shape=jax.ShapeDtypeStruct((b, s, d), F32),
        scratch_shapes=[pltpu.VMEM((4, dw // LANES, tm, LANES), F32)],
        compiler_params=_cparams(("parallel", "parallel")),
        name="merge",
    )(x, ya, hf, hb, by, yc, *o_d, *lse_d, g, wg, gbias, wa, wb, wc, wd, wo)


def _mlp_kernel(x_ref, g_ref, w1_ref, w2_ref, gf_ref, out_ref, *, final_norm):
    x = x_ref[0]
    var = jnp.mean(x * x, axis=-1, keepdims=True)
    hn = (x * lax.rsqrt(var + EPS) * g_ref[...]).astype(BF16)
    hidden = w1_ref.shape[1]
    acc = x
    for c in range(hidden // MLP_CHUNK):
        h = jnp.dot(hn, w1_ref[:, c * MLP_CHUNK:(c + 1) * MLP_CHUNK], preferred_element_type=F32)
        h = jnp.square(jnp.maximum(h, 0.0)).astype(BF16)
        acc = acc + jnp.dot(h, w2_ref[c * MLP_CHUNK:(c + 1) * MLP_CHUNK, :], preferred_element_type=F32)
    if final_norm:
        var = jnp.mean(acc * acc, axis=-1, keepdims=True)
        acc = acc * lax.rsqrt(var + EPS) * gf_ref[...]
    out_ref[0] = acc


def _mlp(x, g, w1, w2, g_final, final_norm):
    b, s, d = x.shape
    tm = min(TM_MLP, s)
    tok = pl.BlockSpec((1, tm, d), lambda bb, i: (bb, i, 0))
    return pl.pallas_call(
        functools.partial(_mlp_kernel, final_norm=final_norm),
        grid=(b, s // tm),
        in_specs=[tok, _resident(g.shape), _resident(w1.shape), _resident(w2.shape), _resident(g_final.shape)],
        out_specs=tok,
        out_shape=jax.ShapeDtypeStruct((b, s, d), F32),
        compiler_params=_cparams(("parallel", "parallel")),
        name="mlp",
    )(x, g, w1, w2, g_final)


def _dense_block_diag(w):
    nb, bw, _ = w.shape
    eye = jnp.eye(nb, dtype=w.dtype)
    return jnp.einsum('ncd,nm->ncmd', w, eye).reshape(nb * bw, nb * bw)


def kernel(x, norm_mix_g, w_in, gate_bias, qk_norm_g, conv_w, conv_b, lru_gate_w, lru_gate_b, lru_lambda, sink_logit, w_proj_a, w_proj_b, w_proj_c, w_proj_d, w_out, norm_mlp_g, w_mlp1, w_mlp2, norm_final_g):
    b, s, d = x.shape
    depth = w_in.shape[0]
    ax_t, seq_t = _rope_tables(s)
    dw = D_HEADS * HEAD_DIM
    n_groups = len(D_PATTERNS)

    for l in range(depth):
        qk_gain = jnp.tile(qk_norm_g[l], (1, LANES // HEAD_DIM))
        w_in_l = w_in[l].astype(BF16)
        gate_w_dense = jnp.stack([jnp.stack([_dense_block_diag(lru_gate_w[l, dd, gg]) for gg in range(2)])
                                  for dd in range(2)]).astype(BF16)
        outs = _inproj(x, norm_mix_g[l][None, :], w_in_l[:, :N_IN], qk_gain, ax_t, seq_t,
                       conv_w[l], conv_b[l][None, :], gate_w_dense, lru_gate_b[l], lru_lambda[l])
        qta, ka, vta, af, uf, ab, ub, by, qc, kc, vc = outs[:11]
        qd, kd, vd = outs[11:11 + n_groups], outs[11 + n_groups:11 + 2 * n_groups], outs[11 + 2 * n_groups:11 + 3 * n_groups]

        ya = _attn_a(qta, ka, vta)

        hf, hb = _lru(af, uf, ab, ub)

        yc = _band_attention(qc[:, None], kc[:, None], vc[:, None], nkv=C_KV_HEADS,
                             group=C_HEADS // C_KV_HEADS, half_window=C_HALF_WINDOW, sink=sink_logit[l])[:, 0]

        o_d, lse_d = [], []
        for gi, (window, dil) in enumerate(D_PATTERNS):
            o, lse = _band_attention(qd[gi], kd[gi], vd[gi], nkv=D_HEADS, group=1,
                                     half_window=window // (2 * dil), want_lse=True)
            o_d.append(o)
            lse_d.append(lse)

        x = _merge(x, ya, hf, hb, by, yc, o_d, lse_d, norm_mix_g[l][None, :], w_in_l[:, N_IN:], gate_bias[l],
                   w_proj_a[l].astype(BF16), w_proj_b[l].astype(BF16), w_proj_c[l].astype(BF16),
                   w_proj_d[l].astype(BF16), w_out[l].astype(BF16))
        x = _mlp(x, norm_mlp_g[l][None, :], w_mlp1[l].astype(BF16), w_mlp2[l].astype(BF16),
                 norm_final_g[None, :], final_norm=(l == depth - 1))
    return x
```

```python
import functools
import math

import jax
import jax.numpy as jnp
from jax import lax
from jax.experimental import pallas as pl
from jax.experimental.pallas import tpu as pltpu

F32 = jnp.float32
BF16 = jnp.bfloat16

HEAD_DIM = 64
ROPE_THETA = 10000.0
GRID_W = 64
EPS = 1e-6
MASK_VALUE = -1e30
LOG2E = math.log2(math.e)

A_HEADS, A_KV_HEADS = 8, 2
LRU_WIDTH, LRU_BLOCKS, LRU_C = 512, 8, 8.0
CONV_W, CONV_LEFT = 4, 2
C_HEADS, C_KV_HEADS, C_HALF_WINDOW = 8, 2, 128
D_PATTERNS = ((128, 1), (512, 4), (2048, 16))
D_HEADS = 4
N_BRANCH = 4

LANES = 128
SUBLANES = 8
VMEM_LIMIT_BYTES = 56 * 1024 * 1024

TM_INPROJ = 512
TQ_ATTN_A = 256
SUB_ATTN_A = 128
TILES_PER_STEP_ATTN_A = 4
LOOKAHEAD_ATTN_A = 14
LOOKAHEAD_BAND = 6
LOOKAHEAD_INPROJ = 6
SUB_BAND = 128
CHUNKS_PER_BODY_ATTN_A = 16
VT_ROWS_A = HEAD_DIM + 16
TQ_BAND = 256
BAND_TILES_PER_STEP = 4
TT_LRU = 1024
TM_MERGE = 512
TM_MLP = 512
MLP_CHUNK = 1024

Q_SCALE = HEAD_DIM ** -0.5 * LOG2E


def _cparams(sem):
    return pltpu.CompilerParams(dimension_semantics=sem, vmem_limit_bytes=VMEM_LIMIT_BYTES)


def _resident(shape):
    nd = len(shape)
    return pl.BlockSpec(shape, lambda *_: (0,) * nd, pipeline_mode=pl.Buffered(1))


def _rope_tables(seq):
    pos = jnp.arange(seq, dtype=F32)
    lane = jnp.arange(LANES)
    d = lane % HEAD_DIM

    inv = ROPE_THETA ** (-jnp.arange(0, HEAD_DIM, 2, dtype=F32) / HEAD_DIM)
    ang = pos[:, None] * inv[None, :]
    f = d % (HEAD_DIM // 2)
    cos_s, sin_s = jnp.cos(ang)[:, f], jnp.sin(ang)[:, f]
    lo = (d < HEAD_DIM // 2)[None, :]
    seq_t = (cos_s, jnp.where(lo, -sin_s, 0.0), jnp.where(lo, 0.0, sin_s))

    quarter = HEAD_DIM // 4
    inv_ax = ROPE_THETA ** (-jnp.arange(0, HEAD_DIM // 2, 2, dtype=F32) / (HEAD_DIM // 2))
    row_pos = (jnp.arange(seq) // GRID_W).astype(F32)
    col_pos = (jnp.arange(seq) % GRID_W).astype(F32)
    e = d % (HEAD_DIM // 2)
    fa = e % quarter
    is_col = (d >= HEAD_DIM // 2)[None, :]
    ang_ax = jnp.where(is_col, col_pos[:, None], row_pos[:, None]) * inv_ax[fa][None, :]
    cos_a, sin_a = jnp.cos(ang_ax), jnp.sin(ang_ax)
    lo_a = (e < quarter)[None, :]
    ax_t = (cos_a, jnp.where(lo_a, -sin_a, 0.0), jnp.where(lo_a, 0.0, sin_a))
    return ax_t, seq_t


def _rotate(x, tables, half):
    cos, sin_a, sin_b = tables
    return x * cos + pltpu.roll(x, LANES - half, 1) * sin_a + pltpu.roll(x, half, 1) * sin_b


def _split_dot(x, m):
    hi = x.astype(BF16)
    lo = (x - hi.astype(F32)).astype(BF16)
    return jnp.dot(jnp.concatenate([hi, lo], axis=1), jnp.concatenate([m, m], axis=0),
                   preferred_element_type=F32)


def _head_mean_matrix():
    r = lax.broadcasted_iota(jnp.int32, (LANES, LANES), 0) // HEAD_DIM
    c = lax.broadcasted_iota(jnp.int32, (LANES, LANES), 1) // HEAD_DIM
    return jnp.where(r == c, 1.0 / HEAD_DIM, 0.0).astype(BF16)


def _head_rmsnorm(x, gain, mean_mat):
    var = _split_dot(x * x, mean_mat)
    return x * lax.rsqrt(var + EPS) * gain


IN_WIDTHS = (512, 128, 128, 512, 512, 512, 128, 128, 768, 768, 768)
IN_OFFSETS = tuple(sum(IN_WIDTHS[:i]) for i in range(len(IN_WIDTHS)))
N_IN = sum(IN_WIDTHS)


def _inproj_kernel(x_ref, xprev_ref, xnext_ref, g_ref, w_ref, qkg_ref,
                   ca_ref, saa_ref, sab_ref, cs_ref, ssa_ref, ssb_ref,
                   cw_ref, cb_ref, gw_ref, gb_ref, lam_ref,
                   qta_ref, ka_ref, vta_ref, af_ref, uf_ref, ab_ref, ub_ref, by_ref,
                   qc_ref, kc_ref, vc_ref, *rest):
    d_refs, perm_scr, xpad_scr = rest[:9], rest[9], rest[10]
    tile = pl.program_id(0)
    n_tiles = pl.num_programs(0)

    def normed(rows):
        var = jnp.mean(rows * rows, axis=-1, keepdims=True)
        return (rows * lax.rsqrt(var + EPS) * g_ref[...]).astype(BF16)

    x = x_ref[0]
    hn = normed(x)

    def proj(seg, lo=0, width=None):
        off = IN_OFFSETS[seg] + lo
        width = IN_WIDTHS[seg] if width is None else width
        return lambda: jnp.dot(hn, w_ref[:, off:off + width], preferred_element_type=F32)

    ax_t = (ca_ref[...], saa_ref[...], sab_ref[...])
    seq_t = (cs_ref[...], ssa_ref[...], ssb_ref[...])
    mean_mat = _head_mean_matrix()
    gq = qkg_ref[0:1, :]
    gk = qkg_ref[1:2, :]
    tm = x.shape[0]
    dw = D_HEADS * HEAD_DIM

    def finish_aq(acc):
        for j in range(IN_WIDTHS[0] // LANES):
            piece = _head_rmsnorm(acc[:, j * LANES:(j + 1) * LANES], gq, mean_mat)
            piece = _rotate(piece, ax_t, HEAD_DIM // 4) * Q_SCALE
            qta_ref[0, j * LANES:(j + 1) * LANES, :] = piece.T.astype(BF16)

    def finish_akv(acc):
        ak = _rotate(_head_rmsnorm(acc[:, :LANES], gk, mean_mat), ax_t, HEAD_DIM // 4).astype(BF16)
        avt = acc[:, LANES:].T.astype(BF16)
        pad_row = lax.broadcasted_iota(jnp.int32, (VT_ROWS_A - HEAD_DIM, tm), 0)
        ones_rows = (pad_row == 0).astype(BF16)
        for g in range(A_KV_HEADS):
            ka_ref[0, g] = ak[:, g * HEAD_DIM:(g + 1) * HEAD_DIM]
            vta_ref[0, g, 0] = jnp.concatenate([avt[g * HEAD_DIM:(g + 1) * HEAD_DIM, :], ones_rows], axis=0)

    lru = {}

    def finish_bx(acc):
        halo = jnp.dot(normed(jnp.concatenate([xprev_ref[0], xnext_ref[0]], axis=0)),
                       w_ref[:, IN_OFFSETS[3]:IN_OFFSETS[3] + LRU_WIDTH], preferred_element_type=F32)
        xpad_scr[0:SUBLANES] = jnp.where(tile > 0, halo[:SUBLANES], 0.0)
        xpad_scr[SUBLANES:SUBLANES + tm] = acc
        xpad_scr[SUBLANES + tm:] = jnp.where(tile < n_tiles - 1, halo[SUBLANES:], 0.0)
        xc = cb_ref[...]
        base = SUBLANES - CONV_LEFT
        for jj in range(CONV_W):
            xc = xc + xpad_scr[base + jj:base + jj + tm, :] * cw_ref[jj:jj + 1, :]
        lru["xc"] = xc
        lru["xcb"] = xc.astype(BF16)

    def gate_dots():
        return [jnp.dot(lru["xcb"], gw_ref[d, g], preferred_element_type=F32) for d in range(2) for g in range(2)]

    def finish_gates(dots):
        xc = lru["xc"]
        for d, (a_ref, u_ref) in enumerate(((af_ref, uf_ref), (ab_ref, ub_ref))):
            c2 = (-0.5 * LRU_C * LOG2E) * _softplus(-lam_ref[d:d + 1, :])
            a = jnp.exp2(c2 * jnp.tanh(0.5 * dots[2 * d] + 0.5 * gb_ref[d, 0:1, :]) + c2)
            gi = _sigmoid(dots[2 * d + 1] + gb_ref[d, 1:2, :])
            a_ref[0] = a
            z = (1.0 - a) * (1.0 + a)
            root = jnp.where(z > 0.0, z * lax.rsqrt(z), 0.0)
            u_ref[0] = root * (gi * xc)

    def finish_by(acc):
        by_ref[0] = acc.astype(BF16)

    def rope_piece(piece, scale):
        piece = _rotate(piece, seq_t, HEAD_DIM // 2)
        return piece if scale is None else piece * scale

    def finish_cq(acc):
        for j in range(IN_WIDTHS[5] // LANES):
            qc_ref[0, :, j * LANES:(j + 1) * LANES] = rope_piece(acc[:, j * LANES:(j + 1) * LANES], Q_SCALE).astype(BF16)

    def finish_ckv(acc):
        kc_ref[0] = rope_piece(acc[:, :LANES], None).astype(BF16)
        vc_ref[0] = acc[:, LANES:].astype(BF16)

    def finish_d(ref, dil, scale, rope):
        def finish(acc):
            for j in range(dw // LANES):
                piece = acc[:, j * LANES:(j + 1) * LANES]
                if rope:
                    piece = rope_piece(piece, scale)
                if dil == 1:
                    ref[0, 0, :, j * LANES:(j + 1) * LANES] = piece.astype(BF16)
                else:
                    perm_scr[j] = piece
            if dil > 1:
                for r in range(dil):
                    rows = [perm_scr[j, pl.ds(r, tm // dil, stride=dil), :] for j in range(dw // LANES)]
                    ref[0, r] = jnp.concatenate(rows, axis=1).astype(BF16)
        return finish

    segments = [(proj(0), finish_aq), (proj(1, 0, 2 * LANES), finish_akv), (proj(3), finish_bx),
                (proj(4), finish_by), (gate_dots, finish_gates), (proj(5), finish_cq),
                (proj(6, 0, 2 * LANES), finish_ckv)]
    for gi, (_, dil) in enumerate(D_PATTERNS):
        for ti, (seg, scale, rope) in enumerate(((8, Q_SCALE, True), (9, None, True), (10, None, False))):
            segments.append((proj(seg, gi * dw, dw), finish_d(d_refs[ti * len(D_PATTERNS) + gi], dil, scale, rope)))

    issued, pending = 0, []
    for idx, (_, finish) in enumerate(segments):
        horizon = min(len(segments), idx + 1 + LOOKAHEAD_INPROJ)
        while issued < horizon and (segments[issued][0] is not gate_dots or "xcb" in lru):
            pending.append(segments[issued][0]())
            issued += 1
        finish(pending.pop(0))


def _inproj(x, g, w_bf16, qk_gain128, ax_t, seq_t, conv_w, conv_b, gate_w, gate_b, lam):
    b, s, d = x.shape
    tm = min(TM_INPROJ, s)
    nt = s // tm
    per, n8 = tm // SUBLANES, s // SUBLANES
    halo_prev = pl.BlockSpec((1, SUBLANES, d), lambda i, bb: (bb, jnp.maximum(i * per - 1, 0), 0))
    halo_next = pl.BlockSpec((1, SUBLANES, d), lambda i, bb: (bb, jnp.minimum((i + 1) * per, n8 - 1), 0))
    tok = lambda width: pl.BlockSpec((1, tm, width), lambda i, bb: (bb, i, 0))
    tab = pl.BlockSpec((tm, LANES), lambda i, bb: (i, 0))
    dw = D_HEADS * HEAD_DIM
    d_shapes = tuple(jax.ShapeDtypeStruct((b, dil, s // dil, dw), BF16) for _, dil in D_PATTERNS)
    d_specs = tuple(pl.BlockSpec((1, dil, tm // dil, dw), lambda i, bb: (bb, 0, i, 0)) for _, dil in D_PATTERNS)
    out_shape = (
        jax.ShapeDtypeStruct((b, A_HEADS * HEAD_DIM, s), BF16),
        jax.ShapeDtypeStruct((b, A_KV_HEADS, s, HEAD_DIM), BF16),
        jax.ShapeDtypeStruct((b, A_KV_HEADS, nt, VT_ROWS_A, tm), BF16),
        jax.ShapeDtypeStruct((b, s, LRU_WIDTH), F32),
        jax.ShapeDtypeStruct((b, s, LRU_WIDTH), F32),
        jax.ShapeDtypeStruct((b, s, LRU_WIDTH), F32),
        jax.ShapeDtypeStruct((b, s, LRU_WIDTH), F32),
        jax.ShapeDtypeStruct((b, s, LRU_WIDTH), BF16),
        jax.ShapeDtypeStruct((b, s, 512), BF16),
        jax.ShapeDtypeStruct((b, s, 128), BF16),
        jax.ShapeDtypeStruct((b, s, 128), BF16),
    ) + d_shapes * 3
    out_specs = (
        pl.BlockSpec((1, A_HEADS * HEAD_DIM, tm), lambda i, bb: (bb, 0, i)),
        pl.BlockSpec((1, A_KV_HEADS, tm, HEAD_DIM), lambda i, bb: (bb, 0, i, 0)),
        pl.BlockSpec((1, A_KV_HEADS, 1, VT_ROWS_A, tm), lambda i, bb: (bb, 0, i, 0, 0)),
        tok(LRU_WIDTH), tok(LRU_WIDTH), tok(LRU_WIDTH), tok(LRU_WIDTH), tok(LRU_WIDTH), tok(512), tok(128), tok(128),
    ) + d_specs * 3
    return pl.pallas_call(
        _inproj_kernel,
        grid=(nt, b),
        in_specs=[tok(d), halo_prev, halo_next, _resident((1, d)), _resident((d, N_IN)), _resident((2, LANES)),
                  tab, tab, tab, tab, tab, tab,
                  _resident(conv_w.shape), _resident(conv_b.shape), _resident(gate_w.shape),
                  _resident(gate_b.shape), _resident(lam.shape)],
        out_specs=out_specs,
        out_shape=out_shape,
        scratch_shapes=[pltpu.VMEM((dw // LANES, tm, LANES), F32),
                        pltpu.VMEM((tm + 2 * SUBLANES, LRU_WIDTH), F32)],
        compiler_params=_cparams(("parallel", "parallel")),
        name="inproj",
    )(x, x, x, g, w_bf16, qk_gain128, *ax_t, *seq_t, conv_w, conv_b, gate_w, gate_b, lam)


def _sublane_allmax(x):
    for shift in (4, 2, 1):
        x = jnp.maximum(x, pltpu.roll(x, shift, 0))
    return x


def _attn_a_kernel(q_ref, k_ref, v_ref, o_ref, m_ref, acc_ref, *, group, sub):
    n_chunks, chunk = v_ref.shape[2], v_ref.shape[4]
    tq = min(TQ_ATTN_A, q_ref.shape[2])
    n_tiles = q_ref.shape[2] // tq
    m_ref[...] = jnp.full(m_ref.shape, MASK_VALUE, F32)
    acc_ref[...] = jnp.zeros(acc_ref.shape, F32)

    per_body = min(CHUNKS_PER_BODY_ATTN_A, n_chunks)
    n_sub = chunk // sub
    pair = 2 if n_sub % 2 == 0 else 1
    units = [(cc, u, h) for cc in range(per_body) for up in range(n_sub // pair)
             for h in range(n_tiles * group) for u in range(up * pair, (up + 1) * pair)]

    def body(it, carry):
        c0 = it * per_body

        def scores(idx):
            cc, u, h = units[idx]
            row0 = pl.multiple_of((c0 + cc) * chunk + u * sub, sub)
            k = k_ref[0, 0, pl.ds(row0, sub), :]
            j, head = divmod(h, group)
            qt = q_ref[0, head * HEAD_DIM:(head + 1) * HEAD_DIM, j * tq:(j + 1) * tq]
            return jnp.dot(k, qt, preferred_element_type=F32)

        pending = [scores(i) for i in range(LOOKAHEAD_ATTN_A)]
        probs = []
        for idx, (cc, u, h) in enumerate(units):
            if idx + LOOKAHEAD_ATTN_A < len(units):
                pending.append(scores(idx + LOOKAHEAD_ATTN_A))
            s3 = pending.pop(0).reshape(sub // SUBLANES, SUBLANES, tq)
            if not probs:
                m_start = m_ref[h]
                m_old = m_start
            m_new = jnp.maximum(m_old, _sublane_allmax(jnp.max(s3, axis=0)))
            if probs:
                beta = jnp.exp2(m_old - m_new).astype(BF16)
                beta = jnp.concatenate([beta, beta], axis=0)[None]
                probs = [(p.reshape(sub // 16, 16, tq) * beta).reshape(sub, tq) for p in probs]
            probs.append(jnp.exp2(s3 - m_new[None]).reshape(sub, tq).astype(BF16))
            m_old = m_new
            if len(probs) == pair:
                u0 = u - (pair - 1)
                vt = v_ref[0, 0, c0 + cc, :, u0 * sub:(u + 1) * sub]
                pv = jnp.dot(vt, jnp.concatenate(probs, axis=0), preferred_element_type=F32)
                alpha = jnp.exp2(m_start - m_new)
                acc_ref[h] = acc_ref[h] * alpha[None] + pv.reshape(VT_ROWS_A // SUBLANES, SUBLANES, tq)
                m_ref[h] = m_new
                probs = []
        return carry

    lax.fori_loop(0, n_chunks // per_body, body, 0)
    for j in range(n_tiles):
        outs = []
        for h in range(j * group, (j + 1) * group):
            acc = acc_ref[h].reshape(VT_ROWS_A, tq)
            outs.append(acc[:HEAD_DIM] / acc[HEAD_DIM:HEAD_DIM + 1])
        o_ref[0, j * tq:(j + 1) * tq, :] = jnp.concatenate(outs, axis=0).T.astype(o_ref.dtype)


def _attn_a(qt, k, vt):
    b, hd_all, s = qt.shape
    nkv, nt, chunk = vt.shape[1], vt.shape[2], vt.shape[4]
    group = hd_all // HEAD_DIM // nkv
    tq = min(TQ_ATTN_A * TILES_PER_STEP_ATTN_A, s)
    n_state = group * (tq // min(TQ_ATTN_A, s))
    gw = group * HEAD_DIM
    return pl.pallas_call(
        functools.partial(_attn_a_kernel, group=group, sub=min(SUB_ATTN_A, chunk)),
        grid=(b, nkv, s // tq),
        in_specs=[pl.BlockSpec((1, gw, tq), lambda bb, g, i: (bb, g, i)),
                  pl.BlockSpec((1, 1, s, HEAD_DIM), lambda bb, g, i: (bb, g, 0, 0)),
                  pl.BlockSpec((1, 1, nt, VT_ROWS_A, chunk), lambda bb, g, i: (bb, g, 0, 0, 0))],
        out_specs=pl.BlockSpec((1, tq, gw), lambda bb, g, i: (bb, i, g)),
        out_shape=jax.ShapeDtypeStruct((b, s, hd_all), BF16),
        scratch_shapes=[pltpu.VMEM((n_state, SUBLANES, min(TQ_ATTN_A, s)), F32),
                        pltpu.VMEM((n_state, VT_ROWS_A // SUBLANES, SUBLANES, min(TQ_ATTN_A, s)), F32)],
        compiler_params=_cparams(("parallel", "parallel", "parallel")),
        name="attn_a",
    )(qt, k, vt)


def _band_kernel(*refs, nkv, group, half_window, seq_len, has_sink, want_lse):
    q_ref, kp_ref, km_ref, kn_ref, vp_ref, vm_ref, vn_ref = refs[:7]
    rest = list(refs[7:])
    sink_ref = rest.pop(0) if has_sink else None
    o_ref = rest.pop(0)
    lse_ref = rest.pop(0) if want_lse else None

    tq_blk = q_ref.shape[2]
    tq = min(TQ_BAND, tq_blk)
    halo = kp_ref.shape[2]
    nk = tq + 2 * halo
    sub = SUB_BAND if nk % SUB_BAND == 0 else halo
    n_heads = nkv * group
    blk0 = pl.program_id(2) * tq_blk

    kcat = jnp.concatenate([kp_ref[0, 0], km_ref[0, 0], kn_ref[0, 0]], axis=0)
    vt = jnp.concatenate([vp_ref[0, 0], vm_ref[0, 0], vn_ref[0, 0]], axis=0).T

    kw = min(LANES, nkv * HEAD_DIM)
    zeros = jnp.zeros((HEAD_DIM, tq), kcat.dtype)
    q_pads = {}

    def padded_queries(j):
        if j not in q_pads:
            qt = q_ref[0, 0, j * tq:(j + 1) * tq, :].T
            pads = []
            for h in range(n_heads):
                slot = ((h // group) * HEAD_DIM % kw) // HEAD_DIM
                parts = [zeros] * (kw // HEAD_DIM)
                parts[slot] = qt[h * HEAD_DIM:(h + 1) * HEAD_DIM, :]
                pads.append(parts[0] if len(parts) == 1 else jnp.concatenate(parts, axis=0))
            q_pads.clear()
            q_pads[j] = pads
        return q_pads[j]

    units = [(j, kb, h) for j in range(tq_blk // tq) for kb in range(nk // sub) for h in range(n_heads)]

    cw = LANES if tq % LANES == 0 else tq
    n_ch = tq // cw

    def query_chunks(kb):
        k_lo = kb * sub - halo
        lo = max(0, (k_lo - half_window) // cw)
        hi = min(n_ch, -(-(k_lo + sub + half_window) // cw))
        return list(range(lo, hi))

    def scores(idx):
        j, kb, h = units[idx]
        row0 = j * tq + kb * sub
        lane0 = (h // group) * HEAD_DIM // kw * kw
        cs = query_chunks(kb)
        return jnp.dot(kcat[row0:row0 + sub, lane0:lane0 + kw],
                       padded_queries(j)[h][:, cs[0] * cw:(cs[-1] + 1) * cw], preferred_element_type=F32)

    key_row = lax.broadcasted_iota(jnp.int32, (sub, cw), 0)
    query = lax.broadcasted_iota(jnp.int32, (sub, cw), 1)

    def valid_mask(j, kb, c):
        kpos = blk0 + j * tq - halo + kb * sub + key_row
        rel = kpos - (blk0 + j * tq + c * cw + query)
        return (jnp.abs(rel) <= half_window) & (kpos >= 0) & (kpos < seq_len)

    def start_state():
        if has_sink:
            m0 = [jnp.full((SUBLANES, cw), sink_ref[h] * LOG2E, F32) for h in range(n_heads)]
            l0 = (lax.broadcasted_iota(jnp.int32, (SUBLANES, cw), 0) == 0).astype(F32)
        else:
            m0 = [jnp.full((SUBLANES, cw), 0.1 * MASK_VALUE, F32)] * n_heads
            l0 = jnp.zeros((SUBLANES, cw), F32)
        a0 = jnp.zeros((HEAD_DIM // SUBLANES, SUBLANES, cw), F32)
        return ([[m0[h]] * n_ch for h in range(n_heads)], [[l0] * n_ch for _ in range(n_heads)],
                [[a0] * n_ch for _ in range(n_heads)])

    def join(parts):
        return parts[0] if len(parts) == 1 else jnp.concatenate(parts, axis=1)

    def finish(j, m, l, acc):
        outs, lses = [], []
        for h in range(n_heads):
            l_row = join([jnp.sum(x, axis=0, keepdims=True) for x in l[h]])
            outs.append(join([x.reshape(HEAD_DIM, cw) for x in acc[h]]) / l_row)
            if want_lse:
                lse = join([x[0:1] for x in m[h]]) + jnp.log2(l_row)
                lses.append(jnp.broadcast_to(lse, (HEAD_DIM, tq)))
        o_ref[0, 0, j * tq:(j + 1) * tq, :] = jnp.concatenate(outs, axis=0).T.astype(o_ref.dtype)
        if want_lse:
            lse_ref[0, 0, j * tq:(j + 1) * tq, :] = jnp.concatenate(lses, axis=0).T

    look = min(LOOKAHEAD_BAND, len(units))
    pending = [scores(i) for i in range(look)]
    mask_key, masks = None, None
    for idx, (j, kb, h) in enumerate(units):
        if idx + look < len(units):
            pending.append(scores(idx + look))
        if kb == 0 and h == 0:
            m, l, acc = start_state()
        cs = query_chunks(kb)
        if mask_key != (j, kb):
            mask_key, masks = (j, kb), [valid_mask(j, kb, c) for c in cs]
        g = h // group
        s = pending.pop(0)
        probs, alphas = [], []
        for i, c in enumerate(cs):
            s_c = s if len(cs) == 1 else s[:, i * cw:(i + 1) * cw]
            s3 = jnp.where(masks[i], s_c, MASK_VALUE).reshape(sub // SUBLANES, SUBLANES, cw)
            m_new = jnp.maximum(m[h][c], _sublane_allmax(jnp.max(s3, axis=0)))
            alpha = jnp.exp2(m[h][c] - m_new)
            p3 = jnp.exp2(s3 - m_new[None])
            l[h][c] = alpha * l[h][c] + jnp.sum(p3, axis=0)
            m[h][c] = m_new
            probs.append(p3.reshape(sub, cw).astype(BF16))
            alphas.append(alpha)
        col0 = j * tq + kb * sub
        pv = jnp.dot(vt[g * HEAD_DIM:(g + 1) * HEAD_DIM, col0:col0 + sub], join(probs),
                     preferred_element_type=F32)
        for i, c in enumerate(cs):
            pv_c = pv if len(cs) == 1 else pv[:, i * cw:(i + 1) * cw]
            acc[h][c] = acc[h][c] * alphas[i][None] + pv_c.reshape(HEAD_DIM // SUBLANES, SUBLANES, cw)
        if kb == nk // sub - 1 and h == n_heads - 1:
            finish(j, m, l, acc)


def _band_attention(q, k, v, *, nkv, group, half_window, sink=None, want_lse=False):
    b, n_seq, seq_len, wq = q.shape
    wk = nkv * HEAD_DIM
    halo = half_window
    tq = min(TQ_BAND * BAND_TILES_PER_STEP, seq_len)
    per = tq // halo
    n_halo = seq_len // halo

    main = lambda width: pl.BlockSpec((1, 1, tq, width), lambda bb, r, i: (bb, r, i, 0))
    prev = lambda width: pl.BlockSpec(
        (1, 1, halo, width), lambda bb, r, i: (bb, r, jnp.maximum(i * per - 1, 0), 0))
    nxt = lambda width: pl.BlockSpec(
        (1, 1, halo, width), lambda bb, r, i: (bb, r, jnp.minimum((i + 1) * per, n_halo - 1), 0))
    in_specs = [main(wq), prev(wk), main(wk), nxt(wk), prev(wk), main(wk), nxt(wk)]
    args = [q, k, k, k, v, v, v]
    if sink is not None:
        in_specs.append(pl.BlockSpec(memory_space=pltpu.SMEM))
        args.append(sink)
    out_block = main(wq)
    out_shape = [jax.ShapeDtypeStruct(q.shape, BF16)]
    out_specs = [out_block]
    if want_lse:
        out_shape.append(jax.ShapeDtypeStruct(q.shape, F32))
        out_specs.append(out_block)
    res = pl.pallas_call(
        functools.partial(_band_kernel, nkv=nkv, group=group, half_window=half_window,
                          seq_len=seq_len, has_sink=sink is not None, want_lse=want_lse),
        grid=(b, n_seq, seq_len // tq),
        in_specs=in_specs,
        out_specs=out_specs,
        out_shape=out_shape,
        compiler_params=_cparams(("parallel", "parallel", "parallel")),
        name="band_attn",
    )(*args)
    return res if want_lse else res[0]


def _sigmoid(x):
    return 0.5 * jnp.tanh(0.5 * x) + 0.5


def _softplus(x):
    return jnp.maximum(x, 0.0) + jnp.log1p(jnp.exp(-jnp.abs(x)))


def _lru_kernel(af_ref, uf_ref, ab_ref, ub_ref, hf_ref, hb_ref, carry_ref):
    i = pl.program_id(1)
    tt = af_ref.shape[1]
    nblk = tt // SUBLANES

    @pl.when(i == 0)
    def _():
        carry_ref[...] = jnp.zeros(carry_ref.shape, F32)

    row = lax.broadcasted_iota(jnp.int32, (SUBLANES, LRU_WIDTH), 0)

    def scan_block(d, blk, carry, a_ref, u_ref, out_ref):
        a8 = a_ref[0, pl.ds(blk * SUBLANES, SUBLANES), :]
        u8 = u_ref[0, pl.ds(blk * SUBLANES, SUBLANES), :]
        entry = row == (0 if d == 0 else SUBLANES - 1)
        u8 = u8 + jnp.where(entry, a8 * carry, 0.0)
        for step in (1, 2, 4):
            if d == 0:
                shift, ident = step, row < step
            else:
                shift, ident = SUBLANES - step, row >= SUBLANES - step
            u8 = u8 + a8 * jnp.where(ident, 0.0, pltpu.roll(u8, shift, 0))
            if step < 4:
                a8 = a8 * pltpu.roll(a8, shift, 0)
        out_ref[0, pl.ds(blk * SUBLANES, SUBLANES), :] = u8
        last = u8[SUBLANES - 1:SUBLANES, :] if d == 0 else u8[0:1, :]
        return jnp.broadcast_to(last, (SUBLANES, LRU_WIDTH))

    def body(kb, carries):
        cf, cb = carries
        cf = scan_block(0, kb, cf, af_ref, uf_ref, hf_ref)
        cb = scan_block(1, nblk - 1 - kb, cb, ab_ref, ub_ref, hb_ref)
        return cf, cb

    cf, cb = lax.fori_loop(0, nblk, body, (carry_ref[0], carry_ref[1]))
    carry_ref[0] = cf
    carry_ref[1] = cb


def _lru(af, uf, ab, ub):
    b, s, w = af.shape
    tt = min(TT_LRU, s)
    nt = s // tt
    fwd = pl.BlockSpec((1, tt, w), lambda bb, i: (bb, i, 0))
    bwd = pl.BlockSpec((1, tt, w), lambda bb, i: (bb, nt - 1 - i, 0))
    return pl.pallas_call(
        _lru_kernel,
        grid=(b, nt),
        in_specs=[fwd, fwd, bwd, bwd],
        out_specs=[fwd, bwd],
        out_shape=[jax.ShapeDtypeStruct((b, s, w), F32), jax.ShapeDtypeStruct((b, s, w), F32)],
        scratch_shapes=[pltpu.VMEM((2, SUBLANES, w), F32)],
        compiler_params=_cparams(("parallel", "arbitrary")),
        name="lru",
    )(af, uf, ab, ub)


def _gelu_tanh(x):
    return 0.5 * x * (1.0 + jnp.tanh(math.sqrt(2.0 / math.pi) * (x + 0.044715 * (x * x * x))))


def _merge_kernel(x_ref, ya_ref, hf_ref, hb_ref, by_ref, yc_ref,
                  o0_ref, o1_ref, o2_ref, l0_ref, l1_ref, l2_ref, g_ref, wg_ref, gbias_ref,
                  wa_ref, wb_ref, wc_ref, wd_ref, wo_ref, out_ref, perm_scr):
    d_model = x_ref.shape[2]
    tm = x_ref.shape[1]
    x = x_ref[0]
    var = jnp.mean(x * x, axis=-1, keepdims=True)
    hn = (x * lax.rsqrt(var + EPS) * g_ref[...]).astype(BF16)

    def natural(ref, slot):
        dil = ref.shape[1]
        if dil == 1:
            return ref[0, 0].astype(F32)
        for r in range(dil):
            rows = ref[0, r].astype(F32)
            for j in range(rows.shape[1] // LANES):
                perm_scr[slot, j, pl.ds(r, tm // dil, stride=dil), :] = rows[:, j * LANES:(j + 1) * LANES]
        return jnp.concatenate([perm_scr[slot, j] for j in range(perm_scr.shape[1])], axis=1)

    yb = ((hf_ref[0] + hb_ref[0]) * _gelu_tanh(by_ref[0].astype(F32))).astype(BF16)

    l0, l1, l2 = natural(l0_ref, 0), natural(l1_ref, 0), natural(l2_ref, 1)
    o0, o1, o2 = natural(o0_ref, 2), natural(o1_ref, 2), natural(o2_ref, 3)
    mx = jnp.maximum(jnp.maximum(l0, l1), l2)
    e0, e1, e2 = jnp.exp2(l0 - mx), jnp.exp2(l1 - mx), jnp.exp2(l2 - mx)
    yd = ((e0 * o0 + e1 * o1 + e2 * o2) / (e0 + e1 + e2)).astype(BF16)

    def gate_logits(kk):
        return lambda: jnp.dot(hn, wg_ref[:, kk * d_model:(kk + 1) * d_model], preferred_element_type=F32)

    def branch(y, w_ref):
        return lambda: jnp.dot(y, w_ref[...], preferred_element_type=F32)

    dots = []
    for kk, (y, w_ref) in enumerate(((ya_ref[0], wa_ref), (yb, wb_ref), (yc_ref[0], wc_ref), (yd, wd_ref))):
        dots += [gate_logits(kk), branch(y, w_ref)]
    pending = [dots[0](), dots[1]()]
    merged = None
    for kk in range(N_BRANCH):
        pending += [d() for d in dots[2 * kk + 2:2 * kk + 4]]
        logits, proj = pending.pop(0), pending.pop(0)
        term = _sigmoid(logits + gbias_ref[kk:kk + 1, :]) * proj
        merged = term if merged is None else merged + term

    out_ref[0] = x + jnp.dot(merged.astype(BF16), wo_ref[...], preferred_element_type=F32)


def _merge(x, ya, hf, hb, by, yc, o_d, lse_d, g, wg, gbias, wa, wb, wc, wd, wo):
    b, s, d = x.shape
    tm = min(TM_MERGE, s)
    tok = lambda width: pl.BlockSpec((1, tm, width), lambda bb, i: (bb, i, 0))
    dw = D_HEADS * HEAD_DIM
    res = [pl.BlockSpec((1, dil, tm // dil, dw), lambda bb, i: (bb, 0, i, 0)) for _, dil in D_PATTERNS]
    return pl.pallas_call(
        _merge_kernel,
        grid=(b, s // tm),
        in_specs=[tok(d), tok(512), tok(LRU_WIDTH), tok(LRU_WIDTH), tok(LRU_WIDTH), tok(512),
                  *res, *res, _resident(g.shape), _resident(wg.shape),
                  _resident(gbias.shape), _resident(wa.shape), _resident(wb.shape),
                  _resident(wc.shape), _resident(wd.shape), _resident(wo.shape)],
        out_specs=tok(d),
        out_shape=jax.ShapeDtypeStruct((b, s, d), F32),
        scratch_shapes=[pltpu.VMEM((4, dw // LANES, tm, LANES), F32)],
        compiler_params=_cparams(("parallel", "parallel")),
        name="merge",
    )(x, ya, hf, hb, by, yc, *o_d, *lse_d, g, wg, gbias, wa, wb, wc, wd, wo)


def _mlp_kernel(x_ref, g_ref, w1_ref, w2_ref, gf_ref, out_ref, *, final_norm):
    x = x_ref[0]
    var = jnp.mean(x * x, axis=-1, keepdims=True)
    hn = (x * lax.rsqrt(var + EPS) * g_ref[...]).astype(BF16)
    hidden = w1_ref.shape[1]
    acc = x
    for c in range(hidden // MLP_CHUNK):
        h = jnp.dot(hn, w1_ref[:, c * MLP_CHUNK:(c + 1) * MLP_CHUNK], preferred_element_type=F32)
        h = jnp.square(jnp.maximum(h, 0.0)).astype(BF16)
        acc = acc + jnp.dot(h, w2_ref[c * MLP_CHUNK:(c + 1) * MLP_CHUNK, :], preferred_element_type=F32)
    if final_norm:
        var = jnp.mean(acc * acc, axis=-1, keepdims=True)
        acc = acc * lax.rsqrt(var + EPS) * gf_ref[...]
    out_ref[0] = acc


def _mlp(x, g, w1, w2, g_final, final_norm):
    b, s, d = x.shape
    tm = min(TM_MLP, s)
    tok = pl.BlockSpec((1, tm, d), lambda bb, i: (bb, i, 0))
    return pl.pallas_call(
        functools.partial(_mlp_kernel, final_norm=final_norm),
        grid=(b, s // tm),
        in_specs=[tok, _resident(g.shape), _resident(w1.shape), _resident(w2.shape), _resident(g_final.shape)],
        out_specs=tok,
        out_shape=jax.ShapeDtypeStruct((b, s, d), F32),
        compiler_params=_cparams(("parallel", "parallel")),
        name="mlp",
    )(x, g, w1, w2, g_final)


def _dense_block_diag(w):
    nb, bw, _ = w.shape
    eye = jnp.eye(nb, dtype=w.dtype)
    return jnp.einsum('ncd,nm->ncmd', w, eye).reshape(nb * bw, nb * bw)


def kernel(x, norm_mix_g, w_in, gate_bias, qk_norm_g, conv_w, conv_b, lru_gate_w, lru_gate_b, lru_lambda, sink_logit, w_proj_a, w_proj_b, w_proj_c, w_proj_d, w_out, norm_mlp_g, w_mlp1, w_mlp2, norm_final_g):
    b, s, d = x.shape
    depth = w_in.shape[0]
    ax_t, seq_t = _rope_tables(s)
    dw = D_HEADS * HEAD_DIM
    n_groups = len(D_PATTERNS)

    for l in range(depth):
        qk_gain = jnp.tile(qk_norm_g[l], (1, LANES // HEAD_DIM))
        w_in_l = w_in[l].astype(BF16)
        gate_w_dense = jnp.stack([jnp.stack([_dense_block_diag(lru_gate_w[l, dd, gg]) for gg in range(2)])
                                  for dd in range(2)]).astype(BF16)
        outs = _inproj(x, norm_mix_g[l][None, :], w_in_l[:, :N_IN], qk_gain, ax_t, seq_t,
                       conv_w[l], conv_b[l][None, :], gate_w_dense, lru_gate_b[l], lru_lambda[l])
        qta, ka, vta, af, uf, ab, ub, by, qc, kc, vc = outs[:11]
        qd, kd, vd = outs[11:11 + n_groups], outs[11 + n_groups:11 + 2 * n_groups], outs[11 + 2 * n_groups:11 + 3 * n_groups]

        ya = _attn_a(qta, ka, vta)

        hf, hb = _lru(af, uf, ab, ub)

        yc = _band_attention(qc[:, None], kc[:, None], vc[:, None], nkv=C_KV_HEADS,
                             group=C_HEADS // C_KV_HEADS, half_window=C_HALF_WINDOW, sink=sink_logit[l])[:, 0]

        o_d, lse_d = [], []
        for gi, (window, dil) in enumerate(D_PATTERNS):
            o, lse = _band_attention(qd[gi], kd[gi], vd[gi], nkv=D_HEADS, group=1,
                                     half_window=window // (2 * dil), want_lse=True)
            o_d.append(o)
            lse_d.append(lse)

        x = _merge(x, ya, hf, hb, by, yc, o_d, lse_d, norm_mix_g[l][None, :], w_in_l[:, N_IN:], gate_bias[l],
                   w_proj_a[l].astype(BF16), w_proj_b[l].astype(BF16), w_proj_c[l].astype(BF16),
                   w_proj_d[l].astype(BF16), w_out[l].astype(BF16))
        x = _mlp(x, norm_mlp_g[l][None, :], w_mlp1[l].astype(BF16), w_mlp2[l].astype(BF16),
                 norm_final_g[None, :], final_norm=(l == depth - 1))
    return x
```

```python
import functools
import math

import jax
import jax.numpy as jnp
from jax import lax
from jax.experimental import pallas as pl
from jax.experimental.pallas import tpu as pltpu

F32 = jnp.float32
BF16 = jnp.bfloat16

HEAD_DIM = 64
ROPE_THETA = 10000.0
GRID_W = 64
EPS = 1e-6
MASK_VALUE = -1e30
LOG2E = math.log2(math.e)

A_HEADS, A_KV_HEADS = 8, 2
LRU_WIDTH, LRU_BLOCKS, LRU_C = 512, 8, 8.0
CONV_W, CONV_LEFT = 4, 2
C_HEADS, C_KV_HEADS, C_HALF_WINDOW = 8, 2, 128
D_PATTERNS = ((128, 1), (512, 4), (2048, 16))
D_HEADS = 4
N_BRANCH = 4

LANES = 128
SUBLANES = 8
VMEM_LIMIT_BYTES = 56 * 1024 * 1024

TM_INPROJ = 512
TQ_ATTN_A = 256
SUB_ATTN_A = 128
TILES_PER_STEP_ATTN_A = 2
LOOKAHEAD_ATTN_A = 14
LOOKAHEAD_BAND = 6
LOOKAHEAD_INPROJ = 6
SUB_BAND = 128
CHUNKS_PER_BODY_ATTN_A = 16
VT_ROWS_A = HEAD_DIM + 16
TQ_BAND = 256
BAND_TILES_PER_STEP = 4
TT_LRU = 1024
TM_MERGE = 512
TM_MLP = 512
MLP_CHUNK = 1024

Q_SCALE = HEAD_DIM ** -0.5 * LOG2E


def _cparams(sem):
    return pltpu.CompilerParams(dimension_semantics=sem, vmem_limit_bytes=VMEM_LIMIT_BYTES)


def _resident(shape):
    nd = len(shape)
    return pl.BlockSpec(shape, lambda *_: (0,) * nd, pipeline_mode=pl.Buffered(1))


def _rope_tables(seq):
    pos = jnp.arange(seq, dtype=F32)
    lane = jnp.arange(LANES)
    d = lane % HEAD_DIM

    inv = ROPE_THETA ** (-jnp.arange(0, HEAD_DIM, 2, dtype=F32) / HEAD_DIM)
    ang = pos[:, None] * inv[None, :]
    f = d % (HEAD_DIM // 2)
    cos_s, sin_s = jnp.cos(ang)[:, f], jnp.sin(ang)[:, f]
    lo = (d < HEAD_DIM // 2)[None, :]
    seq_t = (cos_s, jnp.where(lo, -sin_s, 0.0), jnp.where(lo, 0.0, sin_s))

    quarter = HEAD_DIM // 4
    inv_ax = ROPE_THETA ** (-jnp.arange(0, HEAD_DIM // 2, 2, dtype=F32) / (HEAD_DIM // 2))
    row_pos = (jnp.arange(seq) // GRID_W).astype(F32)
    col_pos = (jnp.arange(seq) % GRID_W).astype(F32)
    e = d % (HEAD_DIM // 2)
    fa = e % quarter
    is_col = (d >= HEAD_DIM // 2)[None, :]
    ang_ax = jnp.where(is_col, col_pos[:, None], row_pos[:, None]) * inv_ax[fa][None, :]
    cos_a, sin_a = jnp.cos(ang_ax), jnp.sin(ang_ax)
    lo_a = (e < quarter)[None, :]
    ax_t = (cos_a, jnp.where(lo_a, -sin_a, 0.0), jnp.where(lo_a, 0.0, sin_a))
    return ax_t, seq_t


def _rotate(x, tables, half):
    cos, sin_a, sin_b = tables
    return x * cos + pltpu.roll(x, LANES - half, 1) * sin_a + pltpu.roll(x, half, 1) * sin_b


def _split_dot(x, m):
    hi = x.astype(BF16)
    lo = (x - hi.astype(F32)).astype(BF16)
    return jnp.dot(jnp.concatenate([hi, lo], axis=1), jnp.concatenate([m, m], axis=0),
                   preferred_element_type=F32)


def _head_mean_matrix():
    r = lax.broadcasted_iota(jnp.int32, (LANES, LANES), 0) // HEAD_DIM
    c = lax.broadcasted_iota(jnp.int32, (LANES, LANES), 1) // HEAD_DIM
    return jnp.where(r == c, 1.0 / HEAD_DIM, 0.0).astype(BF16)


def _head_rmsnorm(x, gain, mean_mat):
    var = _split_dot(x * x, mean_mat)
    return x * lax.rsqrt(var + EPS) * gain


IN_WIDTHS = (512, 128, 128, 512, 512, 512, 128, 128, 768, 768, 768)
IN_OFFSETS = tuple(sum(IN_WIDTHS[:i]) for i in range(len(IN_WIDTHS)))
N_IN = sum(IN_WIDTHS)


def _inproj_kernel(x_ref, xprev_ref, xnext_ref, g_ref, w_ref, qkg_ref,
                   ca_ref, saa_ref, sab_ref, cs_ref, ssa_ref, ssb_ref,
                   cw_ref, cb_ref, gw_ref, gb_ref, lam_ref,
                   qta_ref, ka_ref, vta_ref, af_ref, uf_ref, ab_ref, ub_ref, by_ref,
                   qc_ref, kc_ref, vc_ref, *rest):
    d_refs, perm_scr, xpad_scr = rest[:9], rest[9], rest[10]
    tile = pl.program_id(0)
    n_tiles = pl.num_programs(0)

    def normed(rows):
        var = jnp.mean(rows * rows, axis=-1, keepdims=True)
        return (rows * lax.rsqrt(var + EPS) * g_ref[...]).astype(BF16)

    x = x_ref[0]
    hn = normed(x)

    def proj(seg, lo=0, width=None):
        off = IN_OFFSETS[seg] + lo
        width = IN_WIDTHS[seg] if width is None else width
        return lambda: jnp.dot(hn, w_ref[:, off:off + width], preferred_element_type=F32)

    ax_t = (ca_ref[...], saa_ref[...], sab_ref[...])
    seq_t = (cs_ref[...], ssa_ref[...], ssb_ref[...])
    mean_mat = _head_mean_matrix()
    gq = qkg_ref[0:1, :]
    gk = qkg_ref[1:2, :]
    tm = x.shape[0]
    dw = D_HEADS * HEAD_DIM

    def finish_aq(acc):
        for j in range(IN_WIDTHS[0] // LANES):
            piece = _head_rmsnorm(acc[:, j * LANES:(j + 1) * LANES], gq, mean_mat)
            piece = _rotate(piece, ax_t, HEAD_DIM // 4) * Q_SCALE
            qta_ref[0, j * LANES:(j + 1) * LANES, :] = piece.T.astype(BF16)

    def finish_akv(acc):
        ak = _rotate(_head_rmsnorm(acc[:, :LANES], gk, mean_mat), ax_t, HEAD_DIM // 4).astype(BF16)
        avt = acc[:, LANES:].T.astype(BF16)
        pad_row = lax.broadcasted_iota(jnp.int32, (VT_ROWS_A - HEAD_DIM, tm), 0)
        ones_rows = (pad_row == 0).astype(BF16)
        for g in range(A_KV_HEADS):
            ka_ref[0, g] = ak[:, g * HEAD_DIM:(g + 1) * HEAD_DIM]
            vta_ref[0, g, 0] = jnp.concatenate([avt[g * HEAD_DIM:(g + 1) * HEAD_DIM, :], ones_rows], axis=0)

    lru = {}

    def finish_bx(acc):
        halo = jnp.dot(normed(jnp.concatenate([xprev_ref[0], xnext_ref[0]], axis=0)),
                       w_ref[:, IN_OFFSETS[3]:IN_OFFSETS[3] + LRU_WIDTH], preferred_element_type=F32)
        xpad_scr[0:SUBLANES] = jnp.where(tile > 0, halo[:SUBLANES], 0.0)
        xpad_scr[SUBLANES:SUBLANES + tm] = acc
        xpad_scr[SUBLANES + tm:] = jnp.where(tile < n_tiles - 1, halo[SUBLANES:], 0.0)
        xc = cb_ref[...]
        base = SUBLANES - CONV_LEFT
        for jj in range(CONV_W):
            xc = xc + xpad_scr[base + jj:base + jj + tm, :] * cw_ref[jj:jj + 1, :]
        lru["xc"] = xc
        lru["xcb"] = xc.astype(BF16)

    def gate_dots():
        return [jnp.dot(lru["xcb"], gw_ref[d, g], preferred_element_type=F32) for d in range(2) for g in range(2)]

    def finish_gates(dots):
        xc = lru["xc"]
        for d, (a_ref, u_ref) in enumerate(((af_ref, uf_ref), (ab_ref, ub_ref))):
            c2 = (-0.5 * LRU_C * LOG2E) * _softplus(-lam_ref[d:d + 1, :])
            a = jnp.exp2(c2 * jnp.tanh(0.5 * dots[2 * d] + 0.5 * gb_ref[d, 0:1, :]) + c2)
            gi = _sigmoid(dots[2 * d + 1] + gb_ref[d, 1:2, :])
            a_ref[0] = a
            z = (1.0 - a) * (1.0 + a)
            root = jnp.where(z > 0.0, z * lax.rsqrt(z), 0.0)
            u_ref[0] = root * (gi * xc)

    def finish_by(acc):
        by_ref[0] = acc.astype(BF16)

    def rope_piece(piece, scale):
        piece = _rotate(piece, seq_t, HEAD_DIM // 2)
        return piece if scale is None else piece * scale

    def finish_cq(acc):
        for j in range(IN_WIDTHS[5] // LANES):
            qc_ref[0, :, j * LANES:(j + 1) * LANES] = rope_piece(acc[:, j * LANES:(j + 1) * LANES], Q_SCALE).astype(BF16)

    def finish_ckv(acc):
        kc_ref[0] = rope_piece(acc[:, :LANES], None).astype(BF16)
        vc_ref[0] = acc[:, LANES:].astype(BF16)

    def finish_d(ref, dil, scale, rope):
        def finish(acc):
            for j in range(dw // LANES):
                piece = acc[:, j * LANES:(j + 1) * LANES]
                if rope:
                    piece = rope_piece(piece, scale)
                if dil == 1:
                    ref[0, 0, :, j * LANES:(j + 1) * LANES] = piece.astype(BF16)
                else:
                    perm_scr[j] = piece
            if dil > 1:
                for r in range(dil):
                    rows = [perm_scr[j, pl.ds(r, tm // dil, stride=dil), :] for j in range(dw // LANES)]
                    ref[0, r] = jnp.concatenate(rows, axis=1).astype(BF16)
        return finish

    segments = [(proj(0), finish_aq), (proj(1, 0, 2 * LANES), finish_akv), (proj(3), finish_bx),
                (proj(4), finish_by), (gate_dots, finish_gates), (proj(5), finish_cq),
                (proj(6, 0, 2 * LANES), finish_ckv)]
    for gi, (_, dil) in enumerate(D_PATTERNS):
        for ti, (seg, scale, rope) in enumerate(((8, Q_SCALE, True), (9, None, True), (10, None, False))):
            segments.append((proj(seg, gi * dw, dw), finish_d(d_refs[ti * len(D_PATTERNS) + gi], dil, scale, rope)))

    issued, pending = 0, []
    for idx, (_, finish) in enumerate(segments):
        horizon = min(len(segments), idx + 1 + LOOKAHEAD_INPROJ)
        while issued < horizon and (segments[issued][0] is not gate_dots or "xcb" in lru):
            pending.append(segments[issued][0]())
            issued += 1
        finish(pending.pop(0))


def _inproj(x, g, w_all, layer, qk_gain128, ax_t, seq_t, conv_w, conv_b, gate_w, gate_b, lam):
    b, s, d = x.shape
    tm = min(TM_INPROJ, s)
    nt = s // tm
    per, n8 = tm // SUBLANES, s // SUBLANES
    halo_prev = pl.BlockSpec((1, SUBLANES, d), lambda i, bb: (bb, jnp.maximum(i * per - 1, 0), 0))
    halo_next = pl.BlockSpec((1, SUBLANES, d), lambda i, bb: (bb, jnp.minimum((i + 1) * per, n8 - 1), 0))
    tok = lambda width: pl.BlockSpec((1, tm, width), lambda i, bb: (bb, i, 0))
    tab = pl.BlockSpec((tm, LANES), lambda i, bb: (i, 0))
    dw = D_HEADS * HEAD_DIM
    d_shapes = tuple(jax.ShapeDtypeStruct((b, dil, s // dil, dw), BF16) for _, dil in D_PATTERNS)
    d_specs = tuple(pl.BlockSpec((1, dil, tm // dil, dw), lambda i, bb: (bb, 0, i, 0)) for _, dil in D_PATTERNS)
    out_shape = (
        jax.ShapeDtypeStruct((b, A_HEADS * HEAD_DIM, s), BF16),
        jax.ShapeDtypeStruct((b, A_KV_HEADS, s, HEAD_DIM), BF16),
        jax.ShapeDtypeStruct((b, A_KV_HEADS, nt, VT_ROWS_A, tm), BF16),
        jax.ShapeDtypeStruct((b, s, LRU_WIDTH), F32),
        jax.ShapeDtypeStruct((b, s, LRU_WIDTH), F32),
        jax.ShapeDtypeStruct((b, s, LRU_WIDTH), F32),
        jax.ShapeDtypeStruct((b, s, LRU_WIDTH), F32),
        jax.ShapeDtypeStruct((b, s, LRU_WIDTH), BF16),
        jax.ShapeDtypeStruct((b, s, 512), BF16),
        jax.ShapeDtypeStruct((b, s, 128), BF16),
        jax.ShapeDtypeStruct((b, s, 128), BF16),
    ) + d_shapes * 3
    out_specs = (
        pl.BlockSpec((1, A_HEADS * HEAD_DIM, tm), lambda i, bb: (bb, 0, i)),
        pl.BlockSpec((1, A_KV_HEADS, tm, HEAD_DIM), lambda i, bb: (bb, 0, i, 0)),
        pl.BlockSpec((1, A_KV_HEADS, 1, VT_ROWS_A, tm), lambda i, bb: (bb, 0, i, 0, 0)),
        tok(LRU_WIDTH), tok(LRU_WIDTH), tok(LRU_WIDTH), tok(LRU_WIDTH), tok(LRU_WIDTH), tok(512), tok(128), tok(128),
    ) + d_specs * 3
    return pl.pallas_call(
        _inproj_kernel,
        grid=(nt, b),
        in_specs=[tok(d), halo_prev, halo_next, _resident((1, d)),
                  pl.BlockSpec((None, d, N_IN), lambda i, bb: (layer, 0, 0), pipeline_mode=pl.Buffered(1)),
                  _resident((2, LANES)),
                  tab, tab, tab, tab, tab, tab,
                  _resident(conv_w.shape), _resident(conv_b.shape), _resident(gate_w.shape),
                  _resident(gate_b.shape), _resident(lam.shape)],
        out_specs=out_specs,
        out_shape=out_shape,
        scratch_shapes=[pltpu.VMEM((dw // LANES, tm, LANES), F32),
                        pltpu.VMEM((tm + 2 * SUBLANES, LRU_WIDTH), F32)],
        compiler_params=_cparams(("parallel", "parallel")),
        name="inproj",
    )(x, x, x, g, w_all, qk_gain128, *ax_t, *seq_t, conv_w, conv_b, gate_w, gate_b, lam)


def _sublane_allmax(x):
    for shift in (4, 2, 1):
        x = jnp.maximum(x, pltpu.roll(x, shift, 0))
    return x


def _attn_a_kernel(q_ref, k_ref, v_ref, o_ref, m_ref, acc_ref, *, group, sub):
    n_chunks, chunk = v_ref.shape[2], v_ref.shape[4]
    tq = min(TQ_ATTN_A, q_ref.shape[2])
    n_tiles = q_ref.shape[2] // tq
    m_ref[...] = jnp.full(m_ref.shape, MASK_VALUE, F32)
    acc_ref[...] = jnp.zeros(acc_ref.shape, F32)

    per_body = min(CHUNKS_PER_BODY_ATTN_A, n_chunks)
    n_sub = chunk // sub
    pair = 2 if n_sub % 2 == 0 else 1
    units = [(cc, u, h) for cc in range(per_body) for up in range(n_sub // pair)
             for h in range(n_tiles * group) for u in range(up * pair, (up + 1) * pair)]

    def body(it, carry):
        c0 = it * per_body

        def scores(idx):
            cc, u, h = units[idx]
            row0 = pl.multiple_of((c0 + cc) * chunk + u * sub, sub)
            k = k_ref[0, 0, pl.ds(row0, sub), :]
            j, head = divmod(h, group)
            qt = q_ref[0, head * HEAD_DIM:(head + 1) * HEAD_DIM, j * tq:(j + 1) * tq]
            return jnp.dot(k, qt, preferred_element_type=F32)

        pending = [scores(i) for i in range(LOOKAHEAD_ATTN_A)]
        probs = []
        for idx, (cc, u, h) in enumerate(units):
            if idx + LOOKAHEAD_ATTN_A < len(units):
                pending.append(scores(idx + LOOKAHEAD_ATTN_A))
            s3 = pending.pop(0).reshape(sub // SUBLANES, SUBLANES, tq)
            if not probs:
                m_start = m_ref[h]
                m_old = m_start
            m_new = jnp.maximum(m_old, _sublane_allmax(jnp.max(s3, axis=0)))
            if probs:
                beta = jnp.exp2(m_old - m_new).astype(BF16)
                beta = jnp.concatenate([beta, beta], axis=0)[None]
                probs = [(p.reshape(sub // 16, 16, tq) * beta).reshape(sub, tq) for p in probs]
            probs.append(jnp.exp2(s3 - m_new[None]).reshape(sub, tq).astype(BF16))
            m_old = m_new
            if len(probs) == pair:
                u0 = u - (pair - 1)
                vt = v_ref[0, 0, c0 + cc, :, u0 * sub:(u + 1) * sub]
                pv = jnp.dot(vt, jnp.concatenate(probs, axis=0), preferred_element_type=F32)
                alpha = jnp.exp2(m_start - m_new)
                acc_ref[h] = acc_ref[h] * alpha[None] + pv.reshape(VT_ROWS_A // SUBLANES, SUBLANES, tq)
                m_ref[h] = m_new
                probs = []
        return carry

    lax.fori_loop(0, n_chunks // per_body, body, 0)
    for j in range(n_tiles):
        outs = []
        for h in range(j * group, (j + 1) * group):
            acc = acc_ref[h].reshape(VT_ROWS_A, tq)
            outs.append(acc[:HEAD_DIM] / acc[HEAD_DIM:HEAD_DIM + 1])
        o_ref[0, j * tq:(j + 1) * tq, :] = jnp.concatenate(outs, axis=0).T.astype(o_ref.dtype)


def _attn_a(qt, k, vt):
    b, hd_all, s = qt.shape
    nkv, nt, chunk = vt.shape[1], vt.shape[2], vt.shape[4]
    group = hd_all // HEAD_DIM // nkv
    tq = min(TQ_ATTN_A * TILES_PER_STEP_ATTN_A, s)
    n_state = group * (tq // min(TQ_ATTN_A, s))
    gw = group * HEAD_DIM
    return pl.pallas_call(
        functools.partial(_attn_a_kernel, group=group, sub=min(SUB_ATTN_A, chunk)),
        grid=(b, nkv, s // tq),
        in_specs=[pl.BlockSpec((1, gw, tq), lambda bb, g, i: (bb, g, i)),
                  pl.BlockSpec((1, 1, s, HEAD_DIM), lambda bb, g, i: (bb, g, 0, 0)),
                  pl.BlockSpec((1, 1, nt, VT_ROWS_A, chunk), lambda bb, g, i: (bb, g, 0, 0, 0))],
        out_specs=pl.BlockSpec((1, tq, gw), lambda bb, g, i: (bb, i, g)),
        out_shape=jax.ShapeDtypeStruct((b, s, hd_all), BF16),
        scratch_shapes=[pltpu.VMEM((n_state, SUBLANES, min(TQ_ATTN_A, s)), F32),
                        pltpu.VMEM((n_state, VT_ROWS_A // SUBLANES, SUBLANES, min(TQ_ATTN_A, s)), F32)],
        compiler_params=_cparams(("parallel", "parallel", "parallel")),
        name="attn_a",
    )(qt, k, vt)


def _band_kernel(*refs, nkv, group, half_window, seq_len, has_sink, want_lse):
    q_ref, kp_ref, km_ref, kn_ref, vp_ref, vm_ref, vn_ref = refs[:7]
    rest = list(refs[7:])
    sink_ref = rest.pop(0) if has_sink else None
    o_ref = rest.pop(0)
    lse_ref = rest.pop(0) if want_lse else None

    tq_blk = q_ref.shape[2]
    tq = min(TQ_BAND, tq_blk)
    halo = kp_ref.shape[2]
    nk = tq + 2 * halo
    sub = SUB_BAND if nk % SUB_BAND == 0 else halo
    n_heads = nkv * group
    blk0 = pl.program_id(2) * tq_blk

    kcat = jnp.concatenate([kp_ref[0, 0], km_ref[0, 0], kn_ref[0, 0]], axis=0)
    vt = jnp.concatenate([vp_ref[0, 0], vm_ref[0, 0], vn_ref[0, 0]], axis=0).T

    kw = min(LANES, nkv * HEAD_DIM)
    zeros = jnp.zeros((HEAD_DIM, tq), kcat.dtype)
    q_pads = {}

    def padded_queries(j):
        if j not in q_pads:
            qt = q_ref[0, 0, j * tq:(j + 1) * tq, :].T
            pads = []
            for h in range(n_heads):
                slot = ((h // group) * HEAD_DIM % kw) // HEAD_DIM
                parts = [zeros] * (kw // HEAD_DIM)
                parts[slot] = qt[h * HEAD_DIM:(h + 1) * HEAD_DIM, :]
                pads.append(parts[0] if len(parts) == 1 else jnp.concatenate(parts, axis=0))
            q_pads.clear()
            q_pads[j] = pads
        return q_pads[j]

    units = [(j, kb, h) for j in range(tq_blk // tq) for kb in range(nk // sub) for h in range(n_heads)]

    cw = LANES if tq % LANES == 0 else tq
    n_ch = tq // cw

    def query_chunks(kb):
        k_lo = kb * sub - halo
        lo = max(0, (k_lo - half_window) // cw)
        hi = min(n_ch, -(-(k_lo + sub + half_window) // cw))
        return list(range(lo, hi))

    def scores(idx):
        j, kb, h = units[idx]
        row0 = j * tq + kb * sub
        lane0 = (h // group) * HEAD_DIM // kw * kw
        cs = query_chunks(kb)
        return jnp.dot(kcat[row0:row0 + sub, lane0:lane0 + kw],
                       padded_queries(j)[h][:, cs[0] * cw:(cs[-1] + 1) * cw], preferred_element_type=F32)

    key_row = lax.broadcasted_iota(jnp.int32, (sub, cw), 0)
    query = lax.broadcasted_iota(jnp.int32, (sub, cw), 1)

    def valid_mask(j, kb, c):
        kpos = blk0 + j * tq - halo + kb * sub + key_row
        rel = kpos - (blk0 + j * tq + c * cw + query)
        return (jnp.abs(rel) <= half_window) & (kpos >= 0) & (kpos < seq_len)

    def start_state():
        if has_sink:
            m0 = [jnp.full((SUBLANES, cw), sink_ref[h] * LOG2E, F32) for h in range(n_heads)]
            l0 = (lax.broadcasted_iota(jnp.int32, (SUBLANES, cw), 0) == 0).astype(F32)
        else:
            m0 = [jnp.full((SUBLANES, cw), 0.1 * MASK_VALUE, F32)] * n_heads
            l0 = jnp.zeros((SUBLANES, cw), F32)
        a0 = jnp.zeros((HEAD_DIM // SUBLANES, SUBLANES, cw), F32)
        return ([[m0[h]] * n_ch for h in range(n_heads)], [[l0] * n_ch for _ in range(n_heads)],
                [[a0] * n_ch for _ in range(n_heads)])

    def join(parts):
        return parts[0] if len(parts) == 1 else jnp.concatenate(parts, axis=1)

    def finish(j, m, l, acc):
        outs, lses = [], []
        for h in range(n_heads):
            l_row = join([jnp.sum(x, axis=0, keepdims=True) for x in l[h]])
            outs.append(join([x.reshape(HEAD_DIM, cw) for x in acc[h]]) / l_row)
            if want_lse:
                lse = join([x[0:1] for x in m[h]]) + jnp.log2(l_row)
                lses.append(jnp.broadcast_to(lse, (HEAD_DIM, tq)))
        o_ref[0, 0, j * tq:(j + 1) * tq, :] = jnp.concatenate(outs, axis=0).T.astype(o_ref.dtype)
        if want_lse:
            lse_ref[0, 0, j * tq:(j + 1) * tq, :] = jnp.concatenate(lses, axis=0).T

    look = min(LOOKAHEAD_BAND, len(units))
    pending = [scores(i) for i in range(look)]
    mask_key, masks = None, None
    for idx, (j, kb, h) in enumerate(units):
        if idx + look < len(units):
            pending.append(scores(idx + look))
        if kb == 0 and h == 0:
            m, l, acc = start_state()
        cs = query_chunks(kb)
        if mask_key != (j, kb):
            mask_key, masks = (j, kb), [valid_mask(j, kb, c) for c in cs]
        g = h // group
        s = pending.pop(0)
        probs, alphas = [], []
        for i, c in enumerate(cs):
            s_c = s if len(cs) == 1 else s[:, i * cw:(i + 1) * cw]
            s3 = jnp.where(masks[i], s_c, MASK_VALUE).reshape(sub // SUBLANES, SUBLANES, cw)
            m_new = jnp.maximum(m[h][c], _sublane_allmax(jnp.max(s3, axis=0)))
            alpha = jnp.exp2(m[h][c] - m_new)
            p3 = jnp.exp2(s3 - m_new[None])
            l[h][c] = alpha * l[h][c] + jnp.sum(p3, axis=0)
            m[h][c] = m_new
            probs.append(p3.reshape(sub, cw).astype(BF16))
            alphas.append(alpha)
        col0 = j * tq + kb * sub
        pv = jnp.dot(vt[g * HEAD_DIM:(g + 1) * HEAD_DIM, col0:col0 + sub], join(probs),
                     preferred_element_type=F32)
        for i, c in enumerate(cs):
            pv_c = pv if len(cs) == 1 else pv[:, i * cw:(i + 1) * cw]
            acc[h][c] = acc[h][c] * alphas[i][None] + pv_c.reshape(HEAD_DIM // SUBLANES, SUBLANES, cw)
        if kb == nk // sub - 1 and h == n_heads - 1:
            finish(j, m, l, acc)


def _band_attention(q, k, v, *, nkv, group, half_window, sink=None, want_lse=False):
    b, n_seq, seq_len, wq = q.shape
    wk = nkv * HEAD_DIM
    halo = half_window
    tq = min(TQ_BAND * BAND_TILES_PER_STEP, seq_len)
    per = tq // halo
    n_halo = seq_len // halo

    main = lambda width: pl.BlockSpec((1, 1, tq, width), lambda bb, r, i: (bb, r, i, 0))
    prev = lambda width: pl.BlockSpec(
        (1, 1, halo, width), lambda bb, r, i: (bb, r, jnp.maximum(i * per - 1, 0), 0))
    nxt = lambda width: pl.BlockSpec(
        (1, 1, halo, width), lambda bb, r, i: (bb, r, jnp.minimum((i + 1) * per, n_halo - 1), 0))
    in_specs = [main(wq), prev(wk), main(wk), nxt(wk), prev(wk), main(wk), nxt(wk)]
    args = [q, k, k, k, v, v, v]
    if sink is not None:
        in_specs.append(pl.BlockSpec(memory_space=pltpu.SMEM))
        args.append(sink)
    out_block = main(wq)
    out_shape = [jax.ShapeDtypeStruct(q.shape, BF16)]
    out_specs = [out_block]
    if want_lse:
        out_shape.append(jax.ShapeDtypeStruct(q.shape, F32))
        out_specs.append(out_block)
    res = pl.pallas_call(
        functools.partial(_band_kernel, nkv=nkv, group=group, half_window=half_window,
                          seq_len=seq_len, has_sink=sink is not None, want_lse=want_lse),
        grid=(b, n_seq, seq_len // tq),
        in_specs=in_specs,
        out_specs=out_specs,
        out_shape=out_shape,
        compiler_params=_cparams(("parallel", "parallel", "parallel")),
        name="band_attn",
    )(*args)
    return res if want_lse else res[0]


def _sigmoid(x):
    return 0.5 * jnp.tanh(0.5 * x) + 0.5


def _softplus(x):
    return jnp.maximum(x, 0.0) + jnp.log1p(jnp.exp(-jnp.abs(x)))


def _lru_kernel(af_ref, uf_ref, ab_ref, ub_ref, hf_ref, hb_ref, carry_ref):
    i = pl.program_id(1)
    tt = af_ref.shape[1]
    nblk = tt // SUBLANES

    @pl.when(i == 0)
    def _():
        carry_ref[...] = jnp.zeros(carry_ref.shape, F32)

    row = lax.broadcasted_iota(jnp.int32, (SUBLANES, LRU_WIDTH), 0)

    def scan_block(d, blk, carry, a_ref, u_ref, out_ref):
        a8 = a_ref[0, pl.ds(blk * SUBLANES, SUBLANES), :]
        u8 = u_ref[0, pl.ds(blk * SUBLANES, SUBLANES), :]
        entry = row == (0 if d == 0 else SUBLANES - 1)
        u8 = u8 + jnp.where(entry, a8 * carry, 0.0)
        for step in (1, 2, 4):
            if d == 0:
                shift, ident = step, row < step
            else:
                shift, ident = SUBLANES - step, row >= SUBLANES - step
            u8 = u8 + a8 * jnp.where(ident, 0.0, pltpu.roll(u8, shift, 0))
            if step < 4:
                a8 = a8 * pltpu.roll(a8, shift, 0)
        out_ref[0, pl.ds(blk * SUBLANES, SUBLANES), :] = u8
        last = u8[SUBLANES - 1:SUBLANES, :] if d == 0 else u8[0:1, :]
        return jnp.broadcast_to(last, (SUBLANES, LRU_WIDTH))

    def body(kb, carries):
        cf, cb = carries
        cf = scan_block(0, kb, cf, af_ref, uf_ref, hf_ref)
        cb = scan_block(1, nblk - 1 - kb, cb, ab_ref, ub_ref, hb_ref)
        return cf, cb

    cf, cb = lax.fori_loop(0, nblk, body, (carry_ref[0], carry_ref[1]))
    carry_ref[0] = cf
    carry_ref[1] = cb


def _lru(af, uf, ab, ub):
    b, s, w = af.shape
    tt = min(TT_LRU, s)
    nt = s // tt
    fwd = pl.BlockSpec((1, tt, w), lambda bb, i: (bb, i, 0))
    bwd = pl.BlockSpec((1, tt, w), lambda bb, i: (bb, nt - 1 - i, 0))
    return pl.pallas_call(
        _lru_kernel,
        grid=(b, nt),
        in_specs=[fwd, fwd, bwd, bwd],
        out_specs=[fwd, bwd],
        out_shape=[jax.ShapeDtypeStruct((b, s, w), F32), jax.ShapeDtypeStruct((b, s, w), F32)],
        scratch_shapes=[pltpu.VMEM((2, SUBLANES, w), F32)],
        compiler_params=_cparams(("parallel", "arbitrary")),
        name="lru",
    )(af, uf, ab, ub)


def _gelu_tanh(x):
    return 0.5 * x * (1.0 + jnp.tanh(math.sqrt(2.0 / math.pi) * (x + 0.044715 * (x * x * x))))


def _merge_kernel(x_ref, ya_ref, hf_ref, hb_ref, by_ref, yc_ref,
                  o0_ref, o1_ref, o2_ref, l0_ref, l1_ref, l2_ref, g_ref, wg_ref, gbias_ref,
                  wa_ref, wb_ref, wc_ref, wd_ref, wo_ref, out_ref, perm_scr):
    d_model = x_ref.shape[2]
    tm = x_ref.shape[1]
    x = x_ref[0]
    var = jnp.mean(x * x, axis=-1, keepdims=True)
    hn = (x * lax.rsqrt(var + EPS) * g_ref[...]).astype(BF16)

    def natural(ref, slot):
        dil = ref.shape[1]
        if dil == 1:
            return ref[0, 0].astype(F32)
        for r in range(dil):
            rows = ref[0, r].astype(F32)
            for j in range(rows.shape[1] // LANES):
                perm_scr[slot, j, pl.ds(r, tm // dil, stride=dil), :] = rows[:, j * LANES:(j + 1) * LANES]
        return jnp.concatenate([perm_scr[slot, j] for j in range(perm_scr.shape[1])], axis=1)

    yb = ((hf_ref[0] + hb_ref[0]) * _gelu_tanh(by_ref[0].astype(F32))).astype(BF16)

    l0, l1, l2 = natural(l0_ref, 0), natural(l1_ref, 0), natural(l2_ref, 1)
    o0, o1, o2 = natural(o0_ref, 2), natural(o1_ref, 2), natural(o2_ref, 3)
    mx = jnp.maximum(jnp.maximum(l0, l1), l2)
    e0, e1, e2 = jnp.exp2(l0 - mx), jnp.exp2(l1 - mx), jnp.exp2(l2 - mx)
    yd = ((e0 * o0 + e1 * o1 + e2 * o2) / (e0 + e1 + e2)).astype(BF16)

    def gate_logits(kk):
        off = wg_ref.shape[1] - N_BRANCH * d_model + kk * d_model
        return lambda: jnp.dot(hn, wg_ref[:, off:off + d_model], preferred_element_type=F32)

    def branch(y, w_ref):
        return lambda: jnp.dot(y, w_ref[...], preferred_element_type=F32)

    dots = []
    for kk, (y, w_ref) in enumerate(((ya_ref[0], wa_ref), (yb, wb_ref), (yc_ref[0], wc_ref), (yd, wd_ref))):
        dots += [gate_logits(kk), branch(y, w_ref)]
    pending = [dots[0](), dots[1]()]
    merged = None
    for kk in range(N_BRANCH):
        pending += [d() for d in dots[2 * kk + 2:2 * kk + 4]]
        logits, proj = pending.pop(0), pending.pop(0)
        term = _sigmoid(logits + gbias_ref[kk:kk + 1, :]) * proj
        merged = term if merged is None else merged + term

    out_ref[0] = x + jnp.dot(merged.astype(BF16), wo_ref[...], preferred_element_type=F32)


def _merge(x, ya, hf, hb, by, yc, o_d, lse_d, g, w_all, layer, gbias, wa, wb, wc, wd, wo):
    b, s, d = x.shape
    tm = min(TM_MERGE, s)
    tok = lambda width: pl.BlockSpec((1, tm, width), lambda bb, i: (bb, i, 0))
    dw = D_HEADS * HEAD_DIM
    res = [pl.BlockSpec((1, dil, tm // dil, dw), lambda bb, i: (bb, 0, i, 0)) for _, dil in D_PATTERNS]
    return pl.pallas_call(
        _merge_kernel,
        grid=(b, s // tm),
        in_specs=[tok(d), tok(512), tok(LRU_WIDTH), tok(LRU_WIDTH), tok(LRU_WIDTH), tok(512),
                  *res, *res, _resident(g.shape),
                  pl.BlockSpec((None,) + w_all.shape[1:], lambda bb, i: (layer, 0, 0), pipeline_mode=pl.Buffered(1)),
                  _resident(gbias.shape), _resident(wa.shape), _resident(wb.shape),
                  _resident(wc.shape), _resident(wd.shape), _resident(wo.shape)],
        out_specs=tok(d),
        out_shape=jax.ShapeDtypeStruct((b, s, d), F32),
        scratch_shapes=[pltpu.VMEM((4, dw // LANES, tm, LANES), F32)],
        compiler_params=_cparams(("parallel", "parallel")),
        name="merge",
    )(x, ya, hf, hb, by, yc, *o_d, *lse_d, g, w_all, gbias, wa, wb, wc, wd, wo)


def _mlp_kernel(x_ref, g_ref, w1_ref, w2_ref, gf_ref, out_ref, *, final_norm):
    x = x_ref[0]
    var = jnp.mean(x * x, axis=-1, keepdims=True)
    hn = (x * lax.rsqrt(var + EPS) * g_ref[...]).astype(BF16)
    hidden = w1_ref.shape[1]
    acc = x
    for c in range(hidden // MLP_CHUNK):
        h = jnp.dot(hn, w1_ref[:, c * MLP_CHUNK:(c + 1) * MLP_CHUNK], preferred_element_type=F32)
        h = jnp.square(jnp.maximum(h, 0.0)).astype(BF16)
        acc = acc + jnp.dot(h, w2_ref[c * MLP_CHUNK:(c + 1) * MLP_CHUNK, :], preferred_element_type=F32)
    if final_norm:
        var = jnp.mean(acc * acc, axis=-1, keepdims=True)
        acc = acc * lax.rsqrt(var + EPS) * gf_ref[...]
    out_ref[0] = acc


def _mlp(x, g, w1, w2, g_final, final_norm):
    b, s, d = x.shape
    tm = min(TM_MLP, s)
    tok = pl.BlockSpec((1, tm, d), lambda bb, i: (bb, i, 0))
    return pl.pallas_call(
        functools.partial(_mlp_kernel, final_norm=final_norm),
        grid=(b, s // tm),
        in_specs=[tok, _resident(g.shape), _resident(w1.shape), _resident(w2.shape), _resident(g_final.shape)],
        out_specs=tok,
        out_shape=jax.ShapeDtypeStruct((b, s, d), F32),
        compiler_params=_cparams(("parallel", "parallel")),
        name="mlp",
    )(x, g, w1, w2, g_final)


def _dense_block_diag(w):
    nb, bw, _ = w.shape
    eye = jnp.eye(nb, dtype=w.dtype)
    return jnp.einsum('ncd,nm->ncmd', w, eye).reshape(nb * bw, nb * bw)


def kernel(x, norm_mix_g, w_in, gate_bias, qk_norm_g, conv_w, conv_b, lru_gate_w, lru_gate_b, lru_lambda, sink_logit, w_proj_a, w_proj_b, w_proj_c, w_proj_d, w_out, norm_mlp_g, w_mlp1, w_mlp2, norm_final_g):
    b, s, d = x.shape
    depth = w_in.shape[0]
    ax_t, seq_t = _rope_tables(s)
    dw = D_HEADS * HEAD_DIM
    n_groups = len(D_PATTERNS)
    w_in_bf = w_in.astype(BF16)

    for l in range(depth):
        qk_gain = jnp.tile(qk_norm_g[l], (1, LANES // HEAD_DIM))
        gate_w_dense = jnp.stack([jnp.stack([_dense_block_diag(lru_gate_w[l, dd, gg]) for gg in range(2)])
                                  for dd in range(2)]).astype(BF16)
        outs = _inproj(x, norm_mix_g[l][None, :], w_in_bf, l, qk_gain, ax_t, seq_t,
                       conv_w[l], conv_b[l][None, :], gate_w_dense, lru_gate_b[l], lru_lambda[l])
        qta, ka, vta, af, uf, ab, ub, by, qc, kc, vc = outs[:11]
        qd, kd, vd = outs[11:11 + n_groups], outs[11 + n_groups:11 + 2 * n_groups], outs[11 + 2 * n_groups:11 + 3 * n_groups]

        ya = _attn_a(qta, ka, vta)

        hf, hb = _lru(af, uf, ab, ub)

        yc = _band_attention(qc[:, None], kc[:, None], vc[:, None], nkv=C_KV_HEADS,
                             group=C_HEADS // C_KV_HEADS, half_window=C_HALF_WINDOW, sink=sink_logit[l])[:, 0]

        o_d, lse_d = [], []
        for gi, (window, dil) in enumerate(D_PATTERNS):
            o, lse = _band_attention(qd[gi], kd[gi], vd[gi], nkv=D_HEADS, group=1,
                                     half_window=window // (2 * dil), want_lse=True)
            o_d.append(o)
            lse_d.append(lse)

        x = _merge(x, ya, hf, hb, by, yc, o_d, lse_d, norm_mix_g[l][None, :], w_in_bf, l, gate_bias[l],
                   w_proj_a[l].astype(BF16), w_proj_b[l].astype(BF16), w_proj_c[l].astype(BF16),
                   w_proj_d[l].astype(BF16), w_out[l].astype(BF16))
        x = _mlp(x, norm_mlp_g[l][None, :], w_mlp1[l].astype(BF16), w_mlp2[l].astype(BF16),
                 norm_final_g[None, :], final_norm=(l == depth - 1))
    return x
```

```python
import functools
import math

import jax
import jax.numpy as jnp
from jax import lax
from jax.experimental import pallas as pl
from jax.experimental.pallas import tpu as pltpu

F32 = jnp.float32
BF16 = jnp.bfloat16

HEAD_DIM = 64
ROPE_THETA = 10000.0
GRID_W = 64
EPS = 1e-6
MASK_VALUE = -1e30
LOG2E = math.log2(math.e)

A_HEADS, A_KV_HEADS = 8, 2
LRU_WIDTH, LRU_BLOCKS, LRU_C = 512, 8, 8.0
CONV_W, CONV_LEFT = 4, 2
C_HEADS, C_KV_HEADS, C_HALF_WINDOW = 8, 2, 128
D_PATTERNS = ((128, 1), (512, 4), (2048, 16))
D_HEADS = 4
N_BRANCH = 4

LANES = 128
SUBLANES = 8
VMEM_LIMIT_BYTES = 56 * 1024 * 1024

TM_INPROJ = 512
TQ_ATTN_A = 256
SUB_ATTN_A = 128
TILES_PER_STEP_ATTN_A = 2
LOOKAHEAD_ATTN_A = 14
LOOKAHEAD_BAND = 6
LOOKAHEAD_INPROJ = 6
SUB_BAND = 128
CHUNKS_PER_BODY_ATTN_A = 16
VT_ROWS_A = HEAD_DIM + 16
TQ_BAND = 256
BAND_TILES_PER_STEP = 4
TT_LRU = 1024
TM_MERGE = 512
TM_MLP = 512
MLP_CHUNK = 1024

Q_SCALE = HEAD_DIM ** -0.5 * LOG2E


def _cparams(sem):
    return pltpu.CompilerParams(dimension_semantics=sem, vmem_limit_bytes=VMEM_LIMIT_BYTES)


def _layer(arr, layer):
    zeros = (0,) * (arr.ndim - 1)
    return pl.BlockSpec((None,) + arr.shape[1:], lambda *_: (layer,) + zeros, pipeline_mode=pl.Buffered(1))


def _resident(shape):
    nd = len(shape)
    return pl.BlockSpec(shape, lambda *_: (0,) * nd, pipeline_mode=pl.Buffered(1))


def _rope_tables(seq):
    pos = jnp.arange(seq, dtype=F32)
    lane = jnp.arange(LANES)
    d = lane % HEAD_DIM

    inv = ROPE_THETA ** (-jnp.arange(0, HEAD_DIM, 2, dtype=F32) / HEAD_DIM)
    ang = pos[:, None] * inv[None, :]
    f = d % (HEAD_DIM // 2)
    cos_s, sin_s = jnp.cos(ang)[:, f], jnp.sin(ang)[:, f]
    lo = (d < HEAD_DIM // 2)[None, :]
    seq_t = (cos_s, jnp.where(lo, -sin_s, 0.0), jnp.where(lo, 0.0, sin_s))

    quarter = HEAD_DIM // 4
    inv_ax = ROPE_THETA ** (-jnp.arange(0, HEAD_DIM // 2, 2, dtype=F32) / (HEAD_DIM // 2))
    n_rows = seq // GRID_W
    ang_r = jnp.arange(n_rows, dtype=F32)[:, None] * inv_ax[None, :]
    ang_c = jnp.arange(GRID_W, dtype=F32)[:, None] * inv_ax[None, :]

    def expand(by_row, by_col):
        r = jnp.repeat(by_row, GRID_W, axis=0)
        c = jnp.tile(by_col, (n_rows, 1))
        return jnp.tile(jnp.concatenate([r, r, c, c], axis=1), (1, LANES // HEAD_DIM))

    cos_a, sin_a = expand(jnp.cos(ang_r), jnp.cos(ang_c)), expand(jnp.sin(ang_r), jnp.sin(ang_c))
    lo_a = (d % (HEAD_DIM // 2) < quarter)[None, :]
    ax_t = (cos_a, jnp.where(lo_a, -sin_a, 0.0), jnp.where(lo_a, 0.0, sin_a))
    return ax_t, seq_t


def _rotate(x, tables, half):
    cos, sin_a, sin_b = tables
    return x * cos + pltpu.roll(x, LANES - half, 1) * sin_a + pltpu.roll(x, half, 1) * sin_b


def _split_dot(x, m):
    hi = x.astype(BF16)
    lo = (x - hi.astype(F32)).astype(BF16)
    return jnp.dot(jnp.concatenate([hi, lo], axis=1), jnp.concatenate([m, m], axis=0),
                   preferred_element_type=F32)


def _head_mean_matrix():
    r = lax.broadcasted_iota(jnp.int32, (LANES, LANES), 0) // HEAD_DIM
    c = lax.broadcasted_iota(jnp.int32, (LANES, LANES), 1) // HEAD_DIM
    return jnp.where(r == c, 1.0 / HEAD_DIM, 0.0).astype(BF16)


def _head_rmsnorm(x, gain, mean_mat):
    var = _split_dot(x * x, mean_mat)
    return x * lax.rsqrt(var + EPS) * gain


IN_WIDTHS = (512, 128, 128, 512, 512, 512, 128, 128, 768, 768, 768)
IN_OFFSETS = tuple(sum(IN_WIDTHS[:i]) for i in range(len(IN_WIDTHS)))
N_IN = sum(IN_WIDTHS)


def _inproj_kernel(x_ref, xprev_ref, xnext_ref, g_ref, w_ref, qkg_ref,
                   ca_ref, saa_ref, sab_ref, cs_ref, ssa_ref, ssb_ref,
                   cw_ref, cb_ref, gw_ref, gb_ref, lam_ref,
                   qta_ref, ka_ref, vta_ref, af_ref, uf_ref, ab_ref, ub_ref, by_ref,
                   qc_ref, kc_ref, vc_ref, *rest):
    d_refs, perm_scr, xpad_scr = rest[:9], rest[9], rest[10]
    tile = pl.program_id(0)
    n_tiles = pl.num_programs(0)

    def normed(rows):
        var = jnp.mean(rows * rows, axis=-1, keepdims=True)
        return (rows * lax.rsqrt(var + EPS) * g_ref[...]).astype(BF16)

    x = x_ref[0]
    hn = normed(x)

    def proj(seg, lo=0, width=None):
        off = IN_OFFSETS[seg] + lo
        width = IN_WIDTHS[seg] if width is None else width
        return lambda: jnp.dot(hn, w_ref[:, off:off + width], preferred_element_type=F32)

    ax_t = (ca_ref[...], saa_ref[...], sab_ref[...])
    seq_t = (cs_ref[...], ssa_ref[...], ssb_ref[...])
    mean_mat = _head_mean_matrix()
    gq = qkg_ref[0:1, :]
    gk = qkg_ref[1:2, :]
    tm = x.shape[0]
    dw = D_HEADS * HEAD_DIM

    def finish_aq(acc):
        for j in range(IN_WIDTHS[0] // LANES):
            piece = _head_rmsnorm(acc[:, j * LANES:(j + 1) * LANES], gq, mean_mat)
            piece = _rotate(piece, ax_t, HEAD_DIM // 4) * Q_SCALE
            qta_ref[0, j * LANES:(j + 1) * LANES, :] = piece.T.astype(BF16)

    def finish_akv(acc):
        ak = _rotate(_head_rmsnorm(acc[:, :LANES], gk, mean_mat), ax_t, HEAD_DIM // 4).astype(BF16)
        avt = acc[:, LANES:].T.astype(BF16)
        pad_row = lax.broadcasted_iota(jnp.int32, (VT_ROWS_A - HEAD_DIM, tm), 0)
        ones_rows = (pad_row == 0).astype(BF16)
        for g in range(A_KV_HEADS):
            ka_ref[0, g] = ak[:, g * HEAD_DIM:(g + 1) * HEAD_DIM]
            vta_ref[0, g, 0] = jnp.concatenate([avt[g * HEAD_DIM:(g + 1) * HEAD_DIM, :], ones_rows], axis=0)

    lru = {}

    def finish_bx(acc):
        halo = jnp.dot(normed(jnp.concatenate([xprev_ref[0], xnext_ref[0]], axis=0)),
                       w_ref[:, IN_OFFSETS[3]:IN_OFFSETS[3] + LRU_WIDTH], preferred_element_type=F32)
        xpad_scr[0:SUBLANES] = jnp.where(tile > 0, halo[:SUBLANES], 0.0)
        xpad_scr[SUBLANES:SUBLANES + tm] = acc
        xpad_scr[SUBLANES + tm:] = jnp.where(tile < n_tiles - 1, halo[SUBLANES:], 0.0)
        xc = cb_ref[...]
        base = SUBLANES - CONV_LEFT
        for jj in range(CONV_W):
            xc = xc + xpad_scr[base + jj:base + jj + tm, :] * cw_ref[jj:jj + 1, :]
        lru["xc"] = xc
        lru["xcb"] = xc.astype(BF16)

    def gate_dots():
        return [jnp.dot(lru["xcb"], gw_ref[d, g], preferred_element_type=F32) for d in range(2) for g in range(2)]

    def finish_gates(dots):
        xc = lru["xc"]
        for d, (a_ref, u_ref) in enumerate(((af_ref, uf_ref), (ab_ref, ub_ref))):
            c2 = (-0.5 * LRU_C * LOG2E) * _softplus(-lam_ref[d:d + 1, :])
            a = jnp.exp2(c2 * jnp.tanh(0.5 * dots[2 * d] + 0.5 * gb_ref[d, 0:1, :]) + c2)
            gi = _sigmoid(dots[2 * d + 1] + gb_ref[d, 1:2, :])
            a_ref[0] = a
            z = (1.0 - a) * (1.0 + a)
            root = jnp.where(z > 0.0, z * lax.rsqrt(z), 0.0)
            u_ref[0] = root * (gi * xc)

    def finish_by(acc):
        by_ref[0] = acc.astype(BF16)

    def rope_piece(piece, scale):
        piece = _rotate(piece, seq_t, HEAD_DIM // 2)
        return piece if scale is None else piece * scale

    def finish_cq(acc):
        for j in range(IN_WIDTHS[5] // LANES):
            qc_ref[0, :, j * LANES:(j + 1) * LANES] = rope_piece(acc[:, j * LANES:(j + 1) * LANES], Q_SCALE).astype(BF16)

    def finish_ckv(acc):
        kc_ref[0] = rope_piece(acc[:, :LANES], None).astype(BF16)
        vc_ref[0] = acc[:, LANES:].astype(BF16)

    def finish_d(ref, dil, scale, rope):
        def finish(acc):
            for j in range(dw // LANES):
                piece = acc[:, j * LANES:(j + 1) * LANES]
                if rope:
                    piece = rope_piece(piece, scale)
                if dil == 1:
                    ref[0, 0, :, j * LANES:(j + 1) * LANES] = piece.astype(BF16)
                else:
                    perm_scr[j] = piece
            if dil > 1:
                for r in range(dil):
                    rows = [perm_scr[j, pl.ds(r, tm // dil, stride=dil), :] for j in range(dw // LANES)]
                    ref[0, r] = jnp.concatenate(rows, axis=1).astype(BF16)
        return finish

    segments = [(proj(0), finish_aq), (proj(1, 0, 2 * LANES), finish_akv), (proj(3), finish_bx),
                (proj(4), finish_by), (gate_dots, finish_gates), (proj(5), finish_cq),
                (proj(6, 0, 2 * LANES), finish_ckv)]
    for gi, (_, dil) in enumerate(D_PATTERNS):
        for ti, (seg, scale, rope) in enumerate(((8, Q_SCALE, True), (9, None, True), (10, None, False))):
            segments.append((proj(seg, gi * dw, dw), finish_d(d_refs[ti * len(D_PATTERNS) + gi], dil, scale, rope)))

    issued, pending = 0, []
    for idx, (_, finish) in enumerate(segments):
        horizon = min(len(segments), idx + 1 + LOOKAHEAD_INPROJ)
        while issued < horizon and (segments[issued][0] is not gate_dots or "xcb" in lru):
            pending.append(segments[issued][0]())
            issued += 1
        finish(pending.pop(0))


def _inproj(x, layer, g, w_all, qk_gain128, ax_t, seq_t, conv_w, conv_b, gate_w, gate_b, lam):
    b, s, d = x.shape
    tm = min(TM_INPROJ, s)
    nt = s // tm
    per, n8 = tm // SUBLANES, s // SUBLANES
    halo_prev = pl.BlockSpec((1, SUBLANES, d), lambda i, bb: (bb, jnp.maximum(i * per - 1, 0), 0))
    halo_next = pl.BlockSpec((1, SUBLANES, d), lambda i, bb: (bb, jnp.minimum((i + 1) * per, n8 - 1), 0))
    tok = lambda width: pl.BlockSpec((1, tm, width), lambda i, bb: (bb, i, 0))
    tab = pl.BlockSpec((tm, LANES), lambda i, bb: (i, 0))
    dw = D_HEADS * HEAD_DIM
    d_shapes = tuple(jax.ShapeDtypeStruct((b, dil, s // dil, dw), BF16) for _, dil in D_PATTERNS)
    d_specs = tuple(pl.BlockSpec((1, dil, tm // dil, dw), lambda i, bb: (bb, 0, i, 0)) for _, dil in D_PATTERNS)
    out_shape = (
        jax.ShapeDtypeStruct((b, A_HEADS * HEAD_DIM, s), BF16),
        jax.ShapeDtypeStruct((b, A_KV_HEADS, s, HEAD_DIM), BF16),
        jax.ShapeDtypeStruct((b, A_KV_HEADS, nt, VT_ROWS_A, tm), BF16),
        jax.ShapeDtypeStruct((b, s, LRU_WIDTH), F32),
        jax.ShapeDtypeStruct((b, s, LRU_WIDTH), F32),
        jax.ShapeDtypeStruct((b, s, LRU_WIDTH), F32),
        jax.ShapeDtypeStruct((b, s, LRU_WIDTH), F32),
        jax.ShapeDtypeStruct((b, s, LRU_WIDTH), BF16),
        jax.ShapeDtypeStruct((b, s, 512), BF16),
        jax.ShapeDtypeStruct((b, s, 128), BF16),
        jax.ShapeDtypeStruct((b, s, 128), BF16),
    ) + d_shapes * 3
    out_specs = (
        pl.BlockSpec((1, A_HEADS * HEAD_DIM, tm), lambda i, bb: (bb, 0, i)),
        pl.BlockSpec((1, A_KV_HEADS, tm, HEAD_DIM), lambda i, bb: (bb, 0, i, 0)),
        pl.BlockSpec((1, A_KV_HEADS, 1, VT_ROWS_A, tm), lambda i, bb: (bb, 0, i, 0, 0)),
        tok(LRU_WIDTH), tok(LRU_WIDTH), tok(LRU_WIDTH), tok(LRU_WIDTH), tok(LRU_WIDTH), tok(512), tok(128), tok(128),
    ) + d_specs * 3
    return pl.pallas_call(
        _inproj_kernel,
        grid=(nt, b),
        in_specs=[tok(d), halo_prev, halo_next, _layer(g, layer),
                  pl.BlockSpec((None, d, N_IN), lambda i, bb: (layer, 0, 0), pipeline_mode=pl.Buffered(1)),
                  _layer(qk_gain128, layer),
                  tab, tab, tab, tab, tab, tab,
                  _layer(conv_w, layer), _layer(conv_b, layer), _layer(gate_w, layer),
                  _layer(gate_b, layer), _layer(lam, layer)],
        out_specs=out_specs,
        out_shape=out_shape,
        scratch_shapes=[pltpu.VMEM((dw // LANES, tm, LANES), F32),
                        pltpu.VMEM((tm + 2 * SUBLANES, LRU_WIDTH), F32)],
        compiler_params=_cparams(("parallel", "parallel")),
        name="inproj",
    )(x, x, x, g, w_all, qk_gain128, *ax_t, *seq_t, conv_w, conv_b, gate_w, gate_b, lam)


def _sublane_allmax(x):
    for shift in (4, 2, 1):
        x = jnp.maximum(x, pltpu.roll(x, shift, 0))
    return x


def _attn_a_kernel(q_ref, k_ref, v_ref, o_ref, m_ref, acc_ref, *, group, sub):
    n_chunks, chunk = v_ref.shape[2], v_ref.shape[4]
    tq = min(TQ_ATTN_A, q_ref.shape[2])
    n_tiles = q_ref.shape[2] // tq
    m_ref[...] = jnp.full(m_ref.shape, MASK_VALUE, F32)
    acc_ref[...] = jnp.zeros(acc_ref.shape, F32)

    per_body = min(CHUNKS_PER_BODY_ATTN_A, n_chunks)
    n_sub = chunk // sub
    pair = 2 if n_sub % 2 == 0 else 1
    units = [(cc, u, h) for cc in range(per_body) for up in range(n_sub // pair)
             for h in range(n_tiles * group) for u in range(up * pair, (up + 1) * pair)]

    def body(it, carry):
        c0 = it * per_body

        def scores(idx):
            cc, u, h = units[idx]
            row0 = pl.multiple_of((c0 + cc) * chunk + u * sub, sub)
            k = k_ref[0, 0, pl.ds(row0, sub), :]
            j, head = divmod(h, group)
            qt = q_ref[0, head * HEAD_DIM:(head + 1) * HEAD_DIM, j * tq:(j + 1) * tq]
            return jnp.dot(k, qt, preferred_element_type=F32)

        pending = [scores(i) for i in range(LOOKAHEAD_ATTN_A)]
        probs = []
        for idx, (cc, u, h) in enumerate(units):
            if idx + LOOKAHEAD_ATTN_A < len(units):
                pending.append(scores(idx + LOOKAHEAD_ATTN_A))
            s3 = pending.pop(0).reshape(sub // SUBLANES, SUBLANES, tq)
            if not probs:
                m_start = m_ref[h]
                m_old = m_start
            m_new = jnp.maximum(m_old, _sublane_allmax(jnp.max(s3, axis=0)))
            if probs:
                beta = jnp.exp2(m_old - m_new).astype(BF16)
                beta = jnp.concatenate([beta, beta], axis=0)[None]
                probs = [(p.reshape(sub // 16, 16, tq) * beta).reshape(sub, tq) for p in probs]
            probs.append(jnp.exp2(s3 - m_new[None]).reshape(sub, tq).astype(BF16))
            m_old = m_new
            if len(probs) == pair:
                u0 = u - (pair - 1)
                vt = v_ref[0, 0, c0 + cc, :, u0 * sub:(u + 1) * sub]
                pv = jnp.dot(vt, jnp.concatenate(probs, axis=0), preferred_element_type=F32)
                alpha = jnp.exp2(m_start - m_new)
                acc_ref[h] = acc_ref[h] * alpha[None] + pv.reshape(VT_ROWS_A // SUBLANES, SUBLANES, tq)
                m_ref[h] = m_new
                probs = []
        return carry

    lax.fori_loop(0, n_chunks // per_body, body, 0)
    for j in range(n_tiles):
        outs = []
        for h in range(j * group, (j + 1) * group):
            acc = acc_ref[h].reshape(VT_ROWS_A, tq)
            outs.append(acc[:HEAD_DIM] / acc[HEAD_DIM:HEAD_DIM + 1])
        o_ref[0, j * tq:(j + 1) * tq, :] = jnp.concatenate(outs, axis=0).T.astype(o_ref.dtype)


def _attn_a(qt, k, vt):
    b, hd_all, s = qt.shape
    nkv, nt, chunk = vt.shape[1], vt.shape[2], vt.shape[4]
    group = hd_all // HEAD_DIM // nkv
    tq = min(TQ_ATTN_A * TILES_PER_STEP_ATTN_A, s)
    n_state = group * (tq // min(TQ_ATTN_A, s))
    gw = group * HEAD_DIM
    return pl.pallas_call(
        functools.partial(_attn_a_kernel, group=group, sub=min(SUB_ATTN_A, chunk)),
        grid=(b, nkv, s // tq),
        in_specs=[pl.BlockSpec((1, gw, tq), lambda bb, g, i: (bb, g, i)),
                  pl.BlockSpec((1, 1, s, HEAD_DIM), lambda bb, g, i: (bb, g, 0, 0)),
                  pl.BlockSpec((1, 1, nt, VT_ROWS_A, chunk), lambda bb, g, i: (bb, g, 0, 0, 0))],
        out_specs=pl.BlockSpec((1, tq, gw), lambda bb, g, i: (bb, i, g)),
        out_shape=jax.ShapeDtypeStruct((b, s, hd_all), BF16),
        scratch_shapes=[pltpu.VMEM((n_state, SUBLANES, min(TQ_ATTN_A, s)), F32),
                        pltpu.VMEM((n_state, VT_ROWS_A // SUBLANES, SUBLANES, min(TQ_ATTN_A, s)), F32)],
        compiler_params=_cparams(("parallel", "parallel", "parallel")),
        name="attn_a",
    )(qt, k, vt)


def _band_kernel(*refs, nkv, group, half_window, seq_len, sink_layer, want_lse):
    has_sink = sink_layer is not None
    q_ref, kp_ref, km_ref, kn_ref, vp_ref, vm_ref, vn_ref = refs[:7]
    rest = list(refs[7:])
    sink_ref = rest.pop(0) if has_sink else None
    o_ref = rest.pop(0)
    lse_ref = rest.pop(0) if want_lse else None

    tq_blk = q_ref.shape[2]
    tq = min(TQ_BAND, tq_blk)
    halo = kp_ref.shape[2]
    nk = tq + 2 * halo
    sub = SUB_BAND if nk % SUB_BAND == 0 else halo
    n_heads = nkv * group
    blk0 = pl.program_id(2) * tq_blk

    kcat = jnp.concatenate([kp_ref[0, 0], km_ref[0, 0], kn_ref[0, 0]], axis=0)
    vt = jnp.concatenate([vp_ref[0, 0], vm_ref[0, 0], vn_ref[0, 0]], axis=0).T

    kw = min(LANES, nkv * HEAD_DIM)
    zeros = jnp.zeros((HEAD_DIM, tq), kcat.dtype)
    q_pads = {}

    def padded_queries(j):
        if j not in q_pads:
            qt = q_ref[0, 0, j * tq:(j + 1) * tq, :].T
            pads = []
            for h in range(n_heads):
                slot = ((h // group) * HEAD_DIM % kw) // HEAD_DIM
                parts = [zeros] * (kw // HEAD_DIM)
                parts[slot] = qt[h * HEAD_DIM:(h + 1) * HEAD_DIM, :]
                pads.append(parts[0] if len(parts) == 1 else jnp.concatenate(parts, axis=0))
            q_pads.clear()
            q_pads[j] = pads
        return q_pads[j]

    units = [(j, kb, h) for j in range(tq_blk // tq) for kb in range(nk // sub) for h in range(n_heads)]

    cw = LANES if tq % LANES == 0 else tq
    n_ch = tq // cw

    def query_chunks(kb):
        k_lo = kb * sub - halo
        lo = max(0, (k_lo - half_window) // cw)
        hi = min(n_ch, -(-(k_lo + sub + half_window) // cw))
        return list(range(lo, hi))

    def scores(idx):
        j, kb, h = units[idx]
        row0 = j * tq + kb * sub
        lane0 = (h // group) * HEAD_DIM // kw * kw
        cs = query_chunks(kb)
        return jnp.dot(kcat[row0:row0 + sub, lane0:lane0 + kw],
                       padded_queries(j)[h][:, cs[0] * cw:(cs[-1] + 1) * cw], preferred_element_type=F32)

    key_row = lax.broadcasted_iota(jnp.int32, (sub, cw), 0)
    query = lax.broadcasted_iota(jnp.int32, (sub, cw), 1)

    def valid_mask(j, kb, c):
        kpos = blk0 + j * tq - halo + kb * sub + key_row
        rel = kpos - (blk0 + j * tq + c * cw + query)
        return (jnp.abs(rel) <= half_window) & (kpos >= 0) & (kpos < seq_len)

    def start_state():
        if has_sink:
            m0 = [jnp.full((SUBLANES, cw), sink_ref[sink_layer, h] * LOG2E, F32) for h in range(n_heads)]
            l0 = (lax.broadcasted_iota(jnp.int32, (SUBLANES, cw), 0) == 0).astype(F32)
        else:
            m0 = [jnp.full((SUBLANES, cw), 0.1 * MASK_VALUE, F32)] * n_heads
            l0 = jnp.zeros((SUBLANES, cw), F32)
        a0 = jnp.zeros((HEAD_DIM // SUBLANES, SUBLANES, cw), F32)
        return ([[m0[h]] * n_ch for h in range(n_heads)], [[l0] * n_ch for _ in range(n_heads)],
                [[a0] * n_ch for _ in range(n_heads)])

    def join(parts):
        return parts[0] if len(parts) == 1 else jnp.concatenate(parts, axis=1)

    def finish(j, m, l, acc):
        outs, lses = [], []
        for h in range(n_heads):
            l_row = join([jnp.sum(x, axis=0, keepdims=True) for x in l[h]])
            outs.append(join([x.reshape(HEAD_DIM, cw) for x in acc[h]]) / l_row)
            if want_lse:
                lse = join([x[0:1] for x in m[h]]) + jnp.log2(l_row)
                lses.append(jnp.broadcast_to(lse, (HEAD_DIM, tq)))
        o_ref[0, 0, j * tq:(j + 1) * tq, :] = jnp.concatenate(outs, axis=0).T.astype(o_ref.dtype)
        if want_lse:
            lse_ref[0, 0, j * tq:(j + 1) * tq, :] = jnp.concatenate(lses, axis=0).T

    look = min(LOOKAHEAD_BAND, len(units))
    pending = [scores(i) for i in range(look)]
    mask_key, masks = None, None
    for idx, (j, kb, h) in enumerate(units):
        if idx + look < len(units):
            pending.append(scores(idx + look))
        if kb == 0 and h == 0:
            m, l, acc = start_state()
        cs = query_chunks(kb)
        if mask_key != (j, kb):
            mask_key, masks = (j, kb), [valid_mask(j, kb, c) for c in cs]
        g = h // group
        s = pending.pop(0)
        probs, alphas = [], []
        for i, c in enumerate(cs):
            s_c = s if len(cs) == 1 else s[:, i * cw:(i + 1) * cw]
            s3 = jnp.where(masks[i], s_c, MASK_VALUE).reshape(sub // SUBLANES, SUBLANES, cw)
            m_new = jnp.maximum(m[h][c], _sublane_allmax(jnp.max(s3, axis=0)))
            alpha = jnp.exp2(m[h][c] - m_new)
            p3 = jnp.exp2(s3 - m_new[None])
            l[h][c] = alpha * l[h][c] + jnp.sum(p3, axis=0)
            m[h][c] = m_new
            probs.append(p3.reshape(sub, cw).astype(BF16))
            alphas.append(alpha)
        col0 = j * tq + kb * sub
        pv = jnp.dot(vt[g * HEAD_DIM:(g + 1) * HEAD_DIM, col0:col0 + sub], join(probs),
                     preferred_element_type=F32)
        for i, c in enumerate(cs):
            pv_c = pv if len(cs) == 1 else pv[:, i * cw:(i + 1) * cw]
            acc[h][c] = acc[h][c] * alphas[i][None] + pv_c.reshape(HEAD_DIM // SUBLANES, SUBLANES, cw)
        if kb == nk // sub - 1 and h == n_heads - 1:
            finish(j, m, l, acc)


def _band_attention(q, k, v, *, nkv, group, half_window, sink=None, sink_layer=None, want_lse=False):
    b, n_seq, seq_len, wq = q.shape
    wk = nkv * HEAD_DIM
    halo = half_window
    tq = min(TQ_BAND * BAND_TILES_PER_STEP, seq_len)
    per = tq // halo
    n_halo = seq_len // halo

    main = lambda width: pl.BlockSpec((1, 1, tq, width), lambda bb, r, i: (bb, r, i, 0))
    prev = lambda width: pl.BlockSpec(
        (1, 1, halo, width), lambda bb, r, i: (bb, r, jnp.maximum(i * per - 1, 0), 0))
    nxt = lambda width: pl.BlockSpec(
        (1, 1, halo, width), lambda bb, r, i: (bb, r, jnp.minimum((i + 1) * per, n_halo - 1), 0))
    in_specs = [main(wq), prev(wk), main(wk), nxt(wk), prev(wk), main(wk), nxt(wk)]
    args = [q, k, k, k, v, v, v]
    if sink is not None:
        in_specs.append(pl.BlockSpec(memory_space=pltpu.SMEM))
        args.append(sink)
    out_block = main(wq)
    out_shape = [jax.ShapeDtypeStruct(q.shape, BF16)]
    out_specs = [out_block]
    if want_lse:
        out_shape.append(jax.ShapeDtypeStruct(q.shape, F32))
        out_specs.append(out_block)
    res = pl.pallas_call(
        functools.partial(_band_kernel, nkv=nkv, group=group, half_window=half_window,
                          seq_len=seq_len, sink_layer=sink_layer if sink is not None else None,
                          want_lse=want_lse),
        grid=(b, n_seq, seq_len // tq),
        in_specs=in_specs,
        out_specs=out_specs,
        out_shape=out_shape,
        compiler_params=_cparams(("parallel", "parallel", "parallel")),
        name="band_attn",
    )(*args)
    return res if want_lse else res[0]


def _sigmoid(x):
    return 0.5 * jnp.tanh(0.5 * x) + 0.5


def _softplus(x):
    return jnp.maximum(x, 0.0) + jnp.log1p(jnp.exp(-jnp.abs(x)))


def _lru_kernel(af_ref, uf_ref, ab_ref, ub_ref, hf_ref, hb_ref, carry_ref):
    i = pl.program_id(1)
    tt = af_ref.shape[1]
    nblk = tt // SUBLANES

    @pl.when(i == 0)
    def _():
        carry_ref[...] = jnp.zeros(carry_ref.shape, F32)

    row = lax.broadcasted_iota(jnp.int32, (SUBLANES, LRU_WIDTH), 0)

    def scan_block(d, blk, carry, a_ref, u_ref, out_ref):
        a8 = a_ref[0, pl.ds(blk * SUBLANES, SUBLANES), :]
        u8 = u_ref[0, pl.ds(blk * SUBLANES, SUBLANES), :]
        entry = row == (0 if d == 0 else SUBLANES - 1)
        u8 = u8 + jnp.where(entry, a8 * carry, 0.0)
        for step in (1, 2, 4):
            if d == 0:
                shift, ident = step, row < step
            else:
                shift, ident = SUBLANES - step, row >= SUBLANES - step
            u8 = u8 + a8 * jnp.where(ident, 0.0, pltpu.roll(u8, shift, 0))
            if step < 4:
                a8 = a8 * pltpu.roll(a8, shift, 0)
        out_ref[0, pl.ds(blk * SUBLANES, SUBLANES), :] = u8
        last = u8[SUBLANES - 1:SUBLANES, :] if d == 0 else u8[0:1, :]
        return jnp.broadcast_to(last, (SUBLANES, LRU_WIDTH))

    def body(kb, carries):
        cf, cb = carries
        cf = scan_block(0, kb, cf, af_ref, uf_ref, hf_ref)
        cb = scan_block(1, nblk - 1 - kb, cb, ab_ref, ub_ref, hb_ref)
        return cf, cb

    cf, cb = lax.fori_loop(0, nblk, body, (carry_ref[0], carry_ref[1]))
    carry_ref[0] = cf
    carry_ref[1] = cb


def _lru(af, uf, ab, ub):
    b, s, w = af.shape
    tt = min(TT_LRU, s)
    nt = s // tt
    fwd = pl.BlockSpec((1, tt, w), lambda bb, i: (bb, i, 0))
    bwd = pl.BlockSpec((1, tt, w), lambda bb, i: (bb, nt - 1 - i, 0))
    return pl.pallas_call(
        _lru_kernel,
        grid=(b, nt),
        in_specs=[fwd, fwd, bwd, bwd],
        out_specs=[fwd, bwd],
        out_shape=[jax.ShapeDtypeStruct((b, s, w), F32), jax.ShapeDtypeStruct((b, s, w), F32)],
        scratch_shapes=[pltpu.VMEM((2, SUBLANES, w), F32)],
        compiler_params=_cparams(("parallel", "arbitrary")),
        name="lru",
    )(af, uf, ab, ub)


def _gelu_tanh(x):
    return 0.5 * x * (1.0 + jnp.tanh(math.sqrt(2.0 / math.pi) * (x + 0.044715 * (x * x * x))))


def _merge_kernel(x_ref, ya_ref, hf_ref, hb_ref, by_ref, yc_ref,
                  o0_ref, o1_ref, o2_ref, l0_ref, l1_ref, l2_ref, g_ref, wg_ref, gbias_ref,
                  wa_ref, wb_ref, wc_ref, wd_ref, wo_ref, out_ref, perm_scr):
    d_model = x_ref.shape[2]
    tm = x_ref.shape[1]
    x = x_ref[0]
    var = jnp.mean(x * x, axis=-1, keepdims=True)
    hn = (x * lax.rsqrt(var + EPS) * g_ref[...]).astype(BF16)

    def natural(ref, slot):
        dil = ref.shape[1]
        if dil == 1:
            return ref[0, 0].astype(F32)
        for r in range(dil):
            rows = ref[0, r].astype(F32)
            for j in range(rows.shape[1] // LANES):
                perm_scr[slot, j, pl.ds(r, tm // dil, stride=dil), :] = rows[:, j * LANES:(j + 1) * LANES]
        return jnp.concatenate([perm_scr[slot, j] for j in range(perm_scr.shape[1])], axis=1)

    yb = ((hf_ref[0] + hb_ref[0]) * _gelu_tanh(by_ref[0].astype(F32))).astype(BF16)

    l0, l1, l2 = natural(l0_ref, 0), natural(l1_ref, 0), natural(l2_ref, 1)
    o0, o1, o2 = natural(o0_ref, 2), natural(o1_ref, 2), natural(o2_ref, 3)
    mx = jnp.maximum(jnp.maximum(l0, l1), l2)
    e0, e1, e2 = jnp.exp2(l0 - mx), jnp.exp2(l1 - mx), jnp.exp2(l2 - mx)
    yd = ((e0 * o0 + e1 * o1 + e2 * o2) / (e0 + e1 + e2)).astype(BF16)

    def gate_logits(kk):
        off = wg_ref.shape[1] - N_BRANCH * d_model + kk * d_model
        return lambda: jnp.dot(hn, wg_ref[:, off:off + d_model], preferred_element_type=F32)

    def branch(y, w_ref):
        return lambda: jnp.dot(y, w_ref[...], preferred_element_type=F32)

    dots = []
    for kk, (y, w_ref) in enumerate(((ya_ref[0], wa_ref), (yb, wb_ref), (yc_ref[0], wc_ref), (yd, wd_ref))):
        dots += [gate_logits(kk), branch(y, w_ref)]
    pending = [dots[0](), dots[1]()]
    merged = None
    for kk in range(N_BRANCH):
        pending += [d() for d in dots[2 * kk + 2:2 * kk + 4]]
        logits, proj = pending.pop(0), pending.pop(0)
        term = _sigmoid(logits + gbias_ref[kk:kk + 1, :]) * proj
        merged = term if merged is None else merged + term

    out_ref[0] = x + jnp.dot(merged.astype(BF16), wo_ref[...], preferred_element_type=F32)


def _merge(x, layer, ya, hf, hb, by, yc, o_d, lse_d, g, w_all, gbias, wa, wb, wc, wd, wo):
    b, s, d = x.shape
    tm = min(TM_MERGE, s)
    tok = lambda width: pl.BlockSpec((1, tm, width), lambda bb, i: (bb, i, 0))
    dw = D_HEADS * HEAD_DIM
    res = [pl.BlockSpec((1, dil, tm // dil, dw), lambda bb, i: (bb, 0, i, 0)) for _, dil in D_PATTERNS]
    return pl.pallas_call(
        _merge_kernel,
        grid=(b, s // tm),
        in_specs=[tok(d), tok(512), tok(LRU_WIDTH), tok(LRU_WIDTH), tok(LRU_WIDTH), tok(512),
                  *res, *res, _layer(g, layer), _layer(w_all, layer), _layer(gbias, layer),
                  _layer(wa, layer), _layer(wb, layer), _layer(wc, layer), _layer(wd, layer), _layer(wo, layer)],
        out_specs=tok(d),
        out_shape=jax.ShapeDtypeStruct((b, s, d), F32),
        scratch_shapes=[pltpu.VMEM((4, dw // LANES, tm, LANES), F32)],
        compiler_params=_cparams(("parallel", "parallel")),
        name="merge",
    )(x, ya, hf, hb, by, yc, *o_d, *lse_d, g, w_all, gbias, wa, wb, wc, wd, wo)


def _mlp_kernel(x_ref, g_ref, w1_ref, w2_ref, gf_ref, out_ref, *, final_norm):
    x = x_ref[0]
    var = jnp.mean(x * x, axis=-1, keepdims=True)
    hn = (x * lax.rsqrt(var + EPS) * g_ref[...]).astype(BF16)
    hidden = w1_ref.shape[1]
    acc = x
    for c in range(hidden // MLP_CHUNK):
        h = jnp.dot(hn, w1_ref[:, c * MLP_CHUNK:(c + 1) * MLP_CHUNK], preferred_element_type=F32)
        h = jnp.square(jnp.maximum(h, 0.0)).astype(BF16)
        acc = acc + jnp.dot(h, w2_ref[c * MLP_CHUNK:(c + 1) * MLP_CHUNK, :], preferred_element_type=F32)
    if final_norm:
        var = jnp.mean(acc * acc, axis=-1, keepdims=True)
        acc = acc * lax.rsqrt(var + EPS) * gf_ref[...]
    out_ref[0] = acc


def _mlp(x, layer, g, w1, w2, g_final, final_norm):
    b, s, d = x.shape
    tm = min(TM_MLP, s)
    tok = pl.BlockSpec((1, tm, d), lambda bb, i: (bb, i, 0))
    return pl.pallas_call(
        functools.partial(_mlp_kernel, final_norm=final_norm),
        grid=(b, s // tm),
        in_specs=[tok, _layer(g, layer), _layer(w1, layer), _layer(w2, layer), _resident(g_final.shape)],
        out_specs=tok,
        out_shape=jax.ShapeDtypeStruct((b, s, d), F32),
        compiler_params=_cparams(("parallel", "parallel")),
        name="mlp",
    )(x, g, w1, w2, g_final)


def _dense_block_diag(w):
    nb, bw = w.shape[-3], w.shape[-2]
    eye = jnp.eye(nb, dtype=w.dtype)
    return jnp.einsum('...ncd,nm->...ncmd', w, eye).reshape(w.shape[:-3] + (nb * bw, nb * bw))


def kernel(x, norm_mix_g, w_in, gate_bias, qk_norm_g, conv_w, conv_b, lru_gate_w, lru_gate_b, lru_lambda, sink_logit, w_proj_a, w_proj_b, w_proj_c, w_proj_d, w_out, norm_mlp_g, w_mlp1, w_mlp2, norm_final_g):
    b, s, d = x.shape
    depth = w_in.shape[0]
    ax_t, seq_t = _rope_tables(s)
    dw = D_HEADS * HEAD_DIM
    n_groups = len(D_PATTERNS)
    bf = lambda w: w.astype(BF16)
    w_in_bf, wa, wb, wc, wd, wo, w1, w2 = map(bf, (w_in, w_proj_a, w_proj_b, w_proj_c, w_proj_d, w_out, w_mlp1, w_mlp2))
    g_mix, g_mlp, conv_b3 = norm_mix_g[:, None, :], norm_mlp_g[:, None, :], conv_b[:, None, :]
    qk_gain = jnp.tile(qk_norm_g, (1, 1, LANES // HEAD_DIM))
    gate_w_dense = bf(_dense_block_diag(lru_gate_w))

    for l in range(depth):
        outs = _inproj(x, l, g_mix, w_in_bf, qk_gain, ax_t, seq_t, conv_w, conv_b3, gate_w_dense,
                       lru_gate_b, lru_lambda)
        qta, ka, vta, af, uf, ab, ub, by, qc, kc, vc = outs[:11]
        qd, kd, vd = outs[11:11 + n_groups], outs[11 + n_groups:11 + 2 * n_groups], outs[11 + 2 * n_groups:11 + 3 * n_groups]

        ya = _attn_a(qta, ka, vta)

        hf, hb = _lru(af, uf, ab, ub)

        yc = _band_attention(qc[:, None], kc[:, None], vc[:, None], nkv=C_KV_HEADS, group=C_HEADS // C_KV_HEADS,
                             half_window=C_HALF_WINDOW, sink=sink_logit, sink_layer=l)[:, 0]

        o_d, lse_d = [], []
        for gi, (window, dil) in enumerate(D_PATTERNS):
            o, lse = _band_attention(qd[gi], kd[gi], vd[gi], nkv=D_HEADS, group=1,
                                     half_window=window // (2 * dil), want_lse=True)
            o_d.append(o)
            lse_d.append(lse)

        x = _merge(x, l, ya, hf, hb, by, yc, o_d, lse_d, g_mix, w_in_bf, gate_bias, wa, wb, wc, wd, wo)
        x = _mlp(x, l, g_mlp, w1, w2, norm_final_g[None, :], final_norm=(l == depth - 1))
    return x
```

```python
import functools
import math

import jax
import jax.numpy as jnp
from jax import lax
from jax.experimental import pallas as pl
from jax.experimental.pallas import tpu as pltpu

F32 = jnp.float32
BF16 = jnp.bfloat16

HEAD_DIM = 64
ROPE_THETA = 10000.0
GRID_W = 64
EPS = 1e-6
MASK_VALUE = -1e30
LOG2E = math.log2(math.e)

A_HEADS, A_KV_HEADS = 8, 2
LRU_WIDTH, LRU_BLOCKS, LRU_C = 512, 8, 8.0
CONV_W, CONV_LEFT = 4, 2
C_HEADS, C_KV_HEADS, C_HALF_WINDOW = 8, 2, 128
D_PATTERNS = ((128, 1), (512, 4), (2048, 16))
D_HEADS = 4
N_BRANCH = 4

LANES = 128
SUBLANES = 8
VMEM_LIMIT_BYTES = 56 * 1024 * 1024

TM_INPROJ = 512
TQ_ATTN_A = 256
SUB_ATTN_A = 128
TILES_PER_STEP_ATTN_A = 2
LOOKAHEAD_ATTN_A = 14
LOOKAHEAD_BAND = 6
LOOKAHEAD_INPROJ = 6
SUB_BAND = 128
CHUNKS_PER_BODY_ATTN_A = 16
VT_ROWS_A = HEAD_DIM + 16
TQ_BAND = 256
BAND_TILES_PER_STEP = 4
TT_LRU = 1024
TM_MERGE = 512
TM_MLP = 1024
MLP_CHUNK = 1024

Q_SCALE = HEAD_DIM ** -0.5 * LOG2E


def _cparams(sem):
    return pltpu.CompilerParams(dimension_semantics=sem, vmem_limit_bytes=VMEM_LIMIT_BYTES)


def _layer(arr, layer):
    zeros = (0,) * (arr.ndim - 1)
    return pl.BlockSpec((None,) + arr.shape[1:], lambda *_: (layer,) + zeros, pipeline_mode=pl.Buffered(1))


def _resident(shape):
    nd = len(shape)
    return pl.BlockSpec(shape, lambda *_: (0,) * nd, pipeline_mode=pl.Buffered(1))


def _rope_tables(seq):
    pos = jnp.arange(seq, dtype=F32)
    lane = jnp.arange(LANES)
    d = lane % HEAD_DIM

    inv = ROPE_THETA ** (-jnp.arange(0, HEAD_DIM, 2, dtype=F32) / HEAD_DIM)
    ang = pos[:, None] * inv[None, :]
    f = d % (HEAD_DIM // 2)
    cos_s, sin_s = jnp.cos(ang)[:, f], jnp.sin(ang)[:, f]
    lo = (d < HEAD_DIM // 2)[None, :]
    seq_t = (cos_s, jnp.where(lo, -sin_s, 0.0), jnp.where(lo, 0.0, sin_s))

    quarter = HEAD_DIM // 4
    inv_ax = ROPE_THETA ** (-jnp.arange(0, HEAD_DIM // 2, 2, dtype=F32) / (HEAD_DIM // 2))
    n_rows = seq // GRID_W
    ang_r = jnp.arange(n_rows, dtype=F32)[:, None] * inv_ax[None, :]
    ang_c = jnp.arange(GRID_W, dtype=F32)[:, None] * inv_ax[None, :]

    def expand(by_row, by_col):
        r = jnp.repeat(by_row, GRID_W, axis=0)
        c = jnp.tile(by_col, (n_rows, 1))
        return jnp.tile(jnp.concatenate([r, r, c, c], axis=1), (1, LANES // HEAD_DIM))

    cos_a, sin_a = expand(jnp.cos(ang_r), jnp.cos(ang_c)), expand(jnp.sin(ang_r), jnp.sin(ang_c))
    lo_a = (d % (HEAD_DIM // 2) < quarter)[None, :]
    ax_t = (cos_a, jnp.where(lo_a, -sin_a, 0.0), jnp.where(lo_a, 0.0, sin_a))
    return ax_t, seq_t


def _rotate(x, tables, half):
    cos, sin_a, sin_b = tables
    return x * cos + pltpu.roll(x, LANES - half, 1) * sin_a + pltpu.roll(x, half, 1) * sin_b


def _split_dot(x, m):
    hi = x.astype(BF16)
    lo = (x - hi.astype(F32)).astype(BF16)
    return jnp.dot(jnp.concatenate([hi, lo], axis=1), jnp.concatenate([m, m], axis=0),
                   preferred_element_type=F32)


def _head_mean_matrix():
    r = lax.broadcasted_iota(jnp.int32, (LANES, LANES), 0) // HEAD_DIM
    c = lax.broadcasted_iota(jnp.int32, (LANES, LANES), 1) // HEAD_DIM
    return jnp.where(r == c, 1.0 / HEAD_DIM, 0.0).astype(BF16)


def _head_rmsnorm(x, gain, mean_mat):
    var = _split_dot(x * x, mean_mat)
    return x * lax.rsqrt(var + EPS) * gain


IN_WIDTHS = (512, 128, 128, 512, 512, 512, 128, 128, 768, 768, 768)
IN_OFFSETS = tuple(sum(IN_WIDTHS[:i]) for i in range(len(IN_WIDTHS)))
N_IN = sum(IN_WIDTHS)


def _inproj_kernel(x_ref, xprev_ref, xnext_ref, g_ref, w_ref, qkg_ref,
                   ca_ref, saa_ref, sab_ref, cs_ref, ssa_ref, ssb_ref,
                   cw_ref, cb_ref, gw_ref, gb_ref, lam_ref,
                   qta_ref, ka_ref, vta_ref, af_ref, uf_ref, ab_ref, ub_ref, by_ref,
                   qc_ref, kc_ref, vc_ref, *rest):
    d_refs, perm_scr, xpad_scr = rest[:9], rest[9], rest[10]
    tile = pl.program_id(0)
    n_tiles = pl.num_programs(0)

    def normed(rows):
        var = jnp.mean(rows * rows, axis=-1, keepdims=True)
        return (rows * lax.rsqrt(var + EPS) * g_ref[...]).astype(BF16)

    x = x_ref[0]
    hn = normed(x)

    def proj(seg, lo=0, width=None):
        off = IN_OFFSETS[seg] + lo
        width = IN_WIDTHS[seg] if width is None else width
        return lambda: jnp.dot(hn, w_ref[:, off:off + width], preferred_element_type=F32)

    ax_t = (ca_ref[...], saa_ref[...], sab_ref[...])
    seq_t = (cs_ref[...], ssa_ref[...], ssb_ref[...])
    mean_mat = _head_mean_matrix()
    gq = qkg_ref[0:1, :]
    gk = qkg_ref[1:2, :]
    tm = x.shape[0]
    dw = D_HEADS * HEAD_DIM

    def finish_aq(acc):
        for j in range(IN_WIDTHS[0] // LANES):
            piece = _head_rmsnorm(acc[:, j * LANES:(j + 1) * LANES], gq, mean_mat)
            piece = _rotate(piece, ax_t, HEAD_DIM // 4) * Q_SCALE
            qta_ref[0, j * LANES:(j + 1) * LANES, :] = piece.T.astype(BF16)

    def finish_akv(acc):
        ak = _rotate(_head_rmsnorm(acc[:, :LANES], gk, mean_mat), ax_t, HEAD_DIM // 4).astype(BF16)
        avt = acc[:, LANES:].T.astype(BF16)
        pad_row = lax.broadcasted_iota(jnp.int32, (VT_ROWS_A - HEAD_DIM, tm), 0)
        ones_rows = (pad_row == 0).astype(BF16)
        for g in range(A_KV_HEADS):
            ka_ref[0, g] = ak[:, g * HEAD_DIM:(g + 1) * HEAD_DIM]
            vta_ref[0, g, 0] = jnp.concatenate([avt[g * HEAD_DIM:(g + 1) * HEAD_DIM, :], ones_rows], axis=0)

    lru = {}

    def finish_bx(acc):
        halo = jnp.dot(normed(jnp.concatenate([xprev_ref[0], xnext_ref[0]], axis=0)),
                       w_ref[:, IN_OFFSETS[3]:IN_OFFSETS[3] + LRU_WIDTH], preferred_element_type=F32)
        xpad_scr[0:SUBLANES] = jnp.where(tile > 0, halo[:SUBLANES], 0.0)
        xpad_scr[SUBLANES:SUBLANES + tm] = acc
        xpad_scr[SUBLANES + tm:] = jnp.where(tile < n_tiles - 1, halo[SUBLANES:], 0.0)
        xc = cb_ref[...]
        base = SUBLANES - CONV_LEFT
        for jj in range(CONV_W):
            xc = xc + xpad_scr[base + jj:base + jj + tm, :] * cw_ref[jj:jj + 1, :]
        lru["xc"] = xc
        lru["xcb"] = xc.astype(BF16)

    def gate_dots():
        return [jnp.dot(lru["xcb"], gw_ref[d, g], preferred_element_type=F32) for d in range(2) for g in range(2)]

    def finish_gates(dots):
        xc = lru["xc"]
        for d, (a_ref, u_ref) in enumerate(((af_ref, uf_ref), (ab_ref, ub_ref))):
            c2 = (-0.5 * LRU_C * LOG2E) * _softplus(-lam_ref[d:d + 1, :])
            a = jnp.exp2(c2 * jnp.tanh(0.5 * dots[2 * d] + 0.5 * gb_ref[d, 0:1, :]) + c2)
            gi = _sigmoid(dots[2 * d + 1] + gb_ref[d, 1:2, :])
            a_ref[0] = a
            z = (1.0 - a) * (1.0 + a)
            root = jnp.where(z > 0.0, z * lax.rsqrt(z), 0.0)
            u_ref[0] = root * (gi * xc)

    def finish_by(acc):
        by_ref[0] = acc.astype(BF16)

    def rope_piece(piece, scale):
        piece = _rotate(piece, seq_t, HEAD_DIM // 2)
        return piece if scale is None else piece * scale

    def finish_cq(acc):
        for j in range(IN_WIDTHS[5] // LANES):
            qc_ref[0, :, j * LANES:(j + 1) * LANES] = rope_piece(acc[:, j * LANES:(j + 1) * LANES], Q_SCALE).astype(BF16)

    def finish_ckv(acc):
        kc_ref[0] = rope_piece(acc[:, :LANES], None).astype(BF16)
        vc_ref[0] = acc[:, LANES:].astype(BF16)

    def finish_d(ref, dil, scale, rope):
        def finish(acc):
            for j in range(dw // LANES):
                piece = acc[:, j * LANES:(j + 1) * LANES]
                if rope:
                    piece = rope_piece(piece, scale)
                if dil == 1:
                    ref[0, 0, :, j * LANES:(j + 1) * LANES] = piece.astype(BF16)
                else:
                    perm_scr[j] = piece
            if dil > 1:
                for r in range(dil):
                    rows = [perm_scr[j, pl.ds(r, tm // dil, stride=dil), :] for j in range(dw // LANES)]
                    ref[0, r] = jnp.concatenate(rows, axis=1).astype(BF16)
        return finish

    segments = [(proj(0), finish_aq), (proj(1, 0, 2 * LANES), finish_akv), (proj(3), finish_bx),
                (proj(4), finish_by), (gate_dots, finish_gates), (proj(5), finish_cq),
                (proj(6, 0, 2 * LANES), finish_ckv)]
    for gi, (_, dil) in enumerate(D_PATTERNS):
        for ti, (seg, scale, rope) in enumerate(((8, Q_SCALE, True), (9, None, True), (10, None, False))):
            segments.append((proj(seg, gi * dw, dw), finish_d(d_refs[ti * len(D_PATTERNS) + gi], dil, scale, rope)))

    issued, pending = 0, []
    for idx, (_, finish) in enumerate(segments):
        horizon = min(len(segments), idx + 1 + LOOKAHEAD_INPROJ)
        while issued < horizon and (segments[issued][0] is not gate_dots or "xcb" in lru):
            pending.append(segments[issued][0]())
            issued += 1
        finish(pending.pop(0))


def _inproj(x, layer, g, w_all, qk_gain128, ax_t, seq_t, conv_w, conv_b, gate_w, gate_b, lam):
    b, s, d = x.shape
    tm = min(TM_INPROJ, s)
    nt = s // tm
    per, n8 = tm // SUBLANES, s // SUBLANES
    halo_prev = pl.BlockSpec((1, SUBLANES, d), lambda i, bb: (bb, jnp.maximum(i * per - 1, 0), 0))
    halo_next = pl.BlockSpec((1, SUBLANES, d), lambda i, bb: (bb, jnp.minimum((i + 1) * per, n8 - 1), 0))
    tok = lambda width: pl.BlockSpec((1, tm, width), lambda i, bb: (bb, i, 0))
    tab = pl.BlockSpec((tm, LANES), lambda i, bb: (i, 0))
    dw = D_HEADS * HEAD_DIM
    d_shapes = tuple(jax.ShapeDtypeStruct((b, dil, s // dil, dw), BF16) for _, dil in D_PATTERNS)
    d_specs = tuple(pl.BlockSpec((1, dil, tm // dil, dw), lambda i, bb: (bb, 0, i, 0)) for _, dil in D_PATTERNS)
    out_shape = (
        jax.ShapeDtypeStruct((b, A_HEADS * HEAD_DIM, s), BF16),
        jax.ShapeDtypeStruct((b, A_KV_HEADS, s, HEAD_DIM), BF16),
        jax.ShapeDtypeStruct((b, A_KV_HEADS, nt, VT_ROWS_A, tm), BF16),
        jax.ShapeDtypeStruct((b, s, LRU_WIDTH), F32),
        jax.ShapeDtypeStruct((b, s, LRU_WIDTH), F32),
        jax.ShapeDtypeStruct((b, s, LRU_WIDTH), F32),
        jax.ShapeDtypeStruct((b, s, LRU_WIDTH), F32),
        jax.ShapeDtypeStruct((b, s, LRU_WIDTH), BF16),
        jax.ShapeDtypeStruct((b, s, 512), BF16),
        jax.ShapeDtypeStruct((b, s, 128), BF16),
        jax.ShapeDtypeStruct((b, s, 128), BF16),
    ) + d_shapes * 3
    out_specs = (
        pl.BlockSpec((1, A_HEADS * HEAD_DIM, tm), lambda i, bb: (bb, 0, i)),
        pl.BlockSpec((1, A_KV_HEADS, tm, HEAD_DIM), lambda i, bb: (bb, 0, i, 0)),
        pl.BlockSpec((1, A_KV_HEADS, 1, VT_ROWS_A, tm), lambda i, bb: (bb, 0, i, 0, 0)),
        tok(LRU_WIDTH), tok(LRU_WIDTH), tok(LRU_WIDTH), tok(LRU_WIDTH), tok(LRU_WIDTH), tok(512), tok(128), tok(128),
    ) + d_specs * 3
    return pl.pallas_call(
        _inproj_kernel,
        grid=(nt, b),
        in_specs=[tok(d), halo_prev, halo_next, _layer(g, layer),
                  pl.BlockSpec((None, d, N_IN), lambda i, bb: (layer, 0, 0), pipeline_mode=pl.Buffered(1)),
                  _layer(qk_gain128, layer),
                  tab, tab, tab, tab, tab, tab,
                  _layer(conv_w, layer), _layer(conv_b, layer), _layer(gate_w, layer),
                  _layer(gate_b, layer), _layer(lam, layer)],
        out_specs=out_specs,
        out_shape=out_shape,
        scratch_shapes=[pltpu.VMEM((dw // LANES, tm, LANES), F32),
                        pltpu.VMEM((tm + 2 * SUBLANES, LRU_WIDTH), F32)],
        compiler_params=_cparams(("parallel", "parallel")),
        name="inproj",
    )(x, x, x, g, w_all, qk_gain128, *ax_t, *seq_t, conv_w, conv_b, gate_w, gate_b, lam)


def _sublane_allmax(x):
    for shift in (4, 2, 1):
        x = jnp.maximum(x, pltpu.roll(x, shift, 0))
    return x


def _attn_a_kernel(q_ref, k_ref, v_ref, o_ref, m_ref, acc_ref, *, group, sub):
    n_chunks, chunk = v_ref.shape[2], v_ref.shape[4]
    tq = min(TQ_ATTN_A, q_ref.shape[2])
    n_tiles = q_ref.shape[2] // tq
    m_ref[...] = jnp.full(m_ref.shape, MASK_VALUE, F32)
    acc_ref[...] = jnp.zeros(acc_ref.shape, F32)

    per_body = min(CHUNKS_PER_BODY_ATTN_A, n_chunks)
    n_sub = chunk // sub
    pair = 2 if n_sub % 2 == 0 else 1
    units = [(cc, u, h) for cc in range(per_body) for up in range(n_sub // pair)
             for h in range(n_tiles * group) for u in range(up * pair, (up + 1) * pair)]

    def body(it, carry):
        c0 = it * per_body

        def scores(idx):
            cc, u, h = units[idx]
            row0 = pl.multiple_of((c0 + cc) * chunk + u * sub, sub)
            k = k_ref[0, 0, pl.ds(row0, sub), :]
            j, head = divmod(h, group)
            qt = q_ref[0, head * HEAD_DIM:(head + 1) * HEAD_DIM, j * tq:(j + 1) * tq]
            return jnp.dot(k, qt, preferred_element_type=F32)

        pending = [scores(i) for i in range(LOOKAHEAD_ATTN_A)]
        probs = []
        for idx, (cc, u, h) in enumerate(units):
            if idx + LOOKAHEAD_ATTN_A < len(units):
                pending.append(scores(idx + LOOKAHEAD_ATTN_A))
            s3 = pending.pop(0).reshape(sub // SUBLANES, SUBLANES, tq)
            if not probs:
                m_start = m_ref[h]
                m_old = m_start
            m_new = jnp.maximum(m_old, _sublane_allmax(jnp.max(s3, axis=0)))
            if probs:
                beta = jnp.exp2(m_old - m_new).astype(BF16)
                beta = jnp.concatenate([beta, beta], axis=0)[None]
                probs = [(p.reshape(sub // 16, 16, tq) * beta).reshape(sub, tq) for p in probs]
            probs.append(jnp.exp2(s3 - m_new[None]).reshape(sub, tq).astype(BF16))
            m_old = m_new
            if len(probs) == pair:
                u0 = u - (pair - 1)
                vt = v_ref[0, 0, c0 + cc, :, u0 * sub:(u + 1) * sub]
                pv = jnp.dot(vt, jnp.concatenate(probs, axis=0), preferred_element_type=F32)
                alpha = jnp.exp2(m_start - m_new)
                acc_ref[h] = acc_ref[h] * alpha[None] + pv.reshape(VT_ROWS_A // SUBLANES, SUBLANES, tq)
                m_ref[h] = m_new
                probs = []
        return carry

    lax.fori_loop(0, n_chunks // per_body, body, 0)
    for j in range(n_tiles):
        outs = []
        for h in range(j * group, (j + 1) * group):
            acc = acc_ref[h].reshape(VT_ROWS_A, tq)
            outs.append(acc[:HEAD_DIM] / acc[HEAD_DIM:HEAD_DIM + 1])
        o_ref[0, j * tq:(j + 1) * tq, :] = jnp.concatenate(outs, axis=0).T.astype(o_ref.dtype)


def _attn_a(qt, k, vt):
    b, hd_all, s = qt.shape
    nkv, nt, chunk = vt.shape[1], vt.shape[2], vt.shape[4]
    group = hd_all // HEAD_DIM // nkv
    tq = min(TQ_ATTN_A * TILES_PER_STEP_ATTN_A, s)
    n_state = group * (tq // min(TQ_ATTN_A, s))
    gw = group * HEAD_DIM
    return pl.pallas_call(
        functools.partial(_attn_a_kernel, group=group, sub=min(SUB_ATTN_A, chunk)),
        grid=(b, nkv, s // tq),
        in_specs=[pl.BlockSpec((1, gw, tq), lambda bb, g, i: (bb, g, i)),
                  pl.BlockSpec((1, 1, s, HEAD_DIM), lambda bb, g, i: (bb, g, 0, 0)),
                  pl.BlockSpec((1, 1, nt, VT_ROWS_A, chunk), lambda bb, g, i: (bb, g, 0, 0, 0))],
        out_specs=pl.BlockSpec((1, tq, gw), lambda bb, g, i: (bb, i, g)),
        out_shape=jax.ShapeDtypeStruct((b, s, hd_all), BF16),
        scratch_shapes=[pltpu.VMEM((n_state, SUBLANES, min(TQ_ATTN_A, s)), F32),
                        pltpu.VMEM((n_state, VT_ROWS_A // SUBLANES, SUBLANES, min(TQ_ATTN_A, s)), F32)],
        compiler_params=_cparams(("parallel", "parallel", "parallel")),
        name="attn_a",
    )(qt, k, vt)


def _band_kernel(*refs, nkv, group, half_window, seq_len, sink_layer, want_lse):
    has_sink = sink_layer is not None
    q_ref, kp_ref, km_ref, kn_ref, vp_ref, vm_ref, vn_ref = refs[:7]
    rest = list(refs[7:])
    sink_ref = rest.pop(0) if has_sink else None
    o_ref = rest.pop(0)
    lse_ref = rest.pop(0) if want_lse else None

    tq_blk = q_ref.shape[2]
    tq = min(TQ_BAND, tq_blk)
    halo = kp_ref.shape[2]
    nk = tq + 2 * halo
    sub = SUB_BAND if nk % SUB_BAND == 0 else halo
    n_heads = nkv * group
    blk0 = pl.program_id(2) * tq_blk

    kcat = jnp.concatenate([kp_ref[0, 0], km_ref[0, 0], kn_ref[0, 0]], axis=0)
    vt = jnp.concatenate([vp_ref[0, 0], vm_ref[0, 0], vn_ref[0, 0]], axis=0).T

    kw = min(LANES, nkv * HEAD_DIM)
    zeros = jnp.zeros((HEAD_DIM, tq), kcat.dtype)
    q_pads = {}

    def padded_queries(j):
        if j not in q_pads:
            qt = q_ref[0, 0, j * tq:(j + 1) * tq, :].T
            pads = []
            for h in range(n_heads):
                slot = ((h // group) * HEAD_DIM % kw) // HEAD_DIM
                parts = [zeros] * (kw // HEAD_DIM)
                parts[slot] = qt[h * HEAD_DIM:(h + 1) * HEAD_DIM, :]
                pads.append(parts[0] if len(parts) == 1 else jnp.concatenate(parts, axis=0))
            q_pads.clear()
            q_pads[j] = pads
        return q_pads[j]

    units = [(j, kb, h) for j in range(tq_blk // tq) for kb in range(nk // sub) for h in range(n_heads)]

    cw = LANES if tq % LANES == 0 else tq
    n_ch = tq // cw

    def query_chunks(kb):
        k_lo = kb * sub - halo
        lo = max(0, (k_lo - half_window) // cw)
        hi = min(n_ch, -(-(k_lo + sub + half_window) // cw))
        return list(range(lo, hi))

    def scores(idx):
        j, kb, h = units[idx]
        row0 = j * tq + kb * sub
        lane0 = (h // group) * HEAD_DIM // kw * kw
        cs = query_chunks(kb)
        return jnp.dot(kcat[row0:row0 + sub, lane0:lane0 + kw],
                       padded_queries(j)[h][:, cs[0] * cw:(cs[-1] + 1) * cw], preferred_element_type=F32)

    key_row = lax.broadcasted_iota(jnp.int32, (sub, cw), 0)
    query = lax.broadcasted_iota(jnp.int32, (sub, cw), 1)

    def valid_mask(j, kb, c):
        kpos = blk0 + j * tq - halo + kb * sub + key_row
        rel = kpos - (blk0 + j * tq + c * cw + query)
        return (jnp.abs(rel) <= half_window) & (kpos >= 0) & (kpos < seq_len)

    def start_state():
        if has_sink:
            m0 = [jnp.full((SUBLANES, cw), sink_ref[sink_layer, h] * LOG2E, F32) for h in range(n_heads)]
            l0 = (lax.broadcasted_iota(jnp.int32, (SUBLANES, cw), 0) == 0).astype(F32)
        else:
            m0 = [jnp.full((SUBLANES, cw), 0.1 * MASK_VALUE, F32)] * n_heads
            l0 = jnp.zeros((SUBLANES, cw), F32)
        a0 = jnp.zeros((HEAD_DIM // SUBLANES, SUBLANES, cw), F32)
        return ([[m0[h]] * n_ch for h in range(n_heads)], [[l0] * n_ch for _ in range(n_heads)],
                [[a0] * n_ch for _ in range(n_heads)])

    def join(parts):
        return parts[0] if len(parts) == 1 else jnp.concatenate(parts, axis=1)

    def finish(j, m, l, acc):
        outs, lses = [], []
        for h in range(n_heads):
            l_row = join([jnp.sum(x, axis=0, keepdims=True) for x in l[h]])
            outs.append(join([x.reshape(HEAD_DIM, cw) for x in acc[h]]) / l_row)
            if want_lse:
                lse = join([x[0:1] for x in m[h]]) + jnp.log2(l_row)
                lses.append(jnp.broadcast_to(lse, (HEAD_DIM, tq)))
        o_ref[0, 0, j * tq:(j + 1) * tq, :] = jnp.concatenate(outs, axis=0).T.astype(o_ref.dtype)
        if want_lse:
            lse_ref[0, 0, j * tq:(j + 1) * tq, :] = jnp.concatenate(lses, axis=0).T

    look = min(LOOKAHEAD_BAND, len(units))
    pending = [scores(i) for i in range(look)]
    mask_key, masks = None, None
    for idx, (j, kb, h) in enumerate(units):
        if idx + look < len(units):
            pending.append(scores(idx + look))
        if kb == 0 and h == 0:
            m, l, acc = start_state()
        cs = query_chunks(kb)
        if mask_key != (j, kb):
            mask_key, masks = (j, kb), [valid_mask(j, kb, c) for c in cs]
        g = h // group
        s = pending.pop(0)
        probs, alphas = [], []
        for i, c in enumerate(cs):
            s_c = s if len(cs) == 1 else s[:, i * cw:(i + 1) * cw]
            s3 = jnp.where(masks[i], s_c, MASK_VALUE).reshape(sub // SUBLANES, SUBLANES, cw)
            m_new = jnp.maximum(m[h][c], _sublane_allmax(jnp.max(s3, axis=0)))
            alpha = jnp.exp2(m[h][c] - m_new)
            p3 = jnp.exp2(s3 - m_new[None])
            l[h][c] = alpha * l[h][c] + jnp.sum(p3, axis=0)
            m[h][c] = m_new
            probs.append(p3.reshape(sub, cw).astype(BF16))
            alphas.append(alpha)
        col0 = j * tq + kb * sub
        pv = jnp.dot(vt[g * HEAD_DIM:(g + 1) * HEAD_DIM, col0:col0 + sub], join(probs),
                     preferred_element_type=F32)
        for i, c in enumerate(cs):
            pv_c = pv if len(cs) == 1 else pv[:, i * cw:(i + 1) * cw]
            acc[h][c] = acc[h][c] * alphas[i][None] + pv_c.reshape(HEAD_DIM // SUBLANES, SUBLANES, cw)
        if kb == nk // sub - 1 and h == n_heads - 1:
            finish(j, m, l, acc)


def _band_attention(q, k, v, *, nkv, group, half_window, sink=None, sink_layer=None, want_lse=False):
    b, n_seq, seq_len, wq = q.shape
    wk = nkv * HEAD_DIM
    halo = half_window
    tq = min(TQ_BAND * BAND_TILES_PER_STEP, seq_len)
    per = tq // halo
    n_halo = seq_len // halo

    main = lambda width: pl.BlockSpec((1, 1, tq, width), lambda bb, r, i: (bb, r, i, 0))
    prev = lambda width: pl.BlockSpec(
        (1, 1, halo, width), lambda bb, r, i: (bb, r, jnp.maximum(i * per - 1, 0), 0))
    nxt = lambda width: pl.BlockSpec(
        (1, 1, halo, width), lambda bb, r, i: (bb, r, jnp.minimum((i + 1) * per, n_halo - 1), 0))
    in_specs = [main(wq), prev(wk), main(wk), nxt(wk), prev(wk), main(wk), nxt(wk)]
    args = [q, k, k, k, v, v, v]
    if sink is not None:
        in_specs.append(pl.BlockSpec(memory_space=pltpu.SMEM))
        args.append(sink)
    out_block = main(wq)
    out_shape = [jax.ShapeDtypeStruct(q.shape, BF16)]
    out_specs = [out_block]
    if want_lse:
        out_shape.append(jax.ShapeDtypeStruct(q.shape, F32))
        out_specs.append(out_block)
    res = pl.pallas_call(
        functools.partial(_band_kernel, nkv=nkv, group=group, half_window=half_window,
                          seq_len=seq_len, sink_layer=sink_layer if sink is not None else None,
                          want_lse=want_lse),
        grid=(b, n_seq, seq_len // tq),
        in_specs=in_specs,
        out_specs=out_specs,
        out_shape=out_shape,
        compiler_params=_cparams(("parallel", "parallel", "parallel")),
        name="band_attn",
    )(*args)
    return res if want_lse else res[0]


def _sigmoid(x):
    return 0.5 * jnp.tanh(0.5 * x) + 0.5


def _softplus(x):
    return jnp.maximum(x, 0.0) + jnp.log1p(jnp.exp(-jnp.abs(x)))


def _lru_kernel(af_ref, uf_ref, ab_ref, ub_ref, hf_ref, hb_ref, carry_ref):
    i = pl.program_id(1)
    tt = af_ref.shape[1]
    nblk = tt // SUBLANES

    @pl.when(i == 0)
    def _():
        carry_ref[...] = jnp.zeros(carry_ref.shape, F32)

    row = lax.broadcasted_iota(jnp.int32, (SUBLANES, LRU_WIDTH), 0)

    def scan_block(d, blk, carry, a_ref, u_ref, out_ref):
        a8 = a_ref[0, pl.ds(blk * SUBLANES, SUBLANES), :]
        u8 = u_ref[0, pl.ds(blk * SUBLANES, SUBLANES), :]
        entry = row == (0 if d == 0 else SUBLANES - 1)
        u8 = u8 + jnp.where(entry, a8 * carry, 0.0)
        for step in (1, 2, 4):
            if d == 0:
                shift, ident = step, row < step
            else:
                shift, ident = SUBLANES - step, row >= SUBLANES - step
            u8 = u8 + a8 * jnp.where(ident, 0.0, pltpu.roll(u8, shift, 0))
            if step < 4:
                a8 = a8 * pltpu.roll(a8, shift, 0)
        out_ref[0, pl.ds(blk * SUBLANES, SUBLANES), :] = u8
        last = u8[SUBLANES - 1:SUBLANES, :] if d == 0 else u8[0:1, :]
        return jnp.broadcast_to(last, (SUBLANES, LRU_WIDTH))

    def body(kb, carries):
        cf, cb = carries
        cf = scan_block(0, kb, cf, af_ref, uf_ref, hf_ref)
        cb = scan_block(1, nblk - 1 - kb, cb, ab_ref, ub_ref, hb_ref)
        return cf, cb

    cf, cb = lax.fori_loop(0, nblk, body, (carry_ref[0], carry_ref[1]))
    carry_ref[0] = cf
    carry_ref[1] = cb


def _lru(af, uf, ab, ub):
    b, s, w = af.shape
    tt = min(TT_LRU, s)
    nt = s // tt
    fwd = pl.BlockSpec((1, tt, w), lambda bb, i: (bb, i, 0))
    bwd = pl.BlockSpec((1, tt, w), lambda bb, i: (bb, nt - 1 - i, 0))
    return pl.pallas_call(
        _lru_kernel,
        grid=(b, nt),
        in_specs=[fwd, fwd, bwd, bwd],
        out_specs=[fwd, bwd],
        out_shape=[jax.ShapeDtypeStruct((b, s, w), F32), jax.ShapeDtypeStruct((b, s, w), F32)],
        scratch_shapes=[pltpu.VMEM((2, SUBLANES, w), F32)],
        compiler_params=_cparams(("parallel", "arbitrary")),
        name="lru",
    )(af, uf, ab, ub)


def _gelu_tanh(x):
    return 0.5 * x * (1.0 + jnp.tanh(math.sqrt(2.0 / math.pi) * (x + 0.044715 * (x * x * x))))


def _merge_kernel(x_ref, ya_ref, hf_ref, hb_ref, by_ref, yc_ref,
                  o0_ref, o1_ref, o2_ref, l0_ref, l1_ref, l2_ref, g_ref, wg_ref, gbias_ref,
                  wa_ref, wb_ref, wc_ref, wd_ref, wo_ref, out_ref, perm_scr):
    d_model = x_ref.shape[2]
    tm = x_ref.shape[1]
    x = x_ref[0]
    var = jnp.mean(x * x, axis=-1, keepdims=True)
    hn = (x * lax.rsqrt(var + EPS) * g_ref[...]).astype(BF16)

    def natural(ref, slot):
        dil = ref.shape[1]
        if dil == 1:
            return ref[0, 0].astype(F32)
        for r in range(dil):
            rows = ref[0, r].astype(F32)
            for j in range(rows.shape[1] // LANES):
                perm_scr[slot, j, pl.ds(r, tm // dil, stride=dil), :] = rows[:, j * LANES:(j + 1) * LANES]
        return jnp.concatenate([perm_scr[slot, j] for j in range(perm_scr.shape[1])], axis=1)

    yb = ((hf_ref[0] + hb_ref[0]) * _gelu_tanh(by_ref[0].astype(F32))).astype(BF16)

    l0, l1, l2 = natural(l0_ref, 0), natural(l1_ref, 0), natural(l2_ref, 1)
    o0, o1, o2 = natural(o0_ref, 2), natural(o1_ref, 2), natural(o2_ref, 3)
    mx = jnp.maximum(jnp.maximum(l0, l1), l2)
    e0, e1, e2 = jnp.exp2(l0 - mx), jnp.exp2(l1 - mx), jnp.exp2(l2 - mx)
    yd = ((e0 * o0 + e1 * o1 + e2 * o2) / (e0 + e1 + e2)).astype(BF16)

    def gate_logits(kk):
        off = wg_ref.shape[1] - N_BRANCH * d_model + kk * d_model
        return lambda: jnp.dot(hn, wg_ref[:, off:off + d_model], preferred_element_type=F32)

    def branch(y, w_ref):
        return lambda: jnp.dot(y, w_ref[...], preferred_element_type=F32)

    dots = []
    for kk, (y, w_ref) in enumerate(((ya_ref[0], wa_ref), (yb, wb_ref), (yc_ref[0], wc_ref), (yd, wd_ref))):
        dots += [gate_logits(kk), branch(y, w_ref)]
    pending = [dots[0](), dots[1]()]
    merged = None
    for kk in range(N_BRANCH):
        pending += [d() for d in dots[2 * kk + 2:2 * kk + 4]]
        logits, proj = pending.pop(0), pending.pop(0)
        term = _sigmoid(logits + gbias_ref[kk:kk + 1, :]) * proj
        merged = term if merged is None else merged + term

    out_ref[0] = x + jnp.dot(merged.astype(BF16), wo_ref[...], preferred_element_type=F32)


def _merge(x, layer, ya, hf, hb, by, yc, o_d, lse_d, g, w_all, gbias, wa, wb, wc, wd, wo):
    b, s, d = x.shape
    tm = min(TM_MERGE, s)
    tok = lambda width: pl.BlockSpec((1, tm, width), lambda bb, i: (bb, i, 0))
    dw = D_HEADS * HEAD_DIM
    res = [pl.BlockSpec((1, dil, tm // dil, dw), lambda bb, i: (bb, 0, i, 0)) for _, dil in D_PATTERNS]
    return pl.pallas_call(
        _merge_kernel,
        grid=(b, s // tm),
        in_specs=[tok(d), tok(512), tok(LRU_WIDTH), tok(LRU_WIDTH), tok(LRU_WIDTH), tok(512),
                  *res, *res, _layer(g, layer), _layer(w_all, layer), _layer(gbias, layer),
                  _layer(wa, layer), _layer(wb, layer), _layer(wc, layer), _layer(wd, layer), _layer(wo, layer)],
        out_specs=tok(d),
        out_shape=jax.ShapeDtypeStruct((b, s, d), F32),
        scratch_shapes=[pltpu.VMEM((4, dw // LANES, tm, LANES), F32)],
        compiler_params=_cparams(("parallel", "parallel")),
        name="merge",
    )(x, ya, hf, hb, by, yc, *o_d, *lse_d, g, w_all, gbias, wa, wb, wc, wd, wo)


def _mlp_kernel(x_ref, g_ref, w1_ref, w2_ref, gf_ref, out_ref, *, final_norm):
    x = x_ref[0]
    var = jnp.mean(x * x, axis=-1, keepdims=True)
    hn = (x * lax.rsqrt(var + EPS) * g_ref[...]).astype(BF16)
    hidden = w1_ref.shape[1]
    acc = x
    for c in range(hidden // MLP_CHUNK):
        h = jnp.dot(hn, w1_ref[:, c * MLP_CHUNK:(c + 1) * MLP_CHUNK], preferred_element_type=F32)
        h = jnp.square(jnp.maximum(h, 0.0)).astype(BF16)
        acc = acc + jnp.dot(h, w2_ref[c * MLP_CHUNK:(c + 1) * MLP_CHUNK, :], preferred_element_type=F32)
    if final_norm:
        var = jnp.mean(acc * acc, axis=-1, keepdims=True)
        acc = acc * lax.rsqrt(var + EPS) * gf_ref[...]
    out_ref[0] = acc


def _mlp(x, layer, g, w1, w2, g_final, final_norm):
    b, s, d = x.shape
    tm = min(TM_MLP, s)
    tok = pl.BlockSpec((1, tm, d), lambda bb, i: (bb, i, 0))
    return pl.pallas_call(
        functools.partial(_mlp_kernel, final_norm=final_norm),
        grid=(b, s // tm),
        in_specs=[tok, _layer(g, layer), _layer(w1, layer), _layer(w2, layer), _resident(g_final.shape)],
        out_specs=tok,
        out_shape=jax.ShapeDtypeStruct((b, s, d), F32),
        compiler_params=_cparams(("parallel", "parallel")),
        name="mlp",
    )(x, g, w1, w2, g_final)


def _dense_block_diag(w):
    nb, bw = w.shape[-3], w.shape[-2]
    eye = jnp.eye(nb, dtype=w.dtype)
    return jnp.einsum('...ncd,nm->...ncmd', w, eye).reshape(w.shape[:-3] + (nb * bw, nb * bw))


def kernel(x, norm_mix_g, w_in, gate_bias, qk_norm_g, conv_w, conv_b, lru_gate_w, lru_gate_b, lru_lambda, sink_logit, w_proj_a, w_proj_b, w_proj_c, w_proj_d, w_out, norm_mlp_g, w_mlp1, w_mlp2, norm_final_g):
    b, s, d = x.shape
    depth = w_in.shape[0]
    ax_t, seq_t = _rope_tables(s)
    dw = D_HEADS * HEAD_DIM
    n_groups = len(D_PATTERNS)
    bf = lambda w: w.astype(BF16)
    w_in_bf, wa, wb, wc, wd, wo, w1, w2 = map(bf, (w_in, w_proj_a, w_proj_b, w_proj_c, w_proj_d, w_out, w_mlp1, w_mlp2))
    g_mix, g_mlp, conv_b3 = norm_mix_g[:, None, :], norm_mlp_g[:, None, :], conv_b[:, None, :]
    qk_gain = jnp.tile(qk_norm_g, (1, 1, LANES // HEAD_DIM))
    gate_w_dense = bf(_dense_block_diag(lru_gate_w))

    for l in range(depth):
        outs = _inproj(x, l, g_mix, w_in_bf, qk_gain, ax_t, seq_t, conv_w, conv_b3, gate_w_dense,
                       lru_gate_b, lru_lambda)
        qta, ka, vta, af, uf, ab, ub, by, qc, kc, vc = outs[:11]
        qd, kd, vd = outs[11:11 + n_groups], outs[11 + n_groups:11 + 2 * n_groups], outs[11 + 2 * n_groups:11 + 3 * n_groups]

        ya = _attn_a(qta, ka, vta)

        hf, hb = _lru(af, uf, ab, ub)

        yc = _band_attention(qc[:, None], kc[:, None], vc[:, None], nkv=C_KV_HEADS, group=C_HEADS // C_KV_HEADS,
                             half_window=C_HALF_WINDOW, sink=sink_logit, sink_layer=l)[:, 0]

        o_d, lse_d = [], []
        for gi, (window, dil) in enumerate(D_PATTERNS):
            o, lse = _band_attention(qd[gi], kd[gi], vd[gi], nkv=D_HEADS, group=1,
                                     half_window=window // (2 * dil), want_lse=True)
            o_d.append(o)
            lse_d.append(lse)

        x = _merge(x, l, ya, hf, hb, by, yc, o_d, lse_d, g_mix, w_in_bf, gate_bias, wa, wb, wc, wd, wo)
        x = _mlp(x, l, g_mlp, w1, w2, norm_final_g[None, :], final_norm=(l == depth - 1))
    return x
```

```python
import functools
import math

import jax
import jax.numpy as jnp
from jax import lax
from jax.experimental import pallas as pl
from jax.experimental.pallas import tpu as pltpu

F32 = jnp.float32
BF16 = jnp.bfloat16

HEAD_DIM = 64
ROPE_THETA = 10000.0
GRID_W = 64
EPS = 1e-6
MASK_VALUE = -1e30
LOG2E = math.log2(math.e)

A_HEADS, A_KV_HEADS = 8, 2
LRU_WIDTH, LRU_BLOCKS, LRU_C = 512, 8, 8.0
CONV_W, CONV_LEFT = 4, 2
C_HEADS, C_KV_HEADS, C_HALF_WINDOW = 8, 2, 128
D_PATTERNS = ((128, 1), (512, 4), (2048, 16))
D_HEADS = 4
N_BRANCH = 4

LANES = 128
SUBLANES = 8
VMEM_LIMIT_BYTES = 56 * 1024 * 1024

TM_INPROJ = 512
TQ_ATTN_A = 256
SUB_ATTN_A = 128
TILES_PER_STEP_ATTN_A = 2
LOOKAHEAD_ATTN_A = 14
LOOKAHEAD_BAND = 6
LOOKAHEAD_INPROJ = 6
SUB_BAND = 128
CHUNKS_PER_BODY_ATTN_A = 16
VT_ROWS_A = HEAD_DIM + 16
TQ_BAND = 256
BAND_TILES_PER_STEP = 4
TT_LRU = 1024
TM_MERGE = 512
TM_MLP = 1024
MLP_CHUNK = 1024

Q_SCALE = HEAD_DIM ** -0.5 * LOG2E


def _cparams(sem):
    return pltpu.CompilerParams(dimension_semantics=sem, vmem_limit_bytes=VMEM_LIMIT_BYTES)


def _layer(arr, layer):
    zeros = (0,) * (arr.ndim - 1)
    return pl.BlockSpec((None,) + arr.shape[1:], lambda *_: (layer,) + zeros, pipeline_mode=pl.Buffered(1))


def _resident(shape):
    nd = len(shape)
    return pl.BlockSpec(shape, lambda *_: (0,) * nd, pipeline_mode=pl.Buffered(1))


def _rope_tables(seq):
    pos = jnp.arange(seq, dtype=F32)
    lane = jnp.arange(LANES)
    d = lane % HEAD_DIM

    inv = ROPE_THETA ** (-jnp.arange(0, HEAD_DIM, 2, dtype=F32) / HEAD_DIM)
    ang = pos[:, None] * inv[None, :]
    f = d % (HEAD_DIM // 2)
    cos_s, sin_s = jnp.cos(ang)[:, f], jnp.sin(ang)[:, f]
    lo = (d < HEAD_DIM // 2)[None, :]
    seq_t = (cos_s, jnp.where(lo, -sin_s, 0.0), jnp.where(lo, 0.0, sin_s))

    quarter = HEAD_DIM // 4
    inv_ax = ROPE_THETA ** (-jnp.arange(0, HEAD_DIM // 2, 2, dtype=F32) / (HEAD_DIM // 2))
    n_rows = seq // GRID_W
    ang_r = jnp.arange(n_rows, dtype=F32)[:, None] * inv_ax[None, :]
    ang_c = jnp.arange(GRID_W, dtype=F32)[:, None] * inv_ax[None, :]

    def expand(by_row, by_col):
        r = jnp.repeat(by_row, GRID_W, axis=0)
        c = jnp.tile(by_col, (n_rows, 1))
        return jnp.tile(jnp.concatenate([r, r, c, c], axis=1), (1, LANES // HEAD_DIM))

    cos_a, sin_a = expand(jnp.cos(ang_r), jnp.cos(ang_c)), expand(jnp.sin(ang_r), jnp.sin(ang_c))
    lo_a = (d % (HEAD_DIM // 2) < quarter)[None, :]
    ax_t = (cos_a, jnp.where(lo_a, -sin_a, 0.0), jnp.where(lo_a, 0.0, sin_a))
    return ax_t, seq_t


def _rotate(x, tables, half):
    cos, sin_a, sin_b = tables
    return x * cos + pltpu.roll(x, LANES - half, 1) * sin_a + pltpu.roll(x, half, 1) * sin_b


def _split_dot(x, m):
    hi = x.astype(BF16)
    lo = (x - hi.astype(F32)).astype(BF16)
    return jnp.dot(jnp.concatenate([hi, lo], axis=1), jnp.concatenate([m, m], axis=0),
                   preferred_element_type=F32)


def _head_mean_matrix():
    r = lax.broadcasted_iota(jnp.int32, (LANES, LANES), 0) // HEAD_DIM
    c = lax.broadcasted_iota(jnp.int32, (LANES, LANES), 1) // HEAD_DIM
    return jnp.where(r == c, 1.0 / HEAD_DIM, 0.0).astype(BF16)


def _head_rmsnorm(x, gain, mean_mat):
    var = _split_dot(x * x, mean_mat)
    return x * lax.rsqrt(var + EPS) * gain


IN_WIDTHS = (512, 128, 128, 512, 512, 512, 128, 128, 768, 768, 768)
IN_OFFSETS = tuple(sum(IN_WIDTHS[:i]) for i in range(len(IN_WIDTHS)))
N_IN = sum(IN_WIDTHS)


def _inproj_kernel(x_ref, xprev_ref, xnext_ref, g_ref, w_ref, qkg_ref,
                   ca_ref, saa_ref, sab_ref, cs_ref, ssa_ref, ssb_ref,
                   cw_ref, cb_ref, gw_ref, gb_ref, lam_ref,
                   qta_ref, ka_ref, vta_ref, af_ref, uf_ref, ab_ref, ub_ref, by_ref,
                   qc_ref, kc_ref, vc_ref, *rest):
    d_refs, perm_scr, xpad_scr = rest[:9], rest[9], rest[10]
    tile = pl.program_id(0)
    n_tiles = pl.num_programs(0)

    def normed(rows):
        var = jnp.mean(rows * rows, axis=-1, keepdims=True)
        return (rows * lax.rsqrt(var + EPS) * g_ref[...]).astype(BF16)

    x = x_ref[0]
    hn = normed(x)

    def proj(seg, lo=0, width=None):
        off = IN_OFFSETS[seg] + lo
        width = IN_WIDTHS[seg] if width is None else width
        return lambda: jnp.dot(hn, w_ref[:, off:off + width], preferred_element_type=F32)

    ax_t = (ca_ref[...], saa_ref[...], sab_ref[...])
    seq_t = (cs_ref[...], ssa_ref[...], ssb_ref[...])
    mean_mat = _head_mean_matrix()
    gq = qkg_ref[0:1, :]
    gk = qkg_ref[1:2, :]
    tm = x.shape[0]
    dw = D_HEADS * HEAD_DIM

    def finish_aq(acc):
        for j in range(IN_WIDTHS[0] // LANES):
            piece = _head_rmsnorm(acc[:, j * LANES:(j + 1) * LANES], gq, mean_mat)
            piece = _rotate(piece, ax_t, HEAD_DIM // 4) * Q_SCALE
            qta_ref[0, j * LANES:(j + 1) * LANES, :] = piece.T.astype(BF16)

    def finish_akv(acc):
        ak = _rotate(_head_rmsnorm(acc[:, :LANES], gk, mean_mat), ax_t, HEAD_DIM // 4).astype(BF16)
        avt = acc[:, LANES:].T.astype(BF16)
        pad_row = lax.broadcasted_iota(jnp.int32, (VT_ROWS_A - HEAD_DIM, tm), 0)
        ones_rows = (pad_row == 0).astype(BF16)
        for g in range(A_KV_HEADS):
            ka_ref[0, g] = ak[:, g * HEAD_DIM:(g + 1) * HEAD_DIM]
            vta_ref[0, g, 0] = jnp.concatenate([avt[g * HEAD_DIM:(g + 1) * HEAD_DIM, :], ones_rows], axis=0)

    lru = {}

    def finish_bx(acc):
        halo = jnp.dot(normed(jnp.concatenate([xprev_ref[0], xnext_ref[0]], axis=0)),
                       w_ref[:, IN_OFFSETS[3]:IN_OFFSETS[3] + LRU_WIDTH], preferred_element_type=F32)
        xpad_scr[0:SUBLANES] = jnp.where(tile > 0, halo[:SUBLANES], 0.0)
        xpad_scr[SUBLANES:SUBLANES + tm] = acc
        xpad_scr[SUBLANES + tm:] = jnp.where(tile < n_tiles - 1, halo[SUBLANES:], 0.0)
        xc = cb_ref[...]
        base = SUBLANES - CONV_LEFT
        for jj in range(CONV_W):
            xc = xc + xpad_scr[base + jj:base + jj + tm, :] * cw_ref[jj:jj + 1, :]
        lru["xc"] = xc
        lru["xcb"] = xc.astype(BF16)

    def gate_dots():
        return [jnp.dot(lru["xcb"], gw_ref[d, g], preferred_element_type=F32) for d in range(2) for g in range(2)]

    def finish_gates(dots):
        xc = lru["xc"]
        for d, (a_ref, u_ref) in enumerate(((af_ref, uf_ref), (ab_ref, ub_ref))):
            c2 = (-0.5 * LRU_C * LOG2E) * _softplus(-lam_ref[d:d + 1, :])
            a = jnp.exp2(c2 * jnp.tanh(0.5 * dots[2 * d] + 0.5 * gb_ref[d, 0:1, :]) + c2)
            gi = _sigmoid(dots[2 * d + 1] + gb_ref[d, 1:2, :])
            a_ref[0] = a
            z = (1.0 - a) * (1.0 + a)
            root = jnp.where(z > 0.0, z * lax.rsqrt(z), 0.0)
            u_ref[0] = root * (gi * xc)

    def finish_by(acc):
        by_ref[0] = acc.astype(BF16)

    def rope_piece(piece, scale):
        piece = _rotate(piece, seq_t, HEAD_DIM // 2)
        return piece if scale is None else piece * scale

    def finish_cq(acc):
        for j in range(IN_WIDTHS[5] // LANES):
            qc_ref[0, :, j * LANES:(j + 1) * LANES] = rope_piece(acc[:, j * LANES:(j + 1) * LANES], Q_SCALE).astype(BF16)

    def finish_ckv(acc):
        kc_ref[0] = rope_piece(acc[:, :LANES], None).astype(BF16)
        vc_ref[0] = acc[:, LANES:].astype(BF16)

    def finish_d(ref, dil, scale, rope):
        def finish(acc):
            for j in range(dw // LANES):
                piece = acc[:, j * LANES:(j + 1) * LANES]
                if rope:
                    piece = rope_piece(piece, scale)
                if dil == 1:
                    ref[0, 0, :, j * LANES:(j + 1) * LANES] = piece.astype(BF16)
                else:
                    perm_scr[j] = piece
            if dil > 1:
                for r in range(dil):
                    rows = [perm_scr[j, pl.ds(r, tm // dil, stride=dil), :] for j in range(dw // LANES)]
                    ref[0, r] = jnp.concatenate(rows, axis=1).astype(BF16)
        return finish

    segments = [(proj(0), finish_aq), (proj(1, 0, 2 * LANES), finish_akv), (proj(3), finish_bx),
                (proj(4), finish_by), (gate_dots, finish_gates), (proj(5), finish_cq),
                (proj(6, 0, 2 * LANES), finish_ckv)]
    for gi, (_, dil) in enumerate(D_PATTERNS):
        for ti, (seg, scale, rope) in enumerate(((8, Q_SCALE, True), (9, None, True), (10, None, False))):
            segments.append((proj(seg, gi * dw, dw), finish_d(d_refs[ti * len(D_PATTERNS) + gi], dil, scale, rope)))

    issued, pending = 0, []
    for idx, (_, finish) in enumerate(segments):
        horizon = min(len(segments), idx + 1 + LOOKAHEAD_INPROJ)
        while issued < horizon and (segments[issued][0] is not gate_dots or "xcb" in lru):
            pending.append(segments[issued][0]())
            issued += 1
        finish(pending.pop(0))


def _inproj(x, layer, g, w_all, qk_gain128, ax_t, seq_t, conv_w, conv_b, gate_w, gate_b, lam):
    b, s, d = x.shape
    tm = min(TM_INPROJ, s)
    nt = s // tm
    per, n8 = tm // SUBLANES, s // SUBLANES
    halo_prev = pl.BlockSpec((1, SUBLANES, d), lambda i, bb: (bb, jnp.maximum(i * per - 1, 0), 0))
    halo_next = pl.BlockSpec((1, SUBLANES, d), lambda i, bb: (bb, jnp.minimum((i + 1) * per, n8 - 1), 0))
    tok = lambda width: pl.BlockSpec((1, tm, width), lambda i, bb: (bb, i, 0))
    tab = pl.BlockSpec((tm, LANES), lambda i, bb: (i, 0))
    dw = D_HEADS * HEAD_DIM
    d_shapes = tuple(jax.ShapeDtypeStruct((b, dil, s // dil, dw), BF16) for _, dil in D_PATTERNS)
    d_specs = tuple(pl.BlockSpec((1, dil, tm // dil, dw), lambda i, bb: (bb, 0, i, 0)) for _, dil in D_PATTERNS)
    out_shape = (
        jax.ShapeDtypeStruct((b, A_HEADS * HEAD_DIM, s), BF16),
        jax.ShapeDtypeStruct((b, A_KV_HEADS, s, HEAD_DIM), BF16),
        jax.ShapeDtypeStruct((b, A_KV_HEADS, nt, VT_ROWS_A, tm), BF16),
        jax.ShapeDtypeStruct((b, s, LRU_WIDTH), F32),
        jax.ShapeDtypeStruct((b, s, LRU_WIDTH), F32),
        jax.ShapeDtypeStruct((b, s, LRU_WIDTH), F32),
        jax.ShapeDtypeStruct((b, s, LRU_WIDTH), F32),
        jax.ShapeDtypeStruct((b, s, LRU_WIDTH), BF16),
        jax.ShapeDtypeStruct((b, s, 512), BF16),
        jax.ShapeDtypeStruct((b, s, 128), BF16),
        jax.ShapeDtypeStruct((b, s, 128), BF16),
    ) + d_shapes * 3
    out_specs = (
        pl.BlockSpec((1, A_HEADS * HEAD_DIM, tm), lambda i, bb: (bb, 0, i)),
        pl.BlockSpec((1, A_KV_HEADS, tm, HEAD_DIM), lambda i, bb: (bb, 0, i, 0)),
        pl.BlockSpec((1, A_KV_HEADS, 1, VT_ROWS_A, tm), lambda i, bb: (bb, 0, i, 0, 0)),
        tok(LRU_WIDTH), tok(LRU_WIDTH), tok(LRU_WIDTH), tok(LRU_WIDTH), tok(LRU_WIDTH), tok(512), tok(128), tok(128),
    ) + d_specs * 3
    return pl.pallas_call(
        _inproj_kernel,
        grid=(nt, b),
        in_specs=[tok(d), halo_prev, halo_next, _layer(g, layer),
                  pl.BlockSpec((None, d, N_IN), lambda i, bb: (layer, 0, 0), pipeline_mode=pl.Buffered(1)),
                  _layer(qk_gain128, layer),
                  tab, tab, tab, tab, tab, tab,
                  _layer(conv_w, layer), _layer(conv_b, layer), _layer(gate_w, layer),
                  _layer(gate_b, layer), _layer(lam, layer)],
        out_specs=out_specs,
        out_shape=out_shape,
        scratch_shapes=[pltpu.VMEM((dw // LANES, tm, LANES), F32),
                        pltpu.VMEM((tm + 2 * SUBLANES, LRU_WIDTH), F32)],
        compiler_params=_cparams(("parallel", "parallel")),
        name="inproj",
    )(x, x, x, g, w_all, qk_gain128, *ax_t, *seq_t, conv_w, conv_b, gate_w, gate_b, lam)


def _sublane_allmax(x):
    for shift in (4, 2, 1):
        x = jnp.maximum(x, pltpu.roll(x, shift, 0))
    return x


def _attn_a_kernel(q_ref, k_ref, v_ref, o_ref, m_ref, acc_ref, *, group, sub):
    n_chunks, chunk = v_ref.shape[2], v_ref.shape[4]
    tq = min(TQ_ATTN_A, q_ref.shape[2])
    n_tiles = q_ref.shape[2] // tq
    m_ref[...] = jnp.full(m_ref.shape, MASK_VALUE, F32)
    acc_ref[...] = jnp.zeros(acc_ref.shape, F32)

    per_body = min(CHUNKS_PER_BODY_ATTN_A, n_chunks)
    n_sub = chunk // sub
    pair = 2 if n_sub % 2 == 0 else 1
    units = [(cc, u, h) for cc in range(per_body) for up in range(n_sub // pair)
             for h in range(n_tiles * group) for u in range(up * pair, (up + 1) * pair)]

    def body(it, carry):
        c0 = it * per_body

        def scores(idx):
            cc, u, h = units[idx]
            row0 = pl.multiple_of((c0 + cc) * chunk + u * sub, sub)
            k = k_ref[0, 0, pl.ds(row0, sub), :]
            j, head = divmod(h, group)
            qt = q_ref[0, head * HEAD_DIM:(head + 1) * HEAD_DIM, j * tq:(j + 1) * tq]
            return jnp.dot(k, qt, preferred_element_type=F32)

        pending = [scores(i) for i in range(LOOKAHEAD_ATTN_A)]
        probs = []
        for idx, (cc, u, h) in enumerate(units):
            if idx + LOOKAHEAD_ATTN_A < len(units):
                pending.append(scores(idx + LOOKAHEAD_ATTN_A))
            s3 = pending.pop(0).reshape(sub // SUBLANES, SUBLANES, tq)
            if not probs:
                m_start = m_ref[h]
                m_old = m_start
            m_new = jnp.maximum(m_old, _sublane_allmax(jnp.max(s3, axis=0)))
            if probs:
                beta = jnp.exp2(m_old - m_new).astype(BF16)
                beta = jnp.concatenate([beta, beta], axis=0)[None]
                probs = [(p.reshape(sub // 16, 16, tq) * beta).reshape(sub, tq) for p in probs]
            probs.append(jnp.exp2(s3 - m_new[None]).reshape(sub, tq).astype(BF16))
            m_old = m_new
            if len(probs) == pair:
                u0 = u - (pair - 1)
                vt = v_ref[0, 0, c0 + cc, :, u0 * sub:(u + 1) * sub]
                pv = jnp.dot(vt, jnp.concatenate(probs, axis=0), preferred_element_type=F32)
                alpha = jnp.exp2(m_start - m_new)
                acc_ref[h] = acc_ref[h] * alpha[None] + pv.reshape(VT_ROWS_A // SUBLANES, SUBLANES, tq)
                m_ref[h] = m_new
                probs = []
        return carry

    lax.fori_loop(0, n_chunks // per_body, body, 0)
    for j in range(n_tiles):
        outs = []
        for h in range(j * group, (j + 1) * group):
            acc = acc_ref[h].reshape(VT_ROWS_A, tq)
            outs.append(acc[:HEAD_DIM] / acc[HEAD_DIM:HEAD_DIM + 1])
        o_ref[0, j * tq:(j + 1) * tq, :] = jnp.concatenate(outs, axis=0).T.astype(o_ref.dtype)


def _attn_a(qt, k, vt):
    b, hd_all, s = qt.shape
    nkv, nt, chunk = vt.shape[1], vt.shape[2], vt.shape[4]
    group = hd_all // HEAD_DIM // nkv
    tq = min(TQ_ATTN_A * TILES_PER_STEP_ATTN_A, s)
    n_state = group * (tq // min(TQ_ATTN_A, s))
    gw = group * HEAD_DIM
    return pl.pallas_call(
        functools.partial(_attn_a_kernel, group=group, sub=min(SUB_ATTN_A, chunk)),
        grid=(b, nkv, s // tq),
        in_specs=[pl.BlockSpec((1, gw, tq), lambda bb, g, i: (bb, g, i)),
                  pl.BlockSpec((1, 1, s, HEAD_DIM), lambda bb, g, i: (bb, g, 0, 0)),
                  pl.BlockSpec((1, 1, nt, VT_ROWS_A, chunk), lambda bb, g, i: (bb, g, 0, 0, 0))],
        out_specs=pl.BlockSpec((1, tq, gw), lambda bb, g, i: (bb, i, g)),
        out_shape=jax.ShapeDtypeStruct((b, s, hd_all), BF16),
        scratch_shapes=[pltpu.VMEM((n_state, SUBLANES, min(TQ_ATTN_A, s)), F32),
                        pltpu.VMEM((n_state, VT_ROWS_A // SUBLANES, SUBLANES, min(TQ_ATTN_A, s)), F32)],
        compiler_params=_cparams(("parallel", "parallel", "parallel")),
        name="attn_a",
    )(qt, k, vt)


def _band_kernel(*refs, nkv, group, half_window, seq_len, sink_layer, want_lse):
    has_sink = sink_layer is not None
    q_ref, kp_ref, km_ref, kn_ref, vp_ref, vm_ref, vn_ref = refs[:7]
    rest = list(refs[7:])
    sink_ref = rest.pop(0) if has_sink else None
    o_ref = rest.pop(0)
    lse_ref = rest.pop(0) if want_lse else None

    tq_blk = q_ref.shape[2]
    tq = min(TQ_BAND, tq_blk)
    halo = kp_ref.shape[2]
    nk = tq + 2 * halo
    sub = SUB_BAND if nk % SUB_BAND == 0 else halo
    n_heads = nkv * group
    blk0 = pl.program_id(2) * tq_blk

    kcat = jnp.concatenate([kp_ref[0, 0], km_ref[0, 0], kn_ref[0, 0]], axis=0)
    vt = jnp.concatenate([vp_ref[0, 0], vm_ref[0, 0], vn_ref[0, 0]], axis=0).T
    pad_row = lax.broadcasted_iota(jnp.int32, (VT_ROWS_A - HEAD_DIM, vt.shape[1]), 0)
    ones_rows = (pad_row == 0).astype(vt.dtype)
    vt_ext = [jnp.concatenate([vt[g * HEAD_DIM:(g + 1) * HEAD_DIM], ones_rows], axis=0) for g in range(nkv)]

    kw = min(LANES, nkv * HEAD_DIM)
    zeros = jnp.zeros((HEAD_DIM, tq), kcat.dtype)
    q_pads = {}

    def padded_queries(j):
        if j not in q_pads:
            qt = q_ref[0, 0, j * tq:(j + 1) * tq, :].T
            pads = []
            for h in range(n_heads):
                slot = ((h // group) * HEAD_DIM % kw) // HEAD_DIM
                parts = [zeros] * (kw // HEAD_DIM)
                parts[slot] = qt[h * HEAD_DIM:(h + 1) * HEAD_DIM, :]
                pads.append(parts[0] if len(parts) == 1 else jnp.concatenate(parts, axis=0))
            q_pads.clear()
            q_pads[j] = pads
        return q_pads[j]

    units = [(j, kb, h) for j in range(tq_blk // tq) for kb in range(nk // sub) for h in range(n_heads)]

    cw = LANES if tq % LANES == 0 else tq
    n_ch = tq // cw

    def query_chunks(kb):
        k_lo = kb * sub - halo
        lo = max(0, (k_lo - half_window) // cw)
        hi = min(n_ch, -(-(k_lo + sub + half_window) // cw))
        return list(range(lo, hi))

    def scores(idx):
        j, kb, h = units[idx]
        row0 = j * tq + kb * sub
        lane0 = (h // group) * HEAD_DIM // kw * kw
        cs = query_chunks(kb)
        return jnp.dot(kcat[row0:row0 + sub, lane0:lane0 + kw],
                       padded_queries(j)[h][:, cs[0] * cw:(cs[-1] + 1) * cw], preferred_element_type=F32)

    key_row = lax.broadcasted_iota(jnp.int32, (sub, cw), 0)
    query = lax.broadcasted_iota(jnp.int32, (sub, cw), 1)

    def needs_mask(kb, c):
        k_lo = kb * sub - halo
        in_band = k_lo - (c * cw + cw - 1) >= -half_window and k_lo + sub - 1 - c * cw <= half_window
        return not (in_band and k_lo >= 0 and k_lo + sub <= tq)

    def valid_mask(j, kb, c):
        kpos = blk0 + j * tq - halo + kb * sub + key_row
        rel = kpos - (blk0 + j * tq + c * cw + query)
        return (jnp.abs(rel) <= half_window) & (kpos >= 0) & (kpos < seq_len)

    def start_state():
        shape = (VT_ROWS_A // SUBLANES, SUBLANES, cw)
        if has_sink:
            m0 = [jnp.full((SUBLANES, cw), sink_ref[sink_layer, h] * LOG2E, F32) for h in range(n_heads)]
            denom = ((lax.broadcasted_iota(jnp.int32, shape, 0) == HEAD_DIM // SUBLANES)
                     & (lax.broadcasted_iota(jnp.int32, shape, 1) == 0))
            a0 = denom.astype(F32)
        else:
            m0 = [jnp.full((SUBLANES, cw), 0.1 * MASK_VALUE, F32)] * n_heads
            a0 = jnp.zeros(shape, F32)
        return [[m0[h]] * n_ch for h in range(n_heads)], [[a0] * n_ch for _ in range(n_heads)]

    def join(parts):
        return parts[0] if len(parts) == 1 else jnp.concatenate(parts, axis=1)

    def finish(j, m, acc):
        outs, lses = [], []
        for h in range(n_heads):
            full = join([x.reshape(VT_ROWS_A, cw) for x in acc[h]])
            l_row = full[HEAD_DIM:HEAD_DIM + 1]
            outs.append(full[:HEAD_DIM] / l_row)
            if want_lse:
                lse = join([x[0:1] for x in m[h]]) + jnp.log2(l_row)
                lses.append(jnp.broadcast_to(lse, (HEAD_DIM, tq)))
        o_ref[0, 0, j * tq:(j + 1) * tq, :] = jnp.concatenate(outs, axis=0).T.astype(o_ref.dtype)
        if want_lse:
            lse_ref[0, 0, j * tq:(j + 1) * tq, :] = jnp.concatenate(lses, axis=0).T

    look = min(LOOKAHEAD_BAND, len(units))
    pending = [scores(i) for i in range(look)]
    mask_key, masks = None, None
    for idx, (j, kb, h) in enumerate(units):
        if idx + look < len(units):
            pending.append(scores(idx + look))
        if kb == 0 and h == 0:
            m, acc = start_state()
        cs = query_chunks(kb)
        if mask_key != (j, kb):
            mask_key = (j, kb)
            masks = [valid_mask(j, kb, c) if needs_mask(kb, c) else None for c in cs]
        g = h // group
        s = pending.pop(0)
        probs, alphas = [], []
        for i, c in enumerate(cs):
            s_c = s if len(cs) == 1 else s[:, i * cw:(i + 1) * cw]
            if masks[i] is not None:
                s_c = jnp.where(masks[i], s_c, MASK_VALUE)
            s3 = s_c.reshape(sub // SUBLANES, SUBLANES, cw)
            m_new = jnp.maximum(m[h][c], _sublane_allmax(jnp.max(s3, axis=0)))
            alpha = jnp.exp2(m[h][c] - m_new)
            p3 = jnp.exp2(s3 - m_new[None])
            m[h][c] = m_new
            probs.append(p3.reshape(sub, cw).astype(BF16))
            alphas.append(alpha)
        col0 = j * tq + kb * sub
        pv = jnp.dot(vt_ext[g][:, col0:col0 + sub], join(probs), preferred_element_type=F32)
        for i, c in enumerate(cs):
            pv_c = pv if len(cs) == 1 else pv[:, i * cw:(i + 1) * cw]
            acc[h][c] = acc[h][c] * alphas[i][None] + pv_c.reshape(VT_ROWS_A // SUBLANES, SUBLANES, cw)
        if kb == nk // sub - 1 and h == n_heads - 1:
            finish(j, m, acc)


def _band_attention(q, k, v, *, nkv, group, half_window, sink=None, sink_layer=None, want_lse=False):
    b, n_seq, seq_len, wq = q.shape
    wk = nkv * HEAD_DIM
    halo = half_window
    tq = min(TQ_BAND * BAND_TILES_PER_STEP, seq_len)
    per = tq // halo
    n_halo = seq_len // halo

    main = lambda width: pl.BlockSpec((1, 1, tq, width), lambda bb, r, i: (bb, r, i, 0))
    prev = lambda width: pl.BlockSpec(
        (1, 1, halo, width), lambda bb, r, i: (bb, r, jnp.maximum(i * per - 1, 0), 0))
    nxt = lambda width: pl.BlockSpec(
        (1, 1, halo, width), lambda bb, r, i: (bb, r, jnp.minimum((i + 1) * per, n_halo - 1), 0))
    in_specs = [main(wq), prev(wk), main(wk), nxt(wk), prev(wk), main(wk), nxt(wk)]
    args = [q, k, k, k, v, v, v]
    if sink is not None:
        in_specs.append(pl.BlockSpec(memory_space=pltpu.SMEM))
        args.append(sink)
    out_block = main(wq)
    out_shape = [jax.ShapeDtypeStruct(q.shape, BF16)]
    out_specs = [out_block]
    if want_lse:
        out_shape.append(jax.ShapeDtypeStruct(q.shape, F32))
        out_specs.append(out_block)
    res = pl.pallas_call(
        functools.partial(_band_kernel, nkv=nkv, group=group, half_window=half_window,
                          seq_len=seq_len, sink_layer=sink_layer if sink is not None else None,
                          want_lse=want_lse),
        grid=(b, n_seq, seq_len // tq),
        in_specs=in_specs,
        out_specs=out_specs,
        out_shape=out_shape,
        compiler_params=_cparams(("parallel", "parallel", "parallel")),
        name="band_attn",
    )(*args)
    return res if want_lse else res[0]


def _sigmoid(x):
    return 0.5 * jnp.tanh(0.5 * x) + 0.5


def _softplus(x):
    return jnp.maximum(x, 0.0) + jnp.log1p(jnp.exp(-jnp.abs(x)))


def _lru_kernel(af_ref, uf_ref, ab_ref, ub_ref, hf_ref, hb_ref, carry_ref):
    i = pl.program_id(1)
    tt = af_ref.shape[1]
    nblk = tt // SUBLANES

    @pl.when(i == 0)
    def _():
        carry_ref[...] = jnp.zeros(carry_ref.shape, F32)

    row = lax.broadcasted_iota(jnp.int32, (SUBLANES, LRU_WIDTH), 0)

    def scan_block(d, blk, carry, a_ref, u_ref, out_ref):
        a8 = a_ref[0, pl.ds(blk * SUBLANES, SUBLANES), :]
        u8 = u_ref[0, pl.ds(blk * SUBLANES, SUBLANES), :]
        entry = row == (0 if d == 0 else SUBLANES - 1)
        u8 = u8 + jnp.where(entry, a8 * carry, 0.0)
        for step in (1, 2, 4):
            if d == 0:
                shift, ident = step, row < step
            else:
                shift, ident = SUBLANES - step, row >= SUBLANES - step
            u8 = u8 + a8 * jnp.where(ident, 0.0, pltpu.roll(u8, shift, 0))
            if step < 4:
                a8 = a8 * pltpu.roll(a8, shift, 0)
        out_ref[0, pl.ds(blk * SUBLANES, SUBLANES), :] = u8
        last = u8[SUBLANES - 1:SUBLANES, :] if d == 0 else u8[0:1, :]
        return jnp.broadcast_to(last, (SUBLANES, LRU_WIDTH))

    def body(kb, carries):
        cf, cb = carries
        cf = scan_block(0, kb, cf, af_ref, uf_ref, hf_ref)
        cb = scan_block(1, nblk - 1 - kb, cb, ab_ref, ub_ref, hb_ref)
        return cf, cb

    cf, cb = lax.fori_loop(0, nblk, body, (carry_ref[0], carry_ref[1]))
    carry_ref[0] = cf
    carry_ref[1] = cb


def _lru(af, uf, ab, ub):
    b, s, w = af.shape
    tt = min(TT_LRU, s)
    nt = s // tt
    fwd = pl.BlockSpec((1, tt, w), lambda bb, i: (bb, i, 0))
    bwd = pl.BlockSpec((1, tt, w), lambda bb, i: (bb, nt - 1 - i, 0))
    return pl.pallas_call(
        _lru_kernel,
        grid=(b, nt),
        in_specs=[fwd, fwd, bwd, bwd],
        out_specs=[fwd, bwd],
        out_shape=[jax.ShapeDtypeStruct((b, s, w), F32), jax.ShapeDtypeStruct((b, s, w), F32)],
        scratch_shapes=[pltpu.VMEM((2, SUBLANES, w), F32)],
        compiler_params=_cparams(("parallel", "arbitrary")),
        name="lru",
    )(af, uf, ab, ub)


def _gelu_tanh(x):
    return 0.5 * x * (1.0 + jnp.tanh(math.sqrt(2.0 / math.pi) * (x + 0.044715 * (x * x * x))))


def _merge_kernel(x_ref, ya_ref, hf_ref, hb_ref, by_ref, yc_ref,
                  o0_ref, o1_ref, o2_ref, l0_ref, l1_ref, l2_ref, g_ref, wg_ref, gbias_ref,
                  wa_ref, wb_ref, wc_ref, wd_ref, wo_ref, out_ref, perm_scr):
    d_model = x_ref.shape[2]
    tm = x_ref.shape[1]
    x = x_ref[0]
    var = jnp.mean(x * x, axis=-1, keepdims=True)
    hn = (x * lax.rsqrt(var + EPS) * g_ref[...]).astype(BF16)

    def natural(ref, slot):
        dil = ref.shape[1]
        if dil == 1:
            return ref[0, 0].astype(F32)
        for r in range(dil):
            rows = ref[0, r].astype(F32)
            for j in range(rows.shape[1] // LANES):
                perm_scr[slot, j, pl.ds(r, tm // dil, stride=dil), :] = rows[:, j * LANES:(j + 1) * LANES]
        return jnp.concatenate([perm_scr[slot, j] for j in range(perm_scr.shape[1])], axis=1)

    yb = ((hf_ref[0] + hb_ref[0]) * _gelu_tanh(by_ref[0].astype(F32))).astype(BF16)

    l0, l1, l2 = natural(l0_ref, 0), natural(l1_ref, 0), natural(l2_ref, 1)
    o0, o1, o2 = natural(o0_ref, 2), natural(o1_ref, 2), natural(o2_ref, 3)
    mx = jnp.maximum(jnp.maximum(l0, l1), l2)
    e0, e1, e2 = jnp.exp2(l0 - mx), jnp.exp2(l1 - mx), jnp.exp2(l2 - mx)
    yd = ((e0 * o0 + e1 * o1 + e2 * o2) / (e0 + e1 + e2)).astype(BF16)

    def gate_logits(kk):
        off = wg_ref.shape[1] - N_BRANCH * d_model + kk * d_model
        return lambda: jnp.dot(hn, wg_ref[:, off:off + d_model], preferred_element_type=F32)

    def branch(y, w_ref):
        return lambda: jnp.dot(y, w_ref[...], preferred_element_type=F32)

    dots = []
    for kk, (y, w_ref) in enumerate(((ya_ref[0], wa_ref), (yb, wb_ref), (yc_ref[0], wc_ref), (yd, wd_ref))):
        dots += [gate_logits(kk), branch(y, w_ref)]
    pending = [dots[0](), dots[1]()]
    merged = None
    for kk in range(N_BRANCH):
        pending += [d() for d in dots[2 * kk + 2:2 * kk + 4]]
        logits, proj = pending.pop(0), pending.pop(0)
        term = _sigmoid(logits + gbias_ref[kk:kk + 1, :]) * proj
        merged = term if merged is None else merged + term

    out_ref[0] = x + jnp.dot(merged.astype(BF16), wo_ref[...], preferred_element_type=F32)


def _merge(x, layer, ya, hf, hb, by, yc, o_d, lse_d, g, w_all, gbias, wa, wb, wc, wd, wo):
    b, s, d = x.shape
    tm = min(TM_MERGE, s)
    tok = lambda width: pl.BlockSpec((1, tm, width), lambda bb, i: (bb, i, 0))
    dw = D_HEADS * HEAD_DIM
    res = [pl.BlockSpec((1, dil, tm // dil, dw), lambda bb, i: (bb, 0, i, 0)) for _, dil in D_PATTERNS]
    return pl.pallas_call(
        _merge_kernel,
        grid=(b, s // tm),
        in_specs=[tok(d), tok(512), tok(LRU_WIDTH), tok(LRU_WIDTH), tok(LRU_WIDTH), tok(512),
                  *res, *res, _layer(g, layer), _layer(w_all, layer), _layer(gbias, layer),
                  _layer(wa, layer), _layer(wb, layer), _layer(wc, layer), _layer(wd, layer), _layer(wo, layer)],
        out_specs=tok(d),
        out_shape=jax.ShapeDtypeStruct((b, s, d), F32),
        scratch_shapes=[pltpu.VMEM((4, dw // LANES, tm, LANES), F32)],
        compiler_params=_cparams(("parallel", "parallel")),
        name="merge",
    )(x, ya, hf, hb, by, yc, *o_d, *lse_d, g, w_all, gbias, wa, wb, wc, wd, wo)


def _mlp_kernel(x_ref, g_ref, w1_ref, w2_ref, gf_ref, out_ref, *, final_norm):
    x = x_ref[0]
    var = jnp.mean(x * x, axis=-1, keepdims=True)
    hn = (x * lax.rsqrt(var + EPS) * g_ref[...]).astype(BF16)
    hidden = w1_ref.shape[1]
    acc = x
    for c in range(hidden // MLP_CHUNK):
        h = jnp.dot(hn, w1_ref[:, c * MLP_CHUNK:(c + 1) * MLP_CHUNK], preferred_element_type=F32)
        h = jnp.square(jnp.maximum(h, 0.0)).astype(BF16)
        acc = acc + jnp.dot(h, w2_ref[c * MLP_CHUNK:(c + 1) * MLP_CHUNK, :], preferred_element_type=F32)
    if final_norm:
        var = jnp.mean(acc * acc, axis=-1, keepdims=True)
        acc = acc * lax.rsqrt(var + EPS) * gf_ref[...]
    out_ref[0] = acc


def _mlp(x, layer, g, w1, w2, g_final, final_norm):
    b, s, d = x.shape
    tm = min(TM_MLP, s)
    tok = pl.BlockSpec((1, tm, d), lambda bb, i: (bb, i, 0))
    return pl.pallas_call(
        functools.partial(_mlp_kernel, final_norm=final_norm),
        grid=(b, s // tm),
        in_specs=[tok, _layer(g, layer), _layer(w1, layer), _layer(w2, layer), _resident(g_final.shape)],
        out_specs=tok,
        out_shape=jax.ShapeDtypeStruct((b, s, d), F32),
        compiler_params=_cparams(("parallel", "parallel")),
        name="mlp",
    )(x, g, w1, w2, g_final)


def _dense_block_diag(w):
    nb, bw = w.shape[-3], w.shape[-2]
    eye = jnp.eye(nb, dtype=w.dtype)
    return jnp.einsum('...ncd,nm->...ncmd', w, eye).reshape(w.shape[:-3] + (nb * bw, nb * bw))


def kernel(x, norm_mix_g, w_in, gate_bias, qk_norm_g, conv_w, conv_b, lru_gate_w, lru_gate_b, lru_lambda, sink_logit, w_proj_a, w_proj_b, w_proj_c, w_proj_d, w_out, norm_mlp_g, w_mlp1, w_mlp2, norm_final_g):
    b, s, d = x.shape
    depth = w_in.shape[0]
    ax_t, seq_t = _rope_tables(s)
    dw = D_HEADS * HEAD_DIM
    n_groups = len(D_PATTERNS)
    bf = lambda w: w.astype(BF16)
    w_in_bf, wa, wb, wc, wd, wo, w1, w2 = map(bf, (w_in, w_proj_a, w_proj_b, w_proj_c, w_proj_d, w_out, w_mlp1, w_mlp2))
    g_mix, g_mlp, conv_b3 = norm_mix_g[:, None, :], norm_mlp_g[:, None, :], conv_b[:, None, :]
    qk_gain = jnp.tile(qk_norm_g, (1, 1, LANES // HEAD_DIM))
    gate_w_dense = bf(_dense_block_diag(lru_gate_w))

    for l in range(depth):
        outs = _inproj(x, l, g_mix, w_in_bf, qk_gain, ax_t, seq_t, conv_w, conv_b3, gate_w_dense,
                       lru_gate_b, lru_lambda)
        qta, ka, vta, af, uf, ab, ub, by, qc, kc, vc = outs[:11]
        qd, kd, vd = outs[11:11 + n_groups], outs[11 + n_groups:11 + 2 * n_groups], outs[11 + 2 * n_groups:11 + 3 * n_groups]

        ya = _attn_a(qta, ka, vta)

        hf, hb = _lru(af, uf, ab, ub)

        yc = _band_attention(qc[:, None], kc[:, None], vc[:, None], nkv=C_KV_HEADS, group=C_HEADS // C_KV_HEADS,
                             half_window=C_HALF_WINDOW, sink=sink_logit, sink_layer=l)[:, 0]

        o_d, lse_d = [], []
        for gi, (window, dil) in enumerate(D_PATTERNS):
            o, lse = _band_attention(qd[gi], kd[gi], vd[gi], nkv=D_HEADS, group=1,
                                     half_window=window // (2 * dil), want_lse=True)
            o_d.append(o)
            lse_d.append(lse)

        x = _merge(x, l, ya, hf, hb, by, yc, o_d, lse_d, g_mix, w_in_bf, gate_bias, wa, wb, wc, wd, wo)
        x = _mlp(x, l, g_mlp, w1, w2, norm_final_g[None, :], final_norm=(l == depth - 1))
    return x
```

```python
import functools
import math

import jax
import jax.numpy as jnp
from jax import lax
from jax.experimental import pallas as pl
from jax.experimental.pallas import tpu as pltpu

F32 = jnp.float32
BF16 = jnp.bfloat16

HEAD_DIM = 64
ROPE_THETA = 10000.0
GRID_W = 64
EPS = 1e-6
MASK_VALUE = -1e30
LOG2E = math.log2(math.e)

A_HEADS, A_KV_HEADS = 8, 2
LRU_WIDTH, LRU_BLOCKS, LRU_C = 512, 8, 8.0
CONV_W, CONV_LEFT = 4, 2
C_HEADS, C_KV_HEADS, C_HALF_WINDOW = 8, 2, 128
D_PATTERNS = ((128, 1), (512, 4), (2048, 16))
D_HEADS = 4
N_BRANCH = 4

LANES = 128
SUBLANES = 8
VMEM_LIMIT_BYTES = 56 * 1024 * 1024

TM_INPROJ = 512
TQ_ATTN_A = 256
SUB_ATTN_A = 128
TILES_PER_STEP_ATTN_A = 2
LOOKAHEAD_ATTN_A = 14
LOOKAHEAD_BAND = 6
LOOKAHEAD_INPROJ = 6
SUB_BAND = 128
CHUNKS_PER_BODY_ATTN_A = 16
VT_ROWS_A = HEAD_DIM + 16
TQ_BAND = 256
BAND_TILES_PER_STEP = 4
TT_LRU = 1024
TM_MERGE = 512
TM_MLP = 1024
MLP_CHUNK = 1024

Q_SCALE = HEAD_DIM ** -0.5 * LOG2E


def _cparams(sem):
    return pltpu.CompilerParams(dimension_semantics=sem, vmem_limit_bytes=VMEM_LIMIT_BYTES)


def _layer(arr, layer):
    zeros = (0,) * (arr.ndim - 1)
    return pl.BlockSpec((None,) + arr.shape[1:], lambda *_: (layer,) + zeros, pipeline_mode=pl.Buffered(1))


def _resident(shape):
    nd = len(shape)
    return pl.BlockSpec(shape, lambda *_: (0,) * nd, pipeline_mode=pl.Buffered(1))


def _rope_tables(seq):
    pos = jnp.arange(seq, dtype=F32)
    lane = jnp.arange(LANES)
    d = lane % HEAD_DIM

    inv = ROPE_THETA ** (-jnp.arange(0, HEAD_DIM, 2, dtype=F32) / HEAD_DIM)
    ang = pos[:, None] * inv[None, :]
    f = d % (HEAD_DIM // 2)
    cos_s, sin_s = jnp.cos(ang)[:, f], jnp.sin(ang)[:, f]
    lo = (d < HEAD_DIM // 2)[None, :]
    seq_t = (cos_s, jnp.where(lo, -sin_s, 0.0), jnp.where(lo, 0.0, sin_s))

    quarter = HEAD_DIM // 4
    inv_ax = ROPE_THETA ** (-jnp.arange(0, HEAD_DIM // 2, 2, dtype=F32) / (HEAD_DIM // 2))
    n_rows = seq // GRID_W
    ang_r = jnp.arange(n_rows, dtype=F32)[:, None] * inv_ax[None, :]
    ang_c = jnp.arange(GRID_W, dtype=F32)[:, None] * inv_ax[None, :]

    def expand(by_row, by_col):
        r = jnp.repeat(by_row, GRID_W, axis=0)
        c = jnp.tile(by_col, (n_rows, 1))
        return jnp.tile(jnp.concatenate([r, r, c, c], axis=1), (1, LANES // HEAD_DIM))

    cos_a, sin_a = expand(jnp.cos(ang_r), jnp.cos(ang_c)), expand(jnp.sin(ang_r), jnp.sin(ang_c))
    lo_a = (d % (HEAD_DIM // 2) < quarter)[None, :]
    ax_t = (cos_a, jnp.where(lo_a, -sin_a, 0.0), jnp.where(lo_a, 0.0, sin_a))
    return ax_t, seq_t


def _rotate(x, tables, half):
    cos, sin_a, sin_b = tables
    return x * cos + pltpu.roll(x, LANES - half, 1) * sin_a + pltpu.roll(x, half, 1) * sin_b


def _split_dot(x, m):
    hi = x.astype(BF16)
    lo = (x - hi.astype(F32)).astype(BF16)
    return jnp.dot(jnp.concatenate([hi, lo], axis=1), jnp.concatenate([m, m], axis=0),
                   preferred_element_type=F32)


def _head_mean_matrix():
    r = lax.broadcasted_iota(jnp.int32, (LANES, LANES), 0) // HEAD_DIM
    c = lax.broadcasted_iota(jnp.int32, (LANES, LANES), 1) // HEAD_DIM
    return jnp.where(r == c, 1.0 / HEAD_DIM, 0.0).astype(BF16)


def _head_rmsnorm(x, gain, mean_mat):
    var = _split_dot(x * x, mean_mat)
    return x * lax.rsqrt(var + EPS) * gain


IN_WIDTHS = (512, 128, 128, 512, 512, 512, 128, 128, 768, 768, 768)
IN_OFFSETS = tuple(sum(IN_WIDTHS[:i]) for i in range(len(IN_WIDTHS)))
N_IN = sum(IN_WIDTHS)


def _inproj_kernel(x_ref, xprev_ref, xnext_ref, g_ref, w_ref, qkg_ref,
                   ca_ref, saa_ref, sab_ref, cs_ref, ssa_ref, ssb_ref,
                   cw_ref, cb_ref, gw_ref, gb_ref, lam_ref,
                   qta_ref, ka_ref, vta_ref, af_ref, uf_ref, ab_ref, ub_ref, by_ref,
                   qc_ref, kc_ref, vc_ref, *rest):
    d_refs, perm_scr, xpad_scr = rest[:9], rest[9], rest[10]
    tile = pl.program_id(0)
    n_tiles = pl.num_programs(0)

    def normed(rows):
        var = jnp.mean(rows * rows, axis=-1, keepdims=True)
        return (rows * lax.rsqrt(var + EPS) * g_ref[...]).astype(BF16)

    x = x_ref[0]
    hn = normed(x)

    def proj(seg, lo=0, width=None):
        off = IN_OFFSETS[seg] + lo
        width = IN_WIDTHS[seg] if width is None else width
        return lambda: jnp.dot(hn, w_ref[:, off:off + width], preferred_element_type=F32)

    ax_t = (ca_ref[...], saa_ref[...], sab_ref[...])
    seq_t = (cs_ref[...], ssa_ref[...], ssb_ref[...])
    mean_mat = _head_mean_matrix()
    gq = qkg_ref[0:1, :]
    gk = qkg_ref[1:2, :]
    tm = x.shape[0]
    dw = D_HEADS * HEAD_DIM

    def finish_aq(acc):
        for j in range(IN_WIDTHS[0] // LANES):
            piece = _head_rmsnorm(acc[:, j * LANES:(j + 1) * LANES], gq, mean_mat)
            piece = _rotate(piece, ax_t, HEAD_DIM // 4) * Q_SCALE
            qta_ref[0, j * LANES:(j + 1) * LANES, :] = piece.T.astype(BF16)

    def finish_akv(acc):
        ak = _rotate(_head_rmsnorm(acc[:, :LANES], gk, mean_mat), ax_t, HEAD_DIM // 4).astype(BF16)
        avt = acc[:, LANES:].T.astype(BF16)
        pad_row = lax.broadcasted_iota(jnp.int32, (VT_ROWS_A - HEAD_DIM, tm), 0)
        ones_rows = (pad_row == 0).astype(BF16)
        for g in range(A_KV_HEADS):
            ka_ref[0, g] = ak[:, g * HEAD_DIM:(g + 1) * HEAD_DIM]
            vta_ref[0, g, 0] = jnp.concatenate([avt[g * HEAD_DIM:(g + 1) * HEAD_DIM, :], ones_rows], axis=0)

    lru = {}

    def finish_bx(acc):
        halo = jnp.dot(normed(jnp.concatenate([xprev_ref[0], xnext_ref[0]], axis=0)),
                       w_ref[:, IN_OFFSETS[3]:IN_OFFSETS[3] + LRU_WIDTH], preferred_element_type=F32)
        xpad_scr[0:SUBLANES] = jnp.where(tile > 0, halo[:SUBLANES], 0.0)
        xpad_scr[SUBLANES:SUBLANES + tm] = acc
        xpad_scr[SUBLANES + tm:] = jnp.where(tile < n_tiles - 1, halo[SUBLANES:], 0.0)
        xc = cb_ref[...]
        base = SUBLANES - CONV_LEFT
        for jj in range(CONV_W):
            xc = xc + xpad_scr[base + jj:base + jj + tm, :] * cw_ref[jj:jj + 1, :]
        lru["xc"] = xc
        lru["xcb"] = xc.astype(BF16)

    def gate_dots():
        return [jnp.dot(lru["xcb"], gw_ref[d, g], preferred_element_type=F32) for d in range(2) for g in range(2)]

    def finish_gates(dots):
        xc = lru["xc"]
        for d, (a_ref, u_ref) in enumerate(((af_ref, uf_ref), (ab_ref, ub_ref))):
            c2 = (-0.5 * LRU_C * LOG2E) * _softplus(-lam_ref[d:d + 1, :])
            a = jnp.exp2(c2 * jnp.tanh(0.5 * dots[2 * d] + 0.5 * gb_ref[d, 0:1, :]) + c2)
            gi = _sigmoid(dots[2 * d + 1] + gb_ref[d, 1:2, :])
            a_ref[0] = a
            z = (1.0 - a) * (1.0 + a)
            root = jnp.where(z > 0.0, z * lax.rsqrt(z), 0.0)
            u_ref[0] = root * (gi * xc)

    def finish_by(acc):
        by_ref[0] = acc.astype(BF16)

    def rope_piece(piece, scale):
        piece = _rotate(piece, seq_t, HEAD_DIM // 2)
        return piece if scale is None else piece * scale

    def finish_cq(acc):
        for j in range(IN_WIDTHS[5] // LANES):
            qc_ref[0, :, j * LANES:(j + 1) * LANES] = rope_piece(acc[:, j * LANES:(j + 1) * LANES], Q_SCALE).astype(BF16)

    def finish_ckv(acc):
        kc_ref[0] = rope_piece(acc[:, :LANES], None).astype(BF16)
        vc_ref[0] = acc[:, LANES:].astype(BF16)

    def finish_d(ref, dil, scale, rope):
        def finish(acc):
            for j in range(dw // LANES):
                piece = acc[:, j * LANES:(j + 1) * LANES]
                if rope:
                    piece = rope_piece(piece, scale)
                if dil == 1:
                    ref[0, 0, :, j * LANES:(j + 1) * LANES] = piece.astype(BF16)
                else:
                    perm_scr[j] = piece
            if dil > 1:
                for r in range(dil):
                    rows = [perm_scr[j, pl.ds(r, tm // dil, stride=dil), :] for j in range(dw // LANES)]
                    ref[0, r] = jnp.concatenate(rows, axis=1).astype(BF16)
        return finish

    segments = [(proj(0), finish_aq), (proj(1, 0, 2 * LANES), finish_akv), (proj(3), finish_bx),
                (proj(4), finish_by), (gate_dots, finish_gates), (proj(5), finish_cq),
                (proj(6, 0, 2 * LANES), finish_ckv)]
    for gi, (_, dil) in enumerate(D_PATTERNS):
        for ti, (seg, scale, rope) in enumerate(((8, Q_SCALE, True), (9, None, True), (10, None, False))):
            segments.append((proj(seg, gi * dw, dw), finish_d(d_refs[ti * len(D_PATTERNS) + gi], dil, scale, rope)))

    issued, pending = 0, []
    for idx, (_, finish) in enumerate(segments):
        horizon = min(len(segments), idx + 1 + LOOKAHEAD_INPROJ)
        while issued < horizon and (segments[issued][0] is not gate_dots or "xcb" in lru):
            pending.append(segments[issued][0]())
            issued += 1
        finish(pending.pop(0))


def _inproj(x, layer, g, w_all, qk_gain128, ax_t, seq_t, conv_w, conv_b, gate_w, gate_b, lam):
    b, s, d = x.shape
    tm = min(TM_INPROJ, s)
    nt = s // tm
    per, n8 = tm // SUBLANES, s // SUBLANES
    halo_prev = pl.BlockSpec((1, SUBLANES, d), lambda i, bb: (bb, jnp.maximum(i * per - 1, 0), 0))
    halo_next = pl.BlockSpec((1, SUBLANES, d), lambda i, bb: (bb, jnp.minimum((i + 1) * per, n8 - 1), 0))
    tok = lambda width: pl.BlockSpec((1, tm, width), lambda i, bb: (bb, i, 0))
    tab = pl.BlockSpec((tm, LANES), lambda i, bb: (i, 0))
    dw = D_HEADS * HEAD_DIM
    d_shapes = tuple(jax.ShapeDtypeStruct((b, dil, s // dil, dw), BF16) for _, dil in D_PATTERNS)
    d_specs = tuple(pl.BlockSpec((1, dil, tm // dil, dw), lambda i, bb: (bb, 0, i, 0)) for _, dil in D_PATTERNS)
    out_shape = (
        jax.ShapeDtypeStruct((b, A_HEADS * HEAD_DIM, s), BF16),
        jax.ShapeDtypeStruct((b, A_KV_HEADS, s, HEAD_DIM), BF16),
        jax.ShapeDtypeStruct((b, A_KV_HEADS, nt, VT_ROWS_A, tm), BF16),
        jax.ShapeDtypeStruct((b, s, LRU_WIDTH), F32),
        jax.ShapeDtypeStruct((b, s, LRU_WIDTH), F32),
        jax.ShapeDtypeStruct((b, s, LRU_WIDTH), F32),
        jax.ShapeDtypeStruct((b, s, LRU_WIDTH), F32),
        jax.ShapeDtypeStruct((b, s, LRU_WIDTH), BF16),
        jax.ShapeDtypeStruct((b, s, 512), BF16),
        jax.ShapeDtypeStruct((b, s, 128), BF16),
        jax.ShapeDtypeStruct((b, s, 128), BF16),
    ) + d_shapes * 3
    out_specs = (
        pl.BlockSpec((1, A_HEADS * HEAD_DIM, tm), lambda i, bb: (bb, 0, i)),
        pl.BlockSpec((1, A_KV_HEADS, tm, HEAD_DIM), lambda i, bb: (bb, 0, i, 0)),
        pl.BlockSpec((1, A_KV_HEADS, 1, VT_ROWS_A, tm), lambda i, bb: (bb, 0, i, 0, 0)),
        tok(LRU_WIDTH), tok(LRU_WIDTH), tok(LRU_WIDTH), tok(LRU_WIDTH), tok(LRU_WIDTH), tok(512), tok(128), tok(128),
    ) + d_specs * 3
    return pl.pallas_call(
        _inproj_kernel,
        grid=(nt, b),
        in_specs=[tok(d), halo_prev, halo_next, _layer(g, layer),
                  pl.BlockSpec((None, d, N_IN), lambda i, bb: (layer, 0, 0), pipeline_mode=pl.Buffered(1)),
                  _layer(qk_gain128, layer),
                  tab, tab, tab, tab, tab, tab,
                  _layer(conv_w, layer), _layer(conv_b, layer), _layer(gate_w, layer),
                  _layer(gate_b, layer), _layer(lam, layer)],
        out_specs=out_specs,
        out_shape=out_shape,
        scratch_shapes=[pltpu.VMEM((dw // LANES, tm, LANES), F32),
                        pltpu.VMEM((tm + 2 * SUBLANES, LRU_WIDTH), F32)],
        compiler_params=_cparams(("parallel", "parallel")),
        name="inproj",
    )(x, x, x, g, w_all, qk_gain128, *ax_t, *seq_t, conv_w, conv_b, gate_w, gate_b, lam)


def _sublane_allmax(x):
    for shift in (4, 2, 1):
        x = jnp.maximum(x, pltpu.roll(x, shift, 0))
    return x


def _attn_a_kernel(q_ref, k_ref, v_ref, o_ref, m_ref, acc_ref, *, group, sub):
    n_chunks, chunk = v_ref.shape[2], v_ref.shape[4]
    tq = min(TQ_ATTN_A, q_ref.shape[2])
    n_tiles = q_ref.shape[2] // tq
    m_ref[...] = jnp.full(m_ref.shape, MASK_VALUE, F32)
    acc_ref[...] = jnp.zeros(acc_ref.shape, F32)

    per_body = min(CHUNKS_PER_BODY_ATTN_A, n_chunks)
    n_sub = chunk // sub
    pair = 2 if n_sub % 2 == 0 else 1
    units = [(cc, u, h) for cc in range(per_body) for up in range(n_sub // pair)
             for h in range(n_tiles * group) for u in range(up * pair, (up + 1) * pair)]

    def body(it, carry):
        c0 = it * per_body

        def scores(idx):
            cc, u, h = units[idx]
            row0 = pl.multiple_of((c0 + cc) * chunk + u * sub, sub)
            k = k_ref[0, 0, pl.ds(row0, sub), :]
            j, head = divmod(h, group)
            qt = q_ref[0, head * HEAD_DIM:(head + 1) * HEAD_DIM, j * tq:(j + 1) * tq]
            return jnp.dot(k, qt, preferred_element_type=F32)

        pending = [scores(i) for i in range(LOOKAHEAD_ATTN_A)]
        probs = []
        for idx, (cc, u, h) in enumerate(units):
            if idx + LOOKAHEAD_ATTN_A < len(units):
                pending.append(scores(idx + LOOKAHEAD_ATTN_A))
            s3 = pending.pop(0).reshape(sub // SUBLANES, SUBLANES, tq)
            if not probs:
                m_start = m_ref[h]
                m_old = m_start
            m_new = jnp.maximum(m_old, _sublane_allmax(jnp.max(s3, axis=0)))
            if probs:
                beta = jnp.exp2(m_old - m_new).astype(BF16)
                beta = jnp.concatenate([beta, beta], axis=0)[None]
                probs = [(p.reshape(sub // 16, 16, tq) * beta).reshape(sub, tq) for p in probs]
            probs.append(jnp.exp2(s3 - m_new[None]).reshape(sub, tq).astype(BF16))
            m_old = m_new
            if len(probs) == pair:
                u0 = u - (pair - 1)
                vt = v_ref[0, 0, c0 + cc, :, u0 * sub:(u + 1) * sub]
                pv = jnp.dot(vt, jnp.concatenate(probs, axis=0), preferred_element_type=F32)
                alpha = jnp.exp2(m_start - m_new)
                acc_ref[h] = acc_ref[h] * alpha[None] + pv.reshape(VT_ROWS_A // SUBLANES, SUBLANES, tq)
                m_ref[h] = m_new
                probs = []
        return carry

    lax.fori_loop(0, n_chunks // per_body, body, 0)
    for j in range(n_tiles):
        outs = []
        for h in range(j * group, (j + 1) * group):
            acc = acc_ref[h].reshape(VT_ROWS_A, tq)
            outs.append(acc[:HEAD_DIM] / acc[HEAD_DIM:HEAD_DIM + 1])
        o_ref[0, j * tq:(j + 1) * tq, :] = jnp.concatenate(outs, axis=0).T.astype(o_ref.dtype)


def _attn_a(qt, k, vt):
    b, hd_all, s = qt.shape
    nkv, nt, chunk = vt.shape[1], vt.shape[2], vt.shape[4]
    group = hd_all // HEAD_DIM // nkv
    tq = min(TQ_ATTN_A * TILES_PER_STEP_ATTN_A, s)
    n_state = group * (tq // min(TQ_ATTN_A, s))
    gw = group * HEAD_DIM
    return pl.pallas_call(
        functools.partial(_attn_a_kernel, group=group, sub=min(SUB_ATTN_A, chunk)),
        grid=(b, nkv, s // tq),
        in_specs=[pl.BlockSpec((1, gw, tq), lambda bb, g, i: (bb, g, i)),
                  pl.BlockSpec((1, 1, s, HEAD_DIM), lambda bb, g, i: (bb, g, 0, 0)),
                  pl.BlockSpec((1, 1, nt, VT_ROWS_A, chunk), lambda bb, g, i: (bb, g, 0, 0, 0))],
        out_specs=pl.BlockSpec((1, tq, gw), lambda bb, g, i: (bb, i, g)),
        out_shape=jax.ShapeDtypeStruct((b, s, hd_all), BF16),
        scratch_shapes=[pltpu.VMEM((n_state, SUBLANES, min(TQ_ATTN_A, s)), F32),
                        pltpu.VMEM((n_state, VT_ROWS_A // SUBLANES, SUBLANES, min(TQ_ATTN_A, s)), F32)],
        compiler_params=_cparams(("parallel", "parallel", "parallel")),
        name="attn_a",
    )(qt, k, vt)


def _band_kernel(*refs, nkv, group, half_window, seq_len, sink_layer, want_lse):
    has_sink = sink_layer is not None
    q_ref, kp_ref, km_ref, kn_ref, vp_ref, vm_ref, vn_ref = refs[:7]
    rest = list(refs[7:])
    sink_ref = rest.pop(0) if has_sink else None
    o_ref = rest.pop(0)
    lse_ref = rest.pop(0) if want_lse else None

    n_seq_blk, tq_blk = q_ref.shape[1], q_ref.shape[2]
    tq = min(TQ_BAND, tq_blk)
    tiles_per_seq = tq_blk // tq
    split = lambda j: divmod(j, tiles_per_seq)
    halo = kp_ref.shape[2]
    nk = tq + 2 * halo
    sub = SUB_BAND if nk % SUB_BAND == 0 else halo
    n_heads = nkv * group
    blk0 = pl.program_id(2) * tq_blk

    pad_row = lax.broadcasted_iota(jnp.int32, (VT_ROWS_A - HEAD_DIM, tq_blk + 2 * halo), 0)
    ones_rows = (pad_row == 0).astype(q_ref.dtype)
    kcat, vt_ext = [], []
    for sq in range(n_seq_blk):
        kcat.append(jnp.concatenate([kp_ref[0, sq], km_ref[0, sq], kn_ref[0, sq]], axis=0))
        vt = jnp.concatenate([vp_ref[0, sq], vm_ref[0, sq], vn_ref[0, sq]], axis=0).T
        vt_ext.append([jnp.concatenate([vt[g * HEAD_DIM:(g + 1) * HEAD_DIM], ones_rows], axis=0)
                       for g in range(nkv)])

    kw = min(LANES, nkv * HEAD_DIM)
    zeros = jnp.zeros((HEAD_DIM, tq), q_ref.dtype)
    q_pads = {}

    def padded_queries(j):
        if j not in q_pads:
            sq, jt = split(j)
            qt = q_ref[0, sq, jt * tq:(jt + 1) * tq, :].T
            pads = []
            for h in range(n_heads):
                slot = ((h // group) * HEAD_DIM % kw) // HEAD_DIM
                parts = [zeros] * (kw // HEAD_DIM)
                parts[slot] = qt[h * HEAD_DIM:(h + 1) * HEAD_DIM, :]
                pads.append(parts[0] if len(parts) == 1 else jnp.concatenate(parts, axis=0))
            q_pads.clear()
            q_pads[j] = pads
        return q_pads[j]

    units = [(j, kb, h) for j in range(n_seq_blk * tiles_per_seq) for kb in range(nk // sub)
             for h in range(n_heads)]

    cw = LANES if tq % LANES == 0 else tq
    n_ch = tq // cw

    def query_chunks(kb):
        k_lo = kb * sub - halo
        lo = max(0, (k_lo - half_window) // cw)
        hi = min(n_ch, -(-(k_lo + sub + half_window) // cw))
        return list(range(lo, hi))

    def scores(idx):
        j, kb, h = units[idx]
        sq, jt = split(j)
        row0 = jt * tq + kb * sub
        lane0 = (h // group) * HEAD_DIM // kw * kw
        cs = query_chunks(kb)
        return jnp.dot(kcat[sq][row0:row0 + sub, lane0:lane0 + kw],
                       padded_queries(j)[h][:, cs[0] * cw:(cs[-1] + 1) * cw], preferred_element_type=F32)

    key_row = lax.broadcasted_iota(jnp.int32, (sub, cw), 0)
    query = lax.broadcasted_iota(jnp.int32, (sub, cw), 1)

    def needs_mask(kb, c):
        k_lo = kb * sub - halo
        in_band = k_lo - (c * cw + cw - 1) >= -half_window and k_lo + sub - 1 - c * cw <= half_window
        return not (in_band and k_lo >= 0 and k_lo + sub <= tq)

    def valid_mask(j, kb, c):
        q0 = blk0 + split(j)[1] * tq
        kpos = q0 - halo + kb * sub + key_row
        rel = kpos - (q0 + c * cw + query)
        return (jnp.abs(rel) <= half_window) & (kpos >= 0) & (kpos < seq_len)

    def start_state():
        shape = (VT_ROWS_A // SUBLANES, SUBLANES, cw)
        if has_sink:
            m0 = [jnp.full((SUBLANES, cw), sink_ref[sink_layer, h] * LOG2E, F32) for h in range(n_heads)]
            denom = ((lax.broadcasted_iota(jnp.int32, shape, 0) == HEAD_DIM // SUBLANES)
                     & (lax.broadcasted_iota(jnp.int32, shape, 1) == 0))
            a0 = denom.astype(F32)
        else:
            m0 = [jnp.full((SUBLANES, cw), 0.1 * MASK_VALUE, F32)] * n_heads
            a0 = jnp.zeros(shape, F32)
        return [[m0[h]] * n_ch for h in range(n_heads)], [[a0] * n_ch for _ in range(n_heads)]

    def join(parts):
        return parts[0] if len(parts) == 1 else jnp.concatenate(parts, axis=1)

    def finish(j, m, acc):
        outs, lses = [], []
        for h in range(n_heads):
            full = join([x.reshape(VT_ROWS_A, cw) for x in acc[h]])
            l_row = full[HEAD_DIM:HEAD_DIM + 1]
            outs.append(full[:HEAD_DIM] / l_row)
            if want_lse:
                lse = join([x[0:1] for x in m[h]]) + jnp.log2(l_row)
                lses.append(jnp.broadcast_to(lse, (HEAD_DIM, tq)))
        sq, jt = split(j)
        o_ref[0, sq, jt * tq:(jt + 1) * tq, :] = jnp.concatenate(outs, axis=0).T.astype(o_ref.dtype)
        if want_lse:
            lse_ref[0, sq, jt * tq:(jt + 1) * tq, :] = jnp.concatenate(lses, axis=0).T

    look = min(LOOKAHEAD_BAND, len(units))
    pending = [scores(i) for i in range(look)]
    mask_key, masks = None, None
    for idx, (j, kb, h) in enumerate(units):
        if idx + look < len(units):
            pending.append(scores(idx + look))
        if kb == 0 and h == 0:
            m, acc = start_state()
        cs = query_chunks(kb)
        if mask_key != (j, kb):
            mask_key = (j, kb)
            masks = [valid_mask(j, kb, c) if needs_mask(kb, c) else None for c in cs]
        g = h // group
        s = pending.pop(0)
        probs, alphas = [], []
        for i, c in enumerate(cs):
            s_c = s if len(cs) == 1 else s[:, i * cw:(i + 1) * cw]
            if masks[i] is not None:
                s_c = jnp.where(masks[i], s_c, MASK_VALUE)
            s3 = s_c.reshape(sub // SUBLANES, SUBLANES, cw)
            m_new = jnp.maximum(m[h][c], _sublane_allmax(jnp.max(s3, axis=0)))
            alpha = jnp.exp2(m[h][c] - m_new)
            p3 = jnp.exp2(s3 - m_new[None])
            m[h][c] = m_new
            probs.append(p3.reshape(sub, cw).astype(BF16))
            alphas.append(alpha)
        sq, jt = split(j)
        col0 = jt * tq + kb * sub
        pv = jnp.dot(vt_ext[sq][g][:, col0:col0 + sub], join(probs), preferred_element_type=F32)
        for i, c in enumerate(cs):
            pv_c = pv if len(cs) == 1 else pv[:, i * cw:(i + 1) * cw]
            acc[h][c] = acc[h][c] * alphas[i][None] + pv_c.reshape(VT_ROWS_A // SUBLANES, SUBLANES, cw)
        if kb == nk // sub - 1 and h == n_heads - 1:
            finish(j, m, acc)


def _band_attention(q, k, v, *, nkv, group, half_window, sink=None, sink_layer=None, want_lse=False):
    b, n_seq, seq_len, wq = q.shape
    wk = nkv * HEAD_DIM
    halo = half_window
    tq = min(TQ_BAND * BAND_TILES_PER_STEP, seq_len)
    ns = max(1, min(n_seq, BAND_TILES_PER_STEP * TQ_BAND // tq))
    per = tq // halo
    n_halo = seq_len // halo

    main = lambda width: pl.BlockSpec((1, ns, tq, width), lambda bb, r, i: (bb, r, i, 0))
    prev = lambda width: pl.BlockSpec(
        (1, ns, halo, width), lambda bb, r, i: (bb, r, jnp.maximum(i * per - 1, 0), 0))
    nxt = lambda width: pl.BlockSpec(
        (1, ns, halo, width), lambda bb, r, i: (bb, r, jnp.minimum((i + 1) * per, n_halo - 1), 0))
    in_specs = [main(wq), prev(wk), main(wk), nxt(wk), prev(wk), main(wk), nxt(wk)]
    args = [q, k, k, k, v, v, v]
    if sink is not None:
        in_specs.append(pl.BlockSpec(memory_space=pltpu.SMEM))
        args.append(sink)
    out_block = main(wq)
    out_shape = [jax.ShapeDtypeStruct(q.shape, BF16)]
    out_specs = [out_block]
    if want_lse:
        out_shape.append(jax.ShapeDtypeStruct(q.shape, F32))
        out_specs.append(out_block)
    res = pl.pallas_call(
        functools.partial(_band_kernel, nkv=nkv, group=group, half_window=half_window,
                          seq_len=seq_len, sink_layer=sink_layer if sink is not None else None,
                          want_lse=want_lse),
        grid=(b, n_seq // ns, seq_len // tq),
        in_specs=in_specs,
        out_specs=out_specs,
        out_shape=out_shape,
        compiler_params=_cparams(("parallel", "parallel", "parallel")),
        name="band_attn",
    )(*args)
    return res if want_lse else res[0]


def _sigmoid(x):
    return 0.5 * jnp.tanh(0.5 * x) + 0.5


def _softplus(x):
    return jnp.maximum(x, 0.0) + jnp.log1p(jnp.exp(-jnp.abs(x)))


def _lru_kernel(af_ref, uf_ref, ab_ref, ub_ref, hf_ref, hb_ref, carry_ref):
    i = pl.program_id(1)
    tt = af_ref.shape[1]
    nblk = tt // SUBLANES

    @pl.when(i == 0)
    def _():
        carry_ref[...] = jnp.zeros(carry_ref.shape, F32)

    row = lax.broadcasted_iota(jnp.int32, (SUBLANES, LRU_WIDTH), 0)

    def scan_block(d, blk, carry, a_ref, u_ref, out_ref):
        a8 = a_ref[0, pl.ds(blk * SUBLANES, SUBLANES), :]
        u8 = u_ref[0, pl.ds(blk * SUBLANES, SUBLANES), :]
        entry = row == (0 if d == 0 else SUBLANES - 1)
        u8 = u8 + jnp.where(entry, a8 * carry, 0.0)
        for step in (1, 2, 4):
            if d == 0:
                shift, ident = step, row < step
            else:
                shift, ident = SUBLANES - step, row >= SUBLANES - step
            u8 = u8 + a8 * jnp.where(ident, 0.0, pltpu.roll(u8, shift, 0))
            if step < 4:
                a8 = a8 * pltpu.roll(a8, shift, 0)
        out_ref[0, pl.ds(blk * SUBLANES, SUBLANES), :] = u8
        last = u8[SUBLANES - 1:SUBLANES, :] if d == 0 else u8[0:1, :]
        return jnp.broadcast_to(last, (SUBLANES, LRU_WIDTH))

    def body(kb, carries):
        cf, cb = carries
        cf = scan_block(0, kb, cf, af_ref, uf_ref, hf_ref)
        cb = scan_block(1, nblk - 1 - kb, cb, ab_ref, ub_ref, hb_ref)
        return cf, cb

    cf, cb = lax.fori_loop(0, nblk, body, (carry_ref[0], carry_ref[1]))
    carry_ref[0] = cf
    carry_ref[1] = cb


def _lru(af, uf, ab, ub):
    b, s, w = af.shape
    tt = min(TT_LRU, s)
    nt = s // tt
    fwd = pl.BlockSpec((1, tt, w), lambda bb, i: (bb, i, 0))
    bwd = pl.BlockSpec((1, tt, w), lambda bb, i: (bb, nt - 1 - i, 0))
    return pl.pallas_call(
        _lru_kernel,
        grid=(b, nt),
        in_specs=[fwd, fwd, bwd, bwd],
        out_specs=[fwd, bwd],
        out_shape=[jax.ShapeDtypeStruct((b, s, w), F32), jax.ShapeDtypeStruct((b, s, w), F32)],
        scratch_shapes=[pltpu.VMEM((2, SUBLANES, w), F32)],
        compiler_params=_cparams(("parallel", "arbitrary")),
        name="lru",
    )(af, uf, ab, ub)


def _gelu_tanh(x):
    return 0.5 * x * (1.0 + jnp.tanh(math.sqrt(2.0 / math.pi) * (x + 0.044715 * (x * x * x))))


def _merge_kernel(x_ref, ya_ref, hf_ref, hb_ref, by_ref, yc_ref,
                  o0_ref, o1_ref, o2_ref, l0_ref, l1_ref, l2_ref, g_ref, wg_ref, gbias_ref,
                  wa_ref, wb_ref, wc_ref, wd_ref, wo_ref, out_ref, perm_scr):
    d_model = x_ref.shape[2]
    tm = x_ref.shape[1]
    x = x_ref[0]
    var = jnp.mean(x * x, axis=-1, keepdims=True)
    hn = (x * lax.rsqrt(var + EPS) * g_ref[...]).astype(BF16)

    def natural(ref, slot):
        dil = ref.shape[1]
        if dil == 1:
            return ref[0, 0].astype(F32)
        for r in range(dil):
            rows = ref[0, r].astype(F32)
            for j in range(rows.shape[1] // LANES):
                perm_scr[slot, j, pl.ds(r, tm // dil, stride=dil), :] = rows[:, j * LANES:(j + 1) * LANES]
        return jnp.concatenate([perm_scr[slot, j] for j in range(perm_scr.shape[1])], axis=1)

    yb = ((hf_ref[0] + hb_ref[0]) * _gelu_tanh(by_ref[0].astype(F32))).astype(BF16)

    l0, l1, l2 = natural(l0_ref, 0), natural(l1_ref, 0), natural(l2_ref, 1)
    o0, o1, o2 = natural(o0_ref, 2), natural(o1_ref, 2), natural(o2_ref, 3)
    mx = jnp.maximum(jnp.maximum(l0, l1), l2)
    e0, e1, e2 = jnp.exp2(l0 - mx), jnp.exp2(l1 - mx), jnp.exp2(l2 - mx)
    yd = ((e0 * o0 + e1 * o1 + e2 * o2) / (e0 + e1 + e2)).astype(BF16)

    def gate_logits(kk):
        off = wg_ref.shape[1] - N_BRANCH * d_model + kk * d_model
        return lambda: jnp.dot(hn, wg_ref[:, off:off + d_model], preferred_element_type=F32)

    def branch(y, w_ref):
        return lambda: jnp.dot(y, w_ref[...], preferred_element_type=F32)

    dots = []
    for kk, (y, w_ref) in enumerate(((ya_ref[0], wa_ref), (yb, wb_ref), (yc_ref[0], wc_ref), (yd, wd_ref))):
        dots += [gate_logits(kk), branch(y, w_ref)]
    pending = [dots[0](), dots[1]()]
    merged = None
    for kk in range(N_BRANCH):
        pending += [d() for d in dots[2 * kk + 2:2 * kk + 4]]
        logits, proj = pending.pop(0), pending.pop(0)
        term = _sigmoid(logits + gbias_ref[kk:kk + 1, :]) * proj
        merged = term if merged is None else merged + term

    out_ref[0] = x + jnp.dot(merged.astype(BF16), wo_ref[...], preferred_element_type=F32)


def _merge(x, layer, ya, hf, hb, by, yc, o_d, lse_d, g, w_all, gbias, wa, wb, wc, wd, wo):
    b, s, d = x.shape
    tm = min(TM_MERGE, s)
    tok = lambda width: pl.BlockSpec((1, tm, width), lambda bb, i: (bb, i, 0))
    dw = D_HEADS * HEAD_DIM
    res = [pl.BlockSpec((1, dil, tm // dil, dw), lambda bb, i: (bb, 0, i, 0)) for _, dil in D_PATTERNS]
    return pl.pallas_call(
        _merge_kernel,
        grid=(b, s // tm),
        in_specs=[tok(d), tok(512), tok(LRU_WIDTH), tok(LRU_WIDTH), tok(LRU_WIDTH), tok(512),
                  *res, *res, _layer(g, layer), _layer(w_all, layer), _layer(gbias, layer),
                  _layer(wa, layer), _layer(wb, layer), _layer(wc, layer), _layer(wd, layer), _layer(wo, layer)],
        out_specs=tok(d),
        out_shape=jax.ShapeDtypeStruct((b, s, d), F32),
        scratch_shapes=[pltpu.VMEM((4, dw // LANES, tm, LANES), F32)],
        compiler_params=_cparams(("parallel", "parallel")),
        name="merge",
    )(x, ya, hf, hb, by, yc, *o_d, *lse_d, g, w_all, gbias, wa, wb, wc, wd, wo)


def _mlp_kernel(x_ref, g_ref, w1_ref, w2_ref, gf_ref, out_ref, *, final_norm):
    x = x_ref[0]
    var = jnp.mean(x * x, axis=-1, keepdims=True)
    hn = (x * lax.rsqrt(var + EPS) * g_ref[...]).astype(BF16)
    hidden = w1_ref.shape[1]
    acc = x
    for c in range(hidden // MLP_CHUNK):
        h = jnp.dot(hn, w1_ref[:, c * MLP_CHUNK:(c + 1) * MLP_CHUNK], preferred_element_type=F32)
        h = jnp.square(jnp.maximum(h, 0.0)).astype(BF16)
        acc = acc + jnp.dot(h, w2_ref[c * MLP_CHUNK:(c + 1) * MLP_CHUNK, :], preferred_element_type=F32)
    if final_norm:
        var = jnp.mean(acc * acc, axis=-1, keepdims=True)
        acc = acc * lax.rsqrt(var + EPS) * gf_ref[...]
    out_ref[0] = acc


def _mlp(x, layer, g, w1, w2, g_final, final_norm):
    b, s, d = x.shape
    tm = min(TM_MLP, s)
    tok = pl.BlockSpec((1, tm, d), lambda bb, i: (bb, i, 0))
    return pl.pallas_call(
        functools.partial(_mlp_kernel, final_norm=final_norm),
        grid=(b, s // tm),
        in_specs=[tok, _layer(g, layer), _layer(w1, layer), _layer(w2, layer), _resident(g_final.shape)],
        out_specs=tok,
        out_shape=jax.ShapeDtypeStruct((b, s, d), F32),
        compiler_params=_cparams(("parallel", "parallel")),
        name="mlp",
    )(x, g, w1, w2, g_final)


def _dense_block_diag(w):
    nb, bw = w.shape[-3], w.shape[-2]
    eye = jnp.eye(nb, dtype=w.dtype)
    return jnp.einsum('...ncd,nm->...ncmd', w, eye).reshape(w.shape[:-3] + (nb * bw, nb * bw))


def kernel(x, norm_mix_g, w_in, gate_bias, qk_norm_g, conv_w, conv_b, lru_gate_w, lru_gate_b, lru_lambda, sink_logit, w_proj_a, w_proj_b, w_proj_c, w_proj_d, w_out, norm_mlp_g, w_mlp1, w_mlp2, norm_final_g):
    b, s, d = x.shape
    depth = w_in.shape[0]
    ax_t, seq_t = _rope_tables(s)
    dw = D_HEADS * HEAD_DIM
    n_groups = len(D_PATTERNS)
    bf = lambda w: w.astype(BF16)
    w_in_bf, wa, wb, wc, wd, wo, w1, w2 = map(bf, (w_in, w_proj_a, w_proj_b, w_proj_c, w_proj_d, w_out, w_mlp1, w_mlp2))
    g_mix, g_mlp, conv_b3 = norm_mix_g[:, None, :], norm_mlp_g[:, None, :], conv_b[:, None, :]
    qk_gain = jnp.tile(qk_norm_g, (1, 1, LANES // HEAD_DIM))
    gate_w_dense = bf(_dense_block_diag(lru_gate_w))

    for l in range(depth):
        outs = _inproj(x, l, g_mix, w_in_bf, qk_gain, ax_t, seq_t, conv_w, conv_b3, gate_w_dense,
                       lru_gate_b, lru_lambda)
        qta, ka, vta, af, uf, ab, ub, by, qc, kc, vc = outs[:11]
        qd, kd, vd = outs[11:11 + n_groups], outs[11 + n_groups:11 + 2 * n_groups], outs[11 + 2 * n_groups:11 + 3 * n_groups]

        ya = _attn_a(qta, ka, vta)

        hf, hb = _lru(af, uf, ab, ub)

        yc = _band_attention(qc[:, None], kc[:, None], vc[:, None], nkv=C_KV_HEADS, group=C_HEADS // C_KV_HEADS,
                             half_window=C_HALF_WINDOW, sink=sink_logit, sink_layer=l)[:, 0]

        o_d, lse_d = [], []
        for gi, (window, dil) in enumerate(D_PATTERNS):
            o, lse = _band_attention(qd[gi], kd[gi], vd[gi], nkv=D_HEADS, group=1,
                                     half_window=window // (2 * dil), want_lse=True)
            o_d.append(o)
            lse_d.append(lse)

        x = _merge(x, l, ya, hf, hb, by, yc, o_d, lse_d, g_mix, w_in_bf, gate_bias, wa, wb, wc, wd, wo)
        x = _mlp(x, l, g_mlp, w1, w2, norm_final_g[None, :], final_norm=(l == depth - 1))
    return x
```

```python
import functools
import math

import jax
import jax.numpy as jnp
from jax import lax
from jax.experimental import pallas as pl
from jax.experimental.pallas import tpu as pltpu

F32 = jnp.float32
BF16 = jnp.bfloat16

HEAD_DIM = 64
ROPE_THETA = 10000.0
GRID_W = 64
EPS = 1e-6
MASK_VALUE = -1e30
LOG2E = math.log2(math.e)

A_HEADS, A_KV_HEADS = 8, 2
LRU_WIDTH, LRU_BLOCKS, LRU_C = 512, 8, 8.0
CONV_W, CONV_LEFT = 4, 2
C_HEADS, C_KV_HEADS, C_HALF_WINDOW = 8, 2, 128
D_PATTERNS = ((128, 1), (512, 4), (2048, 16))
D_HEADS = 4
N_BRANCH = 4

LANES = 128
SUBLANES = 8
VMEM_LIMIT_BYTES = 56 * 1024 * 1024

TM_INPROJ = 512
TQ_ATTN_A = 256
SUB_ATTN_A = 128
TILES_PER_STEP_ATTN_A = 2
LOOKAHEAD_ATTN_A = 14
LOOKAHEAD_BAND = 6
LOOKAHEAD_INPROJ = 6
SUB_BAND = 128
CHUNKS_PER_BODY_ATTN_A = 16
VT_ROWS_A = HEAD_DIM + 16
TQ_BAND = 256
BAND_TILES_PER_STEP = 8
TT_LRU = 1024
TM_MERGE = 512
TM_MLP = 1024
MLP_CHUNK = 1024

Q_SCALE = HEAD_DIM ** -0.5 * LOG2E


def _cparams(sem):
    return pltpu.CompilerParams(dimension_semantics=sem, vmem_limit_bytes=VMEM_LIMIT_BYTES)


def _layer(arr, layer):
    zeros = (0,) * (arr.ndim - 1)
    return pl.BlockSpec((None,) + arr.shape[1:], lambda *_: (layer,) + zeros, pipeline_mode=pl.Buffered(1))


def _resident(shape):
    nd = len(shape)
    return pl.BlockSpec(shape, lambda *_: (0,) * nd, pipeline_mode=pl.Buffered(1))


def _rope_tables(seq):
    pos = jnp.arange(seq, dtype=F32)
    lane = jnp.arange(LANES)
    d = lane % HEAD_DIM

    inv = ROPE_THETA ** (-jnp.arange(0, HEAD_DIM, 2, dtype=F32) / HEAD_DIM)
    ang = pos[:, None] * inv[None, :]
    f = d % (HEAD_DIM // 2)
    cos_s, sin_s = jnp.cos(ang)[:, f], jnp.sin(ang)[:, f]
    lo = (d < HEAD_DIM // 2)[None, :]
    seq_t = (cos_s, jnp.where(lo, -sin_s, 0.0), jnp.where(lo, 0.0, sin_s))

    quarter = HEAD_DIM // 4
    inv_ax = ROPE_THETA ** (-jnp.arange(0, HEAD_DIM // 2, 2, dtype=F32) / (HEAD_DIM // 2))
    n_rows = seq // GRID_W
    ang_r = jnp.arange(n_rows, dtype=F32)[:, None] * inv_ax[None, :]
    ang_c = jnp.arange(GRID_W, dtype=F32)[:, None] * inv_ax[None, :]

    def expand(by_row, by_col):
        r = jnp.repeat(by_row, GRID_W, axis=0)
        c = jnp.tile(by_col, (n_rows, 1))
        return jnp.tile(jnp.concatenate([r, r, c, c], axis=1), (1, LANES // HEAD_DIM))

    cos_a, sin_a = expand(jnp.cos(ang_r), jnp.cos(ang_c)), expand(jnp.sin(ang_r), jnp.sin(ang_c))
    lo_a = (d % (HEAD_DIM // 2) < quarter)[None, :]
    ax_t = (cos_a, jnp.where(lo_a, -sin_a, 0.0), jnp.where(lo_a, 0.0, sin_a))
    return ax_t, seq_t


def _rotate(x, tables, half):
    cos, sin_a, sin_b = tables
    return x * cos + pltpu.roll(x, LANES - half, 1) * sin_a + pltpu.roll(x, half, 1) * sin_b


def _split_dot(x, m):
    hi = x.astype(BF16)
    lo = (x - hi.astype(F32)).astype(BF16)
    return jnp.dot(jnp.concatenate([hi, lo], axis=1), jnp.concatenate([m, m], axis=0),
                   preferred_element_type=F32)


def _head_mean_matrix():
    r = lax.broadcasted_iota(jnp.int32, (LANES, LANES), 0) // HEAD_DIM
    c = lax.broadcasted_iota(jnp.int32, (LANES, LANES), 1) // HEAD_DIM
    return jnp.where(r == c, 1.0 / HEAD_DIM, 0.0).astype(BF16)


def _head_rmsnorm(x, gain, mean_mat):
    var = _split_dot(x * x, mean_mat)
    return x * lax.rsqrt(var + EPS) * gain


IN_WIDTHS = (512, 128, 128, 512, 512, 512, 128, 128, 768, 768, 768)
IN_OFFSETS = tuple(sum(IN_WIDTHS[:i]) for i in range(len(IN_WIDTHS)))
N_IN = sum(IN_WIDTHS)


def _inproj_kernel(x_ref, xprev_ref, xnext_ref, g_ref, w_ref, qkg_ref,
                   ca_ref, saa_ref, sab_ref, cs_ref, ssa_ref, ssb_ref,
                   cw_ref, cb_ref, gw_ref, gb_ref, lam_ref,
                   qta_ref, ka_ref, vta_ref, af_ref, uf_ref, ab_ref, ub_ref, by_ref,
                   qc_ref, kc_ref, vc_ref, *rest):
    d_refs, perm_scr, xpad_scr = rest[:9], rest[9], rest[10]
    tile = pl.program_id(0)
    n_tiles = pl.num_programs(0)

    def normed(rows):
        var = jnp.mean(rows * rows, axis=-1, keepdims=True)
        return (rows * lax.rsqrt(var + EPS) * g_ref[...]).astype(BF16)

    x = x_ref[0]
    hn = normed(x)

    def proj(seg, lo=0, width=None):
        off = IN_OFFSETS[seg] + lo
        width = IN_WIDTHS[seg] if width is None else width
        return lambda: jnp.dot(hn, w_ref[:, off:off + width], preferred_element_type=F32)

    ax_t = (ca_ref[...], saa_ref[...], sab_ref[...])
    seq_t = (cs_ref[...], ssa_ref[...], ssb_ref[...])
    mean_mat = _head_mean_matrix()
    gq = qkg_ref[0:1, :]
    gk = qkg_ref[1:2, :]
    tm = x.shape[0]
    dw = D_HEADS * HEAD_DIM

    def finish_aq(acc):
        for j in range(IN_WIDTHS[0] // LANES):
            piece = _head_rmsnorm(acc[:, j * LANES:(j + 1) * LANES], gq, mean_mat)
            piece = _rotate(piece, ax_t, HEAD_DIM // 4) * Q_SCALE
            qta_ref[0, j * LANES:(j + 1) * LANES, :] = piece.T.astype(BF16)

    def finish_akv(acc):
        ak = _rotate(_head_rmsnorm(acc[:, :LANES], gk, mean_mat), ax_t, HEAD_DIM // 4).astype(BF16)
        avt = acc[:, LANES:].T.astype(BF16)
        pad_row = lax.broadcasted_iota(jnp.int32, (VT_ROWS_A - HEAD_DIM, tm), 0)
        ones_rows = (pad_row == 0).astype(BF16)
        for g in range(A_KV_HEADS):
            ka_ref[0, g] = ak[:, g * HEAD_DIM:(g + 1) * HEAD_DIM]
            vta_ref[0, g, 0] = jnp.concatenate([avt[g * HEAD_DIM:(g + 1) * HEAD_DIM, :], ones_rows], axis=0)

    lru = {}

    def finish_bx(acc):
        halo = jnp.dot(normed(jnp.concatenate([xprev_ref[0], xnext_ref[0]], axis=0)),
                       w_ref[:, IN_OFFSETS[3]:IN_OFFSETS[3] + LRU_WIDTH], preferred_element_type=F32)
        xpad_scr[0:SUBLANES] = jnp.where(tile > 0, halo[:SUBLANES], 0.0)
        xpad_scr[SUBLANES:SUBLANES + tm] = acc
        xpad_scr[SUBLANES + tm:] = jnp.where(tile < n_tiles - 1, halo[SUBLANES:], 0.0)
        xc = cb_ref[...]
        base = SUBLANES - CONV_LEFT
        for jj in range(CONV_W):
            xc = xc + xpad_scr[base + jj:base + jj + tm, :] * cw_ref[jj:jj + 1, :]
        lru["xc"] = xc
        lru["xcb"] = xc.astype(BF16)

    def gate_dots():
        return [jnp.dot(lru["xcb"], gw_ref[d, g], preferred_element_type=F32) for d in range(2) for g in range(2)]

    def finish_gates(dots):
        xc = lru["xc"]
        for d, (a_ref, u_ref) in enumerate(((af_ref, uf_ref), (ab_ref, ub_ref))):
            c2 = (-0.5 * LRU_C * LOG2E) * _softplus(-lam_ref[d:d + 1, :])
            a = jnp.exp2(c2 * jnp.tanh(0.5 * dots[2 * d] + 0.5 * gb_ref[d, 0:1, :]) + c2)
            gi = _sigmoid(dots[2 * d + 1] + gb_ref[d, 1:2, :])
            a_ref[0] = a
            z = (1.0 - a) * (1.0 + a)
            root = jnp.where(z > 0.0, z * lax.rsqrt(z), 0.0)
            u_ref[0] = root * (gi * xc)

    def finish_by(acc):
        by_ref[0] = acc.astype(BF16)

    def rope_piece(piece, scale):
        piece = _rotate(piece, seq_t, HEAD_DIM // 2)
        return piece if scale is None else piece * scale

    def finish_cq(acc):
        for j in range(IN_WIDTHS[5] // LANES):
            qc_ref[0, :, j * LANES:(j + 1) * LANES] = rope_piece(acc[:, j * LANES:(j + 1) * LANES], Q_SCALE).astype(BF16)

    def finish_ckv(acc):
        kc_ref[0] = rope_piece(acc[:, :LANES], None).astype(BF16)
        vc_ref[0] = acc[:, LANES:].astype(BF16)

    def finish_d(ref, dil, scale, rope):
        def finish(acc):
            for j in range(dw // LANES):
                piece = acc[:, j * LANES:(j + 1) * LANES]
                if rope:
                    piece = rope_piece(piece, scale)
                if dil == 1:
                    ref[0, 0, :, j * LANES:(j + 1) * LANES] = piece.astype(BF16)
                else:
                    perm_scr[j] = piece
            if dil > 1:
                for r in range(dil):
                    rows = [perm_scr[j, pl.ds(r, tm // dil, stride=dil), :] for j in range(dw // LANES)]
                    ref[0, r] = jnp.concatenate(rows, axis=1).astype(BF16)
        return finish

    segments = [(proj(0), finish_aq), (proj(1, 0, 2 * LANES), finish_akv), (proj(3), finish_bx),
                (proj(4), finish_by), (gate_dots, finish_gates), (proj(5), finish_cq),
                (proj(6, 0, 2 * LANES), finish_ckv)]
    for gi, (_, dil) in enumerate(D_PATTERNS):
        for ti, (seg, scale, rope) in enumerate(((8, Q_SCALE, True), (9, None, True), (10, None, False))):
            segments.append((proj(seg, gi * dw, dw), finish_d(d_refs[ti * len(D_PATTERNS) + gi], dil, scale, rope)))

    issued, pending = 0, []
    for idx, (_, finish) in enumerate(segments):
        horizon = min(len(segments), idx + 1 + LOOKAHEAD_INPROJ)
        while issued < horizon and (segments[issued][0] is not gate_dots or "xcb" in lru):
            pending.append(segments[issued][0]())
            issued += 1
        finish(pending.pop(0))


def _inproj(x, layer, g, w_all, qk_gain128, ax_t, seq_t, conv_w, conv_b, gate_w, gate_b, lam):
    b, s, d = x.shape
    tm = min(TM_INPROJ, s)
    nt = s // tm
    per, n8 = tm // SUBLANES, s // SUBLANES
    halo_prev = pl.BlockSpec((1, SUBLANES, d), lambda i, bb: (bb, jnp.maximum(i * per - 1, 0), 0))
    halo_next = pl.BlockSpec((1, SUBLANES, d), lambda i, bb: (bb, jnp.minimum((i + 1) * per, n8 - 1), 0))
    tok = lambda width: pl.BlockSpec((1, tm, width), lambda i, bb: (bb, i, 0))
    tab = pl.BlockSpec((tm, LANES), lambda i, bb: (i, 0))
    dw = D_HEADS * HEAD_DIM
    d_shapes = tuple(jax.ShapeDtypeStruct((b, dil, s // dil, dw), BF16) for _, dil in D_PATTERNS)
    d_specs = tuple(pl.BlockSpec((1, dil, tm // dil, dw), lambda i, bb: (bb, 0, i, 0)) for _, dil in D_PATTERNS)
    out_shape = (
        jax.ShapeDtypeStruct((b, A_HEADS * HEAD_DIM, s), BF16),
        jax.ShapeDtypeStruct((b, A_KV_HEADS, s, HEAD_DIM), BF16),
        jax.ShapeDtypeStruct((b, A_KV_HEADS, nt, VT_ROWS_A, tm), BF16),
        jax.ShapeDtypeStruct((b, s, LRU_WIDTH), F32),
        jax.ShapeDtypeStruct((b, s, LRU_WIDTH), F32),
        jax.ShapeDtypeStruct((b, s, LRU_WIDTH), F32),
        jax.ShapeDtypeStruct((b, s, LRU_WIDTH), F32),
        jax.ShapeDtypeStruct((b, s, LRU_WIDTH), BF16),
        jax.ShapeDtypeStruct((b, s, 512), BF16),
        jax.ShapeDtypeStruct((b, s, 128), BF16),
        jax.ShapeDtypeStruct((b, s, 128), BF16),
    ) + d_shapes * 3
    out_specs = (
        pl.BlockSpec((1, A_HEADS * HEAD_DIM, tm), lambda i, bb: (bb, 0, i)),
        pl.BlockSpec((1, A_KV_HEADS, tm, HEAD_DIM), lambda i, bb: (bb, 0, i, 0)),
        pl.BlockSpec((1, A_KV_HEADS, 1, VT_ROWS_A, tm), lambda i, bb: (bb, 0, i, 0, 0)),
        tok(LRU_WIDTH), tok(LRU_WIDTH), tok(LRU_WIDTH), tok(LRU_WIDTH), tok(LRU_WIDTH), tok(512), tok(128), tok(128),
    ) + d_specs * 3
    return pl.pallas_call(
        _inproj_kernel,
        grid=(nt, b),
        in_specs=[tok(d), halo_prev, halo_next, _layer(g, layer),
                  pl.BlockSpec((None, d, N_IN), lambda i, bb: (layer, 0, 0), pipeline_mode=pl.Buffered(1)),
                  _layer(qk_gain128, layer),
                  tab, tab, tab, tab, tab, tab,
                  _layer(conv_w, layer), _layer(conv_b, layer), _layer(gate_w, layer),
                  _layer(gate_b, layer), _layer(lam, layer)],
        out_specs=out_specs,
        out_shape=out_shape,
        scratch_shapes=[pltpu.VMEM((dw // LANES, tm, LANES), F32),
                        pltpu.VMEM((tm + 2 * SUBLANES, LRU_WIDTH), F32)],
        compiler_params=_cparams(("parallel", "parallel")),
        name="inproj",
    )(x, x, x, g, w_all, qk_gain128, *ax_t, *seq_t, conv_w, conv_b, gate_w, gate_b, lam)


def _sublane_allmax(x):
    for shift in (4, 2, 1):
        x = jnp.maximum(x, pltpu.roll(x, shift, 0))
    return x


def _attn_a_kernel(q_ref, k_ref, v_ref, o_ref, m_ref, acc_ref, *, group, sub):
    n_chunks, chunk = v_ref.shape[2], v_ref.shape[4]
    tq = min(TQ_ATTN_A, q_ref.shape[2])
    n_tiles = q_ref.shape[2] // tq
    m_ref[...] = jnp.full(m_ref.shape, MASK_VALUE, F32)
    acc_ref[...] = jnp.zeros(acc_ref.shape, F32)

    per_body = min(CHUNKS_PER_BODY_ATTN_A, n_chunks)
    n_sub = chunk // sub
    pair = 2 if n_sub % 2 == 0 else 1
    units = [(cc, u, h) for cc in range(per_body) for up in range(n_sub // pair)
             for h in range(n_tiles * group) for u in range(up * pair, (up + 1) * pair)]

    def body(it, carry):
        c0 = it * per_body

        def scores(idx):
            cc, u, h = units[idx]
            row0 = pl.multiple_of((c0 + cc) * chunk + u * sub, sub)
            k = k_ref[0, 0, pl.ds(row0, sub), :]
            j, head = divmod(h, group)
            qt = q_ref[0, head * HEAD_DIM:(head + 1) * HEAD_DIM, j * tq:(j + 1) * tq]
            return jnp.dot(k, qt, preferred_element_type=F32)

        pending = [scores(i) for i in range(LOOKAHEAD_ATTN_A)]
        probs = []
        for idx, (cc, u, h) in enumerate(units):
            if idx + LOOKAHEAD_ATTN_A < len(units):
                pending.append(scores(idx + LOOKAHEAD_ATTN_A))
            s3 = pending.pop(0).reshape(sub // SUBLANES, SUBLANES, tq)
            if not probs:
                m_start = m_ref[h]
                m_old = m_start
            m_new = jnp.maximum(m_old, _sublane_allmax(jnp.max(s3, axis=0)))
            if probs:
                beta = jnp.exp2(m_old - m_new).astype(BF16)
                beta = jnp.concatenate([beta, beta], axis=0)[None]
                probs = [(p.reshape(sub // 16, 16, tq) * beta).reshape(sub, tq) for p in probs]
            probs.append(jnp.exp2(s3 - m_new[None]).reshape(sub, tq).astype(BF16))
            m_old = m_new
            if len(probs) == pair:
                u0 = u - (pair - 1)
                vt = v_ref[0, 0, c0 + cc, :, u0 * sub:(u + 1) * sub]
                pv = jnp.dot(vt, jnp.concatenate(probs, axis=0), preferred_element_type=F32)
                alpha = jnp.exp2(m_start - m_new)
                acc_ref[h] = acc_ref[h] * alpha[None] + pv.reshape(VT_ROWS_A // SUBLANES, SUBLANES, tq)
                m_ref[h] = m_new
                probs = []
        return carry

    lax.fori_loop(0, n_chunks // per_body, body, 0)
    for j in range(n_tiles):
        outs = []
        for h in range(j * group, (j + 1) * group):
            acc = acc_ref[h].reshape(VT_ROWS_A, tq)
            outs.append(acc[:HEAD_DIM] / acc[HEAD_DIM:HEAD_DIM + 1])
        o_ref[0, j * tq:(j + 1) * tq, :] = jnp.concatenate(outs, axis=0).T.astype(o_ref.dtype)


def _attn_a(qt, k, vt):
    b, hd_all, s = qt.shape
    nkv, nt, chunk = vt.shape[1], vt.shape[2], vt.shape[4]
    group = hd_all // HEAD_DIM // nkv
    tq = min(TQ_ATTN_A * TILES_PER_STEP_ATTN_A, s)
    n_state = group * (tq // min(TQ_ATTN_A, s))
    gw = group * HEAD_DIM
    return pl.pallas_call(
        functools.partial(_attn_a_kernel, group=group, sub=min(SUB_ATTN_A, chunk)),
        grid=(b, nkv, s // tq),
        in_specs=[pl.BlockSpec((1, gw, tq), lambda bb, g, i: (bb, g, i)),
                  pl.BlockSpec((1, 1, s, HEAD_DIM), lambda bb, g, i: (bb, g, 0, 0)),
                  pl.BlockSpec((1, 1, nt, VT_ROWS_A, chunk), lambda bb, g, i: (bb, g, 0, 0, 0))],
        out_specs=pl.BlockSpec((1, tq, gw), lambda bb, g, i: (bb, i, g)),
        out_shape=jax.ShapeDtypeStruct((b, s, hd_all), BF16),
        scratch_shapes=[pltpu.VMEM((n_state, SUBLANES, min(TQ_ATTN_A, s)), F32),
                        pltpu.VMEM((n_state, VT_ROWS_A // SUBLANES, SUBLANES, min(TQ_ATTN_A, s)), F32)],
        compiler_params=_cparams(("parallel", "parallel", "parallel")),
        name="attn_a",
    )(qt, k, vt)


def _band_kernel(*refs, nkv, group, half_window, seq_len, sink_layer, want_lse):
    has_sink = sink_layer is not None
    q_ref, kp_ref, km_ref, kn_ref, vp_ref, vm_ref, vn_ref = refs[:7]
    rest = list(refs[7:])
    sink_ref = rest.pop(0) if has_sink else None
    o_ref = rest.pop(0)
    lse_ref = rest.pop(0) if want_lse else None

    n_seq_blk, tq_blk = q_ref.shape[1], q_ref.shape[2]
    tq = min(TQ_BAND, tq_blk)
    tiles_per_seq = tq_blk // tq
    split = lambda j: divmod(j, tiles_per_seq)
    halo = kp_ref.shape[2]
    nk = tq + 2 * halo
    sub = SUB_BAND if nk % SUB_BAND == 0 else halo
    n_heads = nkv * group
    blk0 = pl.program_id(2) * tq_blk

    pad_row = lax.broadcasted_iota(jnp.int32, (VT_ROWS_A - HEAD_DIM, tq_blk + 2 * halo), 0)
    ones_rows = (pad_row == 0).astype(q_ref.dtype)
    kcat, vt_ext = [], []
    for sq in range(n_seq_blk):
        kcat.append(jnp.concatenate([kp_ref[0, sq], km_ref[0, sq], kn_ref[0, sq]], axis=0))
        vt = jnp.concatenate([vp_ref[0, sq], vm_ref[0, sq], vn_ref[0, sq]], axis=0).T
        vt_ext.append([jnp.concatenate([vt[g * HEAD_DIM:(g + 1) * HEAD_DIM], ones_rows], axis=0)
                       for g in range(nkv)])

    kw = min(LANES, nkv * HEAD_DIM)
    zeros = jnp.zeros((HEAD_DIM, tq), q_ref.dtype)
    q_pads = {}

    def padded_queries(j):
        if j not in q_pads:
            sq, jt = split(j)
            qt = q_ref[0, sq, jt * tq:(jt + 1) * tq, :].T
            pads = []
            for h in range(n_heads):
                slot = ((h // group) * HEAD_DIM % kw) // HEAD_DIM
                parts = [zeros] * (kw // HEAD_DIM)
                parts[slot] = qt[h * HEAD_DIM:(h + 1) * HEAD_DIM, :]
                pads.append(parts[0] if len(parts) == 1 else jnp.concatenate(parts, axis=0))
            q_pads.clear()
            q_pads[j] = pads
        return q_pads[j]

    units = [(j, kb, h) for j in range(n_seq_blk * tiles_per_seq) for kb in range(nk // sub)
             for h in range(n_heads)]

    cw = LANES if tq % LANES == 0 else tq
    n_ch = tq // cw

    def query_chunks(kb):
        k_lo = kb * sub - halo
        lo = max(0, (k_lo - half_window) // cw)
        hi = min(n_ch, -(-(k_lo + sub + half_window) // cw))
        return list(range(lo, hi))

    def scores(idx):
        j, kb, h = units[idx]
        sq, jt = split(j)
        row0 = jt * tq + kb * sub
        lane0 = (h // group) * HEAD_DIM // kw * kw
        cs = query_chunks(kb)
        return jnp.dot(kcat[sq][row0:row0 + sub, lane0:lane0 + kw],
                       padded_queries(j)[h][:, cs[0] * cw:(cs[-1] + 1) * cw], preferred_element_type=F32)

    key_row = lax.broadcasted_iota(jnp.int32, (sub, cw), 0)
    query = lax.broadcasted_iota(jnp.int32, (sub, cw), 1)

    def needs_mask(kb, c):
        k_lo = kb * sub - halo
        in_band = k_lo - (c * cw + cw - 1) >= -half_window and k_lo + sub - 1 - c * cw <= half_window
        return not (in_band and k_lo >= 0 and k_lo + sub <= tq)

    def valid_mask(j, kb, c):
        q0 = blk0 + split(j)[1] * tq
        kpos = q0 - halo + kb * sub + key_row
        rel = kpos - (q0 + c * cw + query)
        return (jnp.abs(rel) <= half_window) & (kpos >= 0) & (kpos < seq_len)

    def start_state():
        shape = (VT_ROWS_A // SUBLANES, SUBLANES, cw)
        if has_sink:
            m0 = [jnp.full((SUBLANES, cw), sink_ref[sink_layer, h] * LOG2E, F32) for h in range(n_heads)]
            denom = ((lax.broadcasted_iota(jnp.int32, shape, 0) == HEAD_DIM // SUBLANES)
                     & (lax.broadcasted_iota(jnp.int32, shape, 1) == 0))
            a0 = denom.astype(F32)
        else:
            m0 = [jnp.full((SUBLANES, cw), 0.1 * MASK_VALUE, F32)] * n_heads
            a0 = jnp.zeros(shape, F32)
        return [[m0[h]] * n_ch for h in range(n_heads)], [[a0] * n_ch for _ in range(n_heads)]

    def join(parts):
        return parts[0] if len(parts) == 1 else jnp.concatenate(parts, axis=1)

    def finish(j, m, acc):
        outs, lses = [], []
        for h in range(n_heads):
            full = join([x.reshape(VT_ROWS_A, cw) for x in acc[h]])
            l_row = full[HEAD_DIM:HEAD_DIM + 1]
            outs.append(full[:HEAD_DIM] / l_row)
            if want_lse:
                lse = join([x[0:1] for x in m[h]]) + jnp.log2(l_row)
                lses.append(jnp.broadcast_to(lse, (HEAD_DIM, tq)))
        sq, jt = split(j)
        o_ref[0, sq, jt * tq:(jt + 1) * tq, :] = jnp.concatenate(outs, axis=0).T.astype(o_ref.dtype)
        if want_lse:
            lse_ref[0, sq, jt * tq:(jt + 1) * tq, :] = jnp.concatenate(lses, axis=0).T

    look = min(LOOKAHEAD_BAND, len(units))
    pending = [scores(i) for i in range(look)]
    mask_key, masks = None, None
    for idx, (j, kb, h) in enumerate(units):
        if idx + look < len(units):
            pending.append(scores(idx + look))
        if kb == 0 and h == 0:
            m, acc = start_state()
        cs = query_chunks(kb)
        if mask_key != (j, kb):
            mask_key = (j, kb)
            masks = [valid_mask(j, kb, c) if needs_mask(kb, c) else None for c in cs]
        g = h // group
        s = pending.pop(0)
        probs, alphas = [], []
        for i, c in enumerate(cs):
            s_c = s if len(cs) == 1 else s[:, i * cw:(i + 1) * cw]
            if masks[i] is not None:
                s_c = jnp.where(masks[i], s_c, MASK_VALUE)
            s3 = s_c.reshape(sub // SUBLANES, SUBLANES, cw)
            m_new = jnp.maximum(m[h][c], _sublane_allmax(jnp.max(s3, axis=0)))
            alpha = jnp.exp2(m[h][c] - m_new)
            p3 = jnp.exp2(s3 - m_new[None])
            m[h][c] = m_new
            probs.append(p3.reshape(sub, cw).astype(BF16))
            alphas.append(alpha)
        sq, jt = split(j)
        col0 = jt * tq + kb * sub
        pv = jnp.dot(vt_ext[sq][g][:, col0:col0 + sub], join(probs), preferred_element_type=F32)
        for i, c in enumerate(cs):
            pv_c = pv if len(cs) == 1 else pv[:, i * cw:(i + 1) * cw]
            acc[h][c] = acc[h][c] * alphas[i][None] + pv_c.reshape(VT_ROWS_A // SUBLANES, SUBLANES, cw)
        if kb == nk // sub - 1 and h == n_heads - 1:
            finish(j, m, acc)


def _band_attention(q, k, v, *, nkv, group, half_window, sink=None, sink_layer=None, want_lse=False):
    b, n_seq, seq_len, wq = q.shape
    wk = nkv * HEAD_DIM
    halo = half_window
    tq = min(TQ_BAND * BAND_TILES_PER_STEP, seq_len)
    ns = max(1, min(n_seq, BAND_TILES_PER_STEP * TQ_BAND // tq))
    per = tq // halo
    n_halo = seq_len // halo

    main = lambda width: pl.BlockSpec((1, ns, tq, width), lambda bb, r, i: (bb, r, i, 0))
    prev = lambda width: pl.BlockSpec(
        (1, ns, halo, width), lambda bb, r, i: (bb, r, jnp.maximum(i * per - 1, 0), 0))
    nxt = lambda width: pl.BlockSpec(
        (1, ns, halo, width), lambda bb, r, i: (bb, r, jnp.minimum((i + 1) * per, n_halo - 1), 0))
    in_specs = [main(wq), prev(wk), main(wk), nxt(wk), prev(wk), main(wk), nxt(wk)]
    args = [q, k, k, k, v, v, v]
    if sink is not None:
        in_specs.append(pl.BlockSpec(memory_space=pltpu.SMEM))
        args.append(sink)
    out_block = main(wq)
    out_shape = [jax.ShapeDtypeStruct(q.shape, BF16)]
    out_specs = [out_block]
    if want_lse:
        out_shape.append(jax.ShapeDtypeStruct(q.shape, F32))
        out_specs.append(out_block)
    res = pl.pallas_call(
        functools.partial(_band_kernel, nkv=nkv, group=group, half_window=half_window,
                          seq_len=seq_len, sink_layer=sink_layer if sink is not None else None,
                          want_lse=want_lse),
        grid=(b, n_seq // ns, seq_len // tq),
        in_specs=in_specs,
        out_specs=out_specs,
        out_shape=out_shape,
        compiler_params=_cparams(("parallel", "parallel", "parallel")),
        name="band_attn",
    )(*args)
    return res if want_lse else res[0]


def _sigmoid(x):
    return 0.5 * jnp.tanh(0.5 * x) + 0.5


def _softplus(x):
    return jnp.maximum(x, 0.0) + jnp.log1p(jnp.exp(-jnp.abs(x)))


def _lru_kernel(af_ref, uf_ref, ab_ref, ub_ref, hf_ref, hb_ref, carry_ref):
    i = pl.program_id(1)
    tt = af_ref.shape[1]
    nblk = tt // SUBLANES

    @pl.when(i == 0)
    def _():
        carry_ref[...] = jnp.zeros(carry_ref.shape, F32)

    row = lax.broadcasted_iota(jnp.int32, (SUBLANES, LRU_WIDTH), 0)

    def scan_block(d, blk, carry, a_ref, u_ref, out_ref):
        a8 = a_ref[0, pl.ds(blk * SUBLANES, SUBLANES), :]
        u8 = u_ref[0, pl.ds(blk * SUBLANES, SUBLANES), :]
        entry = row == (0 if d == 0 else SUBLANES - 1)
        u8 = u8 + jnp.where(entry, a8 * carry, 0.0)
        for step in (1, 2, 4):
            if d == 0:
                shift, ident = step, row < step
            else:
                shift, ident = SUBLANES - step, row >= SUBLANES - step
            u8 = u8 + a8 * jnp.where(ident, 0.0, pltpu.roll(u8, shift, 0))
            if step < 4:
                a8 = a8 * pltpu.roll(a8, shift, 0)
        out_ref[0, pl.ds(blk * SUBLANES, SUBLANES), :] = u8
        last = u8[SUBLANES - 1:SUBLANES, :] if d == 0 else u8[0:1, :]
        return jnp.broadcast_to(last, (SUBLANES, LRU_WIDTH))

    def body(kb, carries):
        cf, cb = carries
        cf = scan_block(0, kb, cf, af_ref, uf_ref, hf_ref)
        cb = scan_block(1, nblk - 1 - kb, cb, ab_ref, ub_ref, hb_ref)
        return cf, cb

    cf, cb = lax.fori_loop(0, nblk, body, (carry_ref[0], carry_ref[1]))
    carry_ref[0] = cf
    carry_ref[1] = cb


def _lru(af, uf, ab, ub):
    b, s, w = af.shape
    tt = min(TT_LRU, s)
    nt = s // tt
    fwd = pl.BlockSpec((1, tt, w), lambda bb, i: (bb, i, 0))
    bwd = pl.BlockSpec((1, tt, w), lambda bb, i: (bb, nt - 1 - i, 0))
    return pl.pallas_call(
        _lru_kernel,
        grid=(b, nt),
        in_specs=[fwd, fwd, bwd, bwd],
        out_specs=[fwd, bwd],
        out_shape=[jax.ShapeDtypeStruct((b, s, w), F32), jax.ShapeDtypeStruct((b, s, w), F32)],
        scratch_shapes=[pltpu.VMEM((2, SUBLANES, w), F32)],
        compiler_params=_cparams(("parallel", "arbitrary")),
        name="lru",
    )(af, uf, ab, ub)


def _gelu_tanh(x):
    return 0.5 * x * (1.0 + jnp.tanh(math.sqrt(2.0 / math.pi) * (x + 0.044715 * (x * x * x))))


def _merge_kernel(x_ref, ya_ref, hf_ref, hb_ref, by_ref, yc_ref,
                  o0_ref, o1_ref, o2_ref, l0_ref, l1_ref, l2_ref, g_ref, wg_ref, gbias_ref,
                  wa_ref, wb_ref, wc_ref, wd_ref, wo_ref, out_ref, perm_scr):
    d_model = x_ref.shape[2]
    tm = x_ref.shape[1]
    x = x_ref[0]
    var = jnp.mean(x * x, axis=-1, keepdims=True)
    hn = (x * lax.rsqrt(var + EPS) * g_ref[...]).astype(BF16)

    def natural(ref, slot):
        dil = ref.shape[1]
        if dil == 1:
            return ref[0, 0].astype(F32)
        for r in range(dil):
            rows = ref[0, r].astype(F32)
            for j in range(rows.shape[1] // LANES):
                perm_scr[slot, j, pl.ds(r, tm // dil, stride=dil), :] = rows[:, j * LANES:(j + 1) * LANES]
        return jnp.concatenate([perm_scr[slot, j] for j in range(perm_scr.shape[1])], axis=1)

    yb = ((hf_ref[0] + hb_ref[0]) * _gelu_tanh(by_ref[0].astype(F32))).astype(BF16)

    l0, l1, l2 = natural(l0_ref, 0), natural(l1_ref, 0), natural(l2_ref, 1)
    o0, o1, o2 = natural(o0_ref, 2), natural(o1_ref, 2), natural(o2_ref, 3)
    mx = jnp.maximum(jnp.maximum(l0, l1), l2)
    e0, e1, e2 = jnp.exp2(l0 - mx), jnp.exp2(l1 - mx), jnp.exp2(l2 - mx)
    yd = ((e0 * o0 + e1 * o1 + e2 * o2) / (e0 + e1 + e2)).astype(BF16)

    def gate_logits(kk):
        off = wg_ref.shape[1] - N_BRANCH * d_model + kk * d_model
        return lambda: jnp.dot(hn, wg_ref[:, off:off + d_model], preferred_element_type=F32)

    def branch(y, w_ref):
        return lambda: jnp.dot(y, w_ref[...], preferred_element_type=F32)

    dots = []
    for kk, (y, w_ref) in enumerate(((ya_ref[0], wa_ref), (yb, wb_ref), (yc_ref[0], wc_ref), (yd, wd_ref))):
        dots += [gate_logits(kk), branch(y, w_ref)]
    pending = [dots[0](), dots[1]()]
    merged = None
    for kk in range(N_BRANCH):
        pending += [d() for d in dots[2 * kk + 2:2 * kk + 4]]
        logits, proj = pending.pop(0), pending.pop(0)
        term = _sigmoid(logits + gbias_ref[kk:kk + 1, :]) * proj
        merged = term if merged is None else merged + term

    out_ref[0] = x + jnp.dot(merged.astype(BF16), wo_ref[...], preferred_element_type=F32)


def _merge(x, layer, ya, hf, hb, by, yc, o_d, lse_d, g, w_all, gbias, wa, wb, wc, wd, wo):
    b, s, d = x.shape
    tm = min(TM_MERGE, s)
    tok = lambda width: pl.BlockSpec((1, tm, width), lambda bb, i: (bb, i, 0))
    dw = D_HEADS * HEAD_DIM
    res = [pl.BlockSpec((1, dil, tm // dil, dw), lambda bb, i: (bb, 0, i, 0)) for _, dil in D_PATTERNS]
    return pl.pallas_call(
        _merge_kernel,
        grid=(b, s // tm),
        in_specs=[tok(d), tok(512), tok(LRU_WIDTH), tok(LRU_WIDTH), tok(LRU_WIDTH), tok(512),
                  *res, *res, _layer(g, layer), _layer(w_all, layer), _layer(gbias, layer),
                  _layer(wa, layer), _layer(wb, layer), _layer(wc, layer), _layer(wd, layer), _layer(wo, layer)],
        out_specs=tok(d),
        out_shape=jax.ShapeDtypeStruct((b, s, d), F32),
        scratch_shapes=[pltpu.VMEM((4, dw // LANES, tm, LANES), F32)],
        compiler_params=_cparams(("parallel", "parallel")),
        name="merge",
    )(x, ya, hf, hb, by, yc, *o_d, *lse_d, g, w_all, gbias, wa, wb, wc, wd, wo)


def _mlp_kernel(x_ref, g_ref, w1_ref, w2_ref, gf_ref, out_ref, *, final_norm):
    x = x_ref[0]
    var = jnp.mean(x * x, axis=-1, keepdims=True)
    hn = (x * lax.rsqrt(var + EPS) * g_ref[...]).astype(BF16)
    hidden = w1_ref.shape[1]
    acc = x
    for c in range(hidden // MLP_CHUNK):
        h = jnp.dot(hn, w1_ref[:, c * MLP_CHUNK:(c + 1) * MLP_CHUNK], preferred_element_type=F32)
        h = jnp.square(jnp.maximum(h, 0.0)).astype(BF16)
        acc = acc + jnp.dot(h, w2_ref[c * MLP_CHUNK:(c + 1) * MLP_CHUNK, :], preferred_element_type=F32)
    if final_norm:
        var = jnp.mean(acc * acc, axis=-1, keepdims=True)
        acc = acc * lax.rsqrt(var + EPS) * gf_ref[...]
    out_ref[0] = acc


def _mlp(x, layer, g, w1, w2, g_final, final_norm):
    b, s, d = x.shape
    tm = min(TM_MLP, s)
    tok = pl.BlockSpec((1, tm, d), lambda bb, i: (bb, i, 0))
    return pl.pallas_call(
        functools.partial(_mlp_kernel, final_norm=final_norm),
        grid=(b, s // tm),
        in_specs=[tok, _layer(g, layer), _layer(w1, layer), _layer(w2, layer), _resident(g_final.shape)],
        out_specs=tok,
        out_shape=jax.ShapeDtypeStruct((b, s, d), F32),
        compiler_params=_cparams(("parallel", "parallel")),
        name="mlp",
    )(x, g, w1, w2, g_final)


def _dense_block_diag(w):
    nb, bw = w.shape[-3], w.shape[-2]
    eye = jnp.eye(nb, dtype=w.dtype)
    return jnp.einsum('...ncd,nm->...ncmd', w, eye).reshape(w.shape[:-3] + (nb * bw, nb * bw))


def kernel(x, norm_mix_g, w_in, gate_bias, qk_norm_g, conv_w, conv_b, lru_gate_w, lru_gate_b, lru_lambda, sink_logit, w_proj_a, w_proj_b, w_proj_c, w_proj_d, w_out, norm_mlp_g, w_mlp1, w_mlp2, norm_final_g):
    b, s, d = x.shape
    depth = w_in.shape[0]
    ax_t, seq_t = _rope_tables(s)
    dw = D_HEADS * HEAD_DIM
    n_groups = len(D_PATTERNS)
    bf = lambda w: w.astype(BF16)
    w_in_bf, wa, wb, wc, wd, wo, w1, w2 = map(bf, (w_in, w_proj_a, w_proj_b, w_proj_c, w_proj_d, w_out, w_mlp1, w_mlp2))
    g_mix, g_mlp, conv_b3 = norm_mix_g[:, None, :], norm_mlp_g[:, None, :], conv_b[:, None, :]
    qk_gain = jnp.tile(qk_norm_g, (1, 1, LANES // HEAD_DIM))
    gate_w_dense = bf(_dense_block_diag(lru_gate_w))

    for l in range(depth):
        outs = _inproj(x, l, g_mix, w_in_bf, qk_gain, ax_t, seq_t, conv_w, conv_b3, gate_w_dense,
                       lru_gate_b, lru_lambda)
        qta, ka, vta, af, uf, ab, ub, by, qc, kc, vc = outs[:11]
        qd, kd, vd = outs[11:11 + n_groups], outs[11 + n_groups:11 + 2 * n_groups], outs[11 + 2 * n_groups:11 + 3 * n_groups]

        ya = _attn_a(qta, ka, vta)

        hf, hb = _lru(af, uf, ab, ub)

        yc = _band_attention(qc[:, None], kc[:, None], vc[:, None], nkv=C_KV_HEADS, group=C_HEADS // C_KV_HEADS,
                             half_window=C_HALF_WINDOW, sink=sink_logit, sink_layer=l)[:, 0]

        o_d, lse_d = [], []
        for gi, (window, dil) in enumerate(D_PATTERNS):
            o, lse = _band_attention(qd[gi], kd[gi], vd[gi], nkv=D_HEADS, group=1,
                                     half_window=window // (2 * dil), want_lse=True)
            o_d.append(o)
            lse_d.append(lse)

        x = _merge(x, l, ya, hf, hb, by, yc, o_d, lse_d, g_mix, w_in_bf, gate_bias, wa, wb, wc, wd, wo)
        x = _mlp(x, l, g_mlp, w1, w2, norm_final_g[None, :], final_norm=(l == depth - 1))
    return x
```

```python
import functools
import math

import jax
import jax.numpy as jnp
from jax import lax
from jax.experimental import pallas as pl
from jax.experimental.pallas import tpu as pltpu

F32 = jnp.float32
BF16 = jnp.bfloat16

HEAD_DIM = 64
ROPE_THETA = 10000.0
GRID_W = 64
EPS = 1e-6
MASK_VALUE = -1e30
LOG2E = math.log2(math.e)

A_HEADS, A_KV_HEADS = 8, 2
LRU_WIDTH, LRU_BLOCKS, LRU_C = 512, 8, 8.0
CONV_W, CONV_LEFT = 4, 2
C_HEADS, C_KV_HEADS, C_HALF_WINDOW = 8, 2, 128
D_PATTERNS = ((128, 1), (512, 4), (2048, 16))
D_HEADS = 4
N_BRANCH = 4

LANES = 128
SUBLANES = 8
VMEM_LIMIT_BYTES = 56 * 1024 * 1024

TM_INPROJ = 512
TQ_ATTN_A = 256
SUB_ATTN_A = 128
TILES_PER_STEP_ATTN_A = 2
LOOKAHEAD_ATTN_A = 14
LOOKAHEAD_BAND = 6
LOOKAHEAD_INPROJ = 6
SUB_BAND = 128
CHUNKS_PER_BODY_ATTN_A = 16
VT_ROWS_A = HEAD_DIM + 16
TQ_BAND = 256
BAND_TILES_PER_STEP = 16
TT_LRU = 1024
TM_MERGE = 512
TM_MLP = 1024
MLP_CHUNK = 1024

Q_SCALE = HEAD_DIM ** -0.5 * LOG2E


def _cparams(sem):
    return pltpu.CompilerParams(dimension_semantics=sem, vmem_limit_bytes=VMEM_LIMIT_BYTES)


def _layer(arr, layer):
    zeros = (0,) * (arr.ndim - 1)
    return pl.BlockSpec((None,) + arr.shape[1:], lambda *_: (layer,) + zeros, pipeline_mode=pl.Buffered(1))


def _resident(shape):
    nd = len(shape)
    return pl.BlockSpec(shape, lambda *_: (0,) * nd, pipeline_mode=pl.Buffered(1))


def _rope_tables(seq):
    pos = jnp.arange(seq, dtype=F32)
    lane = jnp.arange(LANES)
    d = lane % HEAD_DIM

    inv = ROPE_THETA ** (-jnp.arange(0, HEAD_DIM, 2, dtype=F32) / HEAD_DIM)
    ang = pos[:, None] * inv[None, :]
    f = d % (HEAD_DIM // 2)
    cos_s, sin_s = jnp.cos(ang)[:, f], jnp.sin(ang)[:, f]
    lo = (d < HEAD_DIM // 2)[None, :]
    seq_t = (cos_s, jnp.where(lo, -sin_s, 0.0), jnp.where(lo, 0.0, sin_s))

    quarter = HEAD_DIM // 4
    inv_ax = ROPE_THETA ** (-jnp.arange(0, HEAD_DIM // 2, 2, dtype=F32) / (HEAD_DIM // 2))
    n_rows = seq // GRID_W
    ang_r = jnp.arange(n_rows, dtype=F32)[:, None] * inv_ax[None, :]
    ang_c = jnp.arange(GRID_W, dtype=F32)[:, None] * inv_ax[None, :]

    def expand(by_row, by_col):
        r = jnp.repeat(by_row, GRID_W, axis=0)
        c = jnp.tile(by_col, (n_rows, 1))
        return jnp.tile(jnp.concatenate([r, r, c, c], axis=1), (1, LANES // HEAD_DIM))

    cos_a, sin_a = expand(jnp.cos(ang_r), jnp.cos(ang_c)), expand(jnp.sin(ang_r), jnp.sin(ang_c))
    lo_a = (d % (HEAD_DIM // 2) < quarter)[None, :]
    ax_t = (cos_a, jnp.where(lo_a, -sin_a, 0.0), jnp.where(lo_a, 0.0, sin_a))
    return ax_t, seq_t


def _rotate(x, tables, half):
    cos, sin_a, sin_b = tables
    return x * cos + pltpu.roll(x, LANES - half, 1) * sin_a + pltpu.roll(x, half, 1) * sin_b


def _split_dot(x, m):
    hi = x.astype(BF16)
    lo = (x - hi.astype(F32)).astype(BF16)
    return jnp.dot(jnp.concatenate([hi, lo], axis=1), jnp.concatenate([m, m], axis=0),
                   preferred_element_type=F32)


def _head_mean_matrix():
    r = lax.broadcasted_iota(jnp.int32, (LANES, LANES), 0) // HEAD_DIM
    c = lax.broadcasted_iota(jnp.int32, (LANES, LANES), 1) // HEAD_DIM
    return jnp.where(r == c, 1.0 / HEAD_DIM, 0.0).astype(BF16)


def _head_rmsnorm(x, gain, mean_mat):
    var = _split_dot(x * x, mean_mat)
    return x * lax.rsqrt(var + EPS) * gain


IN_WIDTHS = (512, 128, 128, 512, 512, 512, 128, 128, 768, 768, 768)
IN_OFFSETS = tuple(sum(IN_WIDTHS[:i]) for i in range(len(IN_WIDTHS)))
N_IN = sum(IN_WIDTHS)


def _inproj_kernel(x_ref, xprev_ref, xnext_ref, g_ref, w_ref, qkg_ref,
                   ca_ref, saa_ref, sab_ref, cs_ref, ssa_ref, ssb_ref,
                   cw_ref, cb_ref, gw_ref, gb_ref, lam_ref,
                   qta_ref, ka_ref, vta_ref, af_ref, uf_ref, ab_ref, ub_ref, by_ref,
                   qc_ref, kc_ref, vc_ref, *rest):
    d_refs, perm_scr, xpad_scr = rest[:9], rest[9], rest[10]
    tile = pl.program_id(0)
    n_tiles = pl.num_programs(0)

    def normed(rows):
        var = jnp.mean(rows * rows, axis=-1, keepdims=True)
        return (rows * lax.rsqrt(var + EPS) * g_ref[...]).astype(BF16)

    x = x_ref[0]
    hn = normed(x)

    def proj(seg, lo=0, width=None):
        off = IN_OFFSETS[seg] + lo
        width = IN_WIDTHS[seg] if width is None else width
        return lambda: jnp.dot(hn, w_ref[:, off:off + width], preferred_element_type=F32)

    ax_t = (ca_ref[...], saa_ref[...], sab_ref[...])
    seq_t = (cs_ref[...], ssa_ref[...], ssb_ref[...])
    mean_mat = _head_mean_matrix()
    gq = qkg_ref[0:1, :]
    gk = qkg_ref[1:2, :]
    tm = x.shape[0]
    dw = D_HEADS * HEAD_DIM

    def finish_aq(acc):
        for j in range(IN_WIDTHS[0] // LANES):
            piece = _head_rmsnorm(acc[:, j * LANES:(j + 1) * LANES], gq, mean_mat)
            piece = _rotate(piece, ax_t, HEAD_DIM // 4) * Q_SCALE
            qta_ref[0, j * LANES:(j + 1) * LANES, :] = piece.T.astype(BF16)

    def finish_akv(acc):
        ak = _rotate(_head_rmsnorm(acc[:, :LANES], gk, mean_mat), ax_t, HEAD_DIM // 4).astype(BF16)
        avt = acc[:, LANES:].T.astype(BF16)
        pad_row = lax.broadcasted_iota(jnp.int32, (VT_ROWS_A - HEAD_DIM, tm), 0)
        ones_rows = (pad_row == 0).astype(BF16)
        for g in range(A_KV_HEADS):
            ka_ref[0, g] = ak[:, g * HEAD_DIM:(g + 1) * HEAD_DIM]
            vta_ref[0, g, 0] = jnp.concatenate([avt[g * HEAD_DIM:(g + 1) * HEAD_DIM, :], ones_rows], axis=0)

    lru = {}

    def finish_bx(acc):
        halo = jnp.dot(normed(jnp.concatenate([xprev_ref[0], xnext_ref[0]], axis=0)),
                       w_ref[:, IN_OFFSETS[3]:IN_OFFSETS[3] + LRU_WIDTH], preferred_element_type=F32)
        xpad_scr[0:SUBLANES] = jnp.where(tile > 0, halo[:SUBLANES], 0.0)
        xpad_scr[SUBLANES:SUBLANES + tm] = acc
        xpad_scr[SUBLANES + tm:] = jnp.where(tile < n_tiles - 1, halo[SUBLANES:], 0.0)
        xc = cb_ref[...]
        base = SUBLANES - CONV_LEFT
        for jj in range(CONV_W):
            xc = xc + xpad_scr[base + jj:base + jj + tm, :] * cw_ref[jj:jj + 1, :]
        lru["xc"] = xc
        lru["xcb"] = xc.astype(BF16)

    def gate_dots():
        return [jnp.dot(lru["xcb"], gw_ref[d, g], preferred_element_type=F32) for d in range(2) for g in range(2)]

    def finish_gates(dots):
        xc = lru["xc"]
        for d, (a_ref, u_ref) in enumerate(((af_ref, uf_ref), (ab_ref, ub_ref))):
            c2 = (-0.5 * LRU_C * LOG2E) * _softplus(-lam_ref[d:d + 1, :])
            a = jnp.exp2(c2 * jnp.tanh(0.5 * dots[2 * d] + 0.5 * gb_ref[d, 0:1, :]) + c2)
            gi = _sigmoid(dots[2 * d + 1] + gb_ref[d, 1:2, :])
            a_ref[0] = a
            z = (1.0 - a) * (1.0 + a)
            root = jnp.where(z > 0.0, z * lax.rsqrt(z), 0.0)
            u_ref[0] = root * (gi * xc)

    def finish_by(acc):
        by_ref[0] = acc.astype(BF16)

    def rope_piece(piece, scale):
        piece = _rotate(piece, seq_t, HEAD_DIM // 2)
        return piece if scale is None else piece * scale

    def finish_cq(acc):
        for j in range(IN_WIDTHS[5] // LANES):
            qc_ref[0, :, j * LANES:(j + 1) * LANES] = rope_piece(acc[:, j * LANES:(j + 1) * LANES], Q_SCALE).astype(BF16)

    def finish_ckv(acc):
        kc_ref[0] = rope_piece(acc[:, :LANES], None).astype(BF16)
        vc_ref[0] = acc[:, LANES:].astype(BF16)

    def finish_d(ref, dil, scale, rope):
        def finish(acc):
            for j in range(dw // LANES):
                piece = acc[:, j * LANES:(j + 1) * LANES]
                if rope:
                    piece = rope_piece(piece, scale)
                if dil == 1:
                    ref[0, 0, :, j * LANES:(j + 1) * LANES] = piece.astype(BF16)
                else:
                    perm_scr[j] = piece
            if dil > 1:
                for r in range(dil):
                    rows = [perm_scr[j, pl.ds(r, tm // dil, stride=dil), :] for j in range(dw // LANES)]
                    ref[0, r] = jnp.concatenate(rows, axis=1).astype(BF16)
        return finish

    segments = [(proj(0), finish_aq), (proj(1, 0, 2 * LANES), finish_akv), (proj(3), finish_bx),
                (proj(4), finish_by), (gate_dots, finish_gates), (proj(5), finish_cq),
                (proj(6, 0, 2 * LANES), finish_ckv)]
    for gi, (_, dil) in enumerate(D_PATTERNS):
        for ti, (seg, scale, rope) in enumerate(((8, Q_SCALE, True), (9, None, True), (10, None, False))):
            segments.append((proj(seg, gi * dw, dw), finish_d(d_refs[ti * len(D_PATTERNS) + gi], dil, scale, rope)))

    issued, pending = 0, []
    for idx, (_, finish) in enumerate(segments):
        horizon = min(len(segments), idx + 1 + LOOKAHEAD_INPROJ)
        while issued < horizon and (segments[issued][0] is not gate_dots or "xcb" in lru):
            pending.append(segments[issued][0]())
            issued += 1
        finish(pending.pop(0))


def _inproj(x, layer, g, w_all, qk_gain128, ax_t, seq_t, conv_w, conv_b, gate_w, gate_b, lam):
    b, s, d = x.shape
    tm = min(TM_INPROJ, s)
    nt = s // tm
    per, n8 = tm // SUBLANES, s // SUBLANES
    halo_prev = pl.BlockSpec((1, SUBLANES, d), lambda i, bb: (bb, jnp.maximum(i * per - 1, 0), 0))
    halo_next = pl.BlockSpec((1, SUBLANES, d), lambda i, bb: (bb, jnp.minimum((i + 1) * per, n8 - 1), 0))
    tok = lambda width: pl.BlockSpec((1, tm, width), lambda i, bb: (bb, i, 0))
    tab = pl.BlockSpec((tm, LANES), lambda i, bb: (i, 0))
    dw = D_HEADS * HEAD_DIM
    d_shapes = tuple(jax.ShapeDtypeStruct((b, dil, s // dil, dw), BF16) for _, dil in D_PATTERNS)
    d_specs = tuple(pl.BlockSpec((1, dil, tm // dil, dw), lambda i, bb: (bb, 0, i, 0)) for _, dil in D_PATTERNS)
    out_shape = (
        jax.ShapeDtypeStruct((b, A_HEADS * HEAD_DIM, s), BF16),
        jax.ShapeDtypeStruct((b, A_KV_HEADS, s, HEAD_DIM), BF16),
        jax.ShapeDtypeStruct((b, A_KV_HEADS, nt, VT_ROWS_A, tm), BF16),
        jax.ShapeDtypeStruct((b, s, LRU_WIDTH), F32),
        jax.ShapeDtypeStruct((b, s, LRU_WIDTH), F32),
        jax.ShapeDtypeStruct((b, s, LRU_WIDTH), F32),
        jax.ShapeDtypeStruct((b, s, LRU_WIDTH), F32),
        jax.ShapeDtypeStruct((b, s, LRU_WIDTH), BF16),
        jax.ShapeDtypeStruct((b, s, 512), BF16),
        jax.ShapeDtypeStruct((b, s, 128), BF16),
        jax.ShapeDtypeStruct((b, s, 128), BF16),
    ) + d_shapes * 3
    out_specs = (
        pl.BlockSpec((1, A_HEADS * HEAD_DIM, tm), lambda i, bb: (bb, 0, i)),
        pl.BlockSpec((1, A_KV_HEADS, tm, HEAD_DIM), lambda i, bb: (bb, 0, i, 0)),
        pl.BlockSpec((1, A_KV_HEADS, 1, VT_ROWS_A, tm), lambda i, bb: (bb, 0, i, 0, 0)),
        tok(LRU_WIDTH), tok(LRU_WIDTH), tok(LRU_WIDTH), tok(LRU_WIDTH), tok(LRU_WIDTH), tok(512), tok(128), tok(128),
    ) + d_specs * 3
    return pl.pallas_call(
        _inproj_kernel,
        grid=(nt, b),
        in_specs=[tok(d), halo_prev, halo_next, _layer(g, layer),
                  pl.BlockSpec((None, d, N_IN), lambda i, bb: (layer, 0, 0), pipeline_mode=pl.Buffered(1)),
                  _layer(qk_gain128, layer),
                  tab, tab, tab, tab, tab, tab,
                  _layer(conv_w, layer), _layer(conv_b, layer), _layer(gate_w, layer),
                  _layer(gate_b, layer), _layer(lam, layer)],
        out_specs=out_specs,
        out_shape=out_shape,
        scratch_shapes=[pltpu.VMEM((dw // LANES, tm, LANES), F32),
                        pltpu.VMEM((tm + 2 * SUBLANES, LRU_WIDTH), F32)],
        compiler_params=_cparams(("parallel", "parallel")),
        name="inproj",
    )(x, x, x, g, w_all, qk_gain128, *ax_t, *seq_t, conv_w, conv_b, gate_w, gate_b, lam)


def _sublane_allmax(x):
    for shift in (4, 2, 1):
        x = jnp.maximum(x, pltpu.roll(x, shift, 0))
    return x


def _attn_a_kernel(q_ref, k_ref, v_ref, o_ref, m_ref, acc_ref, *, group, sub):
    n_chunks, chunk = v_ref.shape[2], v_ref.shape[4]
    tq = min(TQ_ATTN_A, q_ref.shape[2])
    n_tiles = q_ref.shape[2] // tq
    m_ref[...] = jnp.full(m_ref.shape, MASK_VALUE, F32)
    acc_ref[...] = jnp.zeros(acc_ref.shape, F32)

    per_body = min(CHUNKS_PER_BODY_ATTN_A, n_chunks)
    n_sub = chunk // sub
    pair = 2 if n_sub % 2 == 0 else 1
    units = [(cc, u, h) for cc in range(per_body) for up in range(n_sub // pair)
             for h in range(n_tiles * group) for u in range(up * pair, (up + 1) * pair)]

    def body(it, carry):
        c0 = it * per_body

        def scores(idx):
            cc, u, h = units[idx]
            row0 = pl.multiple_of((c0 + cc) * chunk + u * sub, sub)
            k = k_ref[0, 0, pl.ds(row0, sub), :]
            j, head = divmod(h, group)
            qt = q_ref[0, head * HEAD_DIM:(head + 1) * HEAD_DIM, j * tq:(j + 1) * tq]
            return jnp.dot(k, qt, preferred_element_type=F32)

        pending = [scores(i) for i in range(LOOKAHEAD_ATTN_A)]
        probs = []
        for idx, (cc, u, h) in enumerate(units):
            if idx + LOOKAHEAD_ATTN_A < len(units):
                pending.append(scores(idx + LOOKAHEAD_ATTN_A))
            s3 = pending.pop(0).reshape(sub // SUBLANES, SUBLANES, tq)
            if not probs:
                m_start = m_ref[h]
                m_old = m_start
            m_new = jnp.maximum(m_old, _sublane_allmax(jnp.max(s3, axis=0)))
            if probs:
                beta = jnp.exp2(m_old - m_new).astype(BF16)
                beta = jnp.concatenate([beta, beta], axis=0)[None]
                probs = [(p.reshape(sub // 16, 16, tq) * beta).reshape(sub, tq) for p in probs]
            probs.append(jnp.exp2(s3 - m_new[None]).reshape(sub, tq).astype(BF16))
            m_old = m_new
            if len(probs) == pair:
                u0 = u - (pair - 1)
                vt = v_ref[0, 0, c0 + cc, :, u0 * sub:(u + 1) * sub]
                pv = jnp.dot(vt, jnp.concatenate(probs, axis=0), preferred_element_type=F32)
                alpha = jnp.exp2(m_start - m_new)
                acc_ref[h] = acc_ref[h] * alpha[None] + pv.reshape(VT_ROWS_A // SUBLANES, SUBLANES, tq)
                m_ref[h] = m_new
                probs = []
        return carry

    lax.fori_loop(0, n_chunks // per_body, body, 0)
    for j in range(n_tiles):
        outs = []
        for h in range(j * group, (j + 1) * group):
            acc = acc_ref[h].reshape(VT_ROWS_A, tq)
            outs.append(acc[:HEAD_DIM] / acc[HEAD_DIM:HEAD_DIM + 1])
        o_ref[0, j * tq:(j + 1) * tq, :] = jnp.concatenate(outs, axis=0).T.astype(o_ref.dtype)


def _attn_a(qt, k, vt):
    b, hd_all, s = qt.shape
    nkv, nt, chunk = vt.shape[1], vt.shape[2], vt.shape[4]
    group = hd_all // HEAD_DIM // nkv
    tq = min(TQ_ATTN_A * TILES_PER_STEP_ATTN_A, s)
    n_state = group * (tq // min(TQ_ATTN_A, s))
    gw = group * HEAD_DIM
    return pl.pallas_call(
        functools.partial(_attn_a_kernel, group=group, sub=min(SUB_ATTN_A, chunk)),
        grid=(b, nkv, s // tq),
        in_specs=[pl.BlockSpec((1, gw, tq), lambda bb, g, i: (bb, g, i)),
                  pl.BlockSpec((1, 1, s, HEAD_DIM), lambda bb, g, i: (bb, g, 0, 0)),
                  pl.BlockSpec((1, 1, nt, VT_ROWS_A, chunk), lambda bb, g, i: (bb, g, 0, 0, 0))],
        out_specs=pl.BlockSpec((1, tq, gw), lambda bb, g, i: (bb, i, g)),
        out_shape=jax.ShapeDtypeStruct((b, s, hd_all), BF16),
        scratch_shapes=[pltpu.VMEM((n_state, SUBLANES, min(TQ_ATTN_A, s)), F32),
                        pltpu.VMEM((n_state, VT_ROWS_A // SUBLANES, SUBLANES, min(TQ_ATTN_A, s)), F32)],
        compiler_params=_cparams(("parallel", "parallel", "parallel")),
        name="attn_a",
    )(qt, k, vt)


def _band_kernel(*refs, nkv, group, half_window, seq_len, sink_layer, want_lse):
    has_sink = sink_layer is not None
    q_ref, kp_ref, km_ref, kn_ref, vp_ref, vm_ref, vn_ref = refs[:7]
    rest = list(refs[7:])
    sink_ref = rest.pop(0) if has_sink else None
    o_ref = rest.pop(0)
    lse_ref = rest.pop(0) if want_lse else None

    n_seq_blk, tq_blk = q_ref.shape[1], q_ref.shape[2]
    tq = min(TQ_BAND, tq_blk)
    tiles_per_seq = tq_blk // tq
    split = lambda j: divmod(j, tiles_per_seq)
    halo = kp_ref.shape[2]
    nk = tq + 2 * halo
    sub = SUB_BAND if nk % SUB_BAND == 0 else halo
    n_heads = nkv * group
    blk0 = pl.program_id(2) * tq_blk

    pad_row = lax.broadcasted_iota(jnp.int32, (VT_ROWS_A - HEAD_DIM, tq_blk + 2 * halo), 0)
    ones_rows = (pad_row == 0).astype(q_ref.dtype)
    kcat, vt_ext = [], []
    for sq in range(n_seq_blk):
        kcat.append(jnp.concatenate([kp_ref[0, sq], km_ref[0, sq], kn_ref[0, sq]], axis=0))
        vt = jnp.concatenate([vp_ref[0, sq], vm_ref[0, sq], vn_ref[0, sq]], axis=0).T
        vt_ext.append([jnp.concatenate([vt[g * HEAD_DIM:(g + 1) * HEAD_DIM], ones_rows], axis=0)
                       for g in range(nkv)])

    kw = min(LANES, nkv * HEAD_DIM)
    zeros = jnp.zeros((HEAD_DIM, tq), q_ref.dtype)
    q_pads = {}

    def padded_queries(j):
        if j not in q_pads:
            sq, jt = split(j)
            qt = q_ref[0, sq, jt * tq:(jt + 1) * tq, :].T
            pads = []
            for h in range(n_heads):
                slot = ((h // group) * HEAD_DIM % kw) // HEAD_DIM
                parts = [zeros] * (kw // HEAD_DIM)
                parts[slot] = qt[h * HEAD_DIM:(h + 1) * HEAD_DIM, :]
                pads.append(parts[0] if len(parts) == 1 else jnp.concatenate(parts, axis=0))
            q_pads.clear()
            q_pads[j] = pads
        return q_pads[j]

    units = [(j, kb, h) for j in range(n_seq_blk * tiles_per_seq) for kb in range(nk // sub)
             for h in range(n_heads)]

    cw = LANES if tq % LANES == 0 else tq
    n_ch = tq // cw

    def query_chunks(kb):
        k_lo = kb * sub - halo
        lo = max(0, (k_lo - half_window) // cw)
        hi = min(n_ch, -(-(k_lo + sub + half_window) // cw))
        return list(range(lo, hi))

    def scores(idx):
        j, kb, h = units[idx]
        sq, jt = split(j)
        row0 = jt * tq + kb * sub
        lane0 = (h // group) * HEAD_DIM // kw * kw
        cs = query_chunks(kb)
        return jnp.dot(kcat[sq][row0:row0 + sub, lane0:lane0 + kw],
                       padded_queries(j)[h][:, cs[0] * cw:(cs[-1] + 1) * cw], preferred_element_type=F32)

    key_row = lax.broadcasted_iota(jnp.int32, (sub, cw), 0)
    query = lax.broadcasted_iota(jnp.int32, (sub, cw), 1)

    def needs_mask(kb, c):
        k_lo = kb * sub - halo
        in_band = k_lo - (c * cw + cw - 1) >= -half_window and k_lo + sub - 1 - c * cw <= half_window
        return not (in_band and k_lo >= 0 and k_lo + sub <= tq)

    def valid_mask(j, kb, c):
        q0 = blk0 + split(j)[1] * tq
        kpos = q0 - halo + kb * sub + key_row
        rel = kpos - (q0 + c * cw + query)
        return (jnp.abs(rel) <= half_window) & (kpos >= 0) & (kpos < seq_len)

    def start_state():
        shape = (VT_ROWS_A // SUBLANES, SUBLANES, cw)
        if has_sink:
            m0 = [jnp.full((SUBLANES, cw), sink_ref[sink_layer, h] * LOG2E, F32) for h in range(n_heads)]
            denom = ((lax.broadcasted_iota(jnp.int32, shape, 0) == HEAD_DIM // SUBLANES)
                     & (lax.broadcasted_iota(jnp.int32, shape, 1) == 0))
            a0 = denom.astype(F32)
        else:
            m0 = [jnp.full((SUBLANES, cw), 0.1 * MASK_VALUE, F32)] * n_heads
            a0 = jnp.zeros(shape, F32)
        return [[m0[h]] * n_ch for h in range(n_heads)], [[a0] * n_ch for _ in range(n_heads)]

    def join(parts):
        return parts[0] if len(parts) == 1 else jnp.concatenate(parts, axis=1)

    def finish(j, m, acc):
        outs, lses = [], []
        for h in range(n_heads):
            full = join([x.reshape(VT_ROWS_A, cw) for x in acc[h]])
            l_row = full[HEAD_DIM:HEAD_DIM + 1]
            outs.append(full[:HEAD_DIM] / l_row)
            if want_lse:
                lse = join([x[0:1] for x in m[h]]) + jnp.log2(l_row)
                lses.append(jnp.broadcast_to(lse, (HEAD_DIM, tq)))
        sq, jt = split(j)
        o_ref[0, sq, jt * tq:(jt + 1) * tq, :] = jnp.concatenate(outs, axis=0).T.astype(o_ref.dtype)
        if want_lse:
            lse_ref[0, sq, jt * tq:(jt + 1) * tq, :] = jnp.concatenate(lses, axis=0).T

    look = min(LOOKAHEAD_BAND, len(units))
    pending = [scores(i) for i in range(look)]
    mask_key, masks = None, None
    for idx, (j, kb, h) in enumerate(units):
        if idx + look < len(units):
            pending.append(scores(idx + look))
        if kb == 0 and h == 0:
            m, acc = start_state()
        cs = query_chunks(kb)
        if mask_key != (j, kb):
            mask_key = (j, kb)
            masks = [valid_mask(j, kb, c) if needs_mask(kb, c) else None for c in cs]
        g = h // group
        s = pending.pop(0)
        probs, alphas = [], []
        for i, c in enumerate(cs):
            s_c = s if len(cs) == 1 else s[:, i * cw:(i + 1) * cw]
            if masks[i] is not None:
                s_c = jnp.where(masks[i], s_c, MASK_VALUE)
            s3 = s_c.reshape(sub // SUBLANES, SUBLANES, cw)
            m_new = jnp.maximum(m[h][c], _sublane_allmax(jnp.max(s3, axis=0)))
            alpha = jnp.exp2(m[h][c] - m_new)
            p3 = jnp.exp2(s3 - m_new[None])
            m[h][c] = m_new
            probs.append(p3.reshape(sub, cw).astype(BF16))
            alphas.append(alpha)
        sq, jt = split(j)
        col0 = jt * tq + kb * sub
        pv = jnp.dot(vt_ext[sq][g][:, col0:col0 + sub], join(probs), preferred_element_type=F32)
        for i, c in enumerate(cs):
            pv_c = pv if len(cs) == 1 else pv[:, i * cw:(i + 1) * cw]
            acc[h][c] = acc[h][c] * alphas[i][None] + pv_c.reshape(VT_ROWS_A // SUBLANES, SUBLANES, cw)
        if kb == nk // sub - 1 and h == n_heads - 1:
            finish(j, m, acc)


def _band_attention(q, k, v, *, nkv, group, half_window, sink=None, sink_layer=None, want_lse=False):
    b, n_seq, seq_len, wq = q.shape
    wk = nkv * HEAD_DIM
    halo = half_window
    tq = min(TQ_BAND * BAND_TILES_PER_STEP, seq_len)
    ns = max(1, min(n_seq, BAND_TILES_PER_STEP * TQ_BAND // tq))
    per = tq // halo
    n_halo = seq_len // halo

    main = lambda width: pl.BlockSpec((1, ns, tq, width), lambda bb, r, i: (bb, r, i, 0))
    prev = lambda width: pl.BlockSpec(
        (1, ns, halo, width), lambda bb, r, i: (bb, r, jnp.maximum(i * per - 1, 0), 0))
    nxt = lambda width: pl.BlockSpec(
        (1, ns, halo, width), lambda bb, r, i: (bb, r, jnp.minimum((i + 1) * per, n_halo - 1), 0))
    in_specs = [main(wq), prev(wk), main(wk), nxt(wk), prev(wk), main(wk), nxt(wk)]
    args = [q, k, k, k, v, v, v]
    if sink is not None:
        in_specs.append(pl.BlockSpec(memory_space=pltpu.SMEM))
        args.append(sink)
    out_block = main(wq)
    out_shape = [jax.ShapeDtypeStruct(q.shape, BF16)]
    out_specs = [out_block]
    if want_lse:
        out_shape.append(jax.ShapeDtypeStruct(q.shape, F32))
        out_specs.append(out_block)
    res = pl.pallas_call(
        functools.partial(_band_kernel, nkv=nkv, group=group, half_window=half_window,
                          seq_len=seq_len, sink_layer=sink_layer if sink is not None else None,
                          want_lse=want_lse),
        grid=(b, n_seq // ns, seq_len // tq),
        in_specs=in_specs,
        out_specs=out_specs,
        out_shape=out_shape,
        compiler_params=_cparams(("parallel", "parallel", "parallel")),
        name="band_attn",
    )(*args)
    return res if want_lse else res[0]


def _sigmoid(x):
    return 0.5 * jnp.tanh(0.5 * x) + 0.5


def _softplus(x):
    return jnp.maximum(x, 0.0) + jnp.log1p(jnp.exp(-jnp.abs(x)))


def _lru_kernel(af_ref, uf_ref, ab_ref, ub_ref, hf_ref, hb_ref, carry_ref):
    i = pl.program_id(1)
    tt = af_ref.shape[1]
    nblk = tt // SUBLANES

    @pl.when(i == 0)
    def _():
        carry_ref[...] = jnp.zeros(carry_ref.shape, F32)

    row = lax.broadcasted_iota(jnp.int32, (SUBLANES, LRU_WIDTH), 0)

    def scan_block(d, blk, carry, a_ref, u_ref, out_ref):
        a8 = a_ref[0, pl.ds(blk * SUBLANES, SUBLANES), :]
        u8 = u_ref[0, pl.ds(blk * SUBLANES, SUBLANES), :]
        entry = row == (0 if d == 0 else SUBLANES - 1)
        u8 = u8 + jnp.where(entry, a8 * carry, 0.0)
        for step in (1, 2, 4):
            if d == 0:
                shift, ident = step, row < step
            else:
                shift, ident = SUBLANES - step, row >= SUBLANES - step
            u8 = u8 + a8 * jnp.where(ident, 0.0, pltpu.roll(u8, shift, 0))
            if step < 4:
                a8 = a8 * pltpu.roll(a8, shift, 0)
        out_ref[0, pl.ds(blk * SUBLANES, SUBLANES), :] = u8
        last = u8[SUBLANES - 1:SUBLANES, :] if d == 0 else u8[0:1, :]
        return jnp.broadcast_to(last, (SUBLANES, LRU_WIDTH))

    def body(kb, carries):
        cf, cb = carries
        cf = scan_block(0, kb, cf, af_ref, uf_ref, hf_ref)
        cb = scan_block(1, nblk - 1 - kb, cb, ab_ref, ub_ref, hb_ref)
        return cf, cb

    cf, cb = lax.fori_loop(0, nblk, body, (carry_ref[0], carry_ref[1]))
    carry_ref[0] = cf
    carry_ref[1] = cb


def _lru(af, uf, ab, ub):
    b, s, w = af.shape
    tt = min(TT_LRU, s)
    nt = s // tt
    fwd = pl.BlockSpec((1, tt, w), lambda bb, i: (bb, i, 0))
    bwd = pl.BlockSpec((1, tt, w), lambda bb, i: (bb, nt - 1 - i, 0))
    return pl.pallas_call(
        _lru_kernel,
        grid=(b, nt),
        in_specs=[fwd, fwd, bwd, bwd],
        out_specs=[fwd, bwd],
        out_shape=[jax.ShapeDtypeStruct((b, s, w), F32), jax.ShapeDtypeStruct((b, s, w), F32)],
        scratch_shapes=[pltpu.VMEM((2, SUBLANES, w), F32)],
        compiler_params=_cparams(("parallel", "arbitrary")),
        name="lru",
    )(af, uf, ab, ub)


def _gelu_tanh(x):
    return 0.5 * x * (1.0 + jnp.tanh(math.sqrt(2.0 / math.pi) * (x + 0.044715 * (x * x * x))))


def _merge_kernel(x_ref, ya_ref, hf_ref, hb_ref, by_ref, yc_ref,
                  o0_ref, o1_ref, o2_ref, l0_ref, l1_ref, l2_ref, g_ref, wg_ref, gbias_ref,
                  wa_ref, wb_ref, wc_ref, wd_ref, wo_ref, out_ref, perm_scr):
    d_model = x_ref.shape[2]
    tm = x_ref.shape[1]
    x = x_ref[0]
    var = jnp.mean(x * x, axis=-1, keepdims=True)
    hn = (x * lax.rsqrt(var + EPS) * g_ref[...]).astype(BF16)

    def natural(ref, slot):
        dil = ref.shape[1]
        if dil == 1:
            return ref[0, 0].astype(F32)
        for r in range(dil):
            rows = ref[0, r].astype(F32)
            for j in range(rows.shape[1] // LANES):
                perm_scr[slot, j, pl.ds(r, tm // dil, stride=dil), :] = rows[:, j * LANES:(j + 1) * LANES]
        return jnp.concatenate([perm_scr[slot, j] for j in range(perm_scr.shape[1])], axis=1)

    yb = ((hf_ref[0] + hb_ref[0]) * _gelu_tanh(by_ref[0].astype(F32))).astype(BF16)

    l0, l1, l2 = natural(l0_ref, 0), natural(l1_ref, 0), natural(l2_ref, 1)
    o0, o1, o2 = natural(o0_ref, 2), natural(o1_ref, 2), natural(o2_ref, 3)
    mx = jnp.maximum(jnp.maximum(l0, l1), l2)
    e0, e1, e2 = jnp.exp2(l0 - mx), jnp.exp2(l1 - mx), jnp.exp2(l2 - mx)
    yd = ((e0 * o0 + e1 * o1 + e2 * o2) / (e0 + e1 + e2)).astype(BF16)

    def gate_logits(kk):
        off = wg_ref.shape[1] - N_BRANCH * d_model + kk * d_model
        return lambda: jnp.dot(hn, wg_ref[:, off:off + d_model], preferred_element_type=F32)

    def branch(y, w_ref):
        return lambda: jnp.dot(y, w_ref[...], preferred_element_type=F32)

    dots = []
    for kk, (y, w_ref) in enumerate(((ya_ref[0], wa_ref), (yb, wb_ref), (yc_ref[0], wc_ref), (yd, wd_ref))):
        dots += [gate_logits(kk), branch(y, w_ref)]
    pending = [dots[0](), dots[1]()]
    merged = None
    for kk in range(N_BRANCH):
        pending += [d() for d in dots[2 * kk + 2:2 * kk + 4]]
        logits, proj = pending.pop(0), pending.pop(0)
        term = _sigmoid(logits + gbias_ref[kk:kk + 1, :]) * proj
        merged = term if merged is None else merged + term

    out_ref[0] = x + jnp.dot(merged.astype(BF16), wo_ref[...], preferred_element_type=F32)


def _merge(x, layer, ya, hf, hb, by, yc, o_d, lse_d, g, w_all, gbias, wa, wb, wc, wd, wo):
    b, s, d = x.shape
    tm = min(TM_MERGE, s)
    tok = lambda width: pl.BlockSpec((1, tm, width), lambda bb, i: (bb, i, 0))
    dw = D_HEADS * HEAD_DIM
    res = [pl.BlockSpec((1, dil, tm // dil, dw), lambda bb, i: (bb, 0, i, 0)) for _, dil in D_PATTERNS]
    return pl.pallas_call(
        _merge_kernel,
        grid=(b, s // tm),
        in_specs=[tok(d), tok(512), tok(LRU_WIDTH), tok(LRU_WIDTH), tok(LRU_WIDTH), tok(512),
                  *res, *res, _layer(g, layer), _layer(w_all, layer), _layer(gbias, layer),
                  _layer(wa, layer), _layer(wb, layer), _layer(wc, layer), _layer(wd, layer), _layer(wo, layer)],
        out_specs=tok(d),
        out_shape=jax.ShapeDtypeStruct((b, s, d), F32),
        scratch_shapes=[pltpu.VMEM((4, dw // LANES, tm, LANES), F32)],
        compiler_params=_cparams(("parallel", "parallel")),
        name="merge",
    )(x, ya, hf, hb, by, yc, *o_d, *lse_d, g, w_all, gbias, wa, wb, wc, wd, wo)


def _mlp_kernel(x_ref, g_ref, w1_ref, w2_ref, gf_ref, out_ref, *, final_norm):
    x = x_ref[0]
    var = jnp.mean(x * x, axis=-1, keepdims=True)
    hn = (x * lax.rsqrt(var + EPS) * g_ref[...]).astype(BF16)
    hidden = w1_ref.shape[1]
    acc = x
    for c in range(hidden // MLP_CHUNK):
        h = jnp.dot(hn, w1_ref[:, c * MLP_CHUNK:(c + 1) * MLP_CHUNK], preferred_element_type=F32)
        h = jnp.square(jnp.maximum(h, 0.0)).astype(BF16)
        acc = acc + jnp.dot(h, w2_ref[c * MLP_CHUNK:(c + 1) * MLP_CHUNK, :], preferred_element_type=F32)
    if final_norm:
        var = jnp.mean(acc * acc, axis=-1, keepdims=True)
        acc = acc * lax.rsqrt(var + EPS) * gf_ref[...]
    out_ref[0] = acc


def _mlp(x, layer, g, w1, w2, g_final, final_norm):
    b, s, d = x.shape
    tm = min(TM_MLP, s)
    tok = pl.BlockSpec((1, tm, d), lambda bb, i: (bb, i, 0))
    return pl.pallas_call(
        functools.partial(_mlp_kernel, final_norm=final_norm),
        grid=(b, s // tm),
        in_specs=[tok, _layer(g, layer), _layer(w1, layer), _layer(w2, layer), _resident(g_final.shape)],
        out_specs=tok,
        out_shape=jax.ShapeDtypeStruct((b, s, d), F32),
        compiler_params=_cparams(("parallel", "parallel")),
        name="mlp",
    )(x, g, w1, w2, g_final)


def _dense_block_diag(w):
    nb, bw = w.shape[-3], w.shape[-2]
    eye = jnp.eye(nb, dtype=w.dtype)
    return jnp.einsum('...ncd,nm->...ncmd', w, eye).reshape(w.shape[:-3] + (nb * bw, nb * bw))


def kernel(x, norm_mix_g, w_in, gate_bias, qk_norm_g, conv_w, conv_b, lru_gate_w, lru_gate_b, lru_lambda, sink_logit, w_proj_a, w_proj_b, w_proj_c, w_proj_d, w_out, norm_mlp_g, w_mlp1, w_mlp2, norm_final_g):
    b, s, d = x.shape
    depth = w_in.shape[0]
    ax_t, seq_t = _rope_tables(s)
    dw = D_HEADS * HEAD_DIM
    n_groups = len(D_PATTERNS)
    bf = lambda w: w.astype(BF16)
    w_in_bf, wa, wb, wc, wd, wo, w1, w2 = map(bf, (w_in, w_proj_a, w_proj_b, w_proj_c, w_proj_d, w_out, w_mlp1, w_mlp2))
    g_mix, g_mlp, conv_b3 = norm_mix_g[:, None, :], norm_mlp_g[:, None, :], conv_b[:, None, :]
    qk_gain = jnp.tile(qk_norm_g, (1, 1, LANES // HEAD_DIM))
    gate_w_dense = bf(_dense_block_diag(lru_gate_w))

    for l in range(depth):
        outs = _inproj(x, l, g_mix, w_in_bf, qk_gain, ax_t, seq_t, conv_w, conv_b3, gate_w_dense,
                       lru_gate_b, lru_lambda)
        qta, ka, vta, af, uf, ab, ub, by, qc, kc, vc = outs[:11]
        qd, kd, vd = outs[11:11 + n_groups], outs[11 + n_groups:11 + 2 * n_groups], outs[11 + 2 * n_groups:11 + 3 * n_groups]

        ya = _attn_a(qta, ka, vta)

        hf, hb = _lru(af, uf, ab, ub)

        yc = _band_attention(qc[:, None], kc[:, None], vc[:, None], nkv=C_KV_HEADS, group=C_HEADS // C_KV_HEADS,
                             half_window=C_HALF_WINDOW, sink=sink_logit, sink_layer=l)[:, 0]

        o_d, lse_d = [], []
        for gi, (window, dil) in enumerate(D_PATTERNS):
            o, lse = _band_attention(qd[gi], kd[gi], vd[gi], nkv=D_HEADS, group=1,
                                     half_window=window // (2 * dil), want_lse=True)
            o_d.append(o)
            lse_d.append(lse)

        x = _merge(x, l, ya, hf, hb, by, yc, o_d, lse_d, g_mix, w_in_bf, gate_bias, wa, wb, wc, wd, wo)
        x = _mlp(x, l, g_mlp, w1, w2, norm_final_g[None, :], final_norm=(l == depth - 1))
    return x
```

```python
import functools
import math

import jax
import jax.numpy as jnp
from jax import lax
from jax.experimental import pallas as pl
from jax.experimental.pallas import tpu as pltpu

F32 = jnp.float32
BF16 = jnp.bfloat16

HEAD_DIM = 64
ROPE_THETA = 10000.0
GRID_W = 64
EPS = 1e-6
MASK_VALUE = -1e30
LOG2E = math.log2(math.e)

A_HEADS, A_KV_HEADS = 8, 2
LRU_WIDTH, LRU_BLOCKS, LRU_C = 512, 8, 8.0
CONV_W, CONV_LEFT = 4, 2
C_HEADS, C_KV_HEADS, C_HALF_WINDOW = 8, 2, 128
D_PATTERNS = ((128, 1), (512, 4), (2048, 16))
D_HEADS = 4
N_BRANCH = 4

LANES = 128
SUBLANES = 8
VMEM_LIMIT_BYTES = 56 * 1024 * 1024

TM_INPROJ = 512
TQ_ATTN_A = 256
SUB_ATTN_A = 128
TILES_PER_STEP_ATTN_A = 2
LOOKAHEAD_ATTN_A = 14
LOOKAHEAD_BAND = 6
LOOKAHEAD_INPROJ = 6
SUB_BAND = 128
CHUNKS_PER_BODY_ATTN_A = 16
VT_ROWS_A = HEAD_DIM + 16
TQ_BAND = 256
BAND_TILES_PER_STEP = 16
TT_LRU = 2048
TM_MERGE = 512
TM_MLP = 1024
MLP_CHUNK = 1024

Q_SCALE = HEAD_DIM ** -0.5 * LOG2E


def _cparams(sem):
    return pltpu.CompilerParams(dimension_semantics=sem, vmem_limit_bytes=VMEM_LIMIT_BYTES)


def _layer(arr, layer):
    zeros = (0,) * (arr.ndim - 1)
    return pl.BlockSpec((None,) + arr.shape[1:], lambda *_: (layer,) + zeros, pipeline_mode=pl.Buffered(1))


def _resident(shape):
    nd = len(shape)
    return pl.BlockSpec(shape, lambda *_: (0,) * nd, pipeline_mode=pl.Buffered(1))


def _rope_tables(seq):
    pos = jnp.arange(seq, dtype=F32)
    lane = jnp.arange(LANES)
    d = lane % HEAD_DIM

    inv = ROPE_THETA ** (-jnp.arange(0, HEAD_DIM, 2, dtype=F32) / HEAD_DIM)
    ang = pos[:, None] * inv[None, :]
    f = d % (HEAD_DIM // 2)
    cos_s, sin_s = jnp.cos(ang)[:, f], jnp.sin(ang)[:, f]
    lo = (d < HEAD_DIM // 2)[None, :]
    seq_t = (cos_s, jnp.where(lo, -sin_s, 0.0), jnp.where(lo, 0.0, sin_s))

    quarter = HEAD_DIM // 4
    inv_ax = ROPE_THETA ** (-jnp.arange(0, HEAD_DIM // 2, 2, dtype=F32) / (HEAD_DIM // 2))
    n_rows = seq // GRID_W
    ang_r = jnp.arange(n_rows, dtype=F32)[:, None] * inv_ax[None, :]
    ang_c = jnp.arange(GRID_W, dtype=F32)[:, None] * inv_ax[None, :]

    def expand(by_row, by_col):
        r = jnp.repeat(by_row, GRID_W, axis=0)
        c = jnp.tile(by_col, (n_rows, 1))
        return jnp.tile(jnp.concatenate([r, r, c, c], axis=1), (1, LANES // HEAD_DIM))

    cos_a, sin_a = expand(jnp.cos(ang_r), jnp.cos(ang_c)), expand(jnp.sin(ang_r), jnp.sin(ang_c))
    lo_a = (d % (HEAD_DIM // 2) < quarter)[None, :]
    ax_t = (cos_a, jnp.where(lo_a, -sin_a, 0.0), jnp.where(lo_a, 0.0, sin_a))
    return ax_t, seq_t


def _rotate(x, tables, half):
    cos, sin_a, sin_b = tables
    return x * cos + pltpu.roll(x, LANES - half, 1) * sin_a + pltpu.roll(x, half, 1) * sin_b


def _split_dot(x, m):
    hi = x.astype(BF16)
    lo = (x - hi.astype(F32)).astype(BF16)
    return jnp.dot(jnp.concatenate([hi, lo], axis=1), jnp.concatenate([m, m], axis=0),
                   preferred_element_type=F32)


def _head_mean_matrix():
    r = lax.broadcasted_iota(jnp.int32, (LANES, LANES), 0) // HEAD_DIM
    c = lax.broadcasted_iota(jnp.int32, (LANES, LANES), 1) // HEAD_DIM
    return jnp.where(r == c, 1.0 / HEAD_DIM, 0.0).astype(BF16)


def _head_rmsnorm(x, gain, mean_mat):
    var = _split_dot(x * x, mean_mat)
    return x * lax.rsqrt(var + EPS) * gain


IN_WIDTHS = (512, 128, 128, 512, 512, 512, 128, 128, 768, 768, 768)
IN_OFFSETS = tuple(sum(IN_WIDTHS[:i]) for i in range(len(IN_WIDTHS)))
N_IN = sum(IN_WIDTHS)


def _inproj_kernel(x_ref, xprev_ref, xnext_ref, g_ref, w_ref, qkg_ref,
                   ca_ref, saa_ref, sab_ref, cs_ref, ssa_ref, ssb_ref,
                   cw_ref, cb_ref, gw_ref, gb_ref, lam_ref,
                   qta_ref, ka_ref, vta_ref, af_ref, uf_ref, ab_ref, ub_ref, by_ref,
                   qc_ref, kc_ref, vc_ref, *rest):
    d_refs, perm_scr, xpad_scr = rest[:9], rest[9], rest[10]
    tile = pl.program_id(0)
    n_tiles = pl.num_programs(0)

    def normed(rows):
        var = jnp.mean(rows * rows, axis=-1, keepdims=True)
        return (rows * lax.rsqrt(var + EPS) * g_ref[...]).astype(BF16)

    x = x_ref[0]
    hn = normed(x)

    def proj(seg, lo=0, width=None):
        off = IN_OFFSETS[seg] + lo
        width = IN_WIDTHS[seg] if width is None else width
        return lambda: jnp.dot(hn, w_ref[:, off:off + width], preferred_element_type=F32)

    ax_t = (ca_ref[...], saa_ref[...], sab_ref[...])
    seq_t = (cs_ref[...], ssa_ref[...], ssb_ref[...])
    mean_mat = _head_mean_matrix()
    gq = qkg_ref[0:1, :]
    gk = qkg_ref[1:2, :]
    tm = x.shape[0]
    dw = D_HEADS * HEAD_DIM

    def finish_aq(acc):
        for j in range(IN_WIDTHS[0] // LANES):
            piece = _head_rmsnorm(acc[:, j * LANES:(j + 1) * LANES], gq, mean_mat)
            piece = _rotate(piece, ax_t, HEAD_DIM // 4) * Q_SCALE
            qta_ref[0, j * LANES:(j + 1) * LANES, :] = piece.T.astype(BF16)

    def finish_akv(acc):
        ak = _rotate(_head_rmsnorm(acc[:, :LANES], gk, mean_mat), ax_t, HEAD_DIM // 4).astype(BF16)
        avt = acc[:, LANES:].T.astype(BF16)
        pad_row = lax.broadcasted_iota(jnp.int32, (VT_ROWS_A - HEAD_DIM, tm), 0)
        ones_rows = (pad_row == 0).astype(BF16)
        for g in range(A_KV_HEADS):
            ka_ref[0, g] = ak[:, g * HEAD_DIM:(g + 1) * HEAD_DIM]
            vta_ref[0, g, 0] = jnp.concatenate([avt[g * HEAD_DIM:(g + 1) * HEAD_DIM, :], ones_rows], axis=0)

    lru = {}

    def finish_bx(acc):
        halo = jnp.dot(normed(jnp.concatenate([xprev_ref[0], xnext_ref[0]], axis=0)),
                       w_ref[:, IN_OFFSETS[3]:IN_OFFSETS[3] + LRU_WIDTH], preferred_element_type=F32)
        xpad_scr[0:SUBLANES] = jnp.where(tile > 0, halo[:SUBLANES], 0.0)
        xpad_scr[SUBLANES:SUBLANES + tm] = acc
        xpad_scr[SUBLANES + tm:] = jnp.where(tile < n_tiles - 1, halo[SUBLANES:], 0.0)
        xc = cb_ref[...]
        base = SUBLANES - CONV_LEFT
        for jj in range(CONV_W):
            xc = xc + xpad_scr[base + jj:base + jj + tm, :] * cw_ref[jj:jj + 1, :]
        lru["xc"] = xc
        lru["xcb"] = xc.astype(BF16)

    def gate_dots():
        return [jnp.dot(lru["xcb"], gw_ref[d, g], preferred_element_type=F32) for d in range(2) for g in range(2)]

    def finish_gates(dots):
        xc = lru["xc"]
        for d, (a_ref, u_ref) in enumerate(((af_ref, uf_ref), (ab_ref, ub_ref))):
            c2 = (-0.5 * LRU_C * LOG2E) * _softplus(-lam_ref[d:d + 1, :])
            a = jnp.exp2(c2 * jnp.tanh(0.5 * dots[2 * d] + 0.5 * gb_ref[d, 0:1, :]) + c2)
            gi = _sigmoid(dots[2 * d + 1] + gb_ref[d, 1:2, :])
            a_ref[0] = a
            z = (1.0 - a) * (1.0 + a)
            root = jnp.where(z > 0.0, z * lax.rsqrt(z), 0.0)
            u_ref[0] = root * (gi * xc)

    def finish_by(acc):
        by_ref[0] = acc.astype(BF16)

    def rope_piece(piece, scale):
        piece = _rotate(piece, seq_t, HEAD_DIM // 2)
        return piece if scale is None else piece * scale

    def finish_cq(acc):
        for j in range(IN_WIDTHS[5] // LANES):
            qc_ref[0, :, j * LANES:(j + 1) * LANES] = rope_piece(acc[:, j * LANES:(j + 1) * LANES], Q_SCALE).astype(BF16)

    def finish_ckv(acc):
        kc_ref[0] = rope_piece(acc[:, :LANES], None).astype(BF16)
        vc_ref[0] = acc[:, LANES:].astype(BF16)

    def finish_d(ref, dil, scale, rope):
        def finish(acc):
            for j in range(dw // LANES):
                piece = acc[:, j * LANES:(j + 1) * LANES]
                if rope:
                    piece = rope_piece(piece, scale)
                if dil == 1:
                    ref[0, 0, :, j * LANES:(j + 1) * LANES] = piece.astype(BF16)
                else:
                    perm_scr[j] = piece
            if dil > 1:
                for r in range(dil):
                    rows = [perm_scr[j, pl.ds(r, tm // dil, stride=dil), :] for j in range(dw // LANES)]
                    ref[0, r] = jnp.concatenate(rows, axis=1).astype(BF16)
        return finish

    segments = [(proj(0), finish_aq), (proj(1, 0, 2 * LANES), finish_akv), (proj(3), finish_bx),
                (proj(4), finish_by), (gate_dots, finish_gates), (proj(5), finish_cq),
                (proj(6, 0, 2 * LANES), finish_ckv)]
    for gi, (_, dil) in enumerate(D_PATTERNS):
        for ti, (seg, scale, rope) in enumerate(((8, Q_SCALE, True), (9, None, True), (10, None, False))):
            segments.append((proj(seg, gi * dw, dw), finish_d(d_refs[ti * len(D_PATTERNS) + gi], dil, scale, rope)))

    issued, pending = 0, []
    for idx, (_, finish) in enumerate(segments):
        horizon = min(len(segments), idx + 1 + LOOKAHEAD_INPROJ)
        while issued < horizon and (segments[issued][0] is not gate_dots or "xcb" in lru):
            pending.append(segments[issued][0]())
            issued += 1
        finish(pending.pop(0))


def _inproj(x, layer, g, w_all, qk_gain128, ax_t, seq_t, conv_w, conv_b, gate_w, gate_b, lam):
    b, s, d = x.shape
    tm = min(TM_INPROJ, s)
    nt = s // tm
    per, n8 = tm // SUBLANES, s // SUBLANES
    halo_prev = pl.BlockSpec((1, SUBLANES, d), lambda i, bb: (bb, jnp.maximum(i * per - 1, 0), 0))
    halo_next = pl.BlockSpec((1, SUBLANES, d), lambda i, bb: (bb, jnp.minimum((i + 1) * per, n8 - 1), 0))
    tok = lambda width: pl.BlockSpec((1, tm, width), lambda i, bb: (bb, i, 0))
    tab = pl.BlockSpec((tm, LANES), lambda i, bb: (i, 0))
    dw = D_HEADS * HEAD_DIM
    d_shapes = tuple(jax.ShapeDtypeStruct((b, dil, s // dil, dw), BF16) for _, dil in D_PATTERNS)
    d_specs = tuple(pl.BlockSpec((1, dil, tm // dil, dw), lambda i, bb: (bb, 0, i, 0)) for _, dil in D_PATTERNS)
    out_shape = (
        jax.ShapeDtypeStruct((b, A_HEADS * HEAD_DIM, s), BF16),
        jax.ShapeDtypeStruct((b, A_KV_HEADS, s, HEAD_DIM), BF16),
        jax.ShapeDtypeStruct((b, A_KV_HEADS, nt, VT_ROWS_A, tm), BF16),
        jax.ShapeDtypeStruct((b, s, LRU_WIDTH), F32),
        jax.ShapeDtypeStruct((b, s, LRU_WIDTH), F32),
        jax.ShapeDtypeStruct((b, s, LRU_WIDTH), F32),
        jax.ShapeDtypeStruct((b, s, LRU_WIDTH), F32),
        jax.ShapeDtypeStruct((b, s, LRU_WIDTH), BF16),
        jax.ShapeDtypeStruct((b, s, 512), BF16),
        jax.ShapeDtypeStruct((b, s, 128), BF16),
        jax.ShapeDtypeStruct((b, s, 128), BF16),
    ) + d_shapes * 3
    out_specs = (
        pl.BlockSpec((1, A_HEADS * HEAD_DIM, tm), lambda i, bb: (bb, 0, i)),
        pl.BlockSpec((1, A_KV_HEADS, tm, HEAD_DIM), lambda i, bb: (bb, 0, i, 0)),
        pl.BlockSpec((1, A_KV_HEADS, 1, VT_ROWS_A, tm), lambda i, bb: (bb, 0, i, 0, 0)),
        tok(LRU_WIDTH), tok(LRU_WIDTH), tok(LRU_WIDTH), tok(LRU_WIDTH), tok(LRU_WIDTH), tok(512), tok(128), tok(128),
    ) + d_specs * 3
    return pl.pallas_call(
        _inproj_kernel,
        grid=(nt, b),
        in_specs=[tok(d), halo_prev, halo_next, _layer(g, layer),
                  pl.BlockSpec((None, d, N_IN), lambda i, bb: (layer, 0, 0), pipeline_mode=pl.Buffered(1)),
                  _layer(qk_gain128, layer),
                  tab, tab, tab, tab, tab, tab,
                  _layer(conv_w, layer), _layer(conv_b, layer), _layer(gate_w, layer),
                  _layer(gate_b, layer), _layer(lam, layer)],
        out_specs=out_specs,
        out_shape=out_shape,
        scratch_shapes=[pltpu.VMEM((dw // LANES, tm, LANES), F32),
                        pltpu.VMEM((tm + 2 * SUBLANES, LRU_WIDTH), F32)],
        compiler_params=_cparams(("parallel", "parallel")),
        name="inproj",
    )(x, x, x, g, w_all, qk_gain128, *ax_t, *seq_t, conv_w, conv_b, gate_w, gate_b, lam)


def _sublane_allmax(x):
    for shift in (4, 2, 1):
        x = jnp.maximum(x, pltpu.roll(x, shift, 0))
    return x


def _attn_a_kernel(q_ref, k_ref, v_ref, o_ref, m_ref, acc_ref, *, group, sub):
    n_chunks, chunk = v_ref.shape[2], v_ref.shape[4]
    tq = min(TQ_ATTN_A, q_ref.shape[2])
    n_tiles = q_ref.shape[2] // tq
    m_ref[...] = jnp.full(m_ref.shape, MASK_VALUE, F32)
    acc_ref[...] = jnp.zeros(acc_ref.shape, F32)

    per_body = min(CHUNKS_PER_BODY_ATTN_A, n_chunks)
    n_sub = chunk // sub
    pair = 2 if n_sub % 2 == 0 else 1
    units = [(cc, u, h) for cc in range(per_body) for up in range(n_sub // pair)
             for h in range(n_tiles * group) for u in range(up * pair, (up + 1) * pair)]

    def body(it, carry):
        c0 = it * per_body

        def scores(idx):
            cc, u, h = units[idx]
            row0 = pl.multiple_of((c0 + cc) * chunk + u * sub, sub)
            k = k_ref[0, 0, pl.ds(row0, sub), :]
            j, head = divmod(h, group)
            qt = q_ref[0, head * HEAD_DIM:(head + 1) * HEAD_DIM, j * tq:(j + 1) * tq]
            return jnp.dot(k, qt, preferred_element_type=F32)

        pending = [scores(i) for i in range(LOOKAHEAD_ATTN_A)]
        probs = []
        for idx, (cc, u, h) in enumerate(units):
            if idx + LOOKAHEAD_ATTN_A < len(units):
                pending.append(scores(idx + LOOKAHEAD_ATTN_A))
            s3 = pending.pop(0).reshape(sub // SUBLANES, SUBLANES, tq)
            if not probs:
                m_start = m_ref[h]
                m_old = m_start
            m_new = jnp.maximum(m_old, _sublane_allmax(jnp.max(s3, axis=0)))
            if probs:
                beta = jnp.exp2(m_old - m_new).astype(BF16)
                beta = jnp.concatenate([beta, beta], axis=0)[None]
                probs = [(p.reshape(sub // 16, 16, tq) * beta).reshape(sub, tq) for p in probs]
            probs.append(jnp.exp2(s3 - m_new[None]).reshape(sub, tq).astype(BF16))
            m_old = m_new
            if len(probs) == pair:
                u0 = u - (pair - 1)
                vt = v_ref[0, 0, c0 + cc, :, u0 * sub:(u + 1) * sub]
                pv = jnp.dot(vt, jnp.concatenate(probs, axis=0), preferred_element_type=F32)
                alpha = jnp.exp2(m_start - m_new)
                acc_ref[h] = acc_ref[h] * alpha[None] + pv.reshape(VT_ROWS_A // SUBLANES, SUBLANES, tq)
                m_ref[h] = m_new
                probs = []
        return carry

    lax.fori_loop(0, n_chunks // per_body, body, 0)
    for j in range(n_tiles):
        outs = []
        for h in range(j * group, (j + 1) * group):
            acc = acc_ref[h].reshape(VT_ROWS_A, tq)
            outs.append(acc[:HEAD_DIM] / acc[HEAD_DIM:HEAD_DIM + 1])
        o_ref[0, j * tq:(j + 1) * tq, :] = jnp.concatenate(outs, axis=0).T.astype(o_ref.dtype)


def _attn_a(qt, k, vt):
    b, hd_all, s = qt.shape
    nkv, nt, chunk = vt.shape[1], vt.shape[2], vt.shape[4]
    group = hd_all // HEAD_DIM // nkv
    tq = min(TQ_ATTN_A * TILES_PER_STEP_ATTN_A, s)
    n_state = group * (tq // min(TQ_ATTN_A, s))
    gw = group * HEAD_DIM
    return pl.pallas_call(
        functools.partial(_attn_a_kernel, group=group, sub=min(SUB_ATTN_A, chunk)),
        grid=(b, nkv, s // tq),
        in_specs=[pl.BlockSpec((1, gw, tq), lambda bb, g, i: (bb, g, i)),
                  pl.BlockSpec((1, 1, s, HEAD_DIM), lambda bb, g, i: (bb, g, 0, 0)),
                  pl.BlockSpec((1, 1, nt, VT_ROWS_A, chunk), lambda bb, g, i: (bb, g, 0, 0, 0))],
        out_specs=pl.BlockSpec((1, tq, gw), lambda bb, g, i: (bb, i, g)),
        out_shape=jax.ShapeDtypeStruct((b, s, hd_all), BF16),
        scratch_shapes=[pltpu.VMEM((n_state, SUBLANES, min(TQ_ATTN_A, s)), F32),
                        pltpu.VMEM((n_state, VT_ROWS_A // SUBLANES, SUBLANES, min(TQ_ATTN_A, s)), F32)],
        compiler_params=_cparams(("parallel", "parallel", "parallel")),
        name="attn_a",
    )(qt, k, vt)


def _band_kernel(*refs, nkv, group, half_window, seq_len, sink_layer, want_lse):
    has_sink = sink_layer is not None
    q_ref, kp_ref, km_ref, kn_ref, vp_ref, vm_ref, vn_ref = refs[:7]
    rest = list(refs[7:])
    sink_ref = rest.pop(0) if has_sink else None
    o_ref = rest.pop(0)
    lse_ref = rest.pop(0) if want_lse else None

    n_seq_blk, tq_blk = q_ref.shape[1], q_ref.shape[2]
    tq = min(TQ_BAND, tq_blk)
    tiles_per_seq = tq_blk // tq
    split = lambda j: divmod(j, tiles_per_seq)
    halo = kp_ref.shape[2]
    nk = tq + 2 * halo
    sub = SUB_BAND if nk % SUB_BAND == 0 else halo
    n_heads = nkv * group
    blk0 = pl.program_id(2) * tq_blk

    pad_row = lax.broadcasted_iota(jnp.int32, (VT_ROWS_A - HEAD_DIM, tq_blk + 2 * halo), 0)
    ones_rows = (pad_row == 0).astype(q_ref.dtype)
    kcat, vt_ext = [], []
    for sq in range(n_seq_blk):
        kcat.append(jnp.concatenate([kp_ref[0, sq], km_ref[0, sq], kn_ref[0, sq]], axis=0))
        vt = jnp.concatenate([vp_ref[0, sq], vm_ref[0, sq], vn_ref[0, sq]], axis=0).T
        vt_ext.append([jnp.concatenate([vt[g * HEAD_DIM:(g + 1) * HEAD_DIM], ones_rows], axis=0)
                       for g in range(nkv)])

    kw = min(LANES, nkv * HEAD_DIM)
    zeros = jnp.zeros((HEAD_DIM, tq), q_ref.dtype)
    q_pads = {}

    def padded_queries(j):
        if j not in q_pads:
            sq, jt = split(j)
            qt = q_ref[0, sq, jt * tq:(jt + 1) * tq, :].T
            pads = []
            for h in range(n_heads):
                slot = ((h // group) * HEAD_DIM % kw) // HEAD_DIM
                parts = [zeros] * (kw // HEAD_DIM)
                parts[slot] = qt[h * HEAD_DIM:(h + 1) * HEAD_DIM, :]
                pads.append(parts[0] if len(parts) == 1 else jnp.concatenate(parts, axis=0))
            q_pads.clear()
            q_pads[j] = pads
        return q_pads[j]

    units = [(j, kb, h) for j in range(n_seq_blk * tiles_per_seq) for kb in range(nk // sub)
             for h in range(n_heads)]

    cw = LANES if tq % LANES == 0 else tq
    n_ch = tq // cw

    def query_chunks(kb):
        k_lo = kb * sub - halo
        lo = max(0, (k_lo - half_window) // cw)
        hi = min(n_ch, -(-(k_lo + sub + half_window) // cw))
        return list(range(lo, hi))

    def scores(idx):
        j, kb, h = units[idx]
        sq, jt = split(j)
        row0 = jt * tq + kb * sub
        lane0 = (h // group) * HEAD_DIM // kw * kw
        cs = query_chunks(kb)
        return jnp.dot(kcat[sq][row0:row0 + sub, lane0:lane0 + kw],
                       padded_queries(j)[h][:, cs[0] * cw:(cs[-1] + 1) * cw], preferred_element_type=F32)

    key_row = lax.broadcasted_iota(jnp.int32, (sub, cw), 0)
    query = lax.broadcasted_iota(jnp.int32, (sub, cw), 1)

    def needs_mask(kb, c):
        k_lo = kb * sub - halo
        in_band = k_lo - (c * cw + cw - 1) >= -half_window and k_lo + sub - 1 - c * cw <= half_window
        return not (in_band and k_lo >= 0 and k_lo + sub <= tq)

    def valid_mask(j, kb, c):
        q0 = blk0 + split(j)[1] * tq
        kpos = q0 - halo + kb * sub + key_row
        rel = kpos - (q0 + c * cw + query)
        return (jnp.abs(rel) <= half_window) & (kpos >= 0) & (kpos < seq_len)

    def start_state():
        shape = (VT_ROWS_A // SUBLANES, SUBLANES, cw)
        if has_sink:
            m0 = [jnp.full((SUBLANES, cw), sink_ref[sink_layer, h] * LOG2E, F32) for h in range(n_heads)]
            denom = ((lax.broadcasted_iota(jnp.int32, shape, 0) == HEAD_DIM // SUBLANES)
                     & (lax.broadcasted_iota(jnp.int32, shape, 1) == 0))
            a0 = denom.astype(F32)
        else:
            m0 = [jnp.full((SUBLANES, cw), 0.1 * MASK_VALUE, F32)] * n_heads
            a0 = jnp.zeros(shape, F32)
        return [[m0[h]] * n_ch for h in range(n_heads)], [[a0] * n_ch for _ in range(n_heads)]

    def join(parts):
        return parts[0] if len(parts) == 1 else jnp.concatenate(parts, axis=1)

    def finish(j, m, acc):
        outs, lses = [], []
        for h in range(n_heads):
            full = join([x.reshape(VT_ROWS_A, cw) for x in acc[h]])
            l_row = full[HEAD_DIM:HEAD_DIM + 1]
            outs.append(full[:HEAD_DIM] / l_row)
            if want_lse:
                lse = join([x[0:1] for x in m[h]]) + jnp.log2(l_row)
                lses.append(jnp.broadcast_to(lse, (HEAD_DIM, tq)))
        sq, jt = split(j)
        o_ref[0, sq, jt * tq:(jt + 1) * tq, :] = jnp.concatenate(outs, axis=0).T.astype(o_ref.dtype)
        if want_lse:
            lse_ref[0, sq, jt * tq:(jt + 1) * tq, :] = jnp.concatenate(lses, axis=0).T

    look = min(LOOKAHEAD_BAND, len(units))
    pending = [scores(i) for i in range(look)]
    mask_key, masks = None, None
    for idx, (j, kb, h) in enumerate(units):
        if idx + look < len(units):
            pending.append(scores(idx + look))
        if kb == 0 and h == 0:
            m, acc = start_state()
        cs = query_chunks(kb)
        if mask_key != (j, kb):
            mask_key = (j, kb)
            masks = [valid_mask(j, kb, c) if needs_mask(kb, c) else None for c in cs]
        g = h // group
        s = pending.pop(0)
        probs, alphas = [], []
        for i, c in enumerate(cs):
            s_c = s if len(cs) == 1 else s[:, i * cw:(i + 1) * cw]
            if masks[i] is not None:
                s_c = jnp.where(masks[i], s_c, MASK_VALUE)
            s3 = s_c.reshape(sub // SUBLANES, SUBLANES, cw)
            m_new = jnp.maximum(m[h][c], _sublane_allmax(jnp.max(s3, axis=0)))
            alpha = jnp.exp2(m[h][c] - m_new)
            p3 = jnp.exp2(s3 - m_new[None])
            m[h][c] = m_new
            probs.append(p3.reshape(sub, cw).astype(BF16))
            alphas.append(alpha)
        sq, jt = split(j)
        col0 = jt * tq + kb * sub
        pv = jnp.dot(vt_ext[sq][g][:, col0:col0 + sub], join(probs), preferred_element_type=F32)
        for i, c in enumerate(cs):
            pv_c = pv if len(cs) == 1 else pv[:, i * cw:(i + 1) * cw]
            acc[h][c] = acc[h][c] * alphas[i][None] + pv_c.reshape(VT_ROWS_A // SUBLANES, SUBLANES, cw)
        if kb == nk // sub - 1 and h == n_heads - 1:
            finish(j, m, acc)


def _band_attention(q, k, v, *, nkv, group, half_window, sink=None, sink_layer=None, want_lse=False):
    b, n_seq, seq_len, wq = q.shape
    wk = nkv * HEAD_DIM
    halo = half_window
    tq = min(TQ_BAND * BAND_TILES_PER_STEP, seq_len)
    ns = max(1, min(n_seq, BAND_TILES_PER_STEP * TQ_BAND // tq))
    per = tq // halo
    n_halo = seq_len // halo

    main = lambda width: pl.BlockSpec((1, ns, tq, width), lambda bb, r, i: (bb, r, i, 0))
    prev = lambda width: pl.BlockSpec(
        (1, ns, halo, width), lambda bb, r, i: (bb, r, jnp.maximum(i * per - 1, 0), 0))
    nxt = lambda width: pl.BlockSpec(
        (1, ns, halo, width), lambda bb, r, i: (bb, r, jnp.minimum((i + 1) * per, n_halo - 1), 0))
    in_specs = [main(wq), prev(wk), main(wk), nxt(wk), prev(wk), main(wk), nxt(wk)]
    args = [q, k, k, k, v, v, v]
    if sink is not None:
        in_specs.append(pl.BlockSpec(memory_space=pltpu.SMEM))
        args.append(sink)
    out_block = main(wq)
    out_shape = [jax.ShapeDtypeStruct(q.shape, BF16)]
    out_specs = [out_block]
    if want_lse:
        out_shape.append(jax.ShapeDtypeStruct(q.shape, F32))
        out_specs.append(out_block)
    res = pl.pallas_call(
        functools.partial(_band_kernel, nkv=nkv, group=group, half_window=half_window,
                          seq_len=seq_len, sink_layer=sink_layer if sink is not None else None,
                          want_lse=want_lse),
        grid=(b, n_seq // ns, seq_len // tq),
        in_specs=in_specs,
        out_specs=out_specs,
        out_shape=out_shape,
        compiler_params=_cparams(("parallel", "parallel", "parallel")),
        name="band_attn",
    )(*args)
    return res if want_lse else res[0]


def _sigmoid(x):
    return 0.5 * jnp.tanh(0.5 * x) + 0.5


def _softplus(x):
    return jnp.maximum(x, 0.0) + jnp.log1p(jnp.exp(-jnp.abs(x)))


def _lru_kernel(af_ref, uf_ref, ab_ref, ub_ref, hf_ref, hb_ref, carry_ref):
    i = pl.program_id(1)
    tt = af_ref.shape[1]
    nblk = tt // SUBLANES

    @pl.when(i == 0)
    def _():
        carry_ref[...] = jnp.zeros(carry_ref.shape, F32)

    row = lax.broadcasted_iota(jnp.int32, (SUBLANES, LRU_WIDTH), 0)

    def scan_block(d, blk, carry, a_ref, u_ref, out_ref):
        a8 = a_ref[0, pl.ds(blk * SUBLANES, SUBLANES), :]
        u8 = u_ref[0, pl.ds(blk * SUBLANES, SUBLANES), :]
        entry = row == (0 if d == 0 else SUBLANES - 1)
        u8 = u8 + jnp.where(entry, a8 * carry, 0.0)
        for step in (1, 2, 4):
            if d == 0:
                shift, ident = step, row < step
            else:
                shift, ident = SUBLANES - step, row >= SUBLANES - step
            u8 = u8 + a8 * jnp.where(ident, 0.0, pltpu.roll(u8, shift, 0))
            if step < 4:
                a8 = a8 * pltpu.roll(a8, shift, 0)
        out_ref[0, pl.ds(blk * SUBLANES, SUBLANES), :] = u8
        last = u8[SUBLANES - 1:SUBLANES, :] if d == 0 else u8[0:1, :]
        return jnp.broadcast_to(last, (SUBLANES, LRU_WIDTH))

    def body(kb, carries):
        cf, cb = carries
        cf = scan_block(0, kb, cf, af_ref, uf_ref, hf_ref)
        cb = scan_block(1, nblk - 1 - kb, cb, ab_ref, ub_ref, hb_ref)
        return cf, cb

    cf, cb = lax.fori_loop(0, nblk, body, (carry_ref[0], carry_ref[1]))
    carry_ref[0] = cf
    carry_ref[1] = cb


def _lru(af, uf, ab, ub):
    b, s, w = af.shape
    tt = min(TT_LRU, s)
    nt = s // tt
    fwd = pl.BlockSpec((1, tt, w), lambda bb, i: (bb, i, 0))
    bwd = pl.BlockSpec((1, tt, w), lambda bb, i: (bb, nt - 1 - i, 0))
    return pl.pallas_call(
        _lru_kernel,
        grid=(b, nt),
        in_specs=[fwd, fwd, bwd, bwd],
        out_specs=[fwd, bwd],
        out_shape=[jax.ShapeDtypeStruct((b, s, w), F32), jax.ShapeDtypeStruct((b, s, w), F32)],
        scratch_shapes=[pltpu.VMEM((2, SUBLANES, w), F32)],
        compiler_params=_cparams(("parallel", "arbitrary")),
        name="lru",
    )(af, uf, ab, ub)


def _gelu_tanh(x):
    return 0.5 * x * (1.0 + jnp.tanh(math.sqrt(2.0 / math.pi) * (x + 0.044715 * (x * x * x))))


def _merge_kernel(x_ref, ya_ref, hf_ref, hb_ref, by_ref, yc_ref,
                  o0_ref, o1_ref, o2_ref, l0_ref, l1_ref, l2_ref, g_ref, wg_ref, gbias_ref,
                  wa_ref, wb_ref, wc_ref, wd_ref, wo_ref, out_ref, perm_scr):
    d_model = x_ref.shape[2]
    tm = x_ref.shape[1]
    x = x_ref[0]
    var = jnp.mean(x * x, axis=-1, keepdims=True)
    hn = (x * lax.rsqrt(var + EPS) * g_ref[...]).astype(BF16)

    def natural(ref, slot):
        dil = ref.shape[1]
        if dil == 1:
            return ref[0, 0].astype(F32)
        for r in range(dil):
            rows = ref[0, r].astype(F32)
            for j in range(rows.shape[1] // LANES):
                perm_scr[slot, j, pl.ds(r, tm // dil, stride=dil), :] = rows[:, j * LANES:(j + 1) * LANES]
        return jnp.concatenate([perm_scr[slot, j] for j in range(perm_scr.shape[1])], axis=1)

    yb = ((hf_ref[0] + hb_ref[0]) * _gelu_tanh(by_ref[0].astype(F32))).astype(BF16)

    l0, l1, l2 = natural(l0_ref, 0), natural(l1_ref, 0), natural(l2_ref, 1)
    o0, o1, o2 = natural(o0_ref, 2), natural(o1_ref, 2), natural(o2_ref, 3)
    mx = jnp.maximum(jnp.maximum(l0, l1), l2)
    e0, e1, e2 = jnp.exp2(l0 - mx), jnp.exp2(l1 - mx), jnp.exp2(l2 - mx)
    yd = ((e0 * o0 + e1 * o1 + e2 * o2) / (e0 + e1 + e2)).astype(BF16)

    def gate_logits(kk):
        off = wg_ref.shape[1] - N_BRANCH * d_model + kk * d_model
        return lambda: jnp.dot(hn, wg_ref[:, off:off + d_model], preferred_element_type=F32)

    def branch(y, w_ref):
        return lambda: jnp.dot(y, w_ref[...], preferred_element_type=F32)

    dots = []
    for kk, (y, w_ref) in enumerate(((ya_ref[0], wa_ref), (yb, wb_ref), (yc_ref[0], wc_ref), (yd, wd_ref))):
        dots += [gate_logits(kk), branch(y, w_ref)]
    pending = [dots[0](), dots[1]()]
    merged = None
    for kk in range(N_BRANCH):
        pending += [d() for d in dots[2 * kk + 2:2 * kk + 4]]
        logits, proj = pending.pop(0), pending.pop(0)
        term = _sigmoid(logits + gbias_ref[kk:kk + 1, :]) * proj
        merged = term if merged is None else merged + term

    out_ref[0] = x + jnp.dot(merged.astype(BF16), wo_ref[...], preferred_element_type=F32)


def _merge(x, layer, ya, hf, hb, by, yc, o_d, lse_d, g, w_all, gbias, wa, wb, wc, wd, wo):
    b, s, d = x.shape
    tm = min(TM_MERGE, s)
    tok = lambda width: pl.BlockSpec((1, tm, width), lambda bb, i: (bb, i, 0))
    dw = D_HEADS * HEAD_DIM
    res = [pl.BlockSpec((1, dil, tm // dil, dw), lambda bb, i: (bb, 0, i, 0)) for _, dil in D_PATTERNS]
    return pl.pallas_call(
        _merge_kernel,
        grid=(b, s // tm),
        in_specs=[tok(d), tok(512), tok(LRU_WIDTH), tok(LRU_WIDTH), tok(LRU_WIDTH), tok(512),
                  *res, *res, _layer(g, layer), _layer(w_all, layer), _layer(gbias, layer),
                  _layer(wa, layer), _layer(wb, layer), _layer(wc, layer), _layer(wd, layer), _layer(wo, layer)],
        out_specs=tok(d),
        out_shape=jax.ShapeDtypeStruct((b, s, d), F32),
        scratch_shapes=[pltpu.VMEM((4, dw // LANES, tm, LANES), F32)],
        compiler_params=_cparams(("parallel", "parallel")),
        name="merge",
    )(x, ya, hf, hb, by, yc, *o_d, *lse_d, g, w_all, gbias, wa, wb, wc, wd, wo)


def _mlp_kernel(x_ref, g_ref, w1_ref, w2_ref, gf_ref, out_ref, *, final_norm):
    x = x_ref[0]
    var = jnp.mean(x * x, axis=-1, keepdims=True)
    hn = (x * lax.rsqrt(var + EPS) * g_ref[...]).astype(BF16)
    hidden = w1_ref.shape[1]
    acc = x
    for c in range(hidden // MLP_CHUNK):
        h = jnp.dot(hn, w1_ref[:, c * MLP_CHUNK:(c + 1) * MLP_CHUNK], preferred_element_type=F32)
        h = jnp.square(jnp.maximum(h, 0.0)).astype(BF16)
        acc = acc + jnp.dot(h, w2_ref[c * MLP_CHUNK:(c + 1) * MLP_CHUNK, :], preferred_element_type=F32)
    if final_norm:
        var = jnp.mean(acc * acc, axis=-1, keepdims=True)
        acc = acc * lax.rsqrt(var + EPS) * gf_ref[...]
    out_ref[0] = acc


def _mlp(x, layer, g, w1, w2, g_final, final_norm):
    b, s, d = x.shape
    tm = min(TM_MLP, s)
    tok = pl.BlockSpec((1, tm, d), lambda bb, i: (bb, i, 0))
    return pl.pallas_call(
        functools.partial(_mlp_kernel, final_norm=final_norm),
        grid=(b, s // tm),
        in_specs=[tok, _layer(g, layer), _layer(w1, layer), _layer(w2, layer), _resident(g_final.shape)],
        out_specs=tok,
        out_shape=jax.ShapeDtypeStruct((b, s, d), F32),
        compiler_params=_cparams(("parallel", "parallel")),
        name="mlp",
    )(x, g, w1, w2, g_final)


def _dense_block_diag(w):
    nb, bw = w.shape[-3], w.shape[-2]
    eye = jnp.eye(nb, dtype=w.dtype)
    return jnp.einsum('...ncd,nm->...ncmd', w, eye).reshape(w.shape[:-3] + (nb * bw, nb * bw))


def kernel(x, norm_mix_g, w_in, gate_bias, qk_norm_g, conv_w, conv_b, lru_gate_w, lru_gate_b, lru_lambda, sink_logit, w_proj_a, w_proj_b, w_proj_c, w_proj_d, w_out, norm_mlp_g, w_mlp1, w_mlp2, norm_final_g):
    b, s, d = x.shape
    depth = w_in.shape[0]
    ax_t, seq_t = _rope_tables(s)
    dw = D_HEADS * HEAD_DIM
    n_groups = len(D_PATTERNS)
    bf = lambda w: w.astype(BF16)
    w_in_bf, wa, wb, wc, wd, wo, w1, w2 = map(bf, (w_in, w_proj_a, w_proj_b, w_proj_c, w_proj_d, w_out, w_mlp1, w_mlp2))
    g_mix, g_mlp, conv_b3 = norm_mix_g[:, None, :], norm_mlp_g[:, None, :], conv_b[:, None, :]
    qk_gain = jnp.tile(qk_norm_g, (1, 1, LANES // HEAD_DIM))
    gate_w_dense = bf(_dense_block_diag(lru_gate_w))

    for l in range(depth):
        outs = _inproj(x, l, g_mix, w_in_bf, qk_gain, ax_t, seq_t, conv_w, conv_b3, gate_w_dense,
                       lru_gate_b, lru_lambda)
        qta, ka, vta, af, uf, ab, ub, by, qc, kc, vc = outs[:11]
        qd, kd, vd = outs[11:11 + n_groups], outs[11 + n_groups:11 + 2 * n_groups], outs[11 + 2 * n_groups:11 + 3 * n_groups]

        ya = _attn_a(qta, ka, vta)

        hf, hb = _lru(af, uf, ab, ub)

        yc = _band_attention(qc[:, None], kc[:, None], vc[:, None], nkv=C_KV_HEADS, group=C_HEADS // C_KV_HEADS,
                             half_window=C_HALF_WINDOW, sink=sink_logit, sink_layer=l)[:, 0]

        o_d, lse_d = [], []
        for gi, (window, dil) in enumerate(D_PATTERNS):
            o, lse = _band_attention(qd[gi], kd[gi], vd[gi], nkv=D_HEADS, group=1,
                                     half_window=window // (2 * dil), want_lse=True)
            o_d.append(o)
            lse_d.append(lse)

        x = _merge(x, l, ya, hf, hb, by, yc, o_d, lse_d, g_mix, w_in_bf, gate_bias, wa, wb, wc, wd, wo)
        x = _mlp(x, l, g_mlp, w1, w2, norm_final_g[None, :], final_norm=(l == depth - 1))
    return x
```

```python
import functools
import math

import jax
import jax.numpy as jnp
from jax import lax
from jax.experimental import pallas as pl
from jax.experimental.pallas import tpu as pltpu

F32 = jnp.float32
BF16 = jnp.bfloat16

HEAD_DIM = 64
ROPE_THETA = 10000.0
GRID_W = 64
EPS = 1e-6
MASK_VALUE = -1e30
LOG2E = math.log2(math.e)

A_HEADS, A_KV_HEADS = 8, 2
LRU_WIDTH, LRU_BLOCKS, LRU_C = 512, 8, 8.0
CONV_W, CONV_LEFT = 4, 2
C_HEADS, C_KV_HEADS, C_HALF_WINDOW = 8, 2, 128
D_PATTERNS = ((128, 1), (512, 4), (2048, 16))
D_HEADS = 4
N_BRANCH = 4

LANES = 128
SUBLANES = 8
VMEM_LIMIT_BYTES = 56 * 1024 * 1024

TM_INPROJ = 512
TQ_ATTN_A = 256
SUB_ATTN_A = 128
TILES_PER_STEP_ATTN_A = 2
LOOKAHEAD_ATTN_A = 14
LOOKAHEAD_BAND = 6
LOOKAHEAD_INPROJ = 6
SUB_BAND = 128
CHUNKS_PER_BODY_ATTN_A = 16
VT_ROWS_A = HEAD_DIM + 16
TQ_BAND = 256
BAND_TILES_PER_STEP = 16
TT_LRU = 2048
LRU_ROWS_PER_STEP = 2
TM_MERGE = 512
TM_MLP = 1024
MLP_CHUNK = 1024

Q_SCALE = HEAD_DIM ** -0.5 * LOG2E


def _cparams(sem):
    return pltpu.CompilerParams(dimension_semantics=sem, vmem_limit_bytes=VMEM_LIMIT_BYTES)


def _layer(arr, layer):
    zeros = (0,) * (arr.ndim - 1)
    return pl.BlockSpec((None,) + arr.shape[1:], lambda *_: (layer,) + zeros, pipeline_mode=pl.Buffered(1))


def _resident(shape):
    nd = len(shape)
    return pl.BlockSpec(shape, lambda *_: (0,) * nd, pipeline_mode=pl.Buffered(1))


def _rope_tables(seq):
    pos = jnp.arange(seq, dtype=F32)
    lane = jnp.arange(LANES)
    d = lane % HEAD_DIM

    inv = ROPE_THETA ** (-jnp.arange(0, HEAD_DIM, 2, dtype=F32) / HEAD_DIM)
    ang = pos[:, None] * inv[None, :]
    f = d % (HEAD_DIM // 2)
    cos_s, sin_s = jnp.cos(ang)[:, f], jnp.sin(ang)[:, f]
    lo = (d < HEAD_DIM // 2)[None, :]
    seq_t = (cos_s, jnp.where(lo, -sin_s, 0.0), jnp.where(lo, 0.0, sin_s))

    quarter = HEAD_DIM // 4
    inv_ax = ROPE_THETA ** (-jnp.arange(0, HEAD_DIM // 2, 2, dtype=F32) / (HEAD_DIM // 2))
    n_rows = seq // GRID_W
    ang_r = jnp.arange(n_rows, dtype=F32)[:, None] * inv_ax[None, :]
    ang_c = jnp.arange(GRID_W, dtype=F32)[:, None] * inv_ax[None, :]

    def expand(by_row, by_col):
        r = jnp.repeat(by_row, GRID_W, axis=0)
        c = jnp.tile(by_col, (n_rows, 1))
        return jnp.tile(jnp.concatenate([r, r, c, c], axis=1), (1, LANES // HEAD_DIM))

    cos_a, sin_a = expand(jnp.cos(ang_r), jnp.cos(ang_c)), expand(jnp.sin(ang_r), jnp.sin(ang_c))
    lo_a = (d % (HEAD_DIM // 2) < quarter)[None, :]
    ax_t = (cos_a, jnp.where(lo_a, -sin_a, 0.0), jnp.where(lo_a, 0.0, sin_a))
    return ax_t, seq_t


def _rotate(x, tables, half):
    cos, sin_a, sin_b = tables
    return x * cos + pltpu.roll(x, LANES - half, 1) * sin_a + pltpu.roll(x, half, 1) * sin_b


def _split_dot(x, m):
    hi = x.astype(BF16)
    lo = (x - hi.astype(F32)).astype(BF16)
    return jnp.dot(jnp.concatenate([hi, lo], axis=1), jnp.concatenate([m, m], axis=0),
                   preferred_element_type=F32)


def _head_mean_matrix():
    r = lax.broadcasted_iota(jnp.int32, (LANES, LANES), 0) // HEAD_DIM
    c = lax.broadcasted_iota(jnp.int32, (LANES, LANES), 1) // HEAD_DIM
    return jnp.where(r == c, 1.0 / HEAD_DIM, 0.0).astype(BF16)


def _head_rmsnorm(x, gain, mean_mat):
    var = _split_dot(x * x, mean_mat)
    return x * lax.rsqrt(var + EPS) * gain


IN_WIDTHS = (512, 128, 128, 512, 512, 512, 128, 128, 768, 768, 768)
IN_OFFSETS = tuple(sum(IN_WIDTHS[:i]) for i in range(len(IN_WIDTHS)))
N_IN = sum(IN_WIDTHS)


def _inproj_kernel(x_ref, xprev_ref, xnext_ref, g_ref, w_ref, qkg_ref,
                   ca_ref, saa_ref, sab_ref, cs_ref, ssa_ref, ssb_ref,
                   cw_ref, cb_ref, gw_ref, gb_ref, lam_ref,
                   qta_ref, ka_ref, vta_ref, af_ref, uf_ref, ab_ref, ub_ref, by_ref,
                   qc_ref, kc_ref, vc_ref, *rest):
    d_refs, perm_scr, xpad_scr = rest[:9], rest[9], rest[10]
    tile = pl.program_id(0)
    n_tiles = pl.num_programs(0)

    def normed(rows):
        var = jnp.mean(rows * rows, axis=-1, keepdims=True)
        return (rows * lax.rsqrt(var + EPS) * g_ref[...]).astype(BF16)

    x = x_ref[0]
    hn = normed(x)

    def proj(seg, lo=0, width=None):
        off = IN_OFFSETS[seg] + lo
        width = IN_WIDTHS[seg] if width is None else width
        return lambda: jnp.dot(hn, w_ref[:, off:off + width], preferred_element_type=F32)

    ax_t = (ca_ref[...], saa_ref[...], sab_ref[...])
    seq_t = (cs_ref[...], ssa_ref[...], ssb_ref[...])
    mean_mat = _head_mean_matrix()
    gq = qkg_ref[0:1, :]
    gk = qkg_ref[1:2, :]
    tm = x.shape[0]
    dw = D_HEADS * HEAD_DIM

    def finish_aq(acc):
        for j in range(IN_WIDTHS[0] // LANES):
            piece = _head_rmsnorm(acc[:, j * LANES:(j + 1) * LANES], gq, mean_mat)
            piece = _rotate(piece, ax_t, HEAD_DIM // 4) * Q_SCALE
            qta_ref[0, j * LANES:(j + 1) * LANES, :] = piece.T.astype(BF16)

    def finish_akv(acc):
        ak = _rotate(_head_rmsnorm(acc[:, :LANES], gk, mean_mat), ax_t, HEAD_DIM // 4).astype(BF16)
        avt = acc[:, LANES:].T.astype(BF16)
        pad_row = lax.broadcasted_iota(jnp.int32, (VT_ROWS_A - HEAD_DIM, tm), 0)
        ones_rows = (pad_row == 0).astype(BF16)
        for g in range(A_KV_HEADS):
            ka_ref[0, g] = ak[:, g * HEAD_DIM:(g + 1) * HEAD_DIM]
            vta_ref[0, g, 0] = jnp.concatenate([avt[g * HEAD_DIM:(g + 1) * HEAD_DIM, :], ones_rows], axis=0)

    lru = {}

    def finish_bx(acc):
        halo = jnp.dot(normed(jnp.concatenate([xprev_ref[0], xnext_ref[0]], axis=0)),
                       w_ref[:, IN_OFFSETS[3]:IN_OFFSETS[3] + LRU_WIDTH], preferred_element_type=F32)
        xpad_scr[0:SUBLANES] = jnp.where(tile > 0, halo[:SUBLANES], 0.0)
        xpad_scr[SUBLANES:SUBLANES + tm] = acc
        xpad_scr[SUBLANES + tm:] = jnp.where(tile < n_tiles - 1, halo[SUBLANES:], 0.0)
        xc = cb_ref[...]
        base = SUBLANES - CONV_LEFT
        for jj in range(CONV_W):
            xc = xc + xpad_scr[base + jj:base + jj + tm, :] * cw_ref[jj:jj + 1, :]
        lru["xc"] = xc
        lru["xcb"] = xc.astype(BF16)

    def gate_dots():
        return [jnp.dot(lru["xcb"], gw_ref[d, g], preferred_element_type=F32) for d in range(2) for g in range(2)]

    def finish_gates(dots):
        xc = lru["xc"]
        for d, (a_ref, u_ref) in enumerate(((af_ref, uf_ref), (ab_ref, ub_ref))):
            c2 = (-0.5 * LRU_C * LOG2E) * _softplus(-lam_ref[d:d + 1, :])
            a = jnp.exp2(c2 * jnp.tanh(0.5 * dots[2 * d] + 0.5 * gb_ref[d, 0:1, :]) + c2)
            gi = _sigmoid(dots[2 * d + 1] + gb_ref[d, 1:2, :])
            a_ref[0] = a
            z = (1.0 - a) * (1.0 + a)
            root = jnp.where(z > 0.0, z * lax.rsqrt(z), 0.0)
            u_ref[0] = root * (gi * xc)

    def finish_by(acc):
        by_ref[0] = acc.astype(BF16)

    def rope_piece(piece, scale):
        piece = _rotate(piece, seq_t, HEAD_DIM // 2)
        return piece if scale is None else piece * scale

    def finish_cq(acc):
        for j in range(IN_WIDTHS[5] // LANES):
            qc_ref[0, :, j * LANES:(j + 1) * LANES] = rope_piece(acc[:, j * LANES:(j + 1) * LANES], Q_SCALE).astype(BF16)

    def finish_ckv(acc):
        kc_ref[0] = rope_piece(acc[:, :LANES], None).astype(BF16)
        vc_ref[0] = acc[:, LANES:].astype(BF16)

    def finish_d(ref, dil, scale, rope):
        def finish(acc):
            for j in range(dw // LANES):
                piece = acc[:, j * LANES:(j + 1) * LANES]
                if rope:
                    piece = rope_piece(piece, scale)
                if dil == 1:
                    ref[0, 0, :, j * LANES:(j + 1) * LANES] = piece.astype(BF16)
                else:
                    perm_scr[j] = piece
            if dil > 1:
                for r in range(dil):
                    rows = [perm_scr[j, pl.ds(r, tm // dil, stride=dil), :] for j in range(dw // LANES)]
                    ref[0, r] = jnp.concatenate(rows, axis=1).astype(BF16)
        return finish

    segments = [(proj(0), finish_aq), (proj(1, 0, 2 * LANES), finish_akv), (proj(3), finish_bx),
                (proj(4), finish_by), (gate_dots, finish_gates), (proj(5), finish_cq),
                (proj(6, 0, 2 * LANES), finish_ckv)]
    for gi, (_, dil) in enumerate(D_PATTERNS):
        for ti, (seg, scale, rope) in enumerate(((8, Q_SCALE, True), (9, None, True), (10, None, False))):
            segments.append((proj(seg, gi * dw, dw), finish_d(d_refs[ti * len(D_PATTERNS) + gi], dil, scale, rope)))

    issued, pending = 0, []
    for idx, (_, finish) in enumerate(segments):
        horizon = min(len(segments), idx + 1 + LOOKAHEAD_INPROJ)
        while issued < horizon and (segments[issued][0] is not gate_dots or "xcb" in lru):
            pending.append(segments[issued][0]())
            issued += 1
        finish(pending.pop(0))


def _inproj(x, layer, g, w_all, qk_gain128, ax_t, seq_t, conv_w, conv_b, gate_w, gate_b, lam):
    b, s, d = x.shape
    tm = min(TM_INPROJ, s)
    nt = s // tm
    per, n8 = tm // SUBLANES, s // SUBLANES
    halo_prev = pl.BlockSpec((1, SUBLANES, d), lambda i, bb: (bb, jnp.maximum(i * per - 1, 0), 0))
    halo_next = pl.BlockSpec((1, SUBLANES, d), lambda i, bb: (bb, jnp.minimum((i + 1) * per, n8 - 1), 0))
    tok = lambda width: pl.BlockSpec((1, tm, width), lambda i, bb: (bb, i, 0))
    tab = pl.BlockSpec((tm, LANES), lambda i, bb: (i, 0))
    dw = D_HEADS * HEAD_DIM
    d_shapes = tuple(jax.ShapeDtypeStruct((b, dil, s // dil, dw), BF16) for _, dil in D_PATTERNS)
    d_specs = tuple(pl.BlockSpec((1, dil, tm // dil, dw), lambda i, bb: (bb, 0, i, 0)) for _, dil in D_PATTERNS)
    out_shape = (
        jax.ShapeDtypeStruct((b, A_HEADS * HEAD_DIM, s), BF16),
        jax.ShapeDtypeStruct((b, A_KV_HEADS, s, HEAD_DIM), BF16),
        jax.ShapeDtypeStruct((b, A_KV_HEADS, nt, VT_ROWS_A, tm), BF16),
        jax.ShapeDtypeStruct((b, s, LRU_WIDTH), F32),
        jax.ShapeDtypeStruct((b, s, LRU_WIDTH), F32),
        jax.ShapeDtypeStruct((b, s, LRU_WIDTH), F32),
        jax.ShapeDtypeStruct((b, s, LRU_WIDTH), F32),
        jax.ShapeDtypeStruct((b, s, LRU_WIDTH), BF16),
        jax.ShapeDtypeStruct((b, s, 512), BF16),
        jax.ShapeDtypeStruct((b, s, 128), BF16),
        jax.ShapeDtypeStruct((b, s, 128), BF16),
    ) + d_shapes * 3
    out_specs = (
        pl.BlockSpec((1, A_HEADS * HEAD_DIM, tm), lambda i, bb: (bb, 0, i)),
        pl.BlockSpec((1, A_KV_HEADS, tm, HEAD_DIM), lambda i, bb: (bb, 0, i, 0)),
        pl.BlockSpec((1, A_KV_HEADS, 1, VT_ROWS_A, tm), lambda i, bb: (bb, 0, i, 0, 0)),
        tok(LRU_WIDTH), tok(LRU_WIDTH), tok(LRU_WIDTH), tok(LRU_WIDTH), tok(LRU_WIDTH), tok(512), tok(128), tok(128),
    ) + d_specs * 3
    return pl.pallas_call(
        _inproj_kernel,
        grid=(nt, b),
        in_specs=[tok(d), halo_prev, halo_next, _layer(g, layer),
                  pl.BlockSpec((None, d, N_IN), lambda i, bb: (layer, 0, 0), pipeline_mode=pl.Buffered(1)),
                  _layer(qk_gain128, layer),
                  tab, tab, tab, tab, tab, tab,
                  _layer(conv_w, layer), _layer(conv_b, layer), _layer(gate_w, layer),
                  _layer(gate_b, layer), _layer(lam, layer)],
        out_specs=out_specs,
        out_shape=out_shape,
        scratch_shapes=[pltpu.VMEM((dw // LANES, tm, LANES), F32),
                        pltpu.VMEM((tm + 2 * SUBLANES, LRU_WIDTH), F32)],
        compiler_params=_cparams(("parallel", "parallel")),
        name="inproj",
    )(x, x, x, g, w_all, qk_gain128, *ax_t, *seq_t, conv_w, conv_b, gate_w, gate_b, lam)


def _sublane_allmax(x):
    for shift in (4, 2, 1):
        x = jnp.maximum(x, pltpu.roll(x, shift, 0))
    return x


def _attn_a_kernel(q_ref, k_ref, v_ref, o_ref, m_ref, acc_ref, *, group, sub):
    n_chunks, chunk = v_ref.shape[2], v_ref.shape[4]
    tq = min(TQ_ATTN_A, q_ref.shape[2])
    n_tiles = q_ref.shape[2] // tq
    m_ref[...] = jnp.full(m_ref.shape, MASK_VALUE, F32)
    acc_ref[...] = jnp.zeros(acc_ref.shape, F32)

    per_body = min(CHUNKS_PER_BODY_ATTN_A, n_chunks)
    n_sub = chunk // sub
    pair = 2 if n_sub % 2 == 0 else 1
    units = [(cc, u, h) for cc in range(per_body) for up in range(n_sub // pair)
             for h in range(n_tiles * group) for u in range(up * pair, (up + 1) * pair)]

    def body(it, carry):
        c0 = it * per_body

        def scores(idx):
            cc, u, h = units[idx]
            row0 = pl.multiple_of((c0 + cc) * chunk + u * sub, sub)
            k = k_ref[0, 0, pl.ds(row0, sub), :]
            j, head = divmod(h, group)
            qt = q_ref[0, head * HEAD_DIM:(head + 1) * HEAD_DIM, j * tq:(j + 1) * tq]
            return jnp.dot(k, qt, preferred_element_type=F32)

        pending = [scores(i) for i in range(LOOKAHEAD_ATTN_A)]
        probs = []
        for idx, (cc, u, h) in enumerate(units):
            if idx + LOOKAHEAD_ATTN_A < len(units):
                pending.append(scores(idx + LOOKAHEAD_ATTN_A))
            s3 = pending.pop(0).reshape(sub // SUBLANES, SUBLANES, tq)
            if not probs:
                m_start = m_ref[h]
                m_old = m_start
            m_new = jnp.maximum(m_old, _sublane_allmax(jnp.max(s3, axis=0)))
            if probs:
                beta = jnp.exp2(m_old - m_new).astype(BF16)
                beta = jnp.concatenate([beta, beta], axis=0)[None]
                probs = [(p.reshape(sub // 16, 16, tq) * beta).reshape(sub, tq) for p in probs]
            probs.append(jnp.exp2(s3 - m_new[None]).reshape(sub, tq).astype(BF16))
            m_old = m_new
            if len(probs) == pair:
                u0 = u - (pair - 1)
                vt = v_ref[0, 0, c0 + cc, :, u0 * sub:(u + 1) * sub]
                pv = jnp.dot(vt, jnp.concatenate(probs, axis=0), preferred_element_type=F32)
                alpha = jnp.exp2(m_start - m_new)
                acc_ref[h] = acc_ref[h] * alpha[None] + pv.reshape(VT_ROWS_A // SUBLANES, SUBLANES, tq)
                m_ref[h] = m_new
                probs = []
        return carry

    lax.fori_loop(0, n_chunks // per_body, body, 0)
    for j in range(n_tiles):
        outs = []
        for h in range(j * group, (j + 1) * group):
            acc = acc_ref[h].reshape(VT_ROWS_A, tq)
            outs.append(acc[:HEAD_DIM] / acc[HEAD_DIM:HEAD_DIM + 1])
        o_ref[0, j * tq:(j + 1) * tq, :] = jnp.concatenate(outs, axis=0).T.astype(o_ref.dtype)


def _attn_a(qt, k, vt):
    b, hd_all, s = qt.shape
    nkv, nt, chunk = vt.shape[1], vt.shape[2], vt.shape[4]
    group = hd_all // HEAD_DIM // nkv
    tq = min(TQ_ATTN_A * TILES_PER_STEP_ATTN_A, s)
    n_state = group * (tq // min(TQ_ATTN_A, s))
    gw = group * HEAD_DIM
    return pl.pallas_call(
        functools.partial(_attn_a_kernel, group=group, sub=min(SUB_ATTN_A, chunk)),
        grid=(b, nkv, s // tq),
        in_specs=[pl.BlockSpec((1, gw, tq), lambda bb, g, i: (bb, g, i)),
                  pl.BlockSpec((1, 1, s, HEAD_DIM), lambda bb, g, i: (bb, g, 0, 0)),
                  pl.BlockSpec((1, 1, nt, VT_ROWS_A, chunk), lambda bb, g, i: (bb, g, 0, 0, 0))],
        out_specs=pl.BlockSpec((1, tq, gw), lambda bb, g, i: (bb, i, g)),
        out_shape=jax.ShapeDtypeStruct((b, s, hd_all), BF16),
        scratch_shapes=[pltpu.VMEM((n_state, SUBLANES, min(TQ_ATTN_A, s)), F32),
                        pltpu.VMEM((n_state, VT_ROWS_A // SUBLANES, SUBLANES, min(TQ_ATTN_A, s)), F32)],
        compiler_params=_cparams(("parallel", "parallel", "parallel")),
        name="attn_a",
    )(qt, k, vt)


def _band_kernel(*refs, nkv, group, half_window, seq_len, sink_layer, want_lse):
    has_sink = sink_layer is not None
    q_ref, kp_ref, km_ref, kn_ref, vp_ref, vm_ref, vn_ref = refs[:7]
    rest = list(refs[7:])
    sink_ref = rest.pop(0) if has_sink else None
    o_ref = rest.pop(0)
    lse_ref = rest.pop(0) if want_lse else None

    n_seq_blk, tq_blk = q_ref.shape[1], q_ref.shape[2]
    tq = min(TQ_BAND, tq_blk)
    tiles_per_seq = tq_blk // tq
    split = lambda j: divmod(j, tiles_per_seq)
    halo = kp_ref.shape[2]
    nk = tq + 2 * halo
    sub = SUB_BAND if nk % SUB_BAND == 0 else halo
    n_heads = nkv * group
    blk0 = pl.program_id(2) * tq_blk

    pad_row = lax.broadcasted_iota(jnp.int32, (VT_ROWS_A - HEAD_DIM, tq_blk + 2 * halo), 0)
    ones_rows = (pad_row == 0).astype(q_ref.dtype)
    kcat, vt_ext = [], []
    for sq in range(n_seq_blk):
        kcat.append(jnp.concatenate([kp_ref[0, sq], km_ref[0, sq], kn_ref[0, sq]], axis=0))
        vt = jnp.concatenate([vp_ref[0, sq], vm_ref[0, sq], vn_ref[0, sq]], axis=0).T
        vt_ext.append([jnp.concatenate([vt[g * HEAD_DIM:(g + 1) * HEAD_DIM], ones_rows], axis=0)
                       for g in range(nkv)])

    kw = min(LANES, nkv * HEAD_DIM)
    zeros = jnp.zeros((HEAD_DIM, tq), q_ref.dtype)
    q_pads = {}

    def padded_queries(j):
        if j not in q_pads:
            sq, jt = split(j)
            qt = q_ref[0, sq, jt * tq:(jt + 1) * tq, :].T
            pads = []
            for h in range(n_heads):
                slot = ((h // group) * HEAD_DIM % kw) // HEAD_DIM
                parts = [zeros] * (kw // HEAD_DIM)
                parts[slot] = qt[h * HEAD_DIM:(h + 1) * HEAD_DIM, :]
                pads.append(parts[0] if len(parts) == 1 else jnp.concatenate(parts, axis=0))
            q_pads.clear()
            q_pads[j] = pads
        return q_pads[j]

    units = [(j, kb, h) for j in range(n_seq_blk * tiles_per_seq) for kb in range(nk // sub)
             for h in range(n_heads)]

    cw = LANES if tq % LANES == 0 else tq
    n_ch = tq // cw

    def query_chunks(kb):
        k_lo = kb * sub - halo
        lo = max(0, (k_lo - half_window) // cw)
        hi = min(n_ch, -(-(k_lo + sub + half_window) // cw))
        return list(range(lo, hi))

    def scores(idx):
        j, kb, h = units[idx]
        sq, jt = split(j)
        row0 = jt * tq + kb * sub
        lane0 = (h // group) * HEAD_DIM // kw * kw
        cs = query_chunks(kb)
        return jnp.dot(kcat[sq][row0:row0 + sub, lane0:lane0 + kw],
                       padded_queries(j)[h][:, cs[0] * cw:(cs[-1] + 1) * cw], preferred_element_type=F32)

    key_row = lax.broadcasted_iota(jnp.int32, (sub, cw), 0)
    query = lax.broadcasted_iota(jnp.int32, (sub, cw), 1)

    def needs_mask(kb, c):
        k_lo = kb * sub - halo
        in_band = k_lo - (c * cw + cw - 1) >= -half_window and k_lo + sub - 1 - c * cw <= half_window
        return not (in_band and k_lo >= 0 and k_lo + sub <= tq)

    def valid_mask(j, kb, c):
        q0 = blk0 + split(j)[1] * tq
        kpos = q0 - halo + kb * sub + key_row
        rel = kpos - (q0 + c * cw + query)
        return (jnp.abs(rel) <= half_window) & (kpos >= 0) & (kpos < seq_len)

    def start_state():
        shape = (VT_ROWS_A // SUBLANES, SUBLANES, cw)
        if has_sink:
            m0 = [jnp.full((SUBLANES, cw), sink_ref[sink_layer, h] * LOG2E, F32) for h in range(n_heads)]
            denom = ((lax.broadcasted_iota(jnp.int32, shape, 0) == HEAD_DIM // SUBLANES)
                     & (lax.broadcasted_iota(jnp.int32, shape, 1) == 0))
            a0 = denom.astype(F32)
        else:
            m0 = [jnp.full((SUBLANES, cw), 0.1 * MASK_VALUE, F32)] * n_heads
            a0 = jnp.zeros(shape, F32)
        return [[m0[h]] * n_ch for h in range(n_heads)], [[a0] * n_ch for _ in range(n_heads)]

    def join(parts):
        return parts[0] if len(parts) == 1 else jnp.concatenate(parts, axis=1)

    def finish(j, m, acc):
        outs, lses = [], []
        for h in range(n_heads):
            full = join([x.reshape(VT_ROWS_A, cw) for x in acc[h]])
            l_row = full[HEAD_DIM:HEAD_DIM + 1]
            outs.append(full[:HEAD_DIM] / l_row)
            if want_lse:
                lse = join([x[0:1] for x in m[h]]) + jnp.log2(l_row)
                lses.append(jnp.broadcast_to(lse, (HEAD_DIM, tq)))
        sq, jt = split(j)
        o_ref[0, sq, jt * tq:(jt + 1) * tq, :] = jnp.concatenate(outs, axis=0).T.astype(o_ref.dtype)
        if want_lse:
            lse_ref[0, sq, jt * tq:(jt + 1) * tq, :] = jnp.concatenate(lses, axis=0).T

    look = min(LOOKAHEAD_BAND, len(units))
    pending = [scores(i) for i in range(look)]
    mask_key, masks = None, None
    for idx, (j, kb, h) in enumerate(units):
        if idx + look < len(units):
            pending.append(scores(idx + look))
        if kb == 0 and h == 0:
            m, acc = start_state()
        cs = query_chunks(kb)
        if mask_key != (j, kb):
            mask_key = (j, kb)
            masks = [valid_mask(j, kb, c) if needs_mask(kb, c) else None for c in cs]
        g = h // group
        s = pending.pop(0)
        probs, alphas = [], []
        for i, c in enumerate(cs):
            s_c = s if len(cs) == 1 else s[:, i * cw:(i + 1) * cw]
            if masks[i] is not None:
                s_c = jnp.where(masks[i], s_c, MASK_VALUE)
            s3 = s_c.reshape(sub // SUBLANES, SUBLANES, cw)
            m_new = jnp.maximum(m[h][c], _sublane_allmax(jnp.max(s3, axis=0)))
            alpha = jnp.exp2(m[h][c] - m_new)
            p3 = jnp.exp2(s3 - m_new[None])
            m[h][c] = m_new
            probs.append(p3.reshape(sub, cw).astype(BF16))
            alphas.append(alpha)
        sq, jt = split(j)
        col0 = jt * tq + kb * sub
        pv = jnp.dot(vt_ext[sq][g][:, col0:col0 + sub], join(probs), preferred_element_type=F32)
        for i, c in enumerate(cs):
            pv_c = pv if len(cs) == 1 else pv[:, i * cw:(i + 1) * cw]
            acc[h][c] = acc[h][c] * alphas[i][None] + pv_c.reshape(VT_ROWS_A // SUBLANES, SUBLANES, cw)
        if kb == nk // sub - 1 and h == n_heads - 1:
            finish(j, m, acc)


def _band_attention(q, k, v, *, nkv, group, half_window, sink=None, sink_layer=None, want_lse=False):
    b, n_seq, seq_len, wq = q.shape
    wk = nkv * HEAD_DIM
    halo = half_window
    tq = min(TQ_BAND * BAND_TILES_PER_STEP, seq_len)
    ns = max(1, min(n_seq, BAND_TILES_PER_STEP * TQ_BAND // tq))
    per = tq // halo
    n_halo = seq_len // halo

    main = lambda width: pl.BlockSpec((1, ns, tq, width), lambda bb, r, i: (bb, r, i, 0))
    prev = lambda width: pl.BlockSpec(
        (1, ns, halo, width), lambda bb, r, i: (bb, r, jnp.maximum(i * per - 1, 0), 0))
    nxt = lambda width: pl.BlockSpec(
        (1, ns, halo, width), lambda bb, r, i: (bb, r, jnp.minimum((i + 1) * per, n_halo - 1), 0))
    in_specs = [main(wq), prev(wk), main(wk), nxt(wk), prev(wk), main(wk), nxt(wk)]
    args = [q, k, k, k, v, v, v]
    if sink is not None:
        in_specs.append(pl.BlockSpec(memory_space=pltpu.SMEM))
        args.append(sink)
    out_block = main(wq)
    out_shape = [jax.ShapeDtypeStruct(q.shape, BF16)]
    out_specs = [out_block]
    if want_lse:
        out_shape.append(jax.ShapeDtypeStruct(q.shape, F32))
        out_specs.append(out_block)
    res = pl.pallas_call(
        functools.partial(_band_kernel, nkv=nkv, group=group, half_window=half_window,
                          seq_len=seq_len, sink_layer=sink_layer if sink is not None else None,
                          want_lse=want_lse),
        grid=(b, n_seq // ns, seq_len // tq),
        in_specs=in_specs,
        out_specs=out_specs,
        out_shape=out_shape,
        compiler_params=_cparams(("parallel", "parallel", "parallel")),
        name="band_attn",
    )(*args)
    return res if want_lse else res[0]


def _sigmoid(x):
    return 0.5 * jnp.tanh(0.5 * x) + 0.5


def _softplus(x):
    return jnp.maximum(x, 0.0) + jnp.log1p(jnp.exp(-jnp.abs(x)))


def _lru_kernel(af_ref, uf_ref, ab_ref, ub_ref, hf_ref, hb_ref, carry_ref):
    i = pl.program_id(1)
    n_rows, tt = af_ref.shape[0], af_ref.shape[1]
    nblk = tt // SUBLANES

    @pl.when(i == 0)
    def _():
        carry_ref[...] = jnp.zeros(carry_ref.shape, F32)

    row = lax.broadcasted_iota(jnp.int32, (SUBLANES, LRU_WIDTH), 0)

    def scan_block(d, r, blk, carry, a_ref, u_ref, out_ref):
        a8 = a_ref[r, pl.ds(blk * SUBLANES, SUBLANES), :]
        u8 = u_ref[r, pl.ds(blk * SUBLANES, SUBLANES), :]
        entry = row == (0 if d == 0 else SUBLANES - 1)
        u8 = u8 + jnp.where(entry, a8 * carry, 0.0)
        for step in (1, 2, 4):
            if d == 0:
                shift, ident = step, row < step
            else:
                shift, ident = SUBLANES - step, row >= SUBLANES - step
            u8 = u8 + a8 * jnp.where(ident, 0.0, pltpu.roll(u8, shift, 0))
            if step < 4:
                a8 = a8 * pltpu.roll(a8, shift, 0)
        out_ref[r, pl.ds(blk * SUBLANES, SUBLANES), :] = u8
        last = u8[SUBLANES - 1:SUBLANES, :] if d == 0 else u8[0:1, :]
        return jnp.broadcast_to(last, (SUBLANES, LRU_WIDTH))

    def body(kb, carries):
        new = []
        for r in range(n_rows):
            new.append(scan_block(0, r, kb, carries[2 * r], af_ref, uf_ref, hf_ref))
            new.append(scan_block(1, r, nblk - 1 - kb, carries[2 * r + 1], ab_ref, ub_ref, hb_ref))
        return tuple(new)

    carries = lax.fori_loop(0, nblk, body, tuple(carry_ref[c] for c in range(2 * n_rows)))
    for c in range(2 * n_rows):
        carry_ref[c] = carries[c]


def _lru(af, uf, ab, ub):
    b, s, w = af.shape
    rows = LRU_ROWS_PER_STEP if b % LRU_ROWS_PER_STEP == 0 else 1
    tt = min(TT_LRU // rows, s)
    nt = s // tt
    fwd = pl.BlockSpec((rows, tt, w), lambda bb, i: (bb, i, 0))
    bwd = pl.BlockSpec((rows, tt, w), lambda bb, i: (bb, nt - 1 - i, 0))
    return pl.pallas_call(
        _lru_kernel,
        grid=(b // rows, nt),
        in_specs=[fwd, fwd, bwd, bwd],
        out_specs=[fwd, bwd],
        out_shape=[jax.ShapeDtypeStruct((b, s, w), F32), jax.ShapeDtypeStruct((b, s, w), F32)],
        scratch_shapes=[pltpu.VMEM((2 * rows, SUBLANES, w), F32)],
        compiler_params=_cparams(("parallel", "arbitrary")),
        name="lru",
    )(af, uf, ab, ub)


def _gelu_tanh(x):
    return 0.5 * x * (1.0 + jnp.tanh(math.sqrt(2.0 / math.pi) * (x + 0.044715 * (x * x * x))))


def _merge_kernel(x_ref, ya_ref, hf_ref, hb_ref, by_ref, yc_ref,
                  o0_ref, o1_ref, o2_ref, l0_ref, l1_ref, l2_ref, g_ref, wg_ref, gbias_ref,
                  wa_ref, wb_ref, wc_ref, wd_ref, wo_ref, out_ref, perm_scr):
    d_model = x_ref.shape[2]
    tm = x_ref.shape[1]
    x = x_ref[0]
    var = jnp.mean(x * x, axis=-1, keepdims=True)
    hn = (x * lax.rsqrt(var + EPS) * g_ref[...]).astype(BF16)

    def natural(ref, slot):
        dil = ref.shape[1]
        if dil == 1:
            return ref[0, 0].astype(F32)
        for r in range(dil):
            rows = ref[0, r].astype(F32)
            for j in range(rows.shape[1] // LANES):
                perm_scr[slot, j, pl.ds(r, tm // dil, stride=dil), :] = rows[:, j * LANES:(j + 1) * LANES]
        return jnp.concatenate([perm_scr[slot, j] for j in range(perm_scr.shape[1])], axis=1)

    yb = ((hf_ref[0] + hb_ref[0]) * _gelu_tanh(by_ref[0].astype(F32))).astype(BF16)

    l0, l1, l2 = natural(l0_ref, 0), natural(l1_ref, 0), natural(l2_ref, 1)
    o0, o1, o2 = natural(o0_ref, 2), natural(o1_ref, 2), natural(o2_ref, 3)
    mx = jnp.maximum(jnp.maximum(l0, l1), l2)
    e0, e1, e2 = jnp.exp2(l0 - mx), jnp.exp2(l1 - mx), jnp.exp2(l2 - mx)
    yd = ((e0 * o0 + e1 * o1 + e2 * o2) / (e0 + e1 + e2)).astype(BF16)

    def gate_logits(kk):
        off = wg_ref.shape[1] - N_BRANCH * d_model + kk * d_model
        return lambda: jnp.dot(hn, wg_ref[:, off:off + d_model], preferred_element_type=F32)

    def branch(y, w_ref):
        return lambda: jnp.dot(y, w_ref[...], preferred_element_type=F32)

    dots = []
    for kk, (y, w_ref) in enumerate(((ya_ref[0], wa_ref), (yb, wb_ref), (yc_ref[0], wc_ref), (yd, wd_ref))):
        dots += [gate_logits(kk), branch(y, w_ref)]
    pending = [dots[0](), dots[1]()]
    merged = None
    for kk in range(N_BRANCH):
        pending += [d() for d in dots[2 * kk + 2:2 * kk + 4]]
        logits, proj = pending.pop(0), pending.pop(0)
        term = _sigmoid(logits + gbias_ref[kk:kk + 1, :]) * proj
        merged = term if merged is None else merged + term

    out_ref[0] = x + jnp.dot(merged.astype(BF16), wo_ref[...], preferred_element_type=F32)


def _merge(x, layer, ya, hf, hb, by, yc, o_d, lse_d, g, w_all, gbias, wa, wb, wc, wd, wo):
    b, s, d = x.shape
    tm = min(TM_MERGE, s)
    tok = lambda width: pl.BlockSpec((1, tm, width), lambda bb, i: (bb, i, 0))
    dw = D_HEADS * HEAD_DIM
    res = [pl.BlockSpec((1, dil, tm // dil, dw), lambda bb, i: (bb, 0, i, 0)) for _, dil in D_PATTERNS]
    return pl.pallas_call(
        _merge_kernel,
        grid=(b, s // tm),
        in_specs=[tok(d), tok(512), tok(LRU_WIDTH), tok(LRU_WIDTH), tok(LRU_WIDTH), tok(512),
                  *res, *res, _layer(g, layer), _layer(w_all, layer), _layer(gbias, layer),
                  _layer(wa, layer), _layer(wb, layer), _layer(wc, layer), _layer(wd, layer), _layer(wo, layer)],
        out_specs=tok(d),
        out_shape=jax.ShapeDtypeStruct((b, s, d), F32),
        scratch_shapes=[pltpu.VMEM((4, dw // LANES, tm, LANES), F32)],
        compiler_params=_cparams(("parallel", "parallel")),
        name="merge",
    )(x, ya, hf, hb, by, yc, *o_d, *lse_d, g, w_all, gbias, wa, wb, wc, wd, wo)


def _mlp_kernel(x_ref, g_ref, w1_ref, w2_ref, gf_ref, out_ref, *, final_norm):
    x = x_ref[0]
    var = jnp.mean(x * x, axis=-1, keepdims=True)
    hn = (x * lax.rsqrt(var + EPS) * g_ref[...]).astype(BF16)
    hidden = w1_ref.shape[1]
    acc = x
    for c in range(hidden // MLP_CHUNK):
        h = jnp.dot(hn, w1_ref[:, c * MLP_CHUNK:(c + 1) * MLP_CHUNK], preferred_element_type=F32)
        h = jnp.square(jnp.maximum(h, 0.0)).astype(BF16)
        acc = acc + jnp.dot(h, w2_ref[c * MLP_CHUNK:(c + 1) * MLP_CHUNK, :], preferred_element_type=F32)
    if final_norm:
        var = jnp.mean(acc * acc, axis=-1, keepdims=True)
        acc = acc * lax.rsqrt(var + EPS) * gf_ref[...]
    out_ref[0] = acc


def _mlp(x, layer, g, w1, w2, g_final, final_norm):
    b, s, d = x.shape
    tm = min(TM_MLP, s)
    tok = pl.BlockSpec((1, tm, d), lambda bb, i: (bb, i, 0))
    return pl.pallas_call(
        functools.partial(_mlp_kernel, final_norm=final_norm),
        grid=(b, s // tm),
        in_specs=[tok, _layer(g, layer), _layer(w1, layer), _layer(w2, layer), _resident(g_final.shape)],
        out_specs=tok,
        out_shape=jax.ShapeDtypeStruct((b, s, d), F32),
        compiler_params=_cparams(("parallel", "parallel")),
        name="mlp",
    )(x, g, w1, w2, g_final)


def _dense_block_diag(w):
    nb, bw = w.shape[-3], w.shape[-2]
    eye = jnp.eye(nb, dtype=w.dtype)
    return jnp.einsum('...ncd,nm->...ncmd', w, eye).reshape(w.shape[:-3] + (nb * bw, nb * bw))


def kernel(x, norm_mix_g, w_in, gate_bias, qk_norm_g, conv_w, conv_b, lru_gate_w, lru_gate_b, lru_lambda, sink_logit, w_proj_a, w_proj_b, w_proj_c, w_proj_d, w_out, norm_mlp_g, w_mlp1, w_mlp2, norm_final_g):
    b, s, d = x.shape
    depth = w_in.shape[0]
    ax_t, seq_t = _rope_tables(s)
    dw = D_HEADS * HEAD_DIM
    n_groups = len(D_PATTERNS)
    bf = lambda w: w.astype(BF16)
    w_in_bf, wa, wb, wc, wd, wo, w1, w2 = map(bf, (w_in, w_proj_a, w_proj_b, w_proj_c, w_proj_d, w_out, w_mlp1, w_mlp2))
    g_mix, g_mlp, conv_b3 = norm_mix_g[:, None, :], norm_mlp_g[:, None, :], conv_b[:, None, :]
    qk_gain = jnp.tile(qk_norm_g, (1, 1, LANES // HEAD_DIM))
    gate_w_dense = bf(_dense_block_diag(lru_gate_w))

    for l in range(depth):
        outs = _inproj(x, l, g_mix, w_in_bf, qk_gain, ax_t, seq_t, conv_w, conv_b3, gate_w_dense,
                       lru_gate_b, lru_lambda)
        qta, ka, vta, af, uf, ab, ub, by, qc, kc, vc = outs[:11]
        qd, kd, vd = outs[11:11 + n_groups], outs[11 + n_groups:11 + 2 * n_groups], outs[11 + 2 * n_groups:11 + 3 * n_groups]

        ya = _attn_a(qta, ka, vta)

        hf, hb = _lru(af, uf, ab, ub)

        yc = _band_attention(qc[:, None], kc[:, None], vc[:, None], nkv=C_KV_HEADS, group=C_HEADS // C_KV_HEADS,
                             half_window=C_HALF_WINDOW, sink=sink_logit, sink_layer=l)[:, 0]

        o_d, lse_d = [], []
        for gi, (window, dil) in enumerate(D_PATTERNS):
            o, lse = _band_attention(qd[gi], kd[gi], vd[gi], nkv=D_HEADS, group=1,
                                     half_window=window // (2 * dil), want_lse=True)
            o_d.append(o)
            lse_d.append(lse)

        x = _merge(x, l, ya, hf, hb, by, yc, o_d, lse_d, g_mix, w_in_bf, gate_bias, wa, wb, wc, wd, wo)
        x = _mlp(x, l, g_mlp, w1, w2, norm_final_g[None, :], final_norm=(l == depth - 1))
    return x
```
